```python
import jax, jax.numpy as jnp
from jax import lax
import numpy as np

D_MODEL = 2048
BATCH = 4
SEQ = 2048
DEPTH = 2
DEC_BATCH = 128
DEC_SEQ = 1
PAST_LEN = 16384
PAGE_SIZE = 128

GLA_HEADS = 4
DK_TOTAL = D_MODEL // 2
DV_TOTAL = D_MODEL
DK_HEAD = DK_TOTAL // GLA_HEADS
DV_HEAD = DV_TOTAL // GLA_HEADS
GATE_RANK = 16
GATE_NORMALIZER = 16.0
GLA_CHUNK = 64
GLA_IN_WIDTH = 2 * DK_TOTAL + 2 * DV_TOTAL + GATE_RANK
CONV_W = 3
N_EXPERTS = 16
N_GROUPS = 4
EXPERTS_PER_GROUP = N_EXPERTS // N_GROUPS
TOP_K = 2
D_FF = D_MODEL // 2
N_GLA_LAYERS = (DEPTH + 1) // 2
N_CONV_LAYERS = DEPTH // 2
ALPHA = (2.0 * DEPTH) ** 0.25
BETA = (8.0 * DEPTH) ** -0.25
LN_EPS = 1e-5
RMS_EPS = 1e-6

kernel_name = "gla_shortconv_grouped_moe_deepnorm_step"


def layer_norm(x, g, b):
    x32 = x.astype(jnp.float32)
    mu = jnp.mean(x32, axis=-1, keepdims=True)
    var = jnp.mean(jnp.square(x32 - mu), axis=-1, keepdims=True)
    return ((x32 - mu) * lax.rsqrt(var + LN_EPS) * g.astype(jnp.float32) + b.astype(jnp.float32)).astype(x.dtype)


def gla_chunk_step(S, inp):
    q, k, v, g = inp
    C = q.shape[2]
    b = jnp.cumsum(g, axis=2)
    causal = jnp.tril(jnp.ones((C, C), dtype=bool))
    diff = b[:, :, :, None, :] - b[:, :, None, :, :]
    decay = jnp.exp(jnp.where(causal[None, None, :, :, None], diff, -jnp.inf))
    scores = jnp.einsum('bhid,bhjd,bhijd->bhij', q, k, decay)
    o = jnp.einsum('bhij,bhje->bhie', scores, v) + jnp.einsum('bhid,bhde->bhie', q * jnp.exp(b), S)
    b_last = b[:, :, -1:, :]
    S_new = jnp.exp(b_last[:, :, 0, :])[..., None] * S + jnp.einsum('bhjd,bhje->bhde', k * jnp.exp(b_last - b), v)
    return S_new, o


def gla_mixer(x, S0, w_in, w_gate, b_gate, norm_g, w_out):
    Bn, T, _ = x.shape
    proj = x @ w_in
    q, k, v, r, gl = jnp.split(proj, [DK_TOTAL, 2 * DK_TOTAL, 2 * DK_TOTAL + DV_TOTAL, 2 * DK_TOTAL + 2 * DV_TOTAL], axis=-1)
    g = jax.nn.log_sigmoid((gl @ w_gate + b_gate).astype(jnp.float32)) / GATE_NORMALIZER

    def heads(t, dh):
        return t.astype(jnp.float32).reshape(Bn, T, GLA_HEADS, dh).transpose(0, 2, 1, 3)

    qh = heads(q, DK_HEAD)
    kh = heads(k, DK_HEAD) * (DK_HEAD ** -0.5)
    vh = heads(v, DV_HEAD)
    gh = heads(g, DK_HEAD)
    C = GLA_CHUNK if T % GLA_CHUNK == 0 else T
    NC = T // C

    def chunks(t):
        return jnp.moveaxis(t.reshape(Bn, GLA_HEADS, NC, C, t.shape[-1]), 2, 0)

    S_fin, o = lax.scan(gla_chunk_step, S0.astype(jnp.float32), (chunks(qh), chunks(kh), chunks(vh), chunks(gh)))
    o = jnp.moveaxis(o, 0, 2).reshape(Bn, GLA_HEADS, T, DV_HEAD).transpose(0, 2, 1, 3)
    o = o * lax.rsqrt(jnp.mean(jnp.square(o), axis=-1, keepdims=True) + RMS_EPS)
    o = o.reshape(Bn, T, DV_TOTAL) * norm_g.astype(jnp.float32)
    y = (o.astype(x.dtype) * jax.nn.silu(r)) @ w_out
    return y, S_fin.astype(S0.dtype)


def conv_mixer(x, buf, w_in, w_conv, w_out):
    T = x.shape[1]
    gb, gc, h = jnp.split(x @ w_in, 3, axis=-1)
    u = gc * h
    up = jnp.concatenate([buf.astype(u.dtype), u], axis=1)
    z = up[:, 0:T] * w_conv[0]
    for j in range(1, CONV_W):
        z = z + up[:, j:j + T] * w_conv[j]
    y = (gb * z) @ w_out
    return y, up[:, -(CONV_W - 1):].astype(buf.dtype)


def moe_ffn(x, router_w, router_b, w1, w3, w2):
    Bn, T, D = x.shape
    ntok = Bn * T
    xt = x.reshape(ntok, D)
    logits = xt.astype(jnp.float32) @ router_w.astype(jnp.float32) + router_b.astype(jnp.float32)
    probs = jax.nn.softmax(logits, axis=-1)
    grouped = probs.reshape(ntok, N_GROUPS, EXPERTS_PER_GROUP)
    group_score = lax.top_k(grouped, TOP_K)[0].sum(-1)
    g_sel = jnp.argmax(group_score, axis=-1)
    in_group = jnp.take_along_axis(grouped, g_sel[:, None, None], axis=1)[:, 0]
    top_p, top_i = lax.top_k(in_group, TOP_K)
    expert = (g_sel[:, None] * EXPERTS_PER_GROUP + top_i).reshape(-1).astype(jnp.int32)
    gate = (top_p / jnp.sum(top_p, axis=-1, keepdims=True)).reshape(-1)
    token = jnp.arange(ntok * TOP_K, dtype=jnp.int32) // TOP_K
    order = jnp.argsort(expert)
    tok_sorted = token[order]
    sizes = jnp.bincount(expert, length=N_EXPERTS).astype(jnp.int32)
    xs = xt[tok_sorted]
    hid = jax.nn.silu(lax.ragged_dot(xs, w1, sizes)) * lax.ragged_dot(xs, w3, sizes)
    ys = lax.ragged_dot(hid, w2, sizes) * gate[order][:, None].astype(x.dtype)
    out = jnp.zeros_like(xt).at[tok_sorted].add(ys)
    return out.reshape(Bn, T, D)


def run_trunk(x, state_gla, state_conv, router_w, router_b, gla_w_in, gla_w_gate, gla_b_gate, gla_norm_g,
              gla_w_out, conv_w_in, conv_w, conv_w_out, ln_mix_g, ln_mix_b, ln_ffn_g, ln_ffn_b,
              moe_w1, moe_w3, moe_w2):
    new_gla = []
    new_conv = []
    for i in range(DEPTH):
        j = i // 2
        if i % 2 == 0:
            mix, s = gla_mixer(x, state_gla[j], gla_w_in[j], gla_w_gate[j], gla_b_gate[j], gla_norm_g[j], gla_w_out[j])
            new_gla.append(s)
        else:
            mix, s = conv_mixer(x, state_conv[j], conv_w_in[j], conv_w[j], conv_w_out[j])
            new_conv.append(s)
        x = layer_norm(ALPHA * x + mix, ln_mix_g[i], ln_mix_b[i])
        x = layer_norm(ALPHA * x + moe_ffn(x, router_w, router_b, moe_w1[i], moe_w3[i], moe_w2[i]), ln_ffn_g[i], ln_ffn_b[i])
    return x, jnp.stack(new_gla), jnp.stack(new_conv)


def setup_inputs(seed: int = 0) -> dict:
    key = jax.random.key(seed)
    ks = jax.random.split(key, 24)
    f32 = jnp.float32
    nrm = lambda k, shape, s: jax.random.normal(k, shape, f32) * s
    return {
        "x_prompt": nrm(ks[0], (BATCH, SEQ, D_MODEL), 1.0),
        "x_sample": nrm(ks[1], (DEC_BATCH, DEC_SEQ, D_MODEL), 1.0),
        "state_gla": nrm(ks[2], (N_GLA_LAYERS, DEC_BATCH, GLA_HEADS, DK_HEAD, DV_HEAD), DK_HEAD ** -0.5),
        "state_conv": nrm(ks[3], (N_CONV_LAYERS, DEC_BATCH, CONV_W - 1, D_MODEL), 1.0),
        "router_w": nrm(ks[4], (D_MODEL, N_EXPERTS), D_MODEL ** -0.5),
        "router_b": nrm(ks[5], (N_EXPERTS,), 0.01),
        "gla_w_in": nrm(ks[6], (N_GLA_LAYERS, D_MODEL, GLA_IN_WIDTH), D_MODEL ** -0.5),
        "gla_w_gate": nrm(ks[7], (N_GLA_LAYERS, GATE_RANK, DK_TOTAL), GATE_RANK ** -0.5),
        "gla_b_gate": nrm(ks[8], (N_GLA_LAYERS, DK_TOTAL), 0.1),
        "gla_norm_g": 1.0 + nrm(ks[9], (N_GLA_LAYERS, DV_TOTAL), 0.01),
        "gla_w_out": nrm(ks[10], (N_GLA_LAYERS, DV_TOTAL, D_MODEL), BETA * DV_TOTAL ** -0.5),
        "conv_w_in": nrm(ks[11], (N_CONV_LAYERS, D_MODEL, 3 * D_MODEL), D_MODEL ** -0.5),
        "conv_w": nrm(ks[12], (N_CONV_LAYERS, CONV_W, D_MODEL), CONV_W ** -0.5),
        "conv_w_out": nrm(ks[13], (N_CONV_LAYERS, D_MODEL, D_MODEL), BETA * D_MODEL ** -0.5),
        "ln_mix_g": 1.0 + nrm(ks[14], (DEPTH, D_MODEL), 0.01),
        "ln_mix_b": nrm(ks[15], (DEPTH, D_MODEL), 0.01),
        "ln_ffn_g": 1.0 + nrm(ks[16], (DEPTH, D_MODEL), 0.01),
        "ln_ffn_b": nrm(ks[17], (DEPTH, D_MODEL), 0.01),
        "moe_w1": nrm(ks[18], (DEPTH, N_EXPERTS, D_MODEL, D_FF), D_MODEL ** -0.5),
        "moe_w3": nrm(ks[19], (DEPTH, N_EXPERTS, D_MODEL, D_FF), D_MODEL ** -0.5),
        "moe_w2": nrm(ks[20], (DEPTH, N_EXPERTS, D_FF, D_MODEL), BETA * D_FF ** -0.5),
    }


def reference(x_prompt, x_sample, state_gla, state_conv, router_w, router_b, gla_w_in, gla_w_gate, gla_b_gate,
              gla_norm_g, gla_w_out, conv_w_in, conv_w, conv_w_out, ln_mix_g, ln_mix_b, ln_ffn_g, ln_ffn_b,
              moe_w1, moe_w3, moe_w2):
    bp = x_prompt.shape[0]
    zero_gla = jnp.zeros((N_GLA_LAYERS, bp, GLA_HEADS, DK_HEAD, DV_HEAD), state_gla.dtype)
    zero_conv = jnp.zeros((N_CONV_LAYERS, bp, CONV_W - 1, D_MODEL), state_conv.dtype)
    y_prompt, new_state_gla_prompt, new_state_conv_prompt = run_trunk(
        x_prompt, zero_gla, zero_conv, router_w, router_b, gla_w_in, gla_w_gate, gla_b_gate, gla_norm_g,
        gla_w_out, conv_w_in, conv_w, conv_w_out, ln_mix_g, ln_mix_b, ln_ffn_g, ln_ffn_b, moe_w1, moe_w3, moe_w2)
    y_sample, new_state_gla_sample, new_state_conv_sample = run_trunk(
        x_sample, state_gla, state_conv, router_w, router_b, gla_w_in, gla_w_gate, gla_b_gate, gla_norm_g,
        gla_w_out, conv_w_in, conv_w, conv_w_out, ln_mix_g, ln_mix_b, ln_ffn_g, ln_ffn_b, moe_w1, moe_w3, moe_w2)
    return (y_prompt, y_sample, new_state_gla_prompt, new_state_conv_prompt, new_state_gla_sample, new_state_conv_sample)
```

```python
import functools

import jax
import jax.numpy as jnp
from jax import lax
from jax.experimental import pallas as pl
from jax.experimental.pallas import tpu as pltpu

F32 = jnp.float32
BF16 = jnp.bfloat16

D_MODEL = 2048
N_PROMPT_SEQ = 4
SEQ = 2048
N_PROMPT = N_PROMPT_SEQ * SEQ
N_SAMPLE = 128
N_TOK = N_PROMPT + N_SAMPLE
DEPTH = 2

GLA_HEADS = 4
DK_TOTAL = D_MODEL // 2
DV_TOTAL = D_MODEL
DK_HEAD = DK_TOTAL // GLA_HEADS
DV_HEAD = DV_TOTAL // GLA_HEADS
GATE_RANK = 16
GATE_NORMALIZER = 16.0
GLA_QKVR_WIDTH = 2 * DK_TOTAL + 2 * DV_TOTAL
CONV_W = 3
N_EXPERTS = 16
N_GROUPS = 4
EXPERTS_PER_GROUP = N_EXPERTS // N_GROUPS
TOP_K = 2
D_FF = D_MODEL // 2
ALPHA = (2.0 * DEPTH) ** 0.25
LN_EPS = 1e-5
RMS_EPS = 1e-6

LANES = 128
SUBLANES = 8
VMEM_LIMIT_BYTES = 56 * 1024 * 1024

TOKEN_TILE = 640
ROW_TILE = 320
GLA_CHUNK = 64
GLA_SAMPLE_BATCH = 16
MOE_TILE = 256
N_ASSIGN = N_TOK * TOP_K
MOE_MAX_TILES = N_ASSIGN // MOE_TILE + N_EXPERTS
N_ASSIGN_PAD = MOE_MAX_TILES * MOE_TILE
GATHER_TILE = 256


def _params(*semantics):
    return pltpu.CompilerParams(dimension_semantics=semantics, vmem_limit_bytes=VMEM_LIMIT_BYTES)


def _split3(x):
    hi = x.astype(BF16)
    r1 = x - hi.astype(F32)
    mid = r1.astype(BF16)
    lo = (r1 - mid.astype(F32)).astype(BF16)
    return hi, mid, lo


def _dot_nn(a, b):
    return jnp.dot(a, b, preferred_element_type=F32)


def _dot_tn(a, b):
    return lax.dot_general(a, b, (((0,), (0,)), ((), ())), preferred_element_type=F32)


def _dot_nt(a, b):
    return lax.dot_general(a, b, (((1,), (1,)), ((), ())), preferred_element_type=F32)


def _col_bcast(rows, ones):
    hi, mid, lo = _split3(rows)
    return _dot_tn(hi, ones) + _dot_tn(mid, ones) + _dot_tn(lo, ones)


def _silu(r):
    return r / (1.0 + jnp.exp(-r))


def _layer_norm(h, g, b):
    mu = jnp.mean(h, axis=-1, keepdims=True)
    d = h - mu
    var = jnp.mean(d * d, axis=-1, keepdims=True)
    return d * lax.rsqrt(var + LN_EPS) * g + b


def _matmul_kernel(x_ref, w_ref, o_ref, wb_ref):
    @pl.when(pl.program_id(1) == 0)
    def _():
        wb_ref[...] = w_ref[...].astype(BF16)

    o_ref[...] = _dot_nn(x_ref[...], wb_ref[...]).astype(o_ref.dtype)


def _matmul_cols(x, w, n_cols, tn, out_dtype, name):
    m, k = x.shape
    return pl.pallas_call(
        _matmul_kernel,
        grid=(n_cols // tn, m // TOKEN_TILE),
        in_specs=[pl.BlockSpec((TOKEN_TILE, k), lambda j, i: (i, 0)),
                  pl.BlockSpec((k, tn), lambda j, i: (0, j))],
        out_specs=pl.BlockSpec((TOKEN_TILE, tn), lambda j, i: (i, j)),
        out_shape=jax.ShapeDtypeStruct((m, n_cols), out_dtype),
        scratch_shapes=[pltpu.VMEM((k, tn), BF16)],
        compiler_params=_params("arbitrary", "arbitrary"),
        name=name,
    )(x, w)


def _gate_kernel(x_ref, wgl_ref, wg_ref, bg_ref, g_ref):
    gl = _dot_nn(x_ref[...], wgl_ref[...])
    z = _dot_nn(gl.astype(BF16), wg_ref[...]) + bg_ref[...]
    log_sig = jnp.minimum(z, 0.0) - jnp.log1p(jnp.exp(-jnp.abs(z)))
    g_ref[...] = log_sig * (1.0 / GATE_NORMALIZER)


def _gla_gate(xb, w_gl, w_gate, b_gate):
    wgl = jnp.pad(w_gl, ((0, 0), (0, LANES - GATE_RANK))).astype(BF16)
    wg = jnp.pad(w_gate, ((0, LANES - GATE_RANK), (0, 0))).astype(BF16)
    return pl.pallas_call(
        _gate_kernel,
        grid=(N_TOK // TOKEN_TILE,),
        in_specs=[pl.BlockSpec((TOKEN_TILE, D_MODEL), lambda i: (i, 0)),
                  pl.BlockSpec((D_MODEL, LANES), lambda i: (0, 0)),
                  pl.BlockSpec((LANES, DK_TOTAL), lambda i: (0, 0)),
                  pl.BlockSpec((1, DK_TOTAL), lambda i: (0, 0))],
        out_specs=pl.BlockSpec((TOKEN_TILE, DK_TOTAL), lambda i: (i, 0)),
        out_shape=jax.ShapeDtypeStruct((N_TOK, DK_TOTAL), F32),
        compiler_params=_params("arbitrary"),
        name="gla_gate",
    )(xb, wgl, wg, b_gate.reshape(1, DK_TOTAL))


def _rms_gate(o, norm_g, r):
    o = o * lax.rsqrt(jnp.mean(o * o, axis=-1, keepdims=True) + RMS_EPS)
    return (o * norm_g) * _silu(r)


def _dot_nt_f32(a, b):
    ah, am, al = _split3(a)
    bh, bm, bl = _split3(b)
    return (_dot_nt(ah, bh) + _dot_nt(ah, bm) + _dot_nt(am, bh)
            + _dot_nt(am, bm) + _dot_nt(ah, bl) + _dot_nt(al, bh))


def _gla_prompt_kernel(q_ref, k_ref, v_ref, r_ref, g_ref, ng_ref, og_ref, s_out_ref, s_ref):
    c = pl.program_id(1)
    cc = q_ref.shape[0]

    @pl.when(c == 0)
    def _():
        s_ref[...] = jnp.zeros_like(s_ref)

    row = lax.broadcasted_iota(jnp.int32, (cc, cc), 0)
    col = lax.broadcasted_iota(jnp.int32, (cc, cc), 1)
    causal = col <= row
    tri = jnp.where(causal, 1.0, 0.0).astype(BF16)
    g_hi, g_mid, g_lo = _split3(g_ref[...])
    b = _dot_nn(tri, g_hi) + _dot_nn(tri, g_mid) + _dot_nn(tri, g_lo)
    b_last = b[cc - 1:cc, :]
    b_mid = b[cc // 2 - 1:cc // 2, :]

    q = q_ref[...]
    k = k_ref[...] * (DK_HEAD ** -0.5)
    q_in = (q * jnp.exp(b)).astype(BF16)
    q_s = q * jnp.exp(b - b_mid)
    k_s = k * jnp.exp(b_mid - b)
    k_d = (k * jnp.exp(b_last - b)).astype(BF16)
    e_last = jnp.exp(b_last)

    sub = lax.broadcasted_iota(jnp.int32, (SUBLANES, DK_HEAD), 0)
    ones = jnp.ones((SUBLANES, DV_HEAD), BF16)
    for h in range(GLA_HEADS):
        dk = slice(h * DK_HEAD, (h + 1) * DK_HEAD)
        dv = slice(h * DV_HEAD, (h + 1) * DV_HEAD)
        scores = jnp.where(causal, _dot_nt_f32(q_s[:, dk], k_s[:, dk]), 0.0).astype(BF16)
        v = v_ref[:, dv].astype(BF16)
        s_old = s_ref[h]
        o = _dot_nn(scores, v) + _dot_nn(q_in[:, dk], s_old.astype(BF16))
        decay_rows = jnp.where(sub == 0, jnp.broadcast_to(e_last[:, dk], (SUBLANES, DK_HEAD)), 0.0)
        s_new = _col_bcast(decay_rows, ones) * s_old + _dot_tn(k_d[:, dk], v)
        s_ref[h] = s_new

        @pl.when(c == pl.num_programs(1) - 1)
        def _():
            s_out_ref[0, h] = s_new

        og_ref[:, dv] = _rms_gate(o, ng_ref[:, dv], r_ref[:, dv]).astype(BF16)


def _gla_prompt(qkvr, g, norm_g):
    nc = SEQ // GLA_CHUNK
    row = lambda b, c: b * nc + c
    return pl.pallas_call(
        _gla_prompt_kernel,
        grid=(N_PROMPT_SEQ, nc),
        in_specs=[pl.BlockSpec((GLA_CHUNK, DK_TOTAL), lambda b, c: (row(b, c), 0)),
                  pl.BlockSpec((GLA_CHUNK, DK_TOTAL), lambda b, c: (row(b, c), 1)),
                  pl.BlockSpec((GLA_CHUNK, DV_TOTAL), lambda b, c: (row(b, c), 1)),
                  pl.BlockSpec((GLA_CHUNK, DV_TOTAL), lambda b, c: (row(b, c), 2)),
                  pl.BlockSpec((GLA_CHUNK, DK_TOTAL), lambda b, c: (row(b, c), 0)),
                  pl.BlockSpec((1, DV_TOTAL), lambda b, c: (0, 0))],
        out_specs=[pl.BlockSpec((GLA_CHUNK, DV_TOTAL), lambda b, c: (row(b, c), 0)),
                   pl.BlockSpec((1, GLA_HEADS, DK_HEAD, DV_HEAD), lambda b, c: (b, 0, 0, 0))],
        out_shape=[jax.ShapeDtypeStruct((N_PROMPT, DV_TOTAL), BF16),
                   jax.ShapeDtypeStruct((N_PROMPT_SEQ, GLA_HEADS, DK_HEAD, DV_HEAD), F32)],
        scratch_shapes=[pltpu.VMEM((GLA_HEADS, DK_HEAD, DV_HEAD), F32)],
        compiler_params=_params("arbitrary", "arbitrary"),
        name="gla_prompt",
    )(qkvr, qkvr, qkvr, qkvr, g, norm_g)


def _gla_sample_kernel(q_ref, k_ref, v_ref, r_ref, g_ref, ng_ref, s_ref, og_ref, s_out_ref, o_scr):
    bb = q_ref.shape[0]
    q = q_ref[...]
    k = k_ref[...] * (DK_HEAD ** -0.5)
    v = v_ref[...]
    eg = jnp.exp(g_ref[...])
    qe = q * eg
    qk = jnp.sum(q * k, axis=-1, keepdims=True)
    sub = lax.broadcasted_iota(jnp.int32, (bb, DK_HEAD), 0)
    ones = jnp.ones((bb, DV_HEAD), BF16)
    for bi in range(bb):
        sel = sub == bi
        s_old = s_ref[bi, 0]
        decay = _col_bcast(jnp.where(sel, eg, 0.0), ones)
        k_col = _col_bcast(jnp.where(sel, k, 0.0), ones)
        qe_col = _col_bcast(jnp.where(sel, qe, 0.0), ones)
        s_out_ref[bi, 0] = decay * s_old + k_col * v[bi:bi + 1, :]
        o_scr[bi:bi + 1, :] = jnp.sum(qe_col * s_old, axis=0, keepdims=True)
    o = qk * v + o_scr[...]
    og_ref[...] = _rms_gate(o, ng_ref[...], r_ref[...]).astype(BF16)


def _gla_sample(qkvr, g, norm_g, state):
    bb = GLA_SAMPLE_BATCH
    r0 = N_PROMPT // bb
    hk = DK_TOTAL // DK_HEAD
    hv = 2 * DK_TOTAL // DV_HEAD
    return pl.pallas_call(
        _gla_sample_kernel,
        grid=(N_SAMPLE // bb, GLA_HEADS),
        in_specs=[pl.BlockSpec((bb, DK_HEAD), lambda i, h: (r0 + i, h)),
                  pl.BlockSpec((bb, DK_HEAD), lambda i, h: (r0 + i, hk + h)),
                  pl.BlockSpec((bb, DV_HEAD), lambda i, h: (r0 + i, hv + h)),
                  pl.BlockSpec((bb, DV_HEAD), lambda i, h: (r0 + i, hv + GLA_HEADS + h)),
                  pl.BlockSpec((bb, DK_HEAD), lambda i, h: (r0 + i, h)),
                  pl.BlockSpec((1, DV_HEAD), lambda i, h: (0, h)),
                  pl.BlockSpec((bb, 1, DK_HEAD, DV_HEAD), lambda i, h: (i, h, 0, 0))],
        out_specs=[pl.BlockSpec((bb, DV_HEAD), lambda i, h: (i, h)),
                   pl.BlockSpec((bb, 1, DK_HEAD, DV_HEAD), lambda i, h: (i, h, 0, 0))],
        out_shape=[jax.ShapeDtypeStruct((N_SAMPLE, DV_TOTAL), BF16),
                   jax.ShapeDtypeStruct((N_SAMPLE, GLA_HEADS, DK_HEAD, DV_HEAD), F32)],
        scratch_shapes=[pltpu.VMEM((bb, DV_HEAD), F32)],
        compiler_params=_params("arbitrary", "arbitrary"),
        name="gla_sample",
    )(qkvr, qkvr, qkvr, qkvr, g, norm_g, state)


def _top2_of4(p):
    ranks = []
    for j in range(4):
        rk = jnp.zeros(p[j].shape, jnp.int32)
        for i in range(4):
            if i == j:
                continue
            beats = (p[i] >= p[j]) if i < j else (p[i] > p[j])
            rk = rk + jnp.where(beats, 1, 0)
        ranks.append(rk)

    def pick(rank):
        val = jnp.zeros(p[0].shape, F32)
        idx = jnp.zeros(p[0].shape, jnp.int32)
        for j in range(4):
            hit = ranks[j] == rank
            val = jnp.where(hit, p[j], val)
            idx = jnp.where(hit, j, idx)
        return val, idx

    v1, i1 = pick(0)
    v2, i2 = pick(1)
    return v1, i1, v2, i2


def _route(x, rw_ref, rb_ref):
    logits = _dot_nn(x.astype(BF16), rw_ref[...])
    n = x.shape[0]
    n_pad = -n % LANES
    if n_pad:
        logits = jnp.concatenate([logits, jnp.zeros((n_pad, LANES), F32)], axis=0)
    lt = logits.T[0:N_EXPERTS, 0:n] + rb_ref[...]
    e = jnp.exp(lt - jnp.max(lt, axis=0, keepdims=True))
    probs = e / jnp.sum(e, axis=0, keepdims=True)
    best = None
    for grp in range(N_GROUPS):
        rows = [probs[grp * EXPERTS_PER_GROUP + j:grp * EXPERTS_PER_GROUP + j + 1, :]
                for j in range(EXPERTS_PER_GROUP)]
        v1, i1, v2, i2 = _top2_of4(rows)
        score = v1 + v2
        cand = (score, v1, i1 + grp * EXPERTS_PER_GROUP, v2, i2 + grp * EXPERTS_PER_GROUP)
        if best is None:
            best = cand
        else:
            better = score > best[0]
            best = tuple(jnp.where(better, n, o) for n, o in zip(cand, best))
    _, v1, e1, v2, e2 = best
    denom = v1 + v2
    return (e1, e2), (v1 / denom, v2 / denom)


def _proj_norm_route_kernel(a_ref, x_ref, w_ref, lg_ref, lb_ref, rw_ref, rb_ref,
                            xo_ref, id_ref, gt_ref):
    y = _dot_nn(a_ref[...], w_ref[...])
    xn = _layer_norm(ALPHA * x_ref[...] + y, lg_ref[...], lb_ref[...])
    xo_ref[...] = xn
    (e1, e2), (g1, g2) = _route(xn, rw_ref, rb_ref)
    id_ref[0, 0:1, :] = e1
    id_ref[0, 1:2, :] = e2
    gt_ref[0, 0:1, :] = g1
    gt_ref[0, 1:2, :] = g2


def _router_operands(router_w, router_b):
    rw = jnp.pad(router_w, ((0, 0), (0, LANES - N_EXPERTS))).astype(BF16)
    return rw, router_b.reshape(N_EXPERTS, 1)


def _proj_norm_route(a, x, w_out, ln_g, ln_b, router):
    nt = N_TOK // ROW_TILE
    rw, rb = router
    const = lambda i: (0, 0)
    x1, ids, gates = pl.pallas_call(
        _proj_norm_route_kernel,
        grid=(nt,),
        in_specs=[pl.BlockSpec((ROW_TILE, D_MODEL), lambda i: (i, 0)),
                  pl.BlockSpec((ROW_TILE, D_MODEL), lambda i: (i, 0)),
                  pl.BlockSpec((D_MODEL, D_MODEL), const),
                  pl.BlockSpec((1, D_MODEL), const),
                  pl.BlockSpec((1, D_MODEL), const),
                  pl.BlockSpec((D_MODEL, LANES), const),
                  pl.BlockSpec((N_EXPERTS, 1), const)],
        out_specs=[pl.BlockSpec((ROW_TILE, D_MODEL), lambda i: (i, 0)),
                   pl.BlockSpec((1, TOP_K, ROW_TILE), lambda i: (i, 0, 0)),
                   pl.BlockSpec((1, TOP_K, ROW_TILE), lambda i: (i, 0, 0))],
        out_shape=[jax.ShapeDtypeStruct((N_TOK, D_MODEL), F32),
                   jax.ShapeDtypeStruct((nt, TOP_K, ROW_TILE), jnp.int32),
                   jax.ShapeDtypeStruct((nt, TOP_K, ROW_TILE), F32)],
        compiler_params=_params("arbitrary"),
        name="proj_norm_route",
    )(a, x, w_out.astype(BF16), ln_g.reshape(1, D_MODEL), ln_b.reshape(1, D_MODEL), rw, rb)
    ids = ids.transpose(0, 2, 1).reshape(N_TOK, TOP_K)
    gates = gates.transpose(0, 2, 1).reshape(N_TOK, TOP_K)
    return x1, ids, gates


def _gather_kernel(idx_ref, src_ref, out_ref, buf_ref, sem_ref):
    i = pl.program_id(0)
    n = pl.num_programs(0)
    tg = out_ref.shape[0]

    def row_copy(tile, slot, r):
        row = idx_ref[tile * tg + r]
        return pltpu.make_async_copy(src_ref.at[pl.ds(row, 1)], buf_ref.at[slot, pl.ds(r, 1)],
                                     sem_ref.at[slot])

    def start_tile(tile, slot):
        def body(r, carry):
            row_copy(tile, slot, r).start()
            return carry
        lax.fori_loop(0, tg, body, 0)

    @pl.when(i == 0)
    def _():
        start_tile(0, 0)

    @pl.when(i + 1 < n)
    def _():
        start_tile(i + 1, (i + 1) % 2)

    slot = i % 2

    def wait_body(r, carry):
        row_copy(i, slot, r).wait()
        return carry
    lax.fori_loop(0, tg, wait_body, 0)
    out_ref[...] = buf_ref[slot].astype(out_ref.dtype)


def _gather_rows(src, idx, out_dtype, name):
    m = idx.shape[0]
    width = src.shape[1]
    return pl.pallas_call(
        _gather_kernel,
        grid_spec=pltpu.PrefetchScalarGridSpec(
            num_scalar_prefetch=1,
            grid=(m // GATHER_TILE,),
            in_specs=[pl.BlockSpec(memory_space=pl.ANY)],
            out_specs=pl.BlockSpec((GATHER_TILE, width), lambda i, idx: (i, 0)),
            scratch_shapes=[pltpu.VMEM((2, GATHER_TILE, width), src.dtype),
                            pltpu.SemaphoreType.DMA((2,))]),
        out_shape=jax.ShapeDtypeStruct((m, width), out_dtype),
        compiler_params=_params("arbitrary"),
        name=name,
    )(idx, src)


def _expert_changed(te_ref, i):
    return jnp.logical_or(i == 0, te_ref[i] != te_ref[jnp.maximum(i - 1, 0)])


def _moe_up_kernel(te_ref, nt_ref, xs_ref, w1_ref, w3_ref, hid_ref, w1b_ref, w3b_ref):
    i = pl.program_id(1)

    @pl.when(_expert_changed(te_ref, i))
    def _():
        w1b_ref[...] = w1_ref[0].astype(BF16)
        w3b_ref[...] = w3_ref[0].astype(BF16)

    @pl.when(i < nt_ref[0])
    def _():
        xs = xs_ref[...]
        h1 = _dot_nn(xs, w1b_ref[...])
        h3 = _dot_nn(xs, w3b_ref[...])
        hid_ref[...] = (_silu(h1) * h3).astype(BF16)

    @pl.when(i >= nt_ref[0])
    def _():
        hid_ref[...] = jnp.zeros_like(hid_ref)


def _moe_up(xs, w1, w3, tile_expert, n_tiles):
    tf = D_FF // 2
    last = lambda i, nt: jnp.minimum(i, nt[0] - 1)
    return pl.pallas_call(
        _moe_up_kernel,
        grid_spec=pltpu.PrefetchScalarGridSpec(
            num_scalar_prefetch=2,
            grid=(D_FF // tf, MOE_MAX_TILES),
            in_specs=[pl.BlockSpec((MOE_TILE, D_MODEL), lambda j, i, te, nt: (last(i, nt), 0)),
                      pl.BlockSpec((1, D_MODEL, tf), lambda j, i, te, nt: (te[i], 0, j)),
                      pl.BlockSpec((1, D_MODEL, tf), lambda j, i, te, nt: (te[i], 0, j))],
            out_specs=pl.BlockSpec((MOE_TILE, tf), lambda j, i, te, nt: (i, j)),
            scratch_shapes=[pltpu.VMEM((D_MODEL, tf), BF16), pltpu.VMEM((D_MODEL, tf), BF16)]),
        out_shape=jax.ShapeDtypeStruct((N_ASSIGN_PAD, D_FF), BF16),
        compiler_params=_params("arbitrary", "arbitrary"),
        name="moe_up",
    )(tile_expert, n_tiles, xs, w1, w3)


def _moe_down_kernel(te_ref, nt_ref, hid_ref, w2_ref, ys_ref, w2b_ref):
    i = pl.program_id(0)

    @pl.when(_expert_changed(te_ref, i))
    def _():
        w2b_ref[...] = w2_ref[0].astype(BF16)

    @pl.when(i < nt_ref[0])
    def _():
        ys_ref[...] = _dot_nn(hid_ref[...], w2b_ref[...])

    @pl.when(i >= nt_ref[0])
    def _():
        ys_ref[...] = jnp.zeros_like(ys_ref)


def _moe_down(hid, w2, tile_expert, n_tiles):
    last = lambda i, nt: jnp.minimum(i, nt[0] - 1)
    return pl.pallas_call(
        _moe_down_kernel,
        grid_spec=pltpu.PrefetchScalarGridSpec(
            num_scalar_prefetch=2,
            grid=(MOE_MAX_TILES,),
            in_specs=[pl.BlockSpec((MOE_TILE, D_FF), lambda i, te, nt: (last(i, nt), 0)),
                      pl.BlockSpec((1, D_FF, D_MODEL), lambda i, te, nt: (te[i], 0, 0))],
            out_specs=pl.BlockSpec((MOE_TILE, D_MODEL), lambda i, te, nt: (i, 0)),
            scratch_shapes=[pltpu.VMEM((D_FF, D_MODEL), BF16)]),
        out_shape=jax.ShapeDtypeStruct((N_ASSIGN_PAD, D_MODEL), F32),
        compiler_params=_params("arbitrary"),
        name="moe_down",
    )(tile_expert, n_tiles, hid, w2)


def _combine_norm_kernel(x_ref, c_ref, gt_ref, lg_ref, lb_ref, xo_ref, xb_ref):
    gt = gt_ref[...]
    moe = c_ref[:, 0:D_MODEL] * gt[:, 0:1] + c_ref[:, D_MODEL:2 * D_MODEL] * gt[:, 1:2]
    xn = _layer_norm(ALPHA * x_ref[...] + moe, lg_ref[...], lb_ref[...])
    xo_ref[...] = xn
    xb_ref[...] = xn.astype(BF16)


def _combine_norm(x, pair_rows, gates, ln_g, ln_b):
    const = lambda i: (0, 0)
    return pl.pallas_call(
        _combine_norm_kernel,
        grid=(N_TOK // ROW_TILE,),
        in_specs=[pl.BlockSpec((ROW_TILE, D_MODEL), lambda i: (i, 0)),
                  pl.BlockSpec((ROW_TILE, TOP_K * D_MODEL), lambda i: (i, 0)),
                  pl.BlockSpec((ROW_TILE, TOP_K), lambda i: (i, 0)),
                  pl.BlockSpec((1, D_MODEL), const),
                  pl.BlockSpec((1, D_MODEL), const)],
        out_specs=[pl.BlockSpec((ROW_TILE, D_MODEL), lambda i: (i, 0)),
                   pl.BlockSpec((ROW_TILE, D_MODEL), lambda i: (i, 0))],
        out_shape=[jax.ShapeDtypeStruct((N_TOK, D_MODEL), F32),
                   jax.ShapeDtypeStruct((N_TOK, D_MODEL), BF16)],
        compiler_params=_params("arbitrary"),
        name="combine_norm",
    )(x, pair_rows, gates, ln_g.reshape(1, D_MODEL), ln_b.reshape(1, D_MODEL))


def _routing_tables(ids):
    e = ids.reshape(-1)
    onehot = (e[:, None] == jnp.arange(N_EXPERTS, dtype=jnp.int32)[None, :]).astype(jnp.int32)
    csum = jnp.cumsum(onehot, axis=0)
    rank = jnp.sum(onehot * csum, axis=1) - 1
    sizes = csum[-1]
    tiles_per = (sizes + MOE_TILE - 1) // MOE_TILE
    tile_end = jnp.cumsum(tiles_per)
    tile_start = tile_end - tiles_per
    n_tiles = tile_end[-1]
    pos = (tile_start[e] * MOE_TILE + rank).astype(jnp.int32)
    token = jnp.arange(N_ASSIGN, dtype=jnp.int32) // TOP_K
    row_token = jnp.zeros((N_ASSIGN_PAD,), jnp.int32).at[pos].set(token)
    tile = jnp.arange(MOE_MAX_TILES, dtype=jnp.int32)
    owner = jnp.sum((tile_end[None, :] <= jnp.minimum(tile, n_tiles - 1)[:, None]).astype(jnp.int32), axis=1)
    return pos, row_token, owner.astype(jnp.int32), n_tiles.reshape(1).astype(jnp.int32)


def _moe_block(x, ids, gates, w1, w3, w2, ln_g, ln_b):
    pos, row_token, tile_expert, n_tiles = _routing_tables(ids)
    xs = _gather_rows(x, row_token, BF16, "moe_gather")
    hid = _moe_up(xs, w1, w3, tile_expert, n_tiles)
    ys = _moe_down(hid, w2, tile_expert, n_tiles)
    pair_rows = _gather_rows(ys, pos, F32, "moe_combine_gather").reshape(N_TOK, TOP_K * D_MODEL)
    return _combine_norm(x, pair_rows, gates, ln_g, ln_b)


def _conv_in_kernel(x_ref, wb_ref, wc_ref, wh_ref, cw_ref, s0_ref, s1_ref, gbz_ref, u_ref,
                    wbb_ref, wcb_ref, whb_ref, ubuf_ref):
    i = pl.program_id(1)
    tm = x_ref.shape[0]
    pad = SUBLANES

    @pl.when(i == 0)
    def _():
        wbb_ref[...] = wb_ref[...].astype(BF16)
        wcb_ref[...] = wc_ref[...].astype(BF16)
        whb_ref[...] = wh_ref[...].astype(BF16)
        ubuf_ref[0:pad, :] = jnp.zeros((pad, ubuf_ref.shape[1]), F32)

    x = x_ref[...]
    gb = _dot_nn(x, wbb_ref[...])
    u = _dot_nn(x, wcb_ref[...]) * _dot_nn(x, whb_ref[...])
    u_ref[...] = u
    ubuf_ref[pad:pad + tm, :] = u
    t = (i * tm + lax.broadcasted_iota(jnp.int32, (tm, 1), 0)) & (SEQ - 1)
    u_m1 = jnp.where(t >= 1, ubuf_ref[pad - 1:pad - 1 + tm, :], 0.0)
    u_m2 = jnp.where(t >= 2, ubuf_ref[pad - 2:pad - 2 + tm, :], 0.0)
    w0 = cw_ref[0:1, :]
    w1 = cw_ref[1:2, :]
    w2 = cw_ref[2:3, :]
    z = u_m2 * w0 + u_m1 * w1 + u * w2
    gbz_ref[...] = (gb * z).astype(BF16)
    ubuf_ref[0:pad, :] = ubuf_ref[tm:tm + pad, :]

    @pl.when(i == pl.num_programs(1) - 1)
    def _():
        lo = tm - N_SAMPLE
        z_s = s0_ref[...] * w0 + s1_ref[...] * w1 + u[lo:tm, :] * w2
        gbz_ref[lo:tm, :] = (gb[lo:tm, :] * z_s).astype(BF16)


def _conv_in(xb, w_in, conv_w, state):
    tn = 512
    nb = D_MODEL // tn
    s0 = state[:, 0, :]
    s1 = state[:, 1, :]
    return pl.pallas_call(
        _conv_in_kernel,
        grid=(nb, N_TOK // TOKEN_TILE),
        in_specs=[pl.BlockSpec((TOKEN_TILE, D_MODEL), lambda j, i: (i, 0)),
                  pl.BlockSpec((D_MODEL, tn), lambda j, i: (0, j)),
                  pl.BlockSpec((D_MODEL, tn), lambda j, i: (0, nb + j)),
                  pl.BlockSpec((D_MODEL, tn), lambda j, i: (0, 2 * nb + j)),
                  pl.BlockSpec((CONV_W, tn), lambda j, i: (0, j)),
                  pl.BlockSpec((N_SAMPLE, tn), lambda j, i: (0, j)),
                  pl.BlockSpec((N_SAMPLE, tn), lambda j, i: (0, j))],
        out_specs=[pl.BlockSpec((TOKEN_TILE, tn), lambda j, i: (i, j)),
                   pl.BlockSpec((TOKEN_TILE, tn), lambda j, i: (i, j))],
        out_shape=[jax.ShapeDtypeStruct((N_TOK, D_MODEL), BF16),
                   jax.ShapeDtypeStruct((N_TOK, D_MODEL), F32)],
        scratch_shapes=[pltpu.VMEM((D_MODEL, tn), BF16), pltpu.VMEM((D_MODEL, tn), BF16),
                        pltpu.VMEM((D_MODEL, tn), BF16),
                        pltpu.VMEM((TOKEN_TILE + 2 * SUBLANES, tn), F32)],
        compiler_params=_params("arbitrary", "arbitrary"),
        name="conv_in",
    )(xb, w_in, w_in, w_in, conv_w, s0, s1)


def kernel(x_prompt, x_sample, state_gla, state_conv, router_w, router_b, gla_w_in, gla_w_gate, gla_b_gate,
           gla_norm_g, gla_w_out, conv_w_in, conv_w, conv_w_out, ln_mix_g, ln_mix_b, ln_ffn_g, ln_ffn_b,
           moe_w1, moe_w3, moe_w2):
    x0 = jnp.concatenate([x_prompt.reshape(N_PROMPT, D_MODEL), x_sample.reshape(N_SAMPLE, D_MODEL)], axis=0)
    x0b = x0.astype(BF16)
    router = _router_operands(router_w, router_b)

    qkvr = _matmul_cols(x0b, gla_w_in[0], GLA_QKVR_WIDTH, 1024, F32, "gla_in_proj")
    g = _gla_gate(x0b, gla_w_in[0][:, GLA_QKVR_WIDTH:], gla_w_gate[0], gla_b_gate[0])
    norm_g = gla_norm_g[0].reshape(1, DV_TOTAL)
    og_p, s_prompt = _gla_prompt(qkvr, g, norm_g)
    og_s, s_sample = _gla_sample(qkvr, g, norm_g, state_gla[0])
    og = jnp.concatenate([og_p, og_s], axis=0)
    x1, ids, gates = _proj_norm_route(og, x0, gla_w_out[0], ln_mix_g[0], ln_mix_b[0], router)
    x2, x2b = _moe_block(x1, ids, gates, moe_w1[0], moe_w3[0], moe_w2[0], ln_ffn_g[0], ln_ffn_b[0])

    gbz, u = _conv_in(x2b, conv_w_in[0], conv_w[0], state_conv[0])
    x3, ids, gates = _proj_norm_route(gbz, x2, conv_w_out[0], ln_mix_g[1], ln_mix_b[1], router)
    x4, _ = _moe_block(x3, ids, gates, moe_w1[1], moe_w3[1], moe_w2[1], ln_ffn_g[1], ln_ffn_b[1])

    y_prompt = x4[:N_PROMPT].reshape(N_PROMPT_SEQ, SEQ, D_MODEL)
    y_sample = x4[N_PROMPT:].reshape(N_SAMPLE, 1, D_MODEL)
    conv_prompt = u[:N_PROMPT].reshape(N_PROMPT_SEQ, SEQ, D_MODEL)[:, SEQ - (CONV_W - 1):, :]
    conv_sample = jnp.concatenate([state_conv[0][:, 1:, :], u[N_PROMPT:][:, None, :]], axis=1)
    return (y_prompt, y_sample, s_prompt[None], conv_prompt[None], s_sample[None], conv_sample[None])
```

```python
import functools

import jax
import jax.numpy as jnp
from jax import lax
from jax.experimental import pallas as pl
from jax.experimental.pallas import tpu as pltpu

F32 = jnp.float32
BF16 = jnp.bfloat16

D_MODEL = 2048
N_PROMPT_SEQ = 4
SEQ = 2048
N_PROMPT = N_PROMPT_SEQ * SEQ
N_SAMPLE = 128
N_TOK = N_PROMPT + N_SAMPLE
DEPTH = 2

GLA_HEADS = 4
DK_TOTAL = D_MODEL // 2
DV_TOTAL = D_MODEL
DK_HEAD = DK_TOTAL // GLA_HEADS
DV_HEAD = DV_TOTAL // GLA_HEADS
GATE_RANK = 16
GATE_NORMALIZER = 16.0
GLA_QKVR_WIDTH = 2 * DK_TOTAL + 2 * DV_TOTAL
CONV_W = 3
N_EXPERTS = 16
N_GROUPS = 4
EXPERTS_PER_GROUP = N_EXPERTS // N_GROUPS
TOP_K = 2
D_FF = D_MODEL // 2
ALPHA = (2.0 * DEPTH) ** 0.25
LN_EPS = 1e-5
RMS_EPS = 1e-6

LANES = 128
SUBLANES = 8
VMEM_LIMIT_BYTES = 56 * 1024 * 1024

TOKEN_TILE = 640
ROW_TILE = 320
GLA_CHUNK = 64
GLA_SAMPLE_BATCH = 16
MOE_TILE = 256
N_ASSIGN = N_TOK * TOP_K
MOE_MAX_TILES = N_ASSIGN // MOE_TILE + N_EXPERTS
N_ASSIGN_PAD = MOE_MAX_TILES * MOE_TILE
GATHER_TILE = 256


def _params(*semantics):
    return pltpu.CompilerParams(dimension_semantics=semantics, vmem_limit_bytes=VMEM_LIMIT_BYTES)


def _split3(x):
    hi = x.astype(BF16)
    r1 = x - hi.astype(F32)
    mid = r1.astype(BF16)
    lo = (r1 - mid.astype(F32)).astype(BF16)
    return hi, mid, lo


def _dot_nn(a, b):
    return jnp.dot(a, b, preferred_element_type=F32)


def _dot_tn(a, b):
    return lax.dot_general(a, b, (((0,), (0,)), ((), ())), preferred_element_type=F32)


def _dot_nt(a, b):
    return lax.dot_general(a, b, (((1,), (1,)), ((), ())), preferred_element_type=F32)


def _col_bcast(rows, ones):
    hi, mid, lo = _split3(rows)
    return _dot_tn(hi, ones) + _dot_tn(mid, ones) + _dot_tn(lo, ones)


def _silu(r):
    return r / (1.0 + jnp.exp(-r))


def _layer_norm(h, g, b):
    mu = jnp.mean(h, axis=-1, keepdims=True)
    d = h - mu
    var = jnp.mean(d * d, axis=-1, keepdims=True)
    return d * lax.rsqrt(var + LN_EPS) * g + b


def _matmul_kernel(x_ref, w_ref, o_ref, wb_ref):
    @pl.when(pl.program_id(1) == 0)
    def _():
        wb_ref[...] = w_ref[...].astype(BF16)

    o_ref[...] = _dot_nn(x_ref[...], wb_ref[...]).astype(o_ref.dtype)


def _matmul_cols(x, w, n_cols, tn, out_dtype, name):
    m, k = x.shape
    return pl.pallas_call(
        _matmul_kernel,
        grid=(n_cols // tn, m // TOKEN_TILE),
        in_specs=[pl.BlockSpec((TOKEN_TILE, k), lambda j, i: (i, 0)),
                  pl.BlockSpec((k, tn), lambda j, i: (0, j))],
        out_specs=pl.BlockSpec((TOKEN_TILE, tn), lambda j, i: (i, j)),
        out_shape=jax.ShapeDtypeStruct((m, n_cols), out_dtype),
        scratch_shapes=[pltpu.VMEM((k, tn), BF16)],
        compiler_params=_params("arbitrary", "arbitrary"),
        name=name,
    )(x, w)


def _gate_kernel(x_ref, wgl_ref, wg_ref, bg_ref, g_ref):
    gl = _dot_nn(x_ref[...], wgl_ref[...])
    z = _dot_nn(gl.astype(BF16), wg_ref[...]) + bg_ref[...]
    log_sig = jnp.minimum(z, 0.0) - jnp.log1p(jnp.exp(-jnp.abs(z)))
    g_ref[...] = log_sig * (1.0 / GATE_NORMALIZER)


def _gla_gate(xb, w_gl, w_gate, b_gate):
    wgl = jnp.pad(w_gl, ((0, 0), (0, LANES - GATE_RANK))).astype(BF16)
    wg = jnp.pad(w_gate, ((0, LANES - GATE_RANK), (0, 0))).astype(BF16)
    return pl.pallas_call(
        _gate_kernel,
        grid=(N_TOK // TOKEN_TILE,),
        in_specs=[pl.BlockSpec((TOKEN_TILE, D_MODEL), lambda i: (i, 0)),
                  pl.BlockSpec((D_MODEL, LANES), lambda i: (0, 0)),
                  pl.BlockSpec((LANES, DK_TOTAL), lambda i: (0, 0)),
                  pl.BlockSpec((1, DK_TOTAL), lambda i: (0, 0))],
        out_specs=pl.BlockSpec((TOKEN_TILE, DK_TOTAL), lambda i: (i, 0)),
        out_shape=jax.ShapeDtypeStruct((N_TOK, DK_TOTAL), F32),
        compiler_params=_params("arbitrary"),
        name="gla_gate",
    )(xb, wgl, wg, b_gate.reshape(1, DK_TOTAL))


def _rms_gate(o, norm_g, r):
    o = o * lax.rsqrt(jnp.mean(o * o, axis=-1, keepdims=True) + RMS_EPS)
    return (o * norm_g) * _silu(r)


def _dot_nt_f32(a, b):
    ah, am, al = _split3(a)
    bh, bm, bl = _split3(b)
    return (_dot_nt(ah, bh) + _dot_nt(ah, bm) + _dot_nt(am, bh)
            + _dot_nt(am, bm) + _dot_nt(ah, bl) + _dot_nt(al, bh))


def _gla_prompt_kernel(q_ref, k_ref, v_ref, r_ref, g_ref, ng_ref, og_ref, s_out_ref, s_ref):
    c = pl.program_id(1)
    cc = q_ref.shape[0]

    @pl.when(c == 0)
    def _():
        s_ref[...] = jnp.zeros_like(s_ref)

    row = lax.broadcasted_iota(jnp.int32, (cc, cc), 0)
    col = lax.broadcasted_iota(jnp.int32, (cc, cc), 1)
    causal = col <= row
    tri = jnp.where(causal, 1.0, 0.0).astype(BF16)
    g_hi, g_mid, g_lo = _split3(g_ref[...])
    b = _dot_nn(tri, g_hi) + _dot_nn(tri, g_mid) + _dot_nn(tri, g_lo)
    b_last = b[cc - 1:cc, :]
    b_mid = b[cc // 2 - 1:cc // 2, :]

    q = q_ref[...]
    k = k_ref[...] * (DK_HEAD ** -0.5)
    q_in = (q * jnp.exp(b)).astype(BF16)
    q_s = q * jnp.exp(b - b_mid)
    k_s = k * jnp.exp(b_mid - b)
    k_d = (k * jnp.exp(b_last - b)).astype(BF16)
    e_last = jnp.exp(b_last)

    sub = lax.broadcasted_iota(jnp.int32, (SUBLANES, DK_HEAD), 0)
    ones = jnp.ones((SUBLANES, DV_HEAD), BF16)
    for h in range(GLA_HEADS):
        dk = slice(h * DK_HEAD, (h + 1) * DK_HEAD)
        dv = slice(h * DV_HEAD, (h + 1) * DV_HEAD)
        scores = jnp.where(causal, _dot_nt_f32(q_s[:, dk], k_s[:, dk]), 0.0).astype(BF16)
        v = v_ref[:, dv].astype(BF16)
        s_old = s_ref[h]
        o = _dot_nn(scores, v) + _dot_nn(q_in[:, dk], s_old.astype(BF16))
        decay_rows = jnp.where(sub == 0, jnp.broadcast_to(e_last[:, dk], (SUBLANES, DK_HEAD)), 0.0)
        s_new = _col_bcast(decay_rows, ones) * s_old + _dot_tn(k_d[:, dk], v)
        s_ref[h] = s_new

        @pl.when(c == pl.num_programs(1) - 1)
        def _():
            s_out_ref[0, h] = s_new

        og_ref[:, dv] = _rms_gate(o, ng_ref[:, dv], r_ref[:, dv]).astype(BF16)


def _gla_prompt(qkvr, g, norm_g):
    nc = SEQ // GLA_CHUNK
    row = lambda b, c: b * nc + c
    return pl.pallas_call(
        _gla_prompt_kernel,
        grid=(N_PROMPT_SEQ, nc),
        in_specs=[pl.BlockSpec((GLA_CHUNK, DK_TOTAL), lambda b, c: (row(b, c), 0)),
                  pl.BlockSpec((GLA_CHUNK, DK_TOTAL), lambda b, c: (row(b, c), 1)),
                  pl.BlockSpec((GLA_CHUNK, DV_TOTAL), lambda b, c: (row(b, c), 1)),
                  pl.BlockSpec((GLA_CHUNK, DV_TOTAL), lambda b, c: (row(b, c), 2)),
                  pl.BlockSpec((GLA_CHUNK, DK_TOTAL), lambda b, c: (row(b, c), 0)),
                  pl.BlockSpec((1, DV_TOTAL), lambda b, c: (0, 0))],
        out_specs=[pl.BlockSpec((GLA_CHUNK, DV_TOTAL), lambda b, c: (row(b, c), 0)),
                   pl.BlockSpec((1, GLA_HEADS, DK_HEAD, DV_HEAD), lambda b, c: (b, 0, 0, 0))],
        out_shape=[jax.ShapeDtypeStruct((N_PROMPT, DV_TOTAL), BF16),
                   jax.ShapeDtypeStruct((N_PROMPT_SEQ, GLA_HEADS, DK_HEAD, DV_HEAD), F32)],
        scratch_shapes=[pltpu.VMEM((GLA_HEADS, DK_HEAD, DV_HEAD), F32)],
        compiler_params=_params("arbitrary", "arbitrary"),
        name="gla_prompt",
    )(qkvr, qkvr, qkvr, qkvr, g, norm_g)


def _gla_sample_kernel(q_ref, k_ref, v_ref, r_ref, g_ref, ng_ref, s_ref, og_ref, s_out_ref, o_scr):
    bb = q_ref.shape[0]
    q = q_ref[...]
    k = k_ref[...] * (DK_HEAD ** -0.5)
    v = v_ref[...]
    eg = jnp.exp(g_ref[...])
    qe = q * eg
    qk = jnp.sum(q * k, axis=-1, keepdims=True)
    sub = lax.broadcasted_iota(jnp.int32, (bb, DK_HEAD), 0)
    ones = jnp.ones((bb, DV_HEAD), BF16)
    for bi in range(bb):
        sel = sub == bi
        s_old = s_ref[bi, 0]
        decay = _col_bcast(jnp.where(sel, eg, 0.0), ones)
        k_col = _col_bcast(jnp.where(sel, k, 0.0), ones)
        qe_col = _col_bcast(jnp.where(sel, qe, 0.0), ones)
        s_out_ref[bi, 0] = decay * s_old + k_col * v[bi:bi + 1, :]
        o_scr[bi:bi + 1, :] = jnp.sum(qe_col * s_old, axis=0, keepdims=True)
    o = qk * v + o_scr[...]
    og_ref[...] = _rms_gate(o, ng_ref[...], r_ref[...]).astype(BF16)


def _gla_sample(qkvr, g, norm_g, state):
    bb = GLA_SAMPLE_BATCH
    r0 = N_PROMPT // bb
    hk = DK_TOTAL // DK_HEAD
    hv = 2 * DK_TOTAL // DV_HEAD
    return pl.pallas_call(
        _gla_sample_kernel,
        grid=(N_SAMPLE // bb, GLA_HEADS),
        in_specs=[pl.BlockSpec((bb, DK_HEAD), lambda i, h: (r0 + i, h)),
                  pl.BlockSpec((bb, DK_HEAD), lambda i, h: (r0 + i, hk + h)),
                  pl.BlockSpec((bb, DV_HEAD), lambda i, h: (r0 + i, hv + h)),
                  pl.BlockSpec((bb, DV_HEAD), lambda i, h: (r0 + i, hv + GLA_HEADS + h)),
                  pl.BlockSpec((bb, DK_HEAD), lambda i, h: (r0 + i, h)),
                  pl.BlockSpec((1, DV_HEAD), lambda i, h: (0, h)),
                  pl.BlockSpec((bb, 1, DK_HEAD, DV_HEAD), lambda i, h: (i, h, 0, 0))],
        out_specs=[pl.BlockSpec((bb, DV_HEAD), lambda i, h: (i, h)),
                   pl.BlockSpec((bb, 1, DK_HEAD, DV_HEAD), lambda i, h: (i, h, 0, 0))],
        out_shape=[jax.ShapeDtypeStruct((N_SAMPLE, DV_TOTAL), BF16),
                   jax.ShapeDtypeStruct((N_SAMPLE, GLA_HEADS, DK_HEAD, DV_HEAD), F32)],
        scratch_shapes=[pltpu.VMEM((bb, DV_HEAD), F32)],
        compiler_params=_params("arbitrary", "arbitrary"),
        name="gla_sample",
    )(qkvr, qkvr, qkvr, qkvr, g, norm_g, state)


def _top2_of4(p):
    ranks = []
    for j in range(4):
        rk = jnp.zeros(p[j].shape, jnp.int32)
        for i in range(4):
            if i == j:
                continue
            beats = (p[i] >= p[j]) if i < j else (p[i] > p[j])
            rk = rk + jnp.where(beats, 1, 0)
        ranks.append(rk)

    def pick(rank):
        val = jnp.zeros(p[0].shape, F32)
        idx = jnp.zeros(p[0].shape, jnp.int32)
        for j in range(4):
            hit = ranks[j] == rank
            val = jnp.where(hit, p[j], val)
            idx = jnp.where(hit, j, idx)
        return val, idx

    v1, i1 = pick(0)
    v2, i2 = pick(1)
    return v1, i1, v2, i2


def _route(x, rw_ref, rb_ref):
    logits = _dot_nn(x.astype(BF16), rw_ref[...])
    n = x.shape[0]
    n_pad = -n % LANES
    if n_pad:
        logits = jnp.concatenate([logits, jnp.zeros((n_pad, LANES), F32)], axis=0)
    lt = logits.T[0:N_EXPERTS, 0:n] + rb_ref[...]
    e = jnp.exp(lt - jnp.max(lt, axis=0, keepdims=True))
    probs = e / jnp.sum(e, axis=0, keepdims=True)
    best = None
    for grp in range(N_GROUPS):
        rows = [probs[grp * EXPERTS_PER_GROUP + j:grp * EXPERTS_PER_GROUP + j + 1, :]
                for j in range(EXPERTS_PER_GROUP)]
        v1, i1, v2, i2 = _top2_of4(rows)
        score = v1 + v2
        cand = (score, v1, i1 + grp * EXPERTS_PER_GROUP, v2, i2 + grp * EXPERTS_PER_GROUP)
        if best is None:
            best = cand
        else:
            better = score > best[0]
            best = tuple(jnp.where(better, n, o) for n, o in zip(cand, best))
    _, v1, e1, v2, e2 = best
    denom = v1 + v2
    return (e1, e2), (v1 / denom, v2 / denom)


def _proj_norm_route_kernel(a_ref, x_ref, w_ref, lg_ref, lb_ref, rw_ref, rb_ref,
                            xo_ref, id_ref, gt_ref):
    y = _dot_nn(a_ref[...], w_ref[...])
    xn = _layer_norm(ALPHA * x_ref[...] + y, lg_ref[...], lb_ref[...])
    xo_ref[...] = xn
    (e1, e2), (g1, g2) = _route(xn, rw_ref, rb_ref)
    id_ref[0, 0:1, :] = e1
    id_ref[0, 1:2, :] = e2
    gt_ref[0, 0:1, :] = g1
    gt_ref[0, 1:2, :] = g2


def _router_operands(router_w, router_b):
    rw = jnp.pad(router_w, ((0, 0), (0, LANES - N_EXPERTS))).astype(BF16)
    return rw, router_b.reshape(N_EXPERTS, 1)


def _proj_norm_route(a, x, w_out, ln_g, ln_b, router):
    nt = N_TOK // ROW_TILE
    rw, rb = router
    const = lambda i: (0, 0)
    x1, ids, gates = pl.pallas_call(
        _proj_norm_route_kernel,
        grid=(nt,),
        in_specs=[pl.BlockSpec((ROW_TILE, D_MODEL), lambda i: (i, 0)),
                  pl.BlockSpec((ROW_TILE, D_MODEL), lambda i: (i, 0)),
                  pl.BlockSpec((D_MODEL, D_MODEL), const),
                  pl.BlockSpec((1, D_MODEL), const),
                  pl.BlockSpec((1, D_MODEL), const),
                  pl.BlockSpec((D_MODEL, LANES), const),
                  pl.BlockSpec((N_EXPERTS, 1), const)],
        out_specs=[pl.BlockSpec((ROW_TILE, D_MODEL), lambda i: (i, 0)),
                   pl.BlockSpec((1, TOP_K, ROW_TILE), lambda i: (i, 0, 0)),
                   pl.BlockSpec((1, TOP_K, ROW_TILE), lambda i: (i, 0, 0))],
        out_shape=[jax.ShapeDtypeStruct((N_TOK, D_MODEL), F32),
                   jax.ShapeDtypeStruct((nt, TOP_K, ROW_TILE), jnp.int32),
                   jax.ShapeDtypeStruct((nt, TOP_K, ROW_TILE), F32)],
        compiler_params=_params("arbitrary"),
        name="proj_norm_route",
    )(a, x, w_out.astype(BF16), ln_g.reshape(1, D_MODEL), ln_b.reshape(1, D_MODEL), rw, rb)
    ids = ids.transpose(0, 2, 1).reshape(N_TOK, TOP_K)
    gates = gates.transpose(0, 2, 1).reshape(N_TOK, TOP_K)
    return x1, ids, gates


def _gather_kernel(idx_ref, src_ref, out_ref, buf_ref, sem_ref):
    i = pl.program_id(0)
    n = pl.num_programs(0)
    tg = out_ref.shape[0]

    def start_tile(tile, slot):
        base = tile * tg

        def body(r, carry):
            row = idx_ref[base + r]
            pltpu.make_async_copy(src_ref.at[pl.ds(row, 1)], buf_ref.at[slot, pl.ds(r, 1)],
                                  sem_ref.at[slot]).start()
            return carry
        lax.fori_loop(0, tg, body, 0, unroll=8)

    @pl.when(i == 0)
    def _():
        start_tile(0, 0)

    @pl.when(i + 1 < n)
    def _():
        start_tile(i + 1, (i + 1) % 2)

    slot = i % 2
    pltpu.make_async_copy(src_ref.at[pl.ds(0, tg)], buf_ref.at[slot], sem_ref.at[slot]).wait()
    out_ref[...] = buf_ref[slot].astype(out_ref.dtype)


def _gather_rows(src, idx, out_dtype, name):
    m = idx.shape[0]
    width = src.shape[1]
    return pl.pallas_call(
        _gather_kernel,
        grid_spec=pltpu.PrefetchScalarGridSpec(
            num_scalar_prefetch=1,
            grid=(m // GATHER_TILE,),
            in_specs=[pl.BlockSpec(memory_space=pl.ANY)],
            out_specs=pl.BlockSpec((GATHER_TILE, width), lambda i, idx: (i, 0)),
            scratch_shapes=[pltpu.VMEM((2, GATHER_TILE, width), src.dtype),
                            pltpu.SemaphoreType.DMA((2,))]),
        out_shape=jax.ShapeDtypeStruct((m, width), out_dtype),
        compiler_params=_params("arbitrary"),
        name=name,
    )(idx, src)


def _expert_changed(te_ref, i):
    return jnp.logical_or(i == 0, te_ref[i] != te_ref[jnp.maximum(i - 1, 0)])


def _moe_up_kernel(te_ref, nt_ref, xs_ref, w1_ref, w3_ref, hid_ref, w1b_ref, w3b_ref):
    i = pl.program_id(1)

    @pl.when(_expert_changed(te_ref, i))
    def _():
        w1b_ref[...] = w1_ref[0, 0].astype(BF16)
        w3b_ref[...] = w3_ref[0, 0].astype(BF16)

    @pl.when(i < nt_ref[0])
    def _():
        xs = xs_ref[...]
        h1 = _dot_nn(xs, w1b_ref[...])
        h3 = _dot_nn(xs, w3b_ref[...])
        hid_ref[...] = (_silu(h1) * h3).astype(BF16)

    @pl.when(i >= nt_ref[0])
    def _():
        hid_ref[...] = jnp.zeros_like(hid_ref)


def _moe_up(xs, w1, w3, layer, tile_expert, n_tiles):
    tf = D_FF // 2
    last = lambda i, nt: jnp.minimum(i, nt[0] - 1)
    return pl.pallas_call(
        _moe_up_kernel,
        grid_spec=pltpu.PrefetchScalarGridSpec(
            num_scalar_prefetch=2,
            grid=(D_FF // tf, MOE_MAX_TILES),
            in_specs=[pl.BlockSpec((MOE_TILE, D_MODEL), lambda j, i, te, nt: (last(i, nt), 0)),
                      pl.BlockSpec((1, 1, D_MODEL, tf), lambda j, i, te, nt: (layer, te[i], 0, j)),
                      pl.BlockSpec((1, 1, D_MODEL, tf), lambda j, i, te, nt: (layer, te[i], 0, j))],
            out_specs=pl.BlockSpec((MOE_TILE, tf), lambda j, i, te, nt: (i, j)),
            scratch_shapes=[pltpu.VMEM((D_MODEL, tf), BF16), pltpu.VMEM((D_MODEL, tf), BF16)]),
        out_shape=jax.ShapeDtypeStruct((N_ASSIGN_PAD, D_FF), BF16),
        compiler_params=_params("arbitrary", "arbitrary"),
        name="moe_up",
    )(tile_expert, n_tiles, xs, w1, w3)


def _moe_down_kernel(te_ref, nt_ref, hid_ref, w2_ref, ys_ref, w2b_ref):
    i = pl.program_id(0)

    @pl.when(_expert_changed(te_ref, i))
    def _():
        w2b_ref[...] = w2_ref[0, 0].astype(BF16)

    @pl.when(i < nt_ref[0])
    def _():
        ys_ref[...] = _dot_nn(hid_ref[...], w2b_ref[...])

    @pl.when(i >= nt_ref[0])
    def _():
        ys_ref[...] = jnp.zeros_like(ys_ref)


def _moe_down(hid, w2, layer, tile_expert, n_tiles):
    last = lambda i, nt: jnp.minimum(i, nt[0] - 1)
    return pl.pallas_call(
        _moe_down_kernel,
        grid_spec=pltpu.PrefetchScalarGridSpec(
            num_scalar_prefetch=2,
            grid=(MOE_MAX_TILES,),
            in_specs=[pl.BlockSpec((MOE_TILE, D_FF), lambda i, te, nt: (last(i, nt), 0)),
                      pl.BlockSpec((1, 1, D_FF, D_MODEL), lambda i, te, nt: (layer, te[i], 0, 0))],
            out_specs=pl.BlockSpec((MOE_TILE, D_MODEL), lambda i, te, nt: (i, 0)),
            scratch_shapes=[pltpu.VMEM((D_FF, D_MODEL), BF16)]),
        out_shape=jax.ShapeDtypeStruct((N_ASSIGN_PAD, D_MODEL), F32),
        compiler_params=_params("arbitrary"),
        name="moe_down",
    )(tile_expert, n_tiles, hid, w2)


def _combine_norm_kernel(pos_ref, x_ref, ys_ref, gt_ref, lg_ref, lb_ref, xo_ref, xb_ref, buf_ref, sem_ref):
    i = pl.program_id(0)
    n = pl.num_programs(0)
    tm = x_ref.shape[0]

    def start_tile(tile, slot):
        base = tile * tm * TOP_K

        def body(t, carry):
            for k in range(TOP_K):
                row = pos_ref[base + TOP_K * t + k]
                pltpu.make_async_copy(ys_ref.at[pl.ds(row, 1)], buf_ref.at[slot, k, pl.ds(t, 1)],
                                      sem_ref.at[slot]).start()
            return carry
        lax.fori_loop(0, tm, body, 0, unroll=4)

    @pl.when(i == 0)
    def _():
        start_tile(0, 0)

    @pl.when(i + 1 < n)
    def _():
        start_tile(i + 1, (i + 1) % 2)

    slot = i % 2
    for k in range(TOP_K):
        pltpu.make_async_copy(ys_ref.at[pl.ds(0, tm)], buf_ref.at[slot, k], sem_ref.at[slot]).wait()
    gt = gt_ref[...]
    moe = buf_ref[slot, 0] * gt[:, 0:1] + buf_ref[slot, 1] * gt[:, 1:2]
    xn = _layer_norm(ALPHA * x_ref[...] + moe, lg_ref[...], lb_ref[...])
    xo_ref[...] = xn
    xb_ref[...] = xn.astype(BF16)


def _combine_norm(x, ys, pos, gates, ln_g, ln_b):
    const = lambda i, pos: (0, 0)
    tile = lambda i, pos: (i, 0)
    return pl.pallas_call(
        _combine_norm_kernel,
        grid_spec=pltpu.PrefetchScalarGridSpec(
            num_scalar_prefetch=1,
            grid=(N_TOK // ROW_TILE,),
            in_specs=[pl.BlockSpec((ROW_TILE, D_MODEL), tile),
                      pl.BlockSpec(memory_space=pl.ANY),
                      pl.BlockSpec((ROW_TILE, TOP_K), tile),
                      pl.BlockSpec((1, D_MODEL), const),
                      pl.BlockSpec((1, D_MODEL), const)],
            out_specs=[pl.BlockSpec((ROW_TILE, D_MODEL), tile),
                       pl.BlockSpec((ROW_TILE, D_MODEL), tile)],
            scratch_shapes=[pltpu.VMEM((2, TOP_K, ROW_TILE, D_MODEL), F32),
                            pltpu.SemaphoreType.DMA((2,))]),
        out_shape=[jax.ShapeDtypeStruct((N_TOK, D_MODEL), F32),
                   jax.ShapeDtypeStruct((N_TOK, D_MODEL), BF16)],
        compiler_params=_params("arbitrary"),
        name="combine_norm",
    )(pos, x, ys, gates, ln_g.reshape(1, D_MODEL), ln_b.reshape(1, D_MODEL))


def _routing_tables(ids):
    e = ids.reshape(-1)
    onehot = (e[:, None] == jnp.arange(N_EXPERTS, dtype=jnp.int32)[None, :]).astype(jnp.int32)
    csum = jnp.cumsum(onehot, axis=0)
    rank = jnp.sum(onehot * csum, axis=1) - 1
    sizes = csum[-1]
    tiles_per = (sizes + MOE_TILE - 1) // MOE_TILE
    tile_end = jnp.cumsum(tiles_per)
    tile_start = tile_end - tiles_per
    n_tiles = tile_end[-1]
    pos = (jnp.sum(onehot * (tile_start * MOE_TILE)[None, :], axis=1) + rank).astype(jnp.int32)
    tile = jnp.arange(MOE_MAX_TILES, dtype=jnp.int32)
    owner = jnp.sum((tile_end[None, :] <= jnp.minimum(tile, n_tiles - 1)[:, None]).astype(jnp.int32), axis=1)
    order = jnp.argsort(e, stable=True).astype(jnp.int32)
    owner_hot = (owner[:, None] == jnp.arange(N_EXPERTS, dtype=jnp.int32)[None, :]).astype(jnp.int32)
    size_start = jnp.cumsum(sizes) - sizes
    tile_first = jnp.sum(owner_hot * (size_start - tile_start * MOE_TILE)[None, :], axis=1)
    tile_limit = jnp.sum(owner_hot * (size_start + sizes)[None, :], axis=1)
    src = tile_first[:, None] + jnp.arange(N_ASSIGN_PAD, dtype=jnp.int32).reshape(MOE_MAX_TILES, MOE_TILE)
    valid = src < tile_limit[:, None]
    row_token = jnp.where(valid, order[jnp.clip(src, 0, N_ASSIGN - 1)] // TOP_K, 0).reshape(-1)
    return pos, row_token.astype(jnp.int32), owner.astype(jnp.int32), n_tiles.reshape(1).astype(jnp.int32)


def _moe_block(x, ids, gates, w1, w3, w2, layer, ln_g, ln_b):
    pos, row_token, tile_expert, n_tiles = _routing_tables(ids)
    xs = _gather_rows(x, row_token, BF16, "moe_gather")
    hid = _moe_up(xs, w1, w3, layer, tile_expert, n_tiles)
    ys = _moe_down(hid, w2, layer, tile_expert, n_tiles)
    return _combine_norm(x, ys, pos, gates, ln_g, ln_b)


def _conv_in_kernel(x_ref, wb_ref, wc_ref, wh_ref, cw_ref, s0_ref, s1_ref, gbz_ref, u_ref,
                    wbb_ref, wcb_ref, whb_ref, ubuf_ref):
    i = pl.program_id(1)
    tm = x_ref.shape[0]
    pad = SUBLANES

    @pl.when(i == 0)
    def _():
        wbb_ref[...] = wb_ref[...].astype(BF16)
        wcb_ref[...] = wc_ref[...].astype(BF16)
        whb_ref[...] = wh_ref[...].astype(BF16)
        ubuf_ref[0:pad, :] = jnp.zeros((pad, ubuf_ref.shape[1]), F32)

    x = x_ref[...]
    gb = _dot_nn(x, wbb_ref[...])
    u = _dot_nn(x, wcb_ref[...]) * _dot_nn(x, whb_ref[...])
    u_ref[...] = u
    ubuf_ref[pad:pad + tm, :] = u
    t = (i * tm + lax.broadcasted_iota(jnp.int32, (tm, 1), 0)) & (SEQ - 1)
    u_m1 = jnp.where(t >= 1, ubuf_ref[pad - 1:pad - 1 + tm, :], 0.0)
    u_m2 = jnp.where(t >= 2, ubuf_ref[pad - 2:pad - 2 + tm, :], 0.0)
    w0 = cw_ref[0:1, :]
    w1 = cw_ref[1:2, :]
    w2 = cw_ref[2:3, :]
    z = u_m2 * w0 + u_m1 * w1 + u * w2
    gbz_ref[...] = (gb * z).astype(BF16)
    ubuf_ref[0:pad, :] = ubuf_ref[tm:tm + pad, :]

    @pl.when(i == pl.num_programs(1) - 1)
    def _():
        lo = tm - N_SAMPLE
        z_s = s0_ref[...] * w0 + s1_ref[...] * w1 + u[lo:tm, :] * w2
        gbz_ref[lo:tm, :] = (gb[lo:tm, :] * z_s).astype(BF16)


def _conv_in(xb, w_in, conv_w, state):
    tn = 512
    nb = D_MODEL // tn
    s0 = state[:, 0, :]
    s1 = state[:, 1, :]
    return pl.pallas_call(
        _conv_in_kernel,
        grid=(nb, N_TOK // TOKEN_TILE),
        in_specs=[pl.BlockSpec((TOKEN_TILE, D_MODEL), lambda j, i: (i, 0)),
                  pl.BlockSpec((D_MODEL, tn), lambda j, i: (0, j)),
                  pl.BlockSpec((D_MODEL, tn), lambda j, i: (0, nb + j)),
                  pl.BlockSpec((D_MODEL, tn), lambda j, i: (0, 2 * nb + j)),
                  pl.BlockSpec((CONV_W, tn), lambda j, i: (0, j)),
                  pl.BlockSpec((N_SAMPLE, tn), lambda j, i: (0, j)),
                  pl.BlockSpec((N_SAMPLE, tn), lambda j, i: (0, j))],
        out_specs=[pl.BlockSpec((TOKEN_TILE, tn), lambda j, i: (i, j)),
                   pl.BlockSpec((TOKEN_TILE, tn), lambda j, i: (i, j))],
        out_shape=[jax.ShapeDtypeStruct((N_TOK, D_MODEL), BF16),
                   jax.ShapeDtypeStruct((N_TOK, D_MODEL), F32)],
        scratch_shapes=[pltpu.VMEM((D_MODEL, tn), BF16), pltpu.VMEM((D_MODEL, tn), BF16),
                        pltpu.VMEM((D_MODEL, tn), BF16),
                        pltpu.VMEM((TOKEN_TILE + 2 * SUBLANES, tn), F32)],
        compiler_params=_params("arbitrary", "arbitrary"),
        name="conv_in",
    )(xb, w_in, w_in, w_in, conv_w, s0, s1)


def kernel(x_prompt, x_sample, state_gla, state_conv, router_w, router_b, gla_w_in, gla_w_gate, gla_b_gate,
           gla_norm_g, gla_w_out, conv_w_in, conv_w, conv_w_out, ln_mix_g, ln_mix_b, ln_ffn_g, ln_ffn_b,
           moe_w1, moe_w3, moe_w2):
    x0 = jnp.concatenate([x_prompt.reshape(N_PROMPT, D_MODEL), x_sample.reshape(N_SAMPLE, D_MODEL)], axis=0)
    x0b = x0.astype(BF16)
    router = _router_operands(router_w, router_b)

    qkvr = _matmul_cols(x0b, gla_w_in[0], GLA_QKVR_WIDTH, 1024, F32, "gla_in_proj")
    g = _gla_gate(x0b, gla_w_in[0][:, GLA_QKVR_WIDTH:], gla_w_gate[0], gla_b_gate[0])
    norm_g = gla_norm_g[0].reshape(1, DV_TOTAL)
    og_p, s_prompt = _gla_prompt(qkvr, g, norm_g)
    og_s, s_sample = _gla_sample(qkvr, g, norm_g, state_gla[0])
    og = jnp.concatenate([og_p, og_s], axis=0)
    x1, ids, gates = _proj_norm_route(og, x0, gla_w_out[0], ln_mix_g[0], ln_mix_b[0], router)
    x2, x2b = _moe_block(x1, ids, gates, moe_w1, moe_w3, moe_w2, 0, ln_ffn_g[0], ln_ffn_b[0])

    gbz, u = _conv_in(x2b, conv_w_in[0], conv_w[0], state_conv[0])
    x3, ids, gates = _proj_norm_route(gbz, x2, conv_w_out[0], ln_mix_g[1], ln_mix_b[1], router)
    x4, _ = _moe_block(x3, ids, gates, moe_w1, moe_w3, moe_w2, 1, ln_ffn_g[1], ln_ffn_b[1])

    y_prompt = x4[:N_PROMPT].reshape(N_PROMPT_SEQ, SEQ, D_MODEL)
    y_sample = x4[N_PROMPT:].reshape(N_SAMPLE, 1, D_MODEL)
    conv_prompt = u[:N_PROMPT].reshape(N_PROMPT_SEQ, SEQ, D_MODEL)[:, SEQ - (CONV_W - 1):, :]
    conv_sample = jnp.concatenate([state_conv[0][:, 1:, :], u[N_PROMPT:][:, None, :]], axis=1)
    return (y_prompt, y_sample, s_prompt[None], conv_prompt[None], s_sample[None], conv_sample[None])
```

```python
import functools

import jax
import jax.numpy as jnp
from jax import lax
from jax.experimental import pallas as pl
from jax.experimental.pallas import tpu as pltpu

F32 = jnp.float32
BF16 = jnp.bfloat16

D_MODEL = 2048
N_PROMPT_SEQ = 4
SEQ = 2048
N_PROMPT = N_PROMPT_SEQ * SEQ
N_SAMPLE = 128
N_TOK = N_PROMPT + N_SAMPLE
DEPTH = 2

GLA_HEADS = 4
DK_TOTAL = D_MODEL // 2
DV_TOTAL = D_MODEL
DK_HEAD = DK_TOTAL // GLA_HEADS
DV_HEAD = DV_TOTAL // GLA_HEADS
GATE_RANK = 16
GATE_NORMALIZER = 16.0
GLA_QKVR_WIDTH = 2 * DK_TOTAL + 2 * DV_TOTAL
CONV_W = 3
N_EXPERTS = 16
N_GROUPS = 4
EXPERTS_PER_GROUP = N_EXPERTS // N_GROUPS
TOP_K = 2
D_FF = D_MODEL // 2
ALPHA = (2.0 * DEPTH) ** 0.25
LN_EPS = 1e-5
RMS_EPS = 1e-6

LANES = 128
SUBLANES = 8
VMEM_LIMIT_BYTES = 56 * 1024 * 1024

TOKEN_TILE = 640
ROW_TILE = 320
GLA_CHUNK = 64
GLA_SAMPLE_BATCH = 16
MOE_TILE = 256
N_ASSIGN = N_TOK * TOP_K
MOE_MAX_TILES = N_ASSIGN // MOE_TILE + N_EXPERTS
N_ASSIGN_PAD = MOE_MAX_TILES * MOE_TILE


def _params(*semantics):
    return pltpu.CompilerParams(dimension_semantics=semantics, vmem_limit_bytes=VMEM_LIMIT_BYTES)


def _split3(x):
    hi = x.astype(BF16)
    r1 = x - hi.astype(F32)
    mid = r1.astype(BF16)
    lo = (r1 - mid.astype(F32)).astype(BF16)
    return hi, mid, lo


def _dot_nn(a, b):
    return jnp.dot(a, b, preferred_element_type=F32)


def _dot_tn(a, b):
    return lax.dot_general(a, b, (((0,), (0,)), ((), ())), preferred_element_type=F32)


def _dot_nt(a, b):
    return lax.dot_general(a, b, (((1,), (1,)), ((), ())), preferred_element_type=F32)


def _col_bcast(rows, n):
    ones = jnp.ones((rows.shape[0], LANES), BF16)
    hi, mid, lo = _split3(rows)
    col = _dot_tn(hi, ones) + _dot_tn(mid, ones) + _dot_tn(lo, ones)
    return jnp.concatenate([col] * (n // LANES), axis=1)


def _silu(r):
    return r / (1.0 + jnp.exp(-r))


def _layer_norm(h, g, b):
    mu = jnp.mean(h, axis=-1, keepdims=True)
    d = h - mu
    var = jnp.mean(d * d, axis=-1, keepdims=True)
    return d * lax.rsqrt(var + LN_EPS) * g + b


def _matmul_kernel(x_ref, w_ref, o_ref, wb_ref):
    @pl.when(pl.program_id(1) == 0)
    def _():
        wb_ref[...] = w_ref[...].astype(BF16)

    o_ref[...] = _dot_nn(x_ref[...], wb_ref[...]).astype(o_ref.dtype)


def _matmul_cols(x, w, n_cols, tn, out_dtype, name):
    m, k = x.shape
    return pl.pallas_call(
        _matmul_kernel,
        grid=(n_cols // tn, m // TOKEN_TILE),
        in_specs=[pl.BlockSpec((TOKEN_TILE, k), lambda j, i: (i, 0)),
                  pl.BlockSpec((k, tn), lambda j, i: (0, j))],
        out_specs=pl.BlockSpec((TOKEN_TILE, tn), lambda j, i: (i, j)),
        out_shape=jax.ShapeDtypeStruct((m, n_cols), out_dtype),
        scratch_shapes=[pltpu.VMEM((k, tn), BF16)],
        compiler_params=_params("arbitrary", "arbitrary"),
        name=name,
    )(x, w)


def _gate_kernel(x_ref, wgl_ref, wg_ref, bg_ref, g_ref):
    gl = _dot_nn(x_ref[...], wgl_ref[...])
    z = _dot_nn(gl.astype(BF16), wg_ref[...]) + bg_ref[...]
    log_sig = jnp.minimum(z, 0.0) - jnp.log1p(jnp.exp(-jnp.abs(z)))
    g_ref[...] = log_sig * (1.0 / GATE_NORMALIZER)


def _gla_gate(xb, w_gl, w_gate, b_gate):
    wgl = jnp.pad(w_gl, ((0, 0), (0, LANES - GATE_RANK))).astype(BF16)
    wg = jnp.pad(w_gate, ((0, LANES - GATE_RANK), (0, 0))).astype(BF16)
    return pl.pallas_call(
        _gate_kernel,
        grid=(N_TOK // TOKEN_TILE,),
        in_specs=[pl.BlockSpec((TOKEN_TILE, D_MODEL), lambda i: (i, 0)),
                  pl.BlockSpec((D_MODEL, LANES), lambda i: (0, 0)),
                  pl.BlockSpec((LANES, DK_TOTAL), lambda i: (0, 0)),
                  pl.BlockSpec((1, DK_TOTAL), lambda i: (0, 0))],
        out_specs=pl.BlockSpec((TOKEN_TILE, DK_TOTAL), lambda i: (i, 0)),
        out_shape=jax.ShapeDtypeStruct((N_TOK, DK_TOTAL), F32),
        compiler_params=_params("arbitrary"),
        name="gla_gate",
    )(xb, wgl, wg, b_gate.reshape(1, DK_TOTAL))


def _rms_gate(o, norm_g, r):
    o = o * lax.rsqrt(jnp.mean(o * o, axis=-1, keepdims=True) + RMS_EPS)
    return (o * norm_g) * _silu(r)


def _dot_nt_f32(a, b):
    ah, am, al = _split3(a)
    bh, bm, bl = _split3(b)
    return (_dot_nt(ah, bh) + _dot_nt(ah, bm) + _dot_nt(am, bh)
            + _dot_nt(am, bm) + _dot_nt(ah, bl) + _dot_nt(al, bh))


def _gla_prompt_kernel(q_ref, k_ref, v_ref, r_ref, g_ref, ng_ref, og_ref, s_out_ref, s_ref):
    c = pl.program_id(1)
    cc = q_ref.shape[0]

    @pl.when(c == 0)
    def _():
        s_ref[...] = jnp.zeros_like(s_ref)

    row = lax.broadcasted_iota(jnp.int32, (cc, cc), 0)
    col = lax.broadcasted_iota(jnp.int32, (cc, cc), 1)
    causal = col <= row
    tri = jnp.where(causal, 1.0, 0.0).astype(BF16)
    g_hi, g_mid, g_lo = _split3(g_ref[...])
    b = _dot_nn(tri, g_hi) + _dot_nn(tri, g_mid) + _dot_nn(tri, g_lo)
    b_last = b[cc - 1:cc, :]
    b_mid = b[cc // 2 - 1:cc // 2, :]

    q = q_ref[...]
    k = k_ref[...] * (DK_HEAD ** -0.5)
    q_in = (q * jnp.exp(b)).astype(BF16)
    q_s = q * jnp.exp(b - b_mid)
    k_s = k * jnp.exp(b_mid - b)
    k_d = (k * jnp.exp(b_last - b)).astype(BF16)
    e_last = jnp.exp(b_last)

    sub = lax.broadcasted_iota(jnp.int32, (SUBLANES, DK_HEAD), 0)
    for h in range(GLA_HEADS):
        dk = slice(h * DK_HEAD, (h + 1) * DK_HEAD)
        dv = slice(h * DV_HEAD, (h + 1) * DV_HEAD)
        scores = jnp.where(causal, _dot_nt_f32(q_s[:, dk], k_s[:, dk]), 0.0).astype(BF16)
        v = v_ref[:, dv].astype(BF16)
        s_old = s_ref[h]
        o = _dot_nn(scores, v) + _dot_nn(q_in[:, dk], s_old.astype(BF16))
        decay_rows = jnp.where(sub == 0, jnp.broadcast_to(e_last[:, dk], (SUBLANES, DK_HEAD)), 0.0)
        s_new = _col_bcast(decay_rows, DV_HEAD) * s_old + _dot_tn(k_d[:, dk], v)
        s_ref[h] = s_new

        @pl.when(c == pl.num_programs(1) - 1)
        def _():
            s_out_ref[0, h] = s_new

        og_ref[:, dv] = _rms_gate(o, ng_ref[:, dv], r_ref[:, dv]).astype(BF16)


def _gla_prompt(qkvr, g, norm_g):
    nc = SEQ // GLA_CHUNK
    row = lambda b, c: b * nc + c
    return pl.pallas_call(
        _gla_prompt_kernel,
        grid=(N_PROMPT_SEQ, nc),
        in_specs=[pl.BlockSpec((GLA_CHUNK, DK_TOTAL), lambda b, c: (row(b, c), 0)),
                  pl.BlockSpec((GLA_CHUNK, DK_TOTAL), lambda b, c: (row(b, c), 1)),
                  pl.BlockSpec((GLA_CHUNK, DV_TOTAL), lambda b, c: (row(b, c), 1)),
                  pl.BlockSpec((GLA_CHUNK, DV_TOTAL), lambda b, c: (row(b, c), 2)),
                  pl.BlockSpec((GLA_CHUNK, DK_TOTAL), lambda b, c: (row(b, c), 0)),
                  pl.BlockSpec((1, DV_TOTAL), lambda b, c: (0, 0))],
        out_specs=[pl.BlockSpec((GLA_CHUNK, DV_TOTAL), lambda b, c: (row(b, c), 0)),
                   pl.BlockSpec((1, GLA_HEADS, DK_HEAD, DV_HEAD), lambda b, c: (b, 0, 0, 0))],
        out_shape=[jax.ShapeDtypeStruct((N_PROMPT, DV_TOTAL), BF16),
                   jax.ShapeDtypeStruct((N_PROMPT_SEQ, GLA_HEADS, DK_HEAD, DV_HEAD), F32)],
        scratch_shapes=[pltpu.VMEM((GLA_HEADS, DK_HEAD, DV_HEAD), F32)],
        compiler_params=_params("arbitrary", "arbitrary"),
        name="gla_prompt",
    )(qkvr, qkvr, qkvr, qkvr, g, norm_g)


def _gla_sample_kernel(q_ref, k_ref, v_ref, r_ref, g_ref, ng_ref, s_ref, og_ref, s_out_ref, o_scr):
    bb = q_ref.shape[0]
    q = q_ref[...]
    k = k_ref[...] * (DK_HEAD ** -0.5)
    v = v_ref[...]
    eg = jnp.exp(g_ref[...])
    qe = q * eg
    qk = jnp.sum(q * k, axis=-1, keepdims=True)
    sub = lax.broadcasted_iota(jnp.int32, (bb, DK_HEAD), 0)
    for bi in range(bb):
        sel = sub == bi
        s_old = s_ref[bi, 0]
        decay = _col_bcast(jnp.where(sel, eg, 0.0), DV_HEAD)
        k_col = _col_bcast(jnp.where(sel, k, 0.0), DV_HEAD)
        qe_col = _col_bcast(jnp.where(sel, qe, 0.0), DV_HEAD)
        s_out_ref[bi, 0] = decay * s_old + k_col * v[bi:bi + 1, :]
        o_scr[bi:bi + 1, :] = jnp.sum(qe_col * s_old, axis=0, keepdims=True)
    o = qk * v + o_scr[...]
    og_ref[...] = _rms_gate(o, ng_ref[...], r_ref[...]).astype(BF16)


def _gla_sample(qkvr, g, norm_g, state):
    bb = GLA_SAMPLE_BATCH
    r0 = N_PROMPT // bb
    hk = DK_TOTAL // DK_HEAD
    hv = 2 * DK_TOTAL // DV_HEAD
    return pl.pallas_call(
        _gla_sample_kernel,
        grid=(N_SAMPLE // bb, GLA_HEADS),
        in_specs=[pl.BlockSpec((bb, DK_HEAD), lambda i, h: (r0 + i, h)),
                  pl.BlockSpec((bb, DK_HEAD), lambda i, h: (r0 + i, hk + h)),
                  pl.BlockSpec((bb, DV_HEAD), lambda i, h: (r0 + i, hv + h)),
                  pl.BlockSpec((bb, DV_HEAD), lambda i, h: (r0 + i, hv + GLA_HEADS + h)),
                  pl.BlockSpec((bb, DK_HEAD), lambda i, h: (r0 + i, h)),
                  pl.BlockSpec((1, DV_HEAD), lambda i, h: (0, h)),
                  pl.BlockSpec((bb, 1, DK_HEAD, DV_HEAD), lambda i, h: (i, h, 0, 0))],
        out_specs=[pl.BlockSpec((bb, DV_HEAD), lambda i, h: (i, h)),
                   pl.BlockSpec((bb, 1, DK_HEAD, DV_HEAD), lambda i, h: (i, h, 0, 0))],
        out_shape=[jax.ShapeDtypeStruct((N_SAMPLE, DV_TOTAL), BF16),
                   jax.ShapeDtypeStruct((N_SAMPLE, GLA_HEADS, DK_HEAD, DV_HEAD), F32)],
        scratch_shapes=[pltpu.VMEM((bb, DV_HEAD), F32)],
        compiler_params=_params("arbitrary", "arbitrary"),
        name="gla_sample",
    )(qkvr, qkvr, qkvr, qkvr, g, norm_g, state)


def _top2_of4(p):
    ranks = []
    for j in range(4):
        rk = jnp.zeros(p[j].shape, jnp.int32)
        for i in range(4):
            if i == j:
                continue
            beats = (p[i] >= p[j]) if i < j else (p[i] > p[j])
            rk = rk + jnp.where(beats, 1, 0)
        ranks.append(rk)

    def pick(rank):
        val = jnp.zeros(p[0].shape, F32)
        idx = jnp.zeros(p[0].shape, jnp.int32)
        for j in range(4):
            hit = ranks[j] == rank
            val = jnp.where(hit, p[j], val)
            idx = jnp.where(hit, j, idx)
        return val, idx

    v1, i1 = pick(0)
    v2, i2 = pick(1)
    return v1, i1, v2, i2


def _route(x, rw_ref, rb_ref):
    logits = _dot_nn(x.astype(BF16), rw_ref[...])
    n = x.shape[0]
    n_pad = -n % LANES
    if n_pad:
        logits = jnp.concatenate([logits, jnp.zeros((n_pad, LANES), F32)], axis=0)
    lt = logits.T[0:N_EXPERTS, 0:n] + rb_ref[...]
    e = jnp.exp(lt - jnp.max(lt, axis=0, keepdims=True))
    probs = e / jnp.sum(e, axis=0, keepdims=True)
    best = None
    for grp in range(N_GROUPS):
        rows = [probs[grp * EXPERTS_PER_GROUP + j:grp * EXPERTS_PER_GROUP + j + 1, :]
                for j in range(EXPERTS_PER_GROUP)]
        v1, i1, v2, i2 = _top2_of4(rows)
        score = v1 + v2
        cand = (score, v1, i1 + grp * EXPERTS_PER_GROUP, v2, i2 + grp * EXPERTS_PER_GROUP)
        if best is None:
            best = cand
        else:
            better = score > best[0]
            best = tuple(jnp.where(better, n, o) for n, o in zip(cand, best))
    _, v1, e1, v2, e2 = best
    denom = v1 + v2
    return (e1, e2), (v1 / denom, v2 / denom)


def _proj_norm_route_kernel(a_ref, x_ref, w_ref, lg_ref, lb_ref, rw_ref, rb_ref,
                            xo_ref, id_ref, gt_ref):
    y = _dot_nn(a_ref[...], w_ref[...])
    xn = _layer_norm(ALPHA * x_ref[...] + y, lg_ref[...], lb_ref[...])
    xo_ref[...] = xn
    (e1, e2), (g1, g2) = _route(xn, rw_ref, rb_ref)
    id_ref[0, 0:1, :] = e1
    id_ref[0, 1:2, :] = e2
    gt_ref[0, 0:1, :] = g1
    gt_ref[0, 1:2, :] = g2


def _router_operands(router_w, router_b):
    rw = jnp.pad(router_w, ((0, 0), (0, LANES - N_EXPERTS))).astype(BF16)
    return rw, router_b.reshape(N_EXPERTS, 1)


def _proj_norm_route(a, x, w_out, ln_g, ln_b, router):
    nt = N_TOK // ROW_TILE
    rw, rb = router
    const = lambda i: (0, 0)
    x1, ids, gates = pl.pallas_call(
        _proj_norm_route_kernel,
        grid=(nt,),
        in_specs=[pl.BlockSpec((ROW_TILE, D_MODEL), lambda i: (i, 0)),
                  pl.BlockSpec((ROW_TILE, D_MODEL), lambda i: (i, 0)),
                  pl.BlockSpec((D_MODEL, D_MODEL), const),
                  pl.BlockSpec((1, D_MODEL), const),
                  pl.BlockSpec((1, D_MODEL), const),
                  pl.BlockSpec((D_MODEL, LANES), const),
                  pl.BlockSpec((N_EXPERTS, 1), const)],
        out_specs=[pl.BlockSpec((ROW_TILE, D_MODEL), lambda i: (i, 0)),
                   pl.BlockSpec((1, TOP_K, ROW_TILE), lambda i: (i, 0, 0)),
                   pl.BlockSpec((1, TOP_K, ROW_TILE), lambda i: (i, 0, 0))],
        out_shape=[jax.ShapeDtypeStruct((N_TOK, D_MODEL), F32),
                   jax.ShapeDtypeStruct((nt, TOP_K, ROW_TILE), jnp.int32),
                   jax.ShapeDtypeStruct((nt, TOP_K, ROW_TILE), F32)],
        compiler_params=_params("arbitrary"),
        name="proj_norm_route",
    )(a, x, w_out.astype(BF16), ln_g.reshape(1, D_MODEL), ln_b.reshape(1, D_MODEL), rw, rb)
    ids = ids.transpose(0, 2, 1).reshape(N_TOK, TOP_K)
    gates = gates.transpose(0, 2, 1).reshape(N_TOK, TOP_K)
    return x1, ids, gates


def _gather_kernel(idx_ref, nt_ref, src_ref, out_ref, buf_ref, sem_ref):
    i = pl.program_id(0)
    n = nt_ref[0]
    tg = out_ref.shape[0]

    def start_tile(tile, slot):
        base = tile * tg

        def body(r, carry):
            row = idx_ref[base + r]
            pltpu.make_async_copy(src_ref.at[pl.ds(row, 1)], buf_ref.at[slot, pl.ds(r, 1)],
                                  sem_ref.at[slot]).start()
            return carry
        lax.fori_loop(0, tg, body, 0, unroll=8)

    @pl.when(i == 0)
    def _():
        start_tile(0, 0)

    @pl.when(i + 1 < n)
    def _():
        start_tile(i + 1, (i + 1) % 2)

    @pl.when(i < n)
    def _():
        slot = i % 2
        pltpu.make_async_copy(src_ref.at[pl.ds(0, tg)], buf_ref.at[slot], sem_ref.at[slot]).wait()
        out_ref[...] = buf_ref[slot].astype(out_ref.dtype)

    @pl.when(i >= n)
    def _():
        out_ref[...] = jnp.zeros_like(out_ref)


def _gather_rows(src, idx, n_tiles, out_dtype, name):
    m = idx.shape[0]
    width = src.shape[1]
    return pl.pallas_call(
        _gather_kernel,
        grid_spec=pltpu.PrefetchScalarGridSpec(
            num_scalar_prefetch=2,
            grid=(m // MOE_TILE,),
            in_specs=[pl.BlockSpec(memory_space=pl.ANY)],
            out_specs=pl.BlockSpec((MOE_TILE, width), lambda i, idx, nt: (i, 0)),
            scratch_shapes=[pltpu.VMEM((2, MOE_TILE, width), src.dtype),
                            pltpu.SemaphoreType.DMA((2,))]),
        out_shape=jax.ShapeDtypeStruct((m, width), out_dtype),
        compiler_params=_params("arbitrary"),
        name=name,
    )(idx, n_tiles, src)


def _expert_changed(te_ref, i):
    return jnp.logical_or(i == 0, te_ref[i] != te_ref[jnp.maximum(i - 1, 0)])


def _moe_up_kernel(te_ref, nt_ref, xs_ref, w1_ref, w3_ref, hid_ref, w1b_ref, w3b_ref):
    i = pl.program_id(1)

    @pl.when(_expert_changed(te_ref, i))
    def _():
        w1b_ref[...] = w1_ref[0, 0].astype(BF16)
        w3b_ref[...] = w3_ref[0, 0].astype(BF16)

    @pl.when(i < nt_ref[0])
    def _():
        xs = xs_ref[...]
        h1 = _dot_nn(xs, w1b_ref[...])
        h3 = _dot_nn(xs, w3b_ref[...])
        hid_ref[...] = (_silu(h1) * h3).astype(BF16)

    @pl.when(i >= nt_ref[0])
    def _():
        hid_ref[...] = jnp.zeros_like(hid_ref)


def _moe_up(xs, w1, w3, layer, tile_expert, n_tiles):
    tf = D_FF // 2
    last = lambda i, nt: jnp.minimum(i, nt[0] - 1)
    return pl.pallas_call(
        _moe_up_kernel,
        grid_spec=pltpu.PrefetchScalarGridSpec(
            num_scalar_prefetch=2,
            grid=(D_FF // tf, MOE_MAX_TILES),
            in_specs=[pl.BlockSpec((MOE_TILE, D_MODEL), lambda j, i, te, nt: (last(i, nt), 0)),
                      pl.BlockSpec((1, 1, D_MODEL, tf), lambda j, i, te, nt: (layer, te[i], 0, j)),
                      pl.BlockSpec((1, 1, D_MODEL, tf), lambda j, i, te, nt: (layer, te[i], 0, j))],
            out_specs=pl.BlockSpec((MOE_TILE, tf), lambda j, i, te, nt: (i, j)),
            scratch_shapes=[pltpu.VMEM((D_MODEL, tf), BF16), pltpu.VMEM((D_MODEL, tf), BF16)]),
        out_shape=jax.ShapeDtypeStruct((N_ASSIGN_PAD, D_FF), BF16),
        compiler_params=_params("arbitrary", "arbitrary"),
        name="moe_up",
    )(tile_expert, n_tiles, xs, w1, w3)


def _moe_down_kernel(te_ref, nt_ref, hid_ref, w2_ref, ys_ref, w2b_ref):
    i = pl.program_id(0)

    @pl.when(_expert_changed(te_ref, i))
    def _():
        w2b_ref[...] = w2_ref[0, 0].astype(BF16)

    @pl.when(i < nt_ref[0])
    def _():
        ys_ref[...] = _dot_nn(hid_ref[...], w2b_ref[...])

    @pl.when(i >= nt_ref[0])
    def _():
        ys_ref[...] = jnp.zeros_like(ys_ref)


def _moe_down(hid, w2, layer, tile_expert, n_tiles):
    last = lambda i, nt: jnp.minimum(i, nt[0] - 1)
    return pl.pallas_call(
        _moe_down_kernel,
        grid_spec=pltpu.PrefetchScalarGridSpec(
            num_scalar_prefetch=2,
            grid=(MOE_MAX_TILES,),
            in_specs=[pl.BlockSpec((MOE_TILE, D_FF), lambda i, te, nt: (last(i, nt), 0)),
                      pl.BlockSpec((1, 1, D_FF, D_MODEL), lambda i, te, nt: (layer, te[i], 0, 0))],
            out_specs=pl.BlockSpec((MOE_TILE, D_MODEL), lambda i, te, nt: (i, 0)),
            scratch_shapes=[pltpu.VMEM((D_FF, D_MODEL), BF16)]),
        out_shape=jax.ShapeDtypeStruct((N_ASSIGN_PAD, D_MODEL), F32),
        compiler_params=_params("arbitrary"),
        name="moe_down",
    )(tile_expert, n_tiles, hid, w2)


def _combine_norm_kernel(pos_ref, x_ref, ys_ref, gt_ref, lg_ref, lb_ref, xo_ref, xb_ref, buf_ref, sem_ref):
    i = pl.program_id(0)
    n = pl.num_programs(0)
    tm = x_ref.shape[0]

    def start_tile(tile, slot):
        base = tile * tm * TOP_K

        def body(t, carry):
            for k in range(TOP_K):
                row = pos_ref[base + TOP_K * t + k]
                pltpu.make_async_copy(ys_ref.at[pl.ds(row, 1)], buf_ref.at[slot, k, pl.ds(t, 1)],
                                      sem_ref.at[slot]).start()
            return carry
        lax.fori_loop(0, tm, body, 0, unroll=4)

    @pl.when(i == 0)
    def _():
        start_tile(0, 0)

    @pl.when(i + 1 < n)
    def _():
        start_tile(i + 1, (i + 1) % 2)

    slot = i % 2
    for k in range(TOP_K):
        pltpu.make_async_copy(ys_ref.at[pl.ds(0, tm)], buf_ref.at[slot, k], sem_ref.at[slot]).wait()
    gt = gt_ref[...]
    moe = buf_ref[slot, 0] * gt[:, 0:1] + buf_ref[slot, 1] * gt[:, 1:2]
    xn = _layer_norm(ALPHA * x_ref[...] + moe, lg_ref[...], lb_ref[...])
    xo_ref[...] = xn
    xb_ref[...] = xn.astype(BF16)


def _combine_norm(x, ys, pos, gates, ln_g, ln_b):
    const = lambda i, pos: (0, 0)
    tile = lambda i, pos: (i, 0)
    return pl.pallas_call(
        _combine_norm_kernel,
        grid_spec=pltpu.PrefetchScalarGridSpec(
            num_scalar_prefetch=1,
            grid=(N_TOK // ROW_TILE,),
            in_specs=[pl.BlockSpec((ROW_TILE, D_MODEL), tile),
                      pl.BlockSpec(memory_space=pl.ANY),
                      pl.BlockSpec((ROW_TILE, TOP_K), tile),
                      pl.BlockSpec((1, D_MODEL), const),
                      pl.BlockSpec((1, D_MODEL), const)],
            out_specs=[pl.BlockSpec((ROW_TILE, D_MODEL), tile),
                       pl.BlockSpec((ROW_TILE, D_MODEL), tile)],
            scratch_shapes=[pltpu.VMEM((2, TOP_K, ROW_TILE, D_MODEL), F32),
                            pltpu.SemaphoreType.DMA((2,))]),
        out_shape=[jax.ShapeDtypeStruct((N_TOK, D_MODEL), F32),
                   jax.ShapeDtypeStruct((N_TOK, D_MODEL), BF16)],
        compiler_params=_params("arbitrary"),
        name="combine_norm",
    )(pos, x, ys, gates, ln_g.reshape(1, D_MODEL), ln_b.reshape(1, D_MODEL))


def _routing_tables(ids):
    e = ids.reshape(-1)
    onehot = (e[:, None] == jnp.arange(N_EXPERTS, dtype=jnp.int32)[None, :]).astype(jnp.int32)
    csum = jnp.cumsum(onehot, axis=0)
    rank = jnp.sum(onehot * csum, axis=1) - 1
    sizes = csum[-1]
    tiles_per = (sizes + MOE_TILE - 1) // MOE_TILE
    tile_end = jnp.cumsum(tiles_per)
    tile_start = tile_end - tiles_per
    n_tiles = tile_end[-1]
    pos = (jnp.sum(onehot * (tile_start * MOE_TILE)[None, :], axis=1) + rank).astype(jnp.int32)
    tile = jnp.arange(MOE_MAX_TILES, dtype=jnp.int32)
    owner = jnp.sum((tile_end[None, :] <= jnp.minimum(tile, n_tiles - 1)[:, None]).astype(jnp.int32), axis=1)
    order = jnp.argsort(e, stable=True).astype(jnp.int32)
    owner_hot = (owner[:, None] == jnp.arange(N_EXPERTS, dtype=jnp.int32)[None, :]).astype(jnp.int32)
    size_start = jnp.cumsum(sizes) - sizes
    tile_first = jnp.sum(owner_hot * (size_start - tile_start * MOE_TILE)[None, :], axis=1)
    tile_limit = jnp.sum(owner_hot * (size_start + sizes)[None, :], axis=1)
    row = jnp.arange(N_ASSIGN_PAD, dtype=jnp.int32).reshape(MOE_MAX_TILES, MOE_TILE)
    src = tile_first[:, None] + row
    valid = src < tile_limit[:, None]
    row_token = jnp.where(valid, order[jnp.clip(src, 0, N_ASSIGN - 1)] // TOP_K, row % N_TOK).reshape(-1)
    return pos, row_token.astype(jnp.int32), owner.astype(jnp.int32), n_tiles.reshape(1).astype(jnp.int32)


def _moe_block(x, ids, gates, w1, w3, w2, layer, ln_g, ln_b):
    pos, row_token, tile_expert, n_tiles = _routing_tables(ids)
    xs = _gather_rows(x, row_token, n_tiles, BF16, "moe_gather")
    hid = _moe_up(xs, w1, w3, layer, tile_expert, n_tiles)
    ys = _moe_down(hid, w2, layer, tile_expert, n_tiles)
    return _combine_norm(x, ys, pos, gates, ln_g, ln_b)


def _conv_in_kernel(x_ref, wb_ref, wc_ref, wh_ref, cw_ref, s0_ref, s1_ref, gbz_ref, u_ref,
                    wbb_ref, wcb_ref, whb_ref, ubuf_ref):
    i = pl.program_id(1)
    tm = x_ref.shape[0]
    pad = SUBLANES

    @pl.when(i == 0)
    def _():
        wbb_ref[...] = wb_ref[...].astype(BF16)
        wcb_ref[...] = wc_ref[...].astype(BF16)
        whb_ref[...] = wh_ref[...].astype(BF16)
        ubuf_ref[0:pad, :] = jnp.zeros((pad, ubuf_ref.shape[1]), F32)

    x = x_ref[...]
    gb = _dot_nn(x, wbb_ref[...])
    u = _dot_nn(x, wcb_ref[...]) * _dot_nn(x, whb_ref[...])
    u_ref[...] = u
    ubuf_ref[pad:pad + tm, :] = u
    t = (i * tm + lax.broadcasted_iota(jnp.int32, (tm, 1), 0)) & (SEQ - 1)
    u_m1 = jnp.where(t >= 1, ubuf_ref[pad - 1:pad - 1 + tm, :], 0.0)
    u_m2 = jnp.where(t >= 2, ubuf_ref[pad - 2:pad - 2 + tm, :], 0.0)
    w0 = cw_ref[0:1, :]
    w1 = cw_ref[1:2, :]
    w2 = cw_ref[2:3, :]
    z = u_m2 * w0 + u_m1 * w1 + u * w2
    gbz_ref[...] = (gb * z).astype(BF16)
    ubuf_ref[0:pad, :] = ubuf_ref[tm:tm + pad, :]

    @pl.when(i == pl.num_programs(1) - 1)
    def _():
        lo = tm - N_SAMPLE
        z_s = s0_ref[...] * w0 + s1_ref[...] * w1 + u[lo:tm, :] * w2
        gbz_ref[lo:tm, :] = (gb[lo:tm, :] * z_s).astype(BF16)


def _conv_in(xb, w_in, conv_w, state):
    tn = 512
    nb = D_MODEL // tn
    s0 = state[:, 0, :]
    s1 = state[:, 1, :]
    return pl.pallas_call(
        _conv_in_kernel,
        grid=(nb, N_TOK // TOKEN_TILE),
        in_specs=[pl.BlockSpec((TOKEN_TILE, D_MODEL), lambda j, i: (i, 0)),
                  pl.BlockSpec((D_MODEL, tn), lambda j, i: (0, j)),
                  pl.BlockSpec((D_MODEL, tn), lambda j, i: (0, nb + j)),
                  pl.BlockSpec((D_MODEL, tn), lambda j, i: (0, 2 * nb + j)),
                  pl.BlockSpec((CONV_W, tn), lambda j, i: (0, j)),
                  pl.BlockSpec((N_SAMPLE, tn), lambda j, i: (0, j)),
                  pl.BlockSpec((N_SAMPLE, tn), lambda j, i: (0, j))],
        out_specs=[pl.BlockSpec((TOKEN_TILE, tn), lambda j, i: (i, j)),
                   pl.BlockSpec((TOKEN_TILE, tn), lambda j, i: (i, j))],
        out_shape=[jax.ShapeDtypeStruct((N_TOK, D_MODEL), BF16),
                   jax.ShapeDtypeStruct((N_TOK, D_MODEL), F32)],
        scratch_shapes=[pltpu.VMEM((D_MODEL, tn), BF16), pltpu.VMEM((D_MODEL, tn), BF16),
                        pltpu.VMEM((D_MODEL, tn), BF16),
                        pltpu.VMEM((TOKEN_TILE + 2 * SUBLANES, tn), F32)],
        compiler_params=_params("arbitrary", "arbitrary"),
        name="conv_in",
    )(xb, w_in, w_in, w_in, conv_w, s0, s1)


def kernel(x_prompt, x_sample, state_gla, state_conv, router_w, router_b, gla_w_in, gla_w_gate, gla_b_gate,
           gla_norm_g, gla_w_out, conv_w_in, conv_w, conv_w_out, ln_mix_g, ln_mix_b, ln_ffn_g, ln_ffn_b,
           moe_w1, moe_w3, moe_w2):
    x0 = jnp.concatenate([x_prompt.reshape(N_PROMPT, D_MODEL), x_sample.reshape(N_SAMPLE, D_MODEL)], axis=0)
    x0b = x0.astype(BF16)
    router = _router_operands(router_w, router_b)

    qkvr = _matmul_cols(x0b, gla_w_in[0], GLA_QKVR_WIDTH, 1024, F32, "gla_in_proj")
    g = _gla_gate(x0b, gla_w_in[0][:, GLA_QKVR_WIDTH:], gla_w_gate[0], gla_b_gate[0])
    norm_g = gla_norm_g[0].reshape(1, DV_TOTAL)
    og_p, s_prompt = _gla_prompt(qkvr, g, norm_g)
    og_s, s_sample = _gla_sample(qkvr, g, norm_g, state_gla[0])
    og = jnp.concatenate([og_p, og_s], axis=0)
    x1, ids, gates = _proj_norm_route(og, x0, gla_w_out[0], ln_mix_g[0], ln_mix_b[0], router)
    x2, x2b = _moe_block(x1, ids, gates, moe_w1, moe_w3, moe_w2, 0, ln_ffn_g[0], ln_ffn_b[0])

    gbz, u = _conv_in(x2b, conv_w_in[0], conv_w[0], state_conv[0])
    x3, ids, gates = _proj_norm_route(gbz, x2, conv_w_out[0], ln_mix_g[1], ln_mix_b[1], router)
    x4, _ = _moe_block(x3, ids, gates, moe_w1, moe_w3, moe_w2, 1, ln_ffn_g[1], ln_ffn_b[1])

    y_prompt = x4[:N_PROMPT].reshape(N_PROMPT_SEQ, SEQ, D_MODEL)
    y_sample = x4[N_PROMPT:].reshape(N_SAMPLE, 1, D_MODEL)
    conv_prompt = jnp.stack([u[(b + 1) * SEQ - (CONV_W - 1):(b + 1) * SEQ] for b in range(N_PROMPT_SEQ)])
    conv_sample = jnp.concatenate([state_conv[0][:, 1:, :], u[N_PROMPT:][:, None, :]], axis=1)
    return (y_prompt, y_sample, s_prompt[None], conv_prompt[None], s_sample[None], conv_sample[None])
```

```python
import functools

import jax
import jax.numpy as jnp
from jax import lax
from jax.experimental import pallas as pl
from jax.experimental.pallas import tpu as pltpu

F32 = jnp.float32
BF16 = jnp.bfloat16

D_MODEL = 2048
N_PROMPT_SEQ = 4
SEQ = 2048
N_PROMPT = N_PROMPT_SEQ * SEQ
N_SAMPLE = 128
N_TOK = N_PROMPT + N_SAMPLE
DEPTH = 2

GLA_HEADS = 4
DK_TOTAL = D_MODEL // 2
DV_TOTAL = D_MODEL
DK_HEAD = DK_TOTAL // GLA_HEADS
DV_HEAD = DV_TOTAL // GLA_HEADS
GATE_RANK = 16
GATE_NORMALIZER = 16.0
GLA_QKVR_WIDTH = 2 * DK_TOTAL + 2 * DV_TOTAL
CONV_W = 3
N_EXPERTS = 16
N_GROUPS = 4
EXPERTS_PER_GROUP = N_EXPERTS // N_GROUPS
TOP_K = 2
D_FF = D_MODEL // 2
ALPHA = (2.0 * DEPTH) ** 0.25
LN_EPS = 1e-5
RMS_EPS = 1e-6

LANES = 128
SUBLANES = 8
VMEM_LIMIT_BYTES = 56 * 1024 * 1024

TOKEN_TILE = 640
ROW_TILE = 320
GLA_CHUNK = 64
GLA_SAMPLE_BATCH = 16
MOE_TILE = 256
N_ASSIGN = N_TOK * TOP_K
MOE_MAX_TILES = N_ASSIGN // MOE_TILE + N_EXPERTS
N_ASSIGN_PAD = MOE_MAX_TILES * MOE_TILE


def _params(*semantics):
    return pltpu.CompilerParams(dimension_semantics=semantics, vmem_limit_bytes=VMEM_LIMIT_BYTES)


def _split3(x):
    hi = x.astype(BF16)
    r1 = x - hi.astype(F32)
    mid = r1.astype(BF16)
    lo = (r1 - mid.astype(F32)).astype(BF16)
    return hi, mid, lo


def _dot_nn(a, b):
    return jnp.dot(a, b, preferred_element_type=F32)


def _dot_tn(a, b):
    return lax.dot_general(a, b, (((0,), (0,)), ((), ())), preferred_element_type=F32)


def _dot_nt(a, b):
    return lax.dot_general(a, b, (((1,), (1,)), ((), ())), preferred_element_type=F32)


def _col_bcast(rows, n):
    ones = jnp.ones((rows.shape[0], LANES), BF16)
    hi, mid, lo = _split3(rows)
    col = _dot_tn(hi, ones) + _dot_tn(mid, ones) + _dot_tn(lo, ones)
    return jnp.concatenate([col] * (n // LANES), axis=1)


def _silu(r):
    return r / (1.0 + jnp.exp(-r))


def _layer_norm(h, g, b):
    mu = jnp.mean(h, axis=-1, keepdims=True)
    d = h - mu
    var = jnp.mean(d * d, axis=-1, keepdims=True)
    return d * lax.rsqrt(var + LN_EPS) * g + b


def _matmul_kernel(x_ref, w_ref, o_ref, wb_ref):
    @pl.when(pl.program_id(1) == 0)
    def _():
        wb_ref[...] = w_ref[...].astype(BF16)

    o_ref[...] = _dot_nn(x_ref[...], wb_ref[...]).astype(o_ref.dtype)


def _matmul_cols(x, w, n_cols, tn, out_dtype, name):
    m, k = x.shape
    return pl.pallas_call(
        _matmul_kernel,
        grid=(n_cols // tn, m // TOKEN_TILE),
        in_specs=[pl.BlockSpec((TOKEN_TILE, k), lambda j, i: (i, 0)),
                  pl.BlockSpec((k, tn), lambda j, i: (0, j))],
        out_specs=pl.BlockSpec((TOKEN_TILE, tn), lambda j, i: (i, j)),
        out_shape=jax.ShapeDtypeStruct((m, n_cols), out_dtype),
        scratch_shapes=[pltpu.VMEM((k, tn), BF16)],
        compiler_params=_params("arbitrary", "arbitrary"),
        name=name,
    )(x, w)


def _gate_kernel(x_ref, wgl_ref, wg_ref, bg_ref, g_ref):
    gl = _dot_nn(x_ref[...], wgl_ref[...])
    z = _dot_nn(gl.astype(BF16), wg_ref[...]) + bg_ref[...]
    log_sig = jnp.minimum(z, 0.0) - jnp.log1p(jnp.exp(-jnp.abs(z)))
    g_ref[...] = log_sig * (1.0 / GATE_NORMALIZER)


def _gla_gate(xb, w_gl, w_gate, b_gate):
    wgl = jnp.pad(w_gl, ((0, 0), (0, LANES - GATE_RANK))).astype(BF16)
    wg = jnp.pad(w_gate, ((0, LANES - GATE_RANK), (0, 0))).astype(BF16)
    return pl.pallas_call(
        _gate_kernel,
        grid=(N_TOK // TOKEN_TILE,),
        in_specs=[pl.BlockSpec((TOKEN_TILE, D_MODEL), lambda i: (i, 0)),
                  pl.BlockSpec((D_MODEL, LANES), lambda i: (0, 0)),
                  pl.BlockSpec((LANES, DK_TOTAL), lambda i: (0, 0)),
                  pl.BlockSpec((1, DK_TOTAL), lambda i: (0, 0))],
        out_specs=pl.BlockSpec((TOKEN_TILE, DK_TOTAL), lambda i: (i, 0)),
        out_shape=jax.ShapeDtypeStruct((N_TOK, DK_TOTAL), F32),
        compiler_params=_params("arbitrary"),
        name="gla_gate",
    )(xb, wgl, wg, b_gate.reshape(1, DK_TOTAL))


def _rms_gate(o, norm_g, r):
    o = o * lax.rsqrt(jnp.mean(o * o, axis=-1, keepdims=True) + RMS_EPS)
    return (o * norm_g) * _silu(r)


def _gla_prompt_kernel(q_ref, k_ref, v_ref, r_ref, g_ref, ng_ref, og_ref, s_out_ref, s_ref):
    c = pl.program_id(1)
    cc = q_ref.shape[0]

    @pl.when(c == 0)
    def _():
        s_ref[...] = jnp.zeros_like(s_ref)

    row = lax.broadcasted_iota(jnp.int32, (cc, cc), 0)
    col = lax.broadcasted_iota(jnp.int32, (cc, cc), 1)
    causal = col <= row
    tri = jnp.where(causal, 1.0, 0.0).astype(BF16)
    b = _dot_nn(jnp.concatenate([tri] * 3, axis=1),
                jnp.concatenate(_split3(g_ref[...]), axis=0))
    b_last = b[cc - 1:cc, :]
    b_mid = b[cc // 2 - 1:cc // 2, :]

    q = q_ref[...]
    k = k_ref[...] * (DK_HEAD ** -0.5)
    q_in = (q * jnp.exp(b)).astype(BF16)
    qh, qm, ql = _split3(q * jnp.exp(b - b_mid))
    kh, km, kl = _split3(k * jnp.exp(b_mid - b))
    k_d = (k * jnp.exp(b_last - b)).astype(BF16)

    rows = 2 * SUBLANES
    sub = lax.broadcasted_iota(jnp.int32, (rows, DK_TOTAL), 0)
    decay_rows = jnp.where(sub == 0, jnp.broadcast_to(jnp.exp(b_last), (rows, DK_TOTAL)), 0.0)
    decay_col = _dot_tn(jnp.concatenate(_split3(decay_rows), axis=0),
                        jnp.ones((3 * rows, LANES), BF16))

    for h in range(GLA_HEADS):
        dk = slice(h * DK_HEAD, (h + 1) * DK_HEAD)
        dv = slice(h * DV_HEAD, (h + 1) * DV_HEAD)
        q6 = jnp.concatenate([qh[:, dk], qh[:, dk], qm[:, dk], qm[:, dk], qh[:, dk], ql[:, dk]], axis=1)
        k6 = jnp.concatenate([kh[:, dk], km[:, dk], kh[:, dk], km[:, dk], kl[:, dk], kh[:, dk]], axis=1)
        scores = jnp.where(causal, _dot_nt(q6, k6), 0.0).astype(BF16)
        v = v_ref[:, dv].astype(BF16)
        s_old = s_ref[h]
        o = _dot_nn(jnp.concatenate([q_in[:, dk], scores], axis=1),
                    jnp.concatenate([s_old.astype(BF16), v], axis=0))
        decay = jnp.concatenate([decay_col[dk, :]] * (DV_HEAD // LANES), axis=1)
        s_new = decay * s_old + _dot_tn(k_d[:, dk], v)
        s_ref[h] = s_new

        @pl.when(c == pl.num_programs(1) - 1)
        def _():
            s_out_ref[0, h] = s_new

        og_ref[:, dv] = _rms_gate(o, ng_ref[:, dv], r_ref[:, dv]).astype(BF16)


def _gla_prompt(qkvr, g, norm_g):
    nc = SEQ // GLA_CHUNK
    row = lambda b, c: b * nc + c
    return pl.pallas_call(
        _gla_prompt_kernel,
        grid=(N_PROMPT_SEQ, nc),
        in_specs=[pl.BlockSpec((GLA_CHUNK, DK_TOTAL), lambda b, c: (row(b, c), 0)),
                  pl.BlockSpec((GLA_CHUNK, DK_TOTAL), lambda b, c: (row(b, c), 1)),
                  pl.BlockSpec((GLA_CHUNK, DV_TOTAL), lambda b, c: (row(b, c), 1)),
                  pl.BlockSpec((GLA_CHUNK, DV_TOTAL), lambda b, c: (row(b, c), 2)),
                  pl.BlockSpec((GLA_CHUNK, DK_TOTAL), lambda b, c: (row(b, c), 0)),
                  pl.BlockSpec((1, DV_TOTAL), lambda b, c: (0, 0))],
        out_specs=[pl.BlockSpec((GLA_CHUNK, DV_TOTAL), lambda b, c: (row(b, c), 0)),
                   pl.BlockSpec((1, GLA_HEADS, DK_HEAD, DV_HEAD), lambda b, c: (b, 0, 0, 0))],
        out_shape=[jax.ShapeDtypeStruct((N_PROMPT, DV_TOTAL), BF16),
                   jax.ShapeDtypeStruct((N_PROMPT_SEQ, GLA_HEADS, DK_HEAD, DV_HEAD), F32)],
        scratch_shapes=[pltpu.VMEM((GLA_HEADS, DK_HEAD, DV_HEAD), F32)],
        compiler_params=_params("arbitrary", "arbitrary"),
        name="gla_prompt",
    )(qkvr, qkvr, qkvr, qkvr, g, norm_g)


def _gla_sample_kernel(q_ref, k_ref, v_ref, r_ref, g_ref, ng_ref, s_ref, og_ref, s_out_ref, o_scr):
    bb = q_ref.shape[0]
    q = q_ref[...]
    k = k_ref[...] * (DK_HEAD ** -0.5)
    v = v_ref[...]
    eg = jnp.exp(g_ref[...])
    qe = q * eg
    qk = jnp.sum(q * k, axis=-1, keepdims=True)
    sub = lax.broadcasted_iota(jnp.int32, (bb, DK_HEAD), 0)
    for bi in range(bb):
        sel = sub == bi
        s_old = s_ref[bi, 0]
        decay = _col_bcast(jnp.where(sel, eg, 0.0), DV_HEAD)
        k_col = _col_bcast(jnp.where(sel, k, 0.0), DV_HEAD)
        qe_col = _col_bcast(jnp.where(sel, qe, 0.0), DV_HEAD)
        s_out_ref[bi, 0] = decay * s_old + k_col * v[bi:bi + 1, :]
        o_scr[bi:bi + 1, :] = jnp.sum(qe_col * s_old, axis=0, keepdims=True)
    o = qk * v + o_scr[...]
    og_ref[...] = _rms_gate(o, ng_ref[...], r_ref[...]).astype(BF16)


def _gla_sample(qkvr, g, norm_g, state):
    bb = GLA_SAMPLE_BATCH
    r0 = N_PROMPT // bb
    hk = DK_TOTAL // DK_HEAD
    hv = 2 * DK_TOTAL // DV_HEAD
    return pl.pallas_call(
        _gla_sample_kernel,
        grid=(N_SAMPLE // bb, GLA_HEADS),
        in_specs=[pl.BlockSpec((bb, DK_HEAD), lambda i, h: (r0 + i, h)),
                  pl.BlockSpec((bb, DK_HEAD), lambda i, h: (r0 + i, hk + h)),
                  pl.BlockSpec((bb, DV_HEAD), lambda i, h: (r0 + i, hv + h)),
                  pl.BlockSpec((bb, DV_HEAD), lambda i, h: (r0 + i, hv + GLA_HEADS + h)),
                  pl.BlockSpec((bb, DK_HEAD), lambda i, h: (r0 + i, h)),
                  pl.BlockSpec((1, DV_HEAD), lambda i, h: (0, h)),
                  pl.BlockSpec((bb, 1, DK_HEAD, DV_HEAD), lambda i, h: (i, h, 0, 0))],
        out_specs=[pl.BlockSpec((bb, DV_HEAD), lambda i, h: (i, h)),
                   pl.BlockSpec((bb, 1, DK_HEAD, DV_HEAD), lambda i, h: (i, h, 0, 0))],
        out_shape=[jax.ShapeDtypeStruct((N_SAMPLE, DV_TOTAL), BF16),
                   jax.ShapeDtypeStruct((N_SAMPLE, GLA_HEADS, DK_HEAD, DV_HEAD), F32)],
        scratch_shapes=[pltpu.VMEM((bb, DV_HEAD), F32)],
        compiler_params=_params("arbitrary", "arbitrary"),
        name="gla_sample",
    )(qkvr, qkvr, qkvr, qkvr, g, norm_g, state)


def _top2_of4(p):
    ranks = []
    for j in range(4):
        rk = jnp.zeros(p[j].shape, jnp.int32)
        for i in range(4):
            if i == j:
                continue
            beats = (p[i] >= p[j]) if i < j else (p[i] > p[j])
            rk = rk + jnp.where(beats, 1, 0)
        ranks.append(rk)

    def pick(rank):
        val = jnp.zeros(p[0].shape, F32)
        idx = jnp.zeros(p[0].shape, jnp.int32)
        for j in range(4):
            hit = ranks[j] == rank
            val = jnp.where(hit, p[j], val)
            idx = jnp.where(hit, j, idx)
        return val, idx

    v1, i1 = pick(0)
    v2, i2 = pick(1)
    return v1, i1, v2, i2


def _route(x, rw_ref, rb_ref):
    logits = _dot_nn(x.astype(BF16), rw_ref[...])
    n = x.shape[0]
    n_pad = -n % LANES
    if n_pad:
        logits = jnp.concatenate([logits, jnp.zeros((n_pad, LANES), F32)], axis=0)
    lt = logits.T[0:N_EXPERTS, 0:n] + rb_ref[...]
    e = jnp.exp(lt - jnp.max(lt, axis=0, keepdims=True))
    probs = e / jnp.sum(e, axis=0, keepdims=True)
    best = None
    for grp in range(N_GROUPS):
        rows = [probs[grp * EXPERTS_PER_GROUP + j:grp * EXPERTS_PER_GROUP + j + 1, :]
                for j in range(EXPERTS_PER_GROUP)]
        v1, i1, v2, i2 = _top2_of4(rows)
        score = v1 + v2
        cand = (score, v1, i1 + grp * EXPERTS_PER_GROUP, v2, i2 + grp * EXPERTS_PER_GROUP)
        if best is None:
            best = cand
        else:
            better = score > best[0]
            best = tuple(jnp.where(better, n, o) for n, o in zip(cand, best))
    _, v1, e1, v2, e2 = best
    denom = v1 + v2
    return (e1, e2), (v1 / denom, v2 / denom)


def _proj_norm_route_kernel(a_ref, x_ref, w_ref, lg_ref, lb_ref, rw_ref, rb_ref,
                            xo_ref, id_ref, gt_ref):
    y = _dot_nn(a_ref[...], w_ref[...])
    xn = _layer_norm(ALPHA * x_ref[...] + y, lg_ref[...], lb_ref[...])
    xo_ref[...] = xn
    (e1, e2), (g1, g2) = _route(xn, rw_ref, rb_ref)
    id_ref[0, 0:1, :] = e1
    id_ref[0, 1:2, :] = e2
    gt_ref[0, 0:1, :] = g1
    gt_ref[0, 1:2, :] = g2


def _router_operands(router_w, router_b):
    rw = jnp.pad(router_w, ((0, 0), (0, LANES - N_EXPERTS))).astype(BF16)
    return rw, router_b.reshape(N_EXPERTS, 1)


def _proj_norm_route(a, x, w_out, ln_g, ln_b, router):
    nt = N_TOK // ROW_TILE
    rw, rb = router
    const = lambda i: (0, 0)
    x1, ids, gates = pl.pallas_call(
        _proj_norm_route_kernel,
        grid=(nt,),
        in_specs=[pl.BlockSpec((ROW_TILE, D_MODEL), lambda i: (i, 0)),
                  pl.BlockSpec((ROW_TILE, D_MODEL), lambda i: (i, 0)),
                  pl.BlockSpec((D_MODEL, D_MODEL), const),
                  pl.BlockSpec((1, D_MODEL), const),
                  pl.BlockSpec((1, D_MODEL), const),
                  pl.BlockSpec((D_MODEL, LANES), const),
                  pl.BlockSpec((N_EXPERTS, 1), const)],
        out_specs=[pl.BlockSpec((ROW_TILE, D_MODEL), lambda i: (i, 0)),
                   pl.BlockSpec((1, TOP_K, ROW_TILE), lambda i: (i, 0, 0)),
                   pl.BlockSpec((1, TOP_K, ROW_TILE), lambda i: (i, 0, 0))],
        out_shape=[jax.ShapeDtypeStruct((N_TOK, D_MODEL), F32),
                   jax.ShapeDtypeStruct((nt, TOP_K, ROW_TILE), jnp.int32),
                   jax.ShapeDtypeStruct((nt, TOP_K, ROW_TILE), F32)],
        compiler_params=_params("arbitrary"),
        name="proj_norm_route",
    )(a, x, w_out.astype(BF16), ln_g.reshape(1, D_MODEL), ln_b.reshape(1, D_MODEL), rw, rb)
    ids = ids.transpose(0, 2, 1).reshape(N_TOK, TOP_K)
    gates = gates.transpose(0, 2, 1).reshape(N_TOK, TOP_K)
    return x1, ids, gates


def _gather_kernel(idx_ref, nt_ref, src_ref, out_ref, buf_ref, sem_ref):
    i = pl.program_id(0)
    n = nt_ref[0]
    tg = out_ref.shape[0]

    def start_tile(tile, slot):
        base = tile * tg

        def body(r, carry):
            row = idx_ref[base + r]
            pltpu.make_async_copy(src_ref.at[pl.ds(row, 1)], buf_ref.at[slot, pl.ds(r, 1)],
                                  sem_ref.at[slot]).start()
            return carry
        lax.fori_loop(0, tg, body, 0, unroll=8)

    @pl.when(i == 0)
    def _():
        start_tile(0, 0)

    @pl.when(i + 1 < n)
    def _():
        start_tile(i + 1, (i + 1) % 2)

    @pl.when(i < n)
    def _():
        slot = i % 2
        pltpu.make_async_copy(src_ref.at[pl.ds(0, tg)], buf_ref.at[slot], sem_ref.at[slot]).wait()
        out_ref[...] = buf_ref[slot].astype(out_ref.dtype)

    @pl.when(i >= n)
    def _():
        out_ref[...] = jnp.zeros_like(out_ref)


def _gather_rows(src, idx, n_tiles, out_dtype, name):
    m = idx.shape[0]
    width = src.shape[1]
    return pl.pallas_call(
        _gather_kernel,
        grid_spec=pltpu.PrefetchScalarGridSpec(
            num_scalar_prefetch=2,
            grid=(m // MOE_TILE,),
            in_specs=[pl.BlockSpec(memory_space=pl.ANY)],
            out_specs=pl.BlockSpec((MOE_TILE, width), lambda i, idx, nt: (i, 0)),
            scratch_shapes=[pltpu.VMEM((2, MOE_TILE, width), src.dtype),
                            pltpu.SemaphoreType.DMA((2,))]),
        out_shape=jax.ShapeDtypeStruct((m, width), out_dtype),
        compiler_params=_params("arbitrary"),
        name=name,
    )(idx, n_tiles, src)


def _stream_expert_weights(layer, tables, w_refs, wbuf_ref, wb_refs, sem_ref):
    te_ref, nx_ref, sl_ref, nt_ref = tables
    i = pl.program_id(0)

    def copies(expert, slot):
        return [pltpu.make_async_copy(w.at[layer, expert], wbuf_ref.at[slot, j], sem_ref.at[slot, j])
                for j, w in enumerate(w_refs)]

    @pl.when(i == 0)
    def _():
        for c in copies(te_ref[0], 0):
            c.start()

    first = jnp.logical_or(i == 0, te_ref[i] != te_ref[jnp.maximum(i - 1, 0)])

    @pl.when(jnp.logical_and(first, i < nt_ref[0]))
    def _():
        slot = sl_ref[i]
        for c in copies(te_ref[i], slot):
            c.wait()
        for j, wb in enumerate(wb_refs):
            wb[...] = wbuf_ref[slot, j].astype(BF16)

        @pl.when(nx_ref[i] >= 0)
        def _():
            for c in copies(nx_ref[i], 1 - slot):
                c.start()


def _moe_up_kernel(layer, te_ref, nx_ref, sl_ref, nt_ref, xs_ref, w1_ref, w3_ref, hid_ref,
                   wbuf_ref, w1b_ref, w3b_ref, sem_ref):
    _stream_expert_weights(layer, (te_ref, nx_ref, sl_ref, nt_ref), (w1_ref, w3_ref), wbuf_ref,
                           (w1b_ref, w3b_ref), sem_ref)
    i = pl.program_id(0)

    @pl.when(i < nt_ref[0])
    def _():
        xs = xs_ref[...]
        h1 = _dot_nn(xs, w1b_ref[...])
        h3 = _dot_nn(xs, w3b_ref[...])
        hid_ref[...] = (_silu(h1) * h3).astype(BF16)

    @pl.when(i >= nt_ref[0])
    def _():
        hid_ref[...] = jnp.zeros_like(hid_ref)


def _moe_up(xs, w1, w3, layer, tables):
    last = lambda i, te, nx, sl, nt: (jnp.minimum(i, nt[0] - 1), 0)
    return pl.pallas_call(
        functools.partial(_moe_up_kernel, layer),
        grid_spec=pltpu.PrefetchScalarGridSpec(
            num_scalar_prefetch=4,
            grid=(MOE_MAX_TILES,),
            in_specs=[pl.BlockSpec((MOE_TILE, D_MODEL), last),
                      pl.BlockSpec(memory_space=pl.ANY),
                      pl.BlockSpec(memory_space=pl.ANY)],
            out_specs=pl.BlockSpec((MOE_TILE, D_FF), lambda i, te, nx, sl, nt: (i, 0)),
            scratch_shapes=[pltpu.VMEM((2, 2, D_MODEL, D_FF), F32),
                            pltpu.VMEM((D_MODEL, D_FF), BF16), pltpu.VMEM((D_MODEL, D_FF), BF16),
                            pltpu.SemaphoreType.DMA((2, 2))]),
        out_shape=jax.ShapeDtypeStruct((N_ASSIGN_PAD, D_FF), BF16),
        compiler_params=_params("arbitrary"),
        name="moe_up",
    )(*tables, xs, w1, w3)


def _moe_down_kernel(layer, te_ref, nx_ref, sl_ref, nt_ref, hid_ref, w2_ref, ys_ref,
                     wbuf_ref, w2b_ref, sem_ref):
    _stream_expert_weights(layer, (te_ref, nx_ref, sl_ref, nt_ref), (w2_ref,), wbuf_ref,
                           (w2b_ref,), sem_ref)
    i = pl.program_id(0)

    @pl.when(i < nt_ref[0])
    def _():
        ys_ref[...] = _dot_nn(hid_ref[...], w2b_ref[...])

    @pl.when(i >= nt_ref[0])
    def _():
        ys_ref[...] = jnp.zeros_like(ys_ref)


def _moe_down(hid, w2, layer, tables):
    last = lambda i, te, nx, sl, nt: (jnp.minimum(i, nt[0] - 1), 0)
    return pl.pallas_call(
        functools.partial(_moe_down_kernel, layer),
        grid_spec=pltpu.PrefetchScalarGridSpec(
            num_scalar_prefetch=4,
            grid=(MOE_MAX_TILES,),
            in_specs=[pl.BlockSpec((MOE_TILE, D_FF), last),
                      pl.BlockSpec(memory_space=pl.ANY)],
            out_specs=pl.BlockSpec((MOE_TILE, D_MODEL), lambda i, te, nx, sl, nt: (i, 0)),
            scratch_shapes=[pltpu.VMEM((2, 1, D_FF, D_MODEL), F32),
                            pltpu.VMEM((D_FF, D_MODEL), BF16),
                            pltpu.SemaphoreType.DMA((2, 1))]),
        out_shape=jax.ShapeDtypeStruct((N_ASSIGN_PAD, D_MODEL), F32),
        compiler_params=_params("arbitrary"),
        name="moe_down",
    )(*tables, hid, w2)


def _combine_norm_kernel(pos_ref, x_ref, ys_ref, gt_ref, lg_ref, lb_ref, xo_ref, xb_ref, buf_ref, sem_ref):
    i = pl.program_id(0)
    n = pl.num_programs(0)
    tm = x_ref.shape[0]

    def start_tile(tile, slot):
        base = tile * tm * TOP_K

        def body(t, carry):
            for k in range(TOP_K):
                row = pos_ref[base + TOP_K * t + k]
                pltpu.make_async_copy(ys_ref.at[pl.ds(row, 1)], buf_ref.at[slot, k, pl.ds(t, 1)],
                                      sem_ref.at[slot]).start()
            return carry
        lax.fori_loop(0, tm, body, 0, unroll=4)

    @pl.when(i == 0)
    def _():
        start_tile(0, 0)

    @pl.when(i + 1 < n)
    def _():
        start_tile(i + 1, (i + 1) % 2)

    slot = i % 2
    for k in range(TOP_K):
        pltpu.make_async_copy(ys_ref.at[pl.ds(0, tm)], buf_ref.at[slot, k], sem_ref.at[slot]).wait()
    gt = gt_ref[...]
    moe = buf_ref[slot, 0] * gt[:, 0:1] + buf_ref[slot, 1] * gt[:, 1:2]
    xn = _layer_norm(ALPHA * x_ref[...] + moe, lg_ref[...], lb_ref[...])
    xo_ref[...] = xn
    xb_ref[...] = xn.astype(BF16)


def _combine_norm(x, ys, pos, gates, ln_g, ln_b):
    const = lambda i, pos: (0, 0)
    tile = lambda i, pos: (i, 0)
    return pl.pallas_call(
        _combine_norm_kernel,
        grid_spec=pltpu.PrefetchScalarGridSpec(
            num_scalar_prefetch=1,
            grid=(N_TOK // ROW_TILE,),
            in_specs=[pl.BlockSpec((ROW_TILE, D_MODEL), tile),
                      pl.BlockSpec(memory_space=pl.ANY),
                      pl.BlockSpec((ROW_TILE, TOP_K), tile),
                      pl.BlockSpec((1, D_MODEL), const),
                      pl.BlockSpec((1, D_MODEL), const)],
            out_specs=[pl.BlockSpec((ROW_TILE, D_MODEL), tile),
                       pl.BlockSpec((ROW_TILE, D_MODEL), tile)],
            scratch_shapes=[pltpu.VMEM((2, TOP_K, ROW_TILE, D_MODEL), F32),
                            pltpu.SemaphoreType.DMA((2,))]),
        out_shape=[jax.ShapeDtypeStruct((N_TOK, D_MODEL), F32),
                   jax.ShapeDtypeStruct((N_TOK, D_MODEL), BF16)],
        compiler_params=_params("arbitrary"),
        name="combine_norm",
    )(pos, x, ys, gates, ln_g.reshape(1, D_MODEL), ln_b.reshape(1, D_MODEL))


def _routing_tables(ids):
    e = ids.reshape(-1)
    onehot = (e[:, None] == jnp.arange(N_EXPERTS, dtype=jnp.int32)[None, :]).astype(jnp.int32)
    csum = jnp.cumsum(onehot, axis=0)
    rank = jnp.sum(onehot * csum, axis=1) - 1
    sizes = csum[-1]
    tiles_per = (sizes + MOE_TILE - 1) // MOE_TILE
    tile_end = jnp.cumsum(tiles_per)
    tile_start = tile_end - tiles_per
    n_tiles = tile_end[-1]
    pos = (jnp.sum(onehot * (tile_start * MOE_TILE)[None, :], axis=1) + rank).astype(jnp.int32)
    tile = jnp.arange(MOE_MAX_TILES, dtype=jnp.int32)
    owner = jnp.sum((tile_end[None, :] <= jnp.minimum(tile, n_tiles - 1)[:, None]).astype(jnp.int32), axis=1)
    order = jnp.argsort(e, stable=True).astype(jnp.int32)
    owner_hot = (owner[:, None] == jnp.arange(N_EXPERTS, dtype=jnp.int32)[None, :]).astype(jnp.int32)
    size_start = jnp.cumsum(sizes) - sizes
    tile_first = jnp.sum(owner_hot * (size_start - tile_start * MOE_TILE)[None, :], axis=1)
    tile_limit = jnp.sum(owner_hot * (size_start + sizes)[None, :], axis=1)
    row = jnp.arange(N_ASSIGN_PAD, dtype=jnp.int32).reshape(MOE_MAX_TILES, MOE_TILE)
    src = tile_first[:, None] + row
    valid = src < tile_limit[:, None]
    row_token = jnp.where(valid, order[jnp.clip(src, 0, N_ASSIGN - 1)] // TOP_K, row % N_TOK).reshape(-1)
    experts = jnp.arange(N_EXPERTS, dtype=jnp.int32)
    present = tiles_per > 0
    later = lax.cummin(jnp.where(present, experts, N_EXPERTS), reverse=True)
    next_expert = jnp.concatenate([later[1:], jnp.full((1,), N_EXPERTS, jnp.int32)])
    next_expert = jnp.where(next_expert >= N_EXPERTS, -1, next_expert)
    slot = (jnp.cumsum(present.astype(jnp.int32)) - 1) % 2
    tile_next = jnp.sum(owner_hot * next_expert[None, :], axis=1)
    tile_slot = jnp.sum(owner_hot * slot[None, :], axis=1)
    n_tiles = n_tiles.reshape(1).astype(jnp.int32)
    tables = (owner.astype(jnp.int32), tile_next.astype(jnp.int32), tile_slot.astype(jnp.int32), n_tiles)
    return pos, row_token.astype(jnp.int32), tables


def _moe_block(x, ids, gates, w1, w3, w2, layer, ln_g, ln_b):
    pos, row_token, tables = _routing_tables(ids)
    xs = _gather_rows(x, row_token, tables[3], BF16, "moe_gather")
    hid = _moe_up(xs, w1, w3, layer, tables)
    ys = _moe_down(hid, w2, layer, tables)
    return _combine_norm(x, ys, pos, gates, ln_g, ln_b)


def _conv_in_kernel(x_ref, wb_ref, wc_ref, wh_ref, cw_ref, s0_ref, s1_ref, gbz_ref, u_ref,
                    wbb_ref, wcb_ref, whb_ref, ubuf_ref):
    i = pl.program_id(1)
    tm = x_ref.shape[0]
    pad = SUBLANES

    @pl.when(i == 0)
    def _():
        wbb_ref[...] = wb_ref[...].astype(BF16)
        wcb_ref[...] = wc_ref[...].astype(BF16)
        whb_ref[...] = wh_ref[...].astype(BF16)
        ubuf_ref[0:pad, :] = jnp.zeros((pad, ubuf_ref.shape[1]), F32)

    x = x_ref[...]
    gb = _dot_nn(x, wbb_ref[...])
    u = _dot_nn(x, wcb_ref[...]) * _dot_nn(x, whb_ref[...])
    u_ref[...] = u
    ubuf_ref[pad:pad + tm, :] = u
    t = (i * tm + lax.broadcasted_iota(jnp.int32, (tm, 1), 0)) & (SEQ - 1)
    u_m1 = jnp.where(t >= 1, ubuf_ref[pad - 1:pad - 1 + tm, :], 0.0)
    u_m2 = jnp.where(t >= 2, ubuf_ref[pad - 2:pad - 2 + tm, :], 0.0)
    w0 = cw_ref[0:1, :]
    w1 = cw_ref[1:2, :]
    w2 = cw_ref[2:3, :]
    z = u_m2 * w0 + u_m1 * w1 + u * w2
    gbz_ref[...] = (gb * z).astype(BF16)
    ubuf_ref[0:pad, :] = ubuf_ref[tm:tm + pad, :]

    @pl.when(i == pl.num_programs(1) - 1)
    def _():
        lo = tm - N_SAMPLE
        z_s = s0_ref[...] * w0 + s1_ref[...] * w1 + u[lo:tm, :] * w2
        gbz_ref[lo:tm, :] = (gb[lo:tm, :] * z_s).astype(BF16)


def _conv_in(xb, w_in, conv_w, state):
    tn = 512
    nb = D_MODEL // tn
    s0 = state[:, 0, :]
    s1 = state[:, 1, :]
    return pl.pallas_call(
        _conv_in_kernel,
        grid=(nb, N_TOK // TOKEN_TILE),
        in_specs=[pl.BlockSpec((TOKEN_TILE, D_MODEL), lambda j, i: (i, 0)),
                  pl.BlockSpec((D_MODEL, tn), lambda j, i: (0, j)),
                  pl.BlockSpec((D_MODEL, tn), lambda j, i: (0, nb + j)),
                  pl.BlockSpec((D_MODEL, tn), lambda j, i: (0, 2 * nb + j)),
                  pl.BlockSpec((CONV_W, tn), lambda j, i: (0, j)),
                  pl.BlockSpec((N_SAMPLE, tn), lambda j, i: (0, j)),
                  pl.BlockSpec((N_SAMPLE, tn), lambda j, i: (0, j))],
        out_specs=[pl.BlockSpec((TOKEN_TILE, tn), lambda j, i: (i, j)),
                   pl.BlockSpec((TOKEN_TILE, tn), lambda j, i: (i, j))],
        out_shape=[jax.ShapeDtypeStruct((N_TOK, D_MODEL), BF16),
                   jax.ShapeDtypeStruct((N_TOK, D_MODEL), F32)],
        scratch_shapes=[pltpu.VMEM((D_MODEL, tn), BF16), pltpu.VMEM((D_MODEL, tn), BF16),
                        pltpu.VMEM((D_MODEL, tn), BF16),
                        pltpu.VMEM((TOKEN_TILE + 2 * SUBLANES, tn), F32)],
        compiler_params=_params("arbitrary", "arbitrary"),
        name="conv_in",
    )(xb, w_in, w_in, w_in, conv_w, s0, s1)


def kernel(x_prompt, x_sample, state_gla, state_conv, router_w, router_b, gla_w_in, gla_w_gate, gla_b_gate,
           gla_norm_g, gla_w_out, conv_w_in, conv_w, conv_w_out, ln_mix_g, ln_mix_b, ln_ffn_g, ln_ffn_b,
           moe_w1, moe_w3, moe_w2):
    x0 = jnp.concatenate([x_prompt.reshape(N_PROMPT, D_MODEL), x_sample.reshape(N_SAMPLE, D_MODEL)], axis=0)
    x0b = x0.astype(BF16)
    router = _router_operands(router_w, router_b)

    qkvr = _matmul_cols(x0b, gla_w_in[0], GLA_QKVR_WIDTH, 1024, F32, "gla_in_proj")
    g = _gla_gate(x0b, gla_w_in[0][:, GLA_QKVR_WIDTH:], gla_w_gate[0], gla_b_gate[0])
    norm_g = gla_norm_g[0].reshape(1, DV_TOTAL)
    og_p, s_prompt = _gla_prompt(qkvr, g, norm_g)
    og_s, s_sample = _gla_sample(qkvr, g, norm_g, state_gla[0])
    og = jnp.concatenate([og_p, og_s], axis=0)
    x1, ids, gates = _proj_norm_route(og, x0, gla_w_out[0], ln_mix_g[0], ln_mix_b[0], router)
    x2, x2b = _moe_block(x1, ids, gates, moe_w1, moe_w3, moe_w2, 0, ln_ffn_g[0], ln_ffn_b[0])

    gbz, u = _conv_in(x2b, conv_w_in[0], conv_w[0], state_conv[0])
    x3, ids, gates = _proj_norm_route(gbz, x2, conv_w_out[0], ln_mix_g[1], ln_mix_b[1], router)
    x4, _ = _moe_block(x3, ids, gates, moe_w1, moe_w3, moe_w2, 1, ln_ffn_g[1], ln_ffn_b[1])

    y_prompt = x4[:N_PROMPT].reshape(N_PROMPT_SEQ, SEQ, D_MODEL)
    y_sample = x4[N_PROMPT:].reshape(N_SAMPLE, 1, D_MODEL)
    conv_prompt = jnp.stack([u[(b + 1) * SEQ - (CONV_W - 1):(b + 1) * SEQ] for b in range(N_PROMPT_SEQ)])
    conv_sample = jnp.concatenate([state_conv[0][:, 1:, :], u[N_PROMPT:][:, None, :]], axis=1)
    return (y_prompt, y_sample, s_prompt[None], conv_prompt[None], s_sample[None], conv_sample[None])
```

```python
import functools

import jax
import jax.numpy as jnp
from jax import lax
from jax.experimental import pallas as pl
from jax.experimental.pallas import tpu as pltpu

F32 = jnp.float32
BF16 = jnp.bfloat16

D_MODEL = 2048
N_PROMPT_SEQ = 4
SEQ = 2048
N_PROMPT = N_PROMPT_SEQ * SEQ
N_SAMPLE = 128
N_TOK = N_PROMPT + N_SAMPLE
DEPTH = 2

GLA_HEADS = 4
DK_TOTAL = D_MODEL // 2
DV_TOTAL = D_MODEL
DK_HEAD = DK_TOTAL // GLA_HEADS
DV_HEAD = DV_TOTAL // GLA_HEADS
GATE_RANK = 16
GATE_NORMALIZER = 16.0
GLA_QKVR_WIDTH = 2 * DK_TOTAL + 2 * DV_TOTAL
CONV_W = 3
N_EXPERTS = 16
N_GROUPS = 4
EXPERTS_PER_GROUP = N_EXPERTS // N_GROUPS
TOP_K = 2
D_FF = D_MODEL // 2
ALPHA = (2.0 * DEPTH) ** 0.25
LN_EPS = 1e-5
RMS_EPS = 1e-6

LANES = 128
SUBLANES = 8
VMEM_LIMIT_BYTES = 56 * 1024 * 1024

TOKEN_TILE = 640
ROW_TILE = 320
GLA_CHUNK = 64
GLA_SAMPLE_BATCH = 16
MOE_TILE = 256
MOE_COL_CHUNK = 256
N_ASSIGN = N_TOK * TOP_K
MOE_MAX_TILES = N_ASSIGN // MOE_TILE + N_EXPERTS
N_ASSIGN_PAD = MOE_MAX_TILES * MOE_TILE


def _params(*semantics):
    return pltpu.CompilerParams(dimension_semantics=semantics, vmem_limit_bytes=VMEM_LIMIT_BYTES)


def _split3(x):
    hi = x.astype(BF16)
    r1 = x - hi.astype(F32)
    mid = r1.astype(BF16)
    lo = (r1 - mid.astype(F32)).astype(BF16)
    return hi, mid, lo


def _dot_nn(a, b):
    return jnp.dot(a, b, preferred_element_type=F32)


def _dot_tn(a, b):
    return lax.dot_general(a, b, (((0,), (0,)), ((), ())), preferred_element_type=F32)


def _dot_nt(a, b):
    return lax.dot_general(a, b, (((1,), (1,)), ((), ())), preferred_element_type=F32)


def _col_bcast(rows, n):
    ones = jnp.ones((rows.shape[0], LANES), BF16)
    hi, mid, lo = _split3(rows)
    col = _dot_tn(hi, ones) + _dot_tn(mid, ones) + _dot_tn(lo, ones)
    return jnp.concatenate([col] * (n // LANES), axis=1)


def _silu(r):
    return r / (1.0 + jnp.exp(-r))


def _layer_norm(h, g, b):
    mu = jnp.mean(h, axis=-1, keepdims=True)
    d = h - mu
    var = jnp.mean(d * d, axis=-1, keepdims=True)
    return d * lax.rsqrt(var + LN_EPS) * g + b


def _matmul_kernel(x_ref, w_ref, o_ref, wb_ref):
    @pl.when(pl.program_id(1) == 0)
    def _():
        wb_ref[...] = w_ref[...].astype(BF16)

    o_ref[...] = _dot_nn(x_ref[...], wb_ref[...]).astype(o_ref.dtype)


def _matmul_cols(x, w, n_cols, tn, out_dtype, name):
    m, k = x.shape
    return pl.pallas_call(
        _matmul_kernel,
        grid=(n_cols // tn, m // TOKEN_TILE),
        in_specs=[pl.BlockSpec((TOKEN_TILE, k), lambda j, i: (i, 0)),
                  pl.BlockSpec((k, tn), lambda j, i: (0, j))],
        out_specs=pl.BlockSpec((TOKEN_TILE, tn), lambda j, i: (i, j)),
        out_shape=jax.ShapeDtypeStruct((m, n_cols), out_dtype),
        scratch_shapes=[pltpu.VMEM((k, tn), BF16)],
        compiler_params=_params("arbitrary", "arbitrary"),
        name=name,
    )(x, w)


def _gate_kernel(x_ref, wgl_ref, wg_ref, bg_ref, g_ref):
    gl = _dot_nn(x_ref[...], wgl_ref[...])
    z = _dot_nn(gl.astype(BF16), wg_ref[...]) + bg_ref[...]
    log_sig = jnp.minimum(z, 0.0) - jnp.log1p(jnp.exp(-jnp.abs(z)))
    g_ref[...] = log_sig * (1.0 / GATE_NORMALIZER)


def _gla_gate(xb, w_gl, w_gate, b_gate):
    wgl = jnp.pad(w_gl, ((0, 0), (0, LANES - GATE_RANK))).astype(BF16)
    wg = jnp.pad(w_gate, ((0, LANES - GATE_RANK), (0, 0))).astype(BF16)
    return pl.pallas_call(
        _gate_kernel,
        grid=(N_TOK // TOKEN_TILE,),
        in_specs=[pl.BlockSpec((TOKEN_TILE, D_MODEL), lambda i: (i, 0)),
                  pl.BlockSpec((D_MODEL, LANES), lambda i: (0, 0)),
                  pl.BlockSpec((LANES, DK_TOTAL), lambda i: (0, 0)),
                  pl.BlockSpec((1, DK_TOTAL), lambda i: (0, 0))],
        out_specs=pl.BlockSpec((TOKEN_TILE, DK_TOTAL), lambda i: (i, 0)),
        out_shape=jax.ShapeDtypeStruct((N_TOK, DK_TOTAL), F32),
        compiler_params=_params("arbitrary"),
        name="gla_gate",
    )(xb, wgl, wg, b_gate.reshape(1, DK_TOTAL))


def _rms_gate(o, norm_g, r):
    o = o * lax.rsqrt(jnp.mean(o * o, axis=-1, keepdims=True) + RMS_EPS)
    return (o * norm_g) * _silu(r)


def _gla_prompt_kernel(q_ref, k_ref, v_ref, r_ref, g_ref, ng_ref, og_ref, s_out_ref, s_ref):
    c = pl.program_id(1)
    cc = q_ref.shape[0]

    @pl.when(c == 0)
    def _():
        s_ref[...] = jnp.zeros_like(s_ref)

    row = lax.broadcasted_iota(jnp.int32, (cc, cc), 0)
    col = lax.broadcasted_iota(jnp.int32, (cc, cc), 1)
    causal = col <= row
    tri = jnp.where(causal, 1.0, 0.0).astype(BF16)
    b = _dot_nn(jnp.concatenate([tri] * 3, axis=1),
                jnp.concatenate(_split3(g_ref[...]), axis=0))
    b_last = b[cc - 1:cc, :]
    b_mid = b[cc // 2 - 1:cc // 2, :]

    q = q_ref[...]
    k = k_ref[...] * (DK_HEAD ** -0.5)
    q_in = (q * jnp.exp(b)).astype(BF16)
    qh, qm, ql = _split3(q * jnp.exp(b - b_mid))
    kh, km, kl = _split3(k * jnp.exp(b_mid - b))
    k_d = (k * jnp.exp(b_last - b)).astype(BF16)

    rows = 2 * SUBLANES
    sub = lax.broadcasted_iota(jnp.int32, (rows, DK_TOTAL), 0)
    decay_rows = jnp.where(sub == 0, jnp.broadcast_to(jnp.exp(b_last), (rows, DK_TOTAL)), 0.0)
    decay_col = _dot_tn(jnp.concatenate(_split3(decay_rows), axis=0),
                        jnp.ones((3 * rows, LANES), BF16))

    for h in range(GLA_HEADS):
        dk = slice(h * DK_HEAD, (h + 1) * DK_HEAD)
        dv = slice(h * DV_HEAD, (h + 1) * DV_HEAD)
        q6 = jnp.concatenate([qh[:, dk], qh[:, dk], qm[:, dk], qm[:, dk], qh[:, dk], ql[:, dk]], axis=1)
        k6 = jnp.concatenate([kh[:, dk], km[:, dk], kh[:, dk], km[:, dk], kl[:, dk], kh[:, dk]], axis=1)
        scores = jnp.where(causal, _dot_nt(q6, k6), 0.0).astype(BF16)
        v = v_ref[:, dv].astype(BF16)
        s_old = s_ref[h]
        o = _dot_nn(jnp.concatenate([q_in[:, dk], scores], axis=1),
                    jnp.concatenate([s_old.astype(BF16), v], axis=0))
        decay = jnp.concatenate([decay_col[dk, :]] * (DV_HEAD // LANES), axis=1)
        s_new = decay * s_old + _dot_tn(k_d[:, dk], v)
        s_ref[h] = s_new

        @pl.when(c == pl.num_programs(1) - 1)
        def _():
            s_out_ref[0, h] = s_new

        og_ref[:, dv] = _rms_gate(o, ng_ref[:, dv], r_ref[:, dv]).astype(BF16)


def _gla_prompt(qkvr, g, norm_g):
    nc = SEQ // GLA_CHUNK
    row = lambda b, c: b * nc + c
    return pl.pallas_call(
        _gla_prompt_kernel,
        grid=(N_PROMPT_SEQ, nc),
        in_specs=[pl.BlockSpec((GLA_CHUNK, DK_TOTAL), lambda b, c: (row(b, c), 0)),
                  pl.BlockSpec((GLA_CHUNK, DK_TOTAL), lambda b, c: (row(b, c), 1)),
                  pl.BlockSpec((GLA_CHUNK, DV_TOTAL), lambda b, c: (row(b, c), 1)),
                  pl.BlockSpec((GLA_CHUNK, DV_TOTAL), lambda b, c: (row(b, c), 2)),
                  pl.BlockSpec((GLA_CHUNK, DK_TOTAL), lambda b, c: (row(b, c), 0)),
                  pl.BlockSpec((1, DV_TOTAL), lambda b, c: (0, 0))],
        out_specs=[pl.BlockSpec((GLA_CHUNK, DV_TOTAL), lambda b, c: (row(b, c), 0)),
                   pl.BlockSpec((1, GLA_HEADS, DK_HEAD, DV_HEAD), lambda b, c: (b, 0, 0, 0))],
        out_shape=[jax.ShapeDtypeStruct((N_PROMPT, DV_TOTAL), BF16),
                   jax.ShapeDtypeStruct((N_PROMPT_SEQ, GLA_HEADS, DK_HEAD, DV_HEAD), F32)],
        scratch_shapes=[pltpu.VMEM((GLA_HEADS, DK_HEAD, DV_HEAD), F32)],
        compiler_params=_params("arbitrary", "arbitrary"),
        name="gla_prompt",
    )(qkvr, qkvr, qkvr, qkvr, g, norm_g)


def _gla_sample_kernel(q_ref, k_ref, v_ref, r_ref, g_ref, ng_ref, s_ref, og_ref, s_out_ref, o_scr):
    bb = q_ref.shape[0]
    q = q_ref[...]
    k = k_ref[...] * (DK_HEAD ** -0.5)
    v = v_ref[...]
    eg = jnp.exp(g_ref[...])
    qe = q * eg
    qk = jnp.sum(q * k, axis=-1, keepdims=True)
    sub = lax.broadcasted_iota(jnp.int32, (bb, DK_HEAD), 0)
    for bi in range(bb):
        sel = sub == bi
        s_old = s_ref[bi, 0]
        decay = _col_bcast(jnp.where(sel, eg, 0.0), DV_HEAD)
        k_col = _col_bcast(jnp.where(sel, k, 0.0), DV_HEAD)
        qe_col = _col_bcast(jnp.where(sel, qe, 0.0), DV_HEAD)
        s_out_ref[bi, 0] = decay * s_old + k_col * v[bi:bi + 1, :]
        o_scr[bi:bi + 1, :] = jnp.sum(qe_col * s_old, axis=0, keepdims=True)
    o = qk * v + o_scr[...]
    og_ref[...] = _rms_gate(o, ng_ref[...], r_ref[...]).astype(BF16)


def _gla_sample(qkvr, g, norm_g, state):
    bb = GLA_SAMPLE_BATCH
    r0 = N_PROMPT // bb
    hk = DK_TOTAL // DK_HEAD
    hv = 2 * DK_TOTAL // DV_HEAD
    return pl.pallas_call(
        _gla_sample_kernel,
        grid=(N_SAMPLE // bb, GLA_HEADS),
        in_specs=[pl.BlockSpec((bb, DK_HEAD), lambda i, h: (r0 + i, h)),
                  pl.BlockSpec((bb, DK_HEAD), lambda i, h: (r0 + i, hk + h)),
                  pl.BlockSpec((bb, DV_HEAD), lambda i, h: (r0 + i, hv + h)),
                  pl.BlockSpec((bb, DV_HEAD), lambda i, h: (r0 + i, hv + GLA_HEADS + h)),
                  pl.BlockSpec((bb, DK_HEAD), lambda i, h: (r0 + i, h)),
                  pl.BlockSpec((1, DV_HEAD), lambda i, h: (0, h)),
                  pl.BlockSpec((bb, 1, DK_HEAD, DV_HEAD), lambda i, h: (i, h, 0, 0))],
        out_specs=[pl.BlockSpec((bb, DV_HEAD), lambda i, h: (i, h)),
                   pl.BlockSpec((bb, 1, DK_HEAD, DV_HEAD), lambda i, h: (i, h, 0, 0))],
        out_shape=[jax.ShapeDtypeStruct((N_SAMPLE, DV_TOTAL), BF16),
                   jax.ShapeDtypeStruct((N_SAMPLE, GLA_HEADS, DK_HEAD, DV_HEAD), F32)],
        scratch_shapes=[pltpu.VMEM((bb, DV_HEAD), F32)],
        compiler_params=_params("arbitrary", "arbitrary"),
        name="gla_sample",
    )(qkvr, qkvr, qkvr, qkvr, g, norm_g, state)


def _top2_of4(p):
    ranks = []
    for j in range(4):
        rk = jnp.zeros(p[j].shape, jnp.int32)
        for i in range(4):
            if i == j:
                continue
            beats = (p[i] >= p[j]) if i < j else (p[i] > p[j])
            rk = rk + jnp.where(beats, 1, 0)
        ranks.append(rk)

    def pick(rank):
        val = jnp.zeros(p[0].shape, F32)
        idx = jnp.zeros(p[0].shape, jnp.int32)
        for j in range(4):
            hit = ranks[j] == rank
            val = jnp.where(hit, p[j], val)
            idx = jnp.where(hit, j, idx)
        return val, idx

    v1, i1 = pick(0)
    v2, i2 = pick(1)
    return v1, i1, v2, i2


def _route(x, rw_ref, rb_ref):
    logits = _dot_nn(x.astype(BF16), rw_ref[...])
    n = x.shape[0]
    n_pad = -n % LANES
    if n_pad:
        logits = jnp.concatenate([logits, jnp.zeros((n_pad, LANES), F32)], axis=0)
    lt = logits.T[0:N_EXPERTS, 0:n] + rb_ref[...]
    e = jnp.exp(lt - jnp.max(lt, axis=0, keepdims=True))
    probs = e / jnp.sum(e, axis=0, keepdims=True)
    best = None
    for grp in range(N_GROUPS):
        rows = [probs[grp * EXPERTS_PER_GROUP + j:grp * EXPERTS_PER_GROUP + j + 1, :]
                for j in range(EXPERTS_PER_GROUP)]
        v1, i1, v2, i2 = _top2_of4(rows)
        score = v1 + v2
        cand = (score, v1, i1 + grp * EXPERTS_PER_GROUP, v2, i2 + grp * EXPERTS_PER_GROUP)
        if best is None:
            best = cand
        else:
            better = score > best[0]
            best = tuple(jnp.where(better, n, o) for n, o in zip(cand, best))
    _, v1, e1, v2, e2 = best
    denom = v1 + v2
    return (e1, e2), (v1 / denom, v2 / denom)


def _proj_norm_route_kernel(a_ref, x_ref, w_ref, lg_ref, lb_ref, rw_ref, rb_ref,
                            xo_ref, id_ref, gt_ref):
    y = _dot_nn(a_ref[...], w_ref[...])
    xn = _layer_norm(ALPHA * x_ref[...] + y, lg_ref[...], lb_ref[...])
    xo_ref[...] = xn
    (e1, e2), (g1, g2) = _route(xn, rw_ref, rb_ref)
    id_ref[0, 0:1, :] = e1
    id_ref[0, 1:2, :] = e2
    gt_ref[0, 0:1, :] = g1
    gt_ref[0, 1:2, :] = g2


def _router_operands(router_w, router_b):
    rw = jnp.pad(router_w, ((0, 0), (0, LANES - N_EXPERTS))).astype(BF16)
    return rw, router_b.reshape(N_EXPERTS, 1)


def _proj_norm_route(a, x, w_out, ln_g, ln_b, router):
    nt = N_TOK // ROW_TILE
    rw, rb = router
    const = lambda i: (0, 0)
    x1, ids, gates = pl.pallas_call(
        _proj_norm_route_kernel,
        grid=(nt,),
        in_specs=[pl.BlockSpec((ROW_TILE, D_MODEL), lambda i: (i, 0)),
                  pl.BlockSpec((ROW_TILE, D_MODEL), lambda i: (i, 0)),
                  pl.BlockSpec((D_MODEL, D_MODEL), const),
                  pl.BlockSpec((1, D_MODEL), const),
                  pl.BlockSpec((1, D_MODEL), const),
                  pl.BlockSpec((D_MODEL, LANES), const),
                  pl.BlockSpec((N_EXPERTS, 1), const)],
        out_specs=[pl.BlockSpec((ROW_TILE, D_MODEL), lambda i: (i, 0)),
                   pl.BlockSpec((1, TOP_K, ROW_TILE), lambda i: (i, 0, 0)),
                   pl.BlockSpec((1, TOP_K, ROW_TILE), lambda i: (i, 0, 0))],
        out_shape=[jax.ShapeDtypeStruct((N_TOK, D_MODEL), F32),
                   jax.ShapeDtypeStruct((nt, TOP_K, ROW_TILE), jnp.int32),
                   jax.ShapeDtypeStruct((nt, TOP_K, ROW_TILE), F32)],
        compiler_params=_params("arbitrary"),
        name="proj_norm_route",
    )(a, x, w_out.astype(BF16), ln_g.reshape(1, D_MODEL), ln_b.reshape(1, D_MODEL), rw, rb)
    ids = ids.transpose(0, 2, 1).reshape(N_TOK, TOP_K)
    gates = gates.transpose(0, 2, 1).reshape(N_TOK, TOP_K)
    return x1, ids, gates


def _stream_expert_weights(layer, tables, w_refs, wbuf_ref, wb_refs, sem_ref):
    te_ref, nx_ref, nt_ref = tables
    i = pl.program_id(0)

    def copies(expert):
        return [pltpu.make_async_copy(w.at[layer, expert], wbuf_ref.at[j], sem_ref.at[j])
                for j, w in enumerate(w_refs)]

    @pl.when(i == 0)
    def _():
        for c in copies(te_ref[0]):
            c.start()

    first = jnp.logical_or(i == 0, te_ref[i] != te_ref[jnp.maximum(i - 1, 0)])

    @pl.when(jnp.logical_and(first, i < nt_ref[0]))
    def _():
        for c in copies(te_ref[i]):
            c.wait()
        for j, wb in enumerate(wb_refs):
            if len(wb.shape) == 2:
                wb[...] = wbuf_ref[j].astype(BF16)
            else:
                width = wb.shape[2]
                for c in range(wb.shape[0]):
                    wb[c] = wbuf_ref[j, :, c * width:(c + 1) * width].astype(BF16)

        @pl.when(nx_ref[i] >= 0)
        def _():
            for c in copies(nx_ref[i]):
                c.start()


def _moe_up_kernel(layer, rt_ref, te_ref, nx_ref, nt_ref, x_ref, w1_ref, w3_ref, hid_ref,
                   wbuf_ref, w1b_ref, w3b_ref, sem_ref, xbuf_ref, xb_ref, hbuf_ref, xsem_ref):
    _stream_expert_weights(layer, (te_ref, nx_ref, nt_ref), (w1_ref, w3_ref), wbuf_ref,
                           (w1b_ref, w3b_ref), sem_ref)
    i = pl.program_id(0)
    last = pl.num_programs(0) - 1
    tm = xb_ref.shape[0]
    n_chunks = w1b_ref.shape[0]
    rows_per_chunk = tm // n_chunks
    next_tile = jnp.minimum(i + 1, last)
    next_slot = (i + 1) % 2

    def issue_rows(tile, slot, r0, n):
        for r in range(n):
            row = rt_ref[tile * tm + r0 + r]
            pltpu.make_async_copy(x_ref.at[pl.ds(row, 1)], xbuf_ref.at[slot, pl.ds(r0 + r, 1)],
                                  xsem_ref.at[slot]).start()

    def issue_tile(tile, slot):
        def body(c, carry):
            issue_rows(tile, slot, c * SUBLANES, SUBLANES)
            return carry
        lax.fori_loop(0, tm // SUBLANES, body, 0)

    def wait_tile(slot):
        pltpu.make_async_copy(x_ref.at[pl.ds(0, tm)], xbuf_ref.at[slot], xsem_ref.at[slot]).wait()

    @pl.when(i == 0)
    def _():
        issue_tile(0, 0)

    wait_tile(i % 2)

    @pl.when(i < nt_ref[0])
    def _():
        xb_ref[...] = xbuf_ref[i % 2].astype(BF16)

        def body(c, carry):
            issue_rows(next_tile, next_slot, c * rows_per_chunk, rows_per_chunk)
            xs = xb_ref[...]
            h1 = _dot_nn(xs, w1b_ref[c])
            h3 = _dot_nn(xs, w3b_ref[c])
            hbuf_ref[c] = (_silu(h1) * h3).astype(BF16)
            return carry
        lax.fori_loop(0, n_chunks, body, 0)
        width = hbuf_ref.shape[2]
        for c in range(n_chunks):
            hid_ref[:, c * width:(c + 1) * width] = hbuf_ref[c]

    @pl.when(i >= nt_ref[0])
    def _():
        issue_tile(next_tile, next_slot)
        hid_ref[...] = jnp.zeros_like(hid_ref)

    @pl.when(i == last)
    def _():
        wait_tile(next_slot)


def _moe_up(x, row_token, w1, w3, layer, tables):
    n_chunks = D_FF // MOE_COL_CHUNK
    return pl.pallas_call(
        functools.partial(_moe_up_kernel, layer),
        grid_spec=pltpu.PrefetchScalarGridSpec(
            num_scalar_prefetch=4,
            grid=(MOE_MAX_TILES,),
            in_specs=[pl.BlockSpec(memory_space=pl.ANY),
                      pl.BlockSpec(memory_space=pl.ANY),
                      pl.BlockSpec(memory_space=pl.ANY)],
            out_specs=pl.BlockSpec((MOE_TILE, D_FF), lambda i, rt, te, nx, nt: (i, 0)),
            scratch_shapes=[pltpu.VMEM((2, D_MODEL, D_FF), F32),
                            pltpu.VMEM((n_chunks, D_MODEL, MOE_COL_CHUNK), BF16),
                            pltpu.VMEM((n_chunks, D_MODEL, MOE_COL_CHUNK), BF16),
                            pltpu.SemaphoreType.DMA((2,)),
                            pltpu.VMEM((2, MOE_TILE, D_MODEL), F32),
                            pltpu.VMEM((MOE_TILE, D_MODEL), BF16),
                            pltpu.VMEM((n_chunks, MOE_TILE, MOE_COL_CHUNK), BF16),
                            pltpu.SemaphoreType.DMA((2,))]),
        out_shape=jax.ShapeDtypeStruct((N_ASSIGN_PAD, D_FF), BF16),
        compiler_params=_params("arbitrary"),
        name="moe_up",
    )(row_token, *tables, x, w1, w3)


def _moe_down_kernel(layer, te_ref, nx_ref, nt_ref, hid_ref, w2_ref, ys_ref,
                     wbuf_ref, w2b_ref, sem_ref):
    _stream_expert_weights(layer, (te_ref, nx_ref, nt_ref), (w2_ref,), wbuf_ref,
                           (w2b_ref,), sem_ref)
    i = pl.program_id(0)

    @pl.when(i < nt_ref[0])
    def _():
        ys_ref[...] = _dot_nn(hid_ref[...], w2b_ref[...])

    @pl.when(i >= nt_ref[0])
    def _():
        ys_ref[...] = jnp.zeros_like(ys_ref)


def _moe_down(hid, w2, layer, tables):
    last = lambda i, te, nx, nt: (jnp.minimum(i, nt[0] - 1), 0)
    return pl.pallas_call(
        functools.partial(_moe_down_kernel, layer),
        grid_spec=pltpu.PrefetchScalarGridSpec(
            num_scalar_prefetch=3,
            grid=(MOE_MAX_TILES,),
            in_specs=[pl.BlockSpec((MOE_TILE, D_FF), last),
                      pl.BlockSpec(memory_space=pl.ANY)],
            out_specs=pl.BlockSpec((MOE_TILE, D_MODEL), lambda i, te, nx, nt: (i, 0)),
            scratch_shapes=[pltpu.VMEM((1, D_FF, D_MODEL), F32),
                            pltpu.VMEM((D_FF, D_MODEL), BF16),
                            pltpu.SemaphoreType.DMA((1,))]),
        out_shape=jax.ShapeDtypeStruct((N_ASSIGN_PAD, D_MODEL), F32),
        compiler_params=_params("arbitrary"),
        name="moe_down",
    )(*tables, hid, w2)


def _combine_norm_kernel(pos_ref, x_ref, ys_ref, gt_ref, lg_ref, lb_ref, xo_ref, xb_ref, buf_ref, sem_ref):
    i = pl.program_id(0)
    n = pl.num_programs(0)
    tm = x_ref.shape[0]

    def start_tile(tile, slot):
        base = tile * tm * TOP_K

        def body(t, carry):
            for k in range(TOP_K):
                row = pos_ref[base + TOP_K * t + k]
                pltpu.make_async_copy(ys_ref.at[pl.ds(row, 1)], buf_ref.at[slot, k, pl.ds(t, 1)],
                                      sem_ref.at[slot]).start()
            return carry
        lax.fori_loop(0, tm, body, 0, unroll=4)

    @pl.when(i == 0)
    def _():
        start_tile(0, 0)

    @pl.when(i + 1 < n)
    def _():
        start_tile(i + 1, (i + 1) % 2)

    slot = i % 2
    for k in range(TOP_K):
        pltpu.make_async_copy(ys_ref.at[pl.ds(0, tm)], buf_ref.at[slot, k], sem_ref.at[slot]).wait()
    gt = gt_ref[...]
    moe = buf_ref[slot, 0] * gt[:, 0:1] + buf_ref[slot, 1] * gt[:, 1:2]
    xn = _layer_norm(ALPHA * x_ref[...] + moe, lg_ref[...], lb_ref[...])
    xo_ref[...] = xn
    xb_ref[...] = xn.astype(BF16)


def _combine_norm(x, ys, pos, gates, ln_g, ln_b):
    const = lambda i, pos: (0, 0)
    tile = lambda i, pos: (i, 0)
    return pl.pallas_call(
        _combine_norm_kernel,
        grid_spec=pltpu.PrefetchScalarGridSpec(
            num_scalar_prefetch=1,
            grid=(N_TOK // ROW_TILE,),
            in_specs=[pl.BlockSpec((ROW_TILE, D_MODEL), tile),
                      pl.BlockSpec(memory_space=pl.ANY),
                      pl.BlockSpec((ROW_TILE, TOP_K), tile),
                      pl.BlockSpec((1, D_MODEL), const),
                      pl.BlockSpec((1, D_MODEL), const)],
            out_specs=[pl.BlockSpec((ROW_TILE, D_MODEL), tile),
                       pl.BlockSpec((ROW_TILE, D_MODEL), tile)],
            scratch_shapes=[pltpu.VMEM((2, TOP_K, ROW_TILE, D_MODEL), F32),
                            pltpu.SemaphoreType.DMA((2,))]),
        out_shape=[jax.ShapeDtypeStruct((N_TOK, D_MODEL), F32),
                   jax.ShapeDtypeStruct((N_TOK, D_MODEL), BF16)],
        compiler_params=_params("arbitrary"),
        name="combine_norm",
    )(pos, x, ys, gates, ln_g.reshape(1, D_MODEL), ln_b.reshape(1, D_MODEL))


def _routing_tables(ids):
    e = ids.reshape(-1)
    onehot = (e[:, None] == jnp.arange(N_EXPERTS, dtype=jnp.int32)[None, :]).astype(jnp.int32)
    csum = jnp.cumsum(onehot, axis=0)
    rank = jnp.sum(onehot * csum, axis=1) - 1
    sizes = csum[-1]
    tiles_per = (sizes + MOE_TILE - 1) // MOE_TILE
    tile_end = jnp.cumsum(tiles_per)
    tile_start = tile_end - tiles_per
    n_tiles = tile_end[-1]
    pos = (jnp.sum(onehot * (tile_start * MOE_TILE)[None, :], axis=1) + rank).astype(jnp.int32)
    tile = jnp.arange(MOE_MAX_TILES, dtype=jnp.int32)
    owner = jnp.sum((tile_end[None, :] <= jnp.minimum(tile, n_tiles - 1)[:, None]).astype(jnp.int32), axis=1)
    order = jnp.argsort(e, stable=True).astype(jnp.int32)
    owner_hot = (owner[:, None] == jnp.arange(N_EXPERTS, dtype=jnp.int32)[None, :]).astype(jnp.int32)
    size_start = jnp.cumsum(sizes) - sizes
    tile_first = jnp.sum(owner_hot * (size_start - tile_start * MOE_TILE)[None, :], axis=1)
    tile_limit = jnp.sum(owner_hot * (size_start + sizes)[None, :], axis=1)
    row = jnp.arange(N_ASSIGN_PAD, dtype=jnp.int32).reshape(MOE_MAX_TILES, MOE_TILE)
    src = tile_first[:, None] + row
    valid = src < tile_limit[:, None]
    row_token = jnp.where(valid, order[jnp.clip(src, 0, N_ASSIGN - 1)] // TOP_K, row % N_TOK).reshape(-1)
    experts = jnp.arange(N_EXPERTS, dtype=jnp.int32)
    later = lax.cummin(jnp.where(tiles_per > 0, experts, N_EXPERTS), reverse=True)
    next_expert = jnp.concatenate([later[1:], jnp.full((1,), N_EXPERTS, jnp.int32)])
    next_expert = jnp.where(next_expert >= N_EXPERTS, -1, next_expert)
    tile_next = jnp.sum(owner_hot * next_expert[None, :], axis=1)
    n_tiles = n_tiles.reshape(1).astype(jnp.int32)
    tables = (owner.astype(jnp.int32), tile_next.astype(jnp.int32), n_tiles)
    return pos, row_token.astype(jnp.int32), tables


def _moe_block(x, ids, gates, w1, w3, w2, layer, ln_g, ln_b):
    pos, row_token, tables = _routing_tables(ids)
    hid = _moe_up(x, row_token, w1, w3, layer, tables)
    ys = _moe_down(hid, w2, layer, tables)
    return _combine_norm(x, ys, pos, gates, ln_g, ln_b)


def _conv_in_kernel(x_ref, wb_ref, wc_ref, wh_ref, cw_ref, s0_ref, s1_ref, gbz_ref, u_ref,
                    wbb_ref, wcb_ref, whb_ref, ubuf_ref):
    i = pl.program_id(1)
    tm = x_ref.shape[0]
    pad = SUBLANES

    @pl.when(i == 0)
    def _():
        wbb_ref[...] = wb_ref[...].astype(BF16)
        wcb_ref[...] = wc_ref[...].astype(BF16)
        whb_ref[...] = wh_ref[...].astype(BF16)
        ubuf_ref[0:pad, :] = jnp.zeros((pad, ubuf_ref.shape[1]), F32)

    x = x_ref[...]
    gb = _dot_nn(x, wbb_ref[...])
    u = _dot_nn(x, wcb_ref[...]) * _dot_nn(x, whb_ref[...])
    u_ref[...] = u
    ubuf_ref[pad:pad + tm, :] = u
    t = (i * tm + lax.broadcasted_iota(jnp.int32, (tm, 1), 0)) & (SEQ - 1)
    u_m1 = jnp.where(t >= 1, ubuf_ref[pad - 1:pad - 1 + tm, :], 0.0)
    u_m2 = jnp.where(t >= 2, ubuf_ref[pad - 2:pad - 2 + tm, :], 0.0)
    w0 = cw_ref[0:1, :]
    w1 = cw_ref[1:2, :]
    w2 = cw_ref[2:3, :]
    z = u_m2 * w0 + u_m1 * w1 + u * w2
    gbz_ref[...] = (gb * z).astype(BF16)
    ubuf_ref[0:pad, :] = ubuf_ref[tm:tm + pad, :]

    @pl.when(i == pl.num_programs(1) - 1)
    def _():
        lo = tm - N_SAMPLE
        z_s = s0_ref[...] * w0 + s1_ref[...] * w1 + u[lo:tm, :] * w2
        gbz_ref[lo:tm, :] = (gb[lo:tm, :] * z_s).astype(BF16)


def _conv_in(xb, w_in, conv_w, state):
    tn = 512
    nb = D_MODEL // tn
    s0 = state[:, 0, :]
    s1 = state[:, 1, :]
    return pl.pallas_call(
        _conv_in_kernel,
        grid=(nb, N_TOK // TOKEN_TILE),
        in_specs=[pl.BlockSpec((TOKEN_TILE, D_MODEL), lambda j, i: (i, 0)),
                  pl.BlockSpec((D_MODEL, tn), lambda j, i: (0, j)),
                  pl.BlockSpec((D_MODEL, tn), lambda j, i: (0, nb + j)),
                  pl.BlockSpec((D_MODEL, tn), lambda j, i: (0, 2 * nb + j)),
                  pl.BlockSpec((CONV_W, tn), lambda j, i: (0, j)),
                  pl.BlockSpec((N_SAMPLE, tn), lambda j, i: (0, j)),
                  pl.BlockSpec((N_SAMPLE, tn), lambda j, i: (0, j))],
        out_specs=[pl.BlockSpec((TOKEN_TILE, tn), lambda j, i: (i, j)),
                   pl.BlockSpec((TOKEN_TILE, tn), lambda j, i: (i, j))],
        out_shape=[jax.ShapeDtypeStruct((N_TOK, D_MODEL), BF16),
                   jax.ShapeDtypeStruct((N_TOK, D_MODEL), F32)],
        scratch_shapes=[pltpu.VMEM((D_MODEL, tn), BF16), pltpu.VMEM((D_MODEL, tn), BF16),
                        pltpu.VMEM((D_MODEL, tn), BF16),
                        pltpu.VMEM((TOKEN_TILE + 2 * SUBLANES, tn), F32)],
        compiler_params=_params("arbitrary", "arbitrary"),
        name="conv_in",
    )(xb, w_in, w_in, w_in, conv_w, s0, s1)


def kernel(x_prompt, x_sample, state_gla, state_conv, router_w, router_b, gla_w_in, gla_w_gate, gla_b_gate,
           gla_norm_g, gla_w_out, conv_w_in, conv_w, conv_w_out, ln_mix_g, ln_mix_b, ln_ffn_g, ln_ffn_b,
           moe_w1, moe_w3, moe_w2):
    x0 = jnp.concatenate([x_prompt.reshape(N_PROMPT, D_MODEL), x_sample.reshape(N_SAMPLE, D_MODEL)], axis=0)
    x0b = x0.astype(BF16)
    router = _router_operands(router_w, router_b)

    qkvr = _matmul_cols(x0b, gla_w_in[0], GLA_QKVR_WIDTH, 1024, F32, "gla_in_proj")
    g = _gla_gate(x0b, gla_w_in[0][:, GLA_QKVR_WIDTH:], gla_w_gate[0], gla_b_gate[0])
    norm_g = gla_norm_g[0].reshape(1, DV_TOTAL)
    og_p, s_prompt = _gla_prompt(qkvr, g, norm_g)
    og_s, s_sample = _gla_sample(qkvr, g, norm_g, state_gla[0])
    og = jnp.concatenate([og_p, og_s], axis=0)
    x1, ids, gates = _proj_norm_route(og, x0, gla_w_out[0], ln_mix_g[0], ln_mix_b[0], router)
    x2, x2b = _moe_block(x1, ids, gates, moe_w1, moe_w3, moe_w2, 0, ln_ffn_g[0], ln_ffn_b[0])

    gbz, u = _conv_in(x2b, conv_w_in[0], conv_w[0], state_conv[0])
    x3, ids, gates = _proj_norm_route(gbz, x2, conv_w_out[0], ln_mix_g[1], ln_mix_b[1], router)
    x4, _ = _moe_block(x3, ids, gates, moe_w1, moe_w3, moe_w2, 1, ln_ffn_g[1], ln_ffn_b[1])

    y_prompt = x4[:N_PROMPT].reshape(N_PROMPT_SEQ, SEQ, D_MODEL)
    y_sample = x4[N_PROMPT:].reshape(N_SAMPLE, 1, D_MODEL)
    conv_prompt = jnp.stack([u[(b + 1) * SEQ - (CONV_W - 1):(b + 1) * SEQ] for b in range(N_PROMPT_SEQ)])
    conv_sample = jnp.concatenate([state_conv[0][:, 1:, :], u[N_PROMPT:][:, None, :]], axis=1)
    return (y_prompt, y_sample, s_prompt[None], conv_prompt[None], s_sample[None], conv_sample[None])
```

```python
import functools

import jax
import jax.numpy as jnp
from jax import lax
from jax.experimental import pallas as pl
from jax.experimental.pallas import tpu as pltpu

F32 = jnp.float32
BF16 = jnp.bfloat16

D_MODEL = 2048
N_PROMPT_SEQ = 4
SEQ = 2048
N_PROMPT = N_PROMPT_SEQ * SEQ
N_SAMPLE = 128
N_TOK = N_PROMPT + N_SAMPLE
DEPTH = 2

GLA_HEADS = 4
DK_TOTAL = D_MODEL // 2
DV_TOTAL = D_MODEL
DK_HEAD = DK_TOTAL // GLA_HEADS
DV_HEAD = DV_TOTAL // GLA_HEADS
GATE_RANK = 16
GATE_NORMALIZER = 16.0
GLA_QKVR_WIDTH = 2 * DK_TOTAL + 2 * DV_TOTAL
CONV_W = 3
N_EXPERTS = 16
N_GROUPS = 4
EXPERTS_PER_GROUP = N_EXPERTS // N_GROUPS
TOP_K = 2
D_FF = D_MODEL // 2
ALPHA = (2.0 * DEPTH) ** 0.25
LN_EPS = 1e-5
RMS_EPS = 1e-6

LANES = 128
SUBLANES = 8
VMEM_LIMIT_BYTES = 56 * 1024 * 1024

TOKEN_TILE = 640
ROW_TILE = 320
GLA_CHUNK = 64
GLA_SAMPLE_BATCH = 16
MOE_TILE = 256
N_ASSIGN = N_TOK * TOP_K
MOE_MAX_TILES = N_ASSIGN // MOE_TILE + N_EXPERTS
N_ASSIGN_PAD = MOE_MAX_TILES * MOE_TILE


def _params(*semantics):
    return pltpu.CompilerParams(dimension_semantics=semantics, vmem_limit_bytes=VMEM_LIMIT_BYTES)


def _split3(x):
    hi = x.astype(BF16)
    r1 = x - hi.astype(F32)
    mid = r1.astype(BF16)
    lo = (r1 - mid.astype(F32)).astype(BF16)
    return hi, mid, lo


def _dot_nn(a, b):
    return jnp.dot(a, b, preferred_element_type=F32)


def _dot_tn(a, b):
    return lax.dot_general(a, b, (((0,), (0,)), ((), ())), preferred_element_type=F32)


def _dot_nt(a, b):
    return lax.dot_general(a, b, (((1,), (1,)), ((), ())), preferred_element_type=F32)


def _col_bcast(rows, n):
    ones = jnp.ones((rows.shape[0], LANES), BF16)
    hi, mid, lo = _split3(rows)
    col = _dot_tn(hi, ones) + _dot_tn(mid, ones) + _dot_tn(lo, ones)
    return jnp.concatenate([col] * (n // LANES), axis=1)


def _silu(r):
    return r / (1.0 + jnp.exp(-r))


def _layer_norm(h, g, b):
    mu = jnp.mean(h, axis=-1, keepdims=True)
    d = h - mu
    var = jnp.mean(d * d, axis=-1, keepdims=True)
    return d * lax.rsqrt(var + LN_EPS) * g + b


def _matmul_kernel(x_ref, w_ref, o_ref, wb_ref):
    @pl.when(pl.program_id(1) == 0)
    def _():
        wb_ref[...] = w_ref[...].astype(BF16)

    o_ref[...] = _dot_nn(x_ref[...].astype(BF16), wb_ref[...]).astype(o_ref.dtype)


def _matmul_cols(x, w, n_cols, tn, out_dtype, name):
    m, k = x.shape
    return pl.pallas_call(
        _matmul_kernel,
        grid=(n_cols // tn, m // TOKEN_TILE),
        in_specs=[pl.BlockSpec((TOKEN_TILE, k), lambda j, i: (i, 0)),
                  pl.BlockSpec((k, tn), lambda j, i: (0, j))],
        out_specs=pl.BlockSpec((TOKEN_TILE, tn), lambda j, i: (i, j)),
        out_shape=jax.ShapeDtypeStruct((m, n_cols), out_dtype),
        scratch_shapes=[pltpu.VMEM((k, tn), BF16)],
        compiler_params=_params("arbitrary", "arbitrary"),
        name=name,
    )(x, w)


def _gate_kernel(x_ref, wgl_ref, wg_ref, bg_ref, g_ref):
    gl = _dot_nn(x_ref[...].astype(BF16), wgl_ref[...])
    z = _dot_nn(gl.astype(BF16), wg_ref[...]) + bg_ref[...]
    log_sig = jnp.minimum(z, 0.0) - jnp.log1p(jnp.exp(-jnp.abs(z)))
    g_ref[...] = log_sig * (1.0 / GATE_NORMALIZER)


def _gla_gate(xb, w_gl, w_gate, b_gate):
    wgl = jnp.pad(w_gl, ((0, 0), (0, LANES - GATE_RANK))).astype(BF16)
    wg = jnp.pad(w_gate, ((0, LANES - GATE_RANK), (0, 0))).astype(BF16)
    return pl.pallas_call(
        _gate_kernel,
        grid=(N_TOK // TOKEN_TILE,),
        in_specs=[pl.BlockSpec((TOKEN_TILE, D_MODEL), lambda i: (i, 0)),
                  pl.BlockSpec((D_MODEL, LANES), lambda i: (0, 0)),
                  pl.BlockSpec((LANES, DK_TOTAL), lambda i: (0, 0)),
                  pl.BlockSpec((1, DK_TOTAL), lambda i: (0, 0))],
        out_specs=pl.BlockSpec((TOKEN_TILE, DK_TOTAL), lambda i: (i, 0)),
        out_shape=jax.ShapeDtypeStruct((N_TOK, DK_TOTAL), F32),
        compiler_params=_params("arbitrary"),
        name="gla_gate",
    )(xb, wgl, wg, b_gate.reshape(1, DK_TOTAL))


def _rms_gate(o, norm_g, r):
    o = o * lax.rsqrt(jnp.mean(o * o, axis=-1, keepdims=True) + RMS_EPS)
    return (o * norm_g) * _silu(r)


def _gla_prompt_kernel(q_ref, k_ref, v_ref, r_ref, g_ref, ng_ref, og_in_ref, og_ref, s_out_ref, s_ref):
    del og_in_ref
    c = pl.program_id(1)
    cc = q_ref.shape[0]

    @pl.when(c == 0)
    def _():
        s_ref[...] = jnp.zeros_like(s_ref)

    row = lax.broadcasted_iota(jnp.int32, (cc, cc), 0)
    col = lax.broadcasted_iota(jnp.int32, (cc, cc), 1)
    causal = col <= row
    tri = jnp.where(causal, 1.0, 0.0).astype(BF16)
    b = _dot_nn(jnp.concatenate([tri] * 3, axis=1),
                jnp.concatenate(_split3(g_ref[...]), axis=0))
    b_last = b[cc - 1:cc, :]
    b_mid = b[cc // 2 - 1:cc // 2, :]

    q = q_ref[...]
    k = k_ref[...] * (DK_HEAD ** -0.5)
    q_in = (q * jnp.exp(b)).astype(BF16)
    qh, qm, _ = _split3(q * jnp.exp(b - b_mid))
    kh, km, _ = _split3(k * jnp.exp(b_mid - b))
    k_d = (k * jnp.exp(b_last - b)).astype(BF16)

    rows = 2 * SUBLANES
    sub = lax.broadcasted_iota(jnp.int32, (rows, DK_TOTAL), 0)
    decay_rows = jnp.where(sub == 0, jnp.broadcast_to(jnp.exp(b_last), (rows, DK_TOTAL)), 0.0)
    decay_col = _dot_tn(jnp.concatenate(_split3(decay_rows), axis=0),
                        jnp.ones((3 * rows, LANES), BF16))

    for h in range(GLA_HEADS):
        dk = slice(h * DK_HEAD, (h + 1) * DK_HEAD)
        dv = slice(h * DV_HEAD, (h + 1) * DV_HEAD)
        q3 = jnp.concatenate([qh[:, dk], qh[:, dk], qm[:, dk]], axis=1)
        k3 = jnp.concatenate([kh[:, dk], km[:, dk], kh[:, dk]], axis=1)
        scores = jnp.where(causal, _dot_nt(q3, k3), 0.0).astype(BF16)
        v = v_ref[:, dv].astype(BF16)
        s_old = s_ref[h]
        o = _dot_nn(jnp.concatenate([q_in[:, dk], scores], axis=1),
                    jnp.concatenate([s_old.astype(BF16), v], axis=0))
        decay = jnp.concatenate([decay_col[dk, :]] * (DV_HEAD // LANES), axis=1)
        s_new = decay * s_old + _dot_tn(k_d[:, dk], v)
        s_ref[h] = s_new

        @pl.when(c == pl.num_programs(1) - 1)
        def _():
            s_out_ref[0, h] = s_new

        og_ref[:, dv] = _rms_gate(o, ng_ref[:, dv], r_ref[:, dv]).astype(BF16)


def _gla_prompt(qkvr, g, norm_g):
    nc = SEQ // GLA_CHUNK
    row = lambda b, c: b * nc + c
    return pl.pallas_call(
        _gla_prompt_kernel,
        grid=(N_PROMPT_SEQ, nc),
        in_specs=[pl.BlockSpec((GLA_CHUNK, DK_TOTAL), lambda b, c: (row(b, c), 0)),
                  pl.BlockSpec((GLA_CHUNK, DK_TOTAL), lambda b, c: (row(b, c), 1)),
                  pl.BlockSpec((GLA_CHUNK, DV_TOTAL), lambda b, c: (row(b, c), 1)),
                  pl.BlockSpec((GLA_CHUNK, DV_TOTAL), lambda b, c: (row(b, c), 2)),
                  pl.BlockSpec((GLA_CHUNK, DK_TOTAL), lambda b, c: (row(b, c), 0)),
                  pl.BlockSpec((1, DV_TOTAL), lambda b, c: (0, 0)),
                  pl.BlockSpec(memory_space=pl.ANY)],
        out_specs=[pl.BlockSpec((GLA_CHUNK, DV_TOTAL), lambda b, c: (row(b, c), 0)),
                   pl.BlockSpec((1, GLA_HEADS, DK_HEAD, DV_HEAD), lambda b, c: (b, 0, 0, 0))],
        out_shape=[jax.ShapeDtypeStruct((N_TOK, DV_TOTAL), BF16),
                   jax.ShapeDtypeStruct((N_PROMPT_SEQ, GLA_HEADS, DK_HEAD, DV_HEAD), F32)],
        scratch_shapes=[pltpu.VMEM((GLA_HEADS, DK_HEAD, DV_HEAD), F32)],
        input_output_aliases={6: 0},
        compiler_params=_params("arbitrary", "arbitrary"),
        name="gla_prompt",
    )(qkvr, qkvr, qkvr, qkvr, g, norm_g, jnp.zeros((N_TOK, DV_TOTAL), BF16))


def _gla_sample_kernel(q_ref, k_ref, v_ref, r_ref, g_ref, ng_ref, s_ref, og_in_ref, og_ref, s_out_ref,
                       o_scr):
    del og_in_ref
    bb = q_ref.shape[0]
    q = q_ref[...]
    k = k_ref[...] * (DK_HEAD ** -0.5)
    v = v_ref[...]
    eg = jnp.exp(g_ref[...])
    qe = q * eg
    qk = jnp.sum(q * k, axis=-1, keepdims=True)
    sub = lax.broadcasted_iota(jnp.int32, (bb, DK_HEAD), 0)
    for bi in range(bb):
        sel = sub == bi
        s_old = s_ref[bi, 0]
        decay = _col_bcast(jnp.where(sel, eg, 0.0), DV_HEAD)
        k_col = _col_bcast(jnp.where(sel, k, 0.0), DV_HEAD)
        qe_col = _col_bcast(jnp.where(sel, qe, 0.0), DV_HEAD)
        s_out_ref[bi, 0] = decay * s_old + k_col * v[bi:bi + 1, :]
        o_scr[bi:bi + 1, :] = jnp.sum(qe_col * s_old, axis=0, keepdims=True)
    o = qk * v + o_scr[...]
    og_ref[...] = _rms_gate(o, ng_ref[...], r_ref[...]).astype(BF16)


def _gla_sample(qkvr, g, norm_g, state, og):
    bb = GLA_SAMPLE_BATCH
    r0 = N_PROMPT // bb
    hk = DK_TOTAL // DK_HEAD
    hv = 2 * DK_TOTAL // DV_HEAD
    return pl.pallas_call(
        _gla_sample_kernel,
        grid=(N_SAMPLE // bb, GLA_HEADS),
        in_specs=[pl.BlockSpec((bb, DK_HEAD), lambda i, h: (r0 + i, h)),
                  pl.BlockSpec((bb, DK_HEAD), lambda i, h: (r0 + i, hk + h)),
                  pl.BlockSpec((bb, DV_HEAD), lambda i, h: (r0 + i, hv + h)),
                  pl.BlockSpec((bb, DV_HEAD), lambda i, h: (r0 + i, hv + GLA_HEADS + h)),
                  pl.BlockSpec((bb, DK_HEAD), lambda i, h: (r0 + i, h)),
                  pl.BlockSpec((1, DV_HEAD), lambda i, h: (0, h)),
                  pl.BlockSpec((bb, 1, DK_HEAD, DV_HEAD), lambda i, h: (i, h, 0, 0)),
                  pl.BlockSpec(memory_space=pl.ANY)],
        out_specs=[pl.BlockSpec((bb, DV_HEAD), lambda i, h: (r0 + i, h)),
                   pl.BlockSpec((bb, 1, DK_HEAD, DV_HEAD), lambda i, h: (i, h, 0, 0))],
        out_shape=[jax.ShapeDtypeStruct((N_TOK, DV_TOTAL), BF16),
                   jax.ShapeDtypeStruct((N_SAMPLE, GLA_HEADS, DK_HEAD, DV_HEAD), F32)],
        scratch_shapes=[pltpu.VMEM((bb, DV_HEAD), F32)],
        input_output_aliases={7: 0},
        compiler_params=_params("arbitrary", "arbitrary"),
        name="gla_sample",
    )(qkvr, qkvr, qkvr, qkvr, g, norm_g, state, og)


def _top2_of4(p):
    ranks = []
    for j in range(4):
        rk = jnp.zeros(p[j].shape, jnp.int32)
        for i in range(4):
            if i == j:
                continue
            beats = (p[i] >= p[j]) if i < j else (p[i] > p[j])
            rk = rk + jnp.where(beats, 1, 0)
        ranks.append(rk)

    def pick(rank):
        val = jnp.zeros(p[0].shape, F32)
        idx = jnp.zeros(p[0].shape, jnp.int32)
        for j in range(4):
            hit = ranks[j] == rank
            val = jnp.where(hit, p[j], val)
            idx = jnp.where(hit, j, idx)
        return val, idx

    v1, i1 = pick(0)
    v2, i2 = pick(1)
    return v1, i1, v2, i2


def _route(x, rw_ref, rb_ref):
    logits = _dot_nn(x.astype(BF16), rw_ref[...])
    n = x.shape[0]
    n_pad = -n % LANES
    if n_pad:
        logits = jnp.concatenate([logits, jnp.zeros((n_pad, LANES), F32)], axis=0)
    lt = logits.T[0:N_EXPERTS, 0:n] + rb_ref[...]
    e = jnp.exp(lt - jnp.max(lt, axis=0, keepdims=True))
    probs = e / jnp.sum(e, axis=0, keepdims=True)
    best = None
    for grp in range(N_GROUPS):
        rows = [probs[grp * EXPERTS_PER_GROUP + j:grp * EXPERTS_PER_GROUP + j + 1, :]
                for j in range(EXPERTS_PER_GROUP)]
        v1, i1, v2, i2 = _top2_of4(rows)
        score = v1 + v2
        cand = (score, v1, i1 + grp * EXPERTS_PER_GROUP, v2, i2 + grp * EXPERTS_PER_GROUP)
        if best is None:
            best = cand
        else:
            better = score > best[0]
            best = tuple(jnp.where(better, n, o) for n, o in zip(cand, best))
    _, v1, e1, v2, e2 = best
    denom = v1 + v2
    return (e1, e2), (v1 / denom, v2 / denom)


def _proj_norm_route_kernel(a_ref, x_ref, w_ref, lg_ref, lb_ref, rw_ref, rb_ref,
                            xo_ref, id_ref, gt_ref):
    y = _dot_nn(a_ref[...], w_ref[...])
    xn = _layer_norm(ALPHA * x_ref[...] + y, lg_ref[...], lb_ref[...])
    xo_ref[...] = xn
    (e1, e2), (g1, g2) = _route(xn, rw_ref, rb_ref)
    id_ref[0, 0:1, :] = e1
    id_ref[0, 1:2, :] = e2
    gt_ref[0, 0:1, :] = g1
    gt_ref[0, 1:2, :] = g2


def _router_operands(router_w, router_b):
    rw = jnp.pad(router_w, ((0, 0), (0, LANES - N_EXPERTS))).astype(BF16)
    return rw, router_b.reshape(N_EXPERTS, 1)


def _proj_norm_route(a, x, w_out, ln_g, ln_b, router):
    nt = N_TOK // ROW_TILE
    rw, rb = router
    const = lambda i: (0, 0)
    x1, ids, gates = pl.pallas_call(
        _proj_norm_route_kernel,
        grid=(nt,),
        in_specs=[pl.BlockSpec((ROW_TILE, D_MODEL), lambda i: (i, 0)),
                  pl.BlockSpec((ROW_TILE, D_MODEL), lambda i: (i, 0)),
                  pl.BlockSpec((D_MODEL, D_MODEL), const),
                  pl.BlockSpec((1, D_MODEL), const),
                  pl.BlockSpec((1, D_MODEL), const),
                  pl.BlockSpec((D_MODEL, LANES), const),
                  pl.BlockSpec((N_EXPERTS, 1), const)],
        out_specs=[pl.BlockSpec((ROW_TILE, D_MODEL), lambda i: (i, 0)),
                   pl.BlockSpec((1, TOP_K, ROW_TILE), lambda i: (i, 0, 0)),
                   pl.BlockSpec((1, TOP_K, ROW_TILE), lambda i: (i, 0, 0))],
        out_shape=[jax.ShapeDtypeStruct((N_TOK, D_MODEL), F32),
                   jax.ShapeDtypeStruct((nt, TOP_K, ROW_TILE), jnp.int32),
                   jax.ShapeDtypeStruct((nt, TOP_K, ROW_TILE), F32)],
        compiler_params=_params("arbitrary"),
        name="proj_norm_route",
    )(a, x, w_out.astype(BF16), ln_g.reshape(1, D_MODEL), ln_b.reshape(1, D_MODEL), rw, rb)
    ids = ids.transpose(0, 2, 1).reshape(N_TOK, TOP_K)
    gates = gates.transpose(0, 2, 1).reshape(N_TOK, TOP_K)
    return x1, ids, gates


def _stream_expert_weights(layer, tables, w_refs, wbuf_ref, wb_refs, sem_ref):
    te_ref, nx_ref, nt_ref = tables
    i = pl.program_id(0)

    def copies(expert):
        return [pltpu.make_async_copy(w.at[layer, expert], wbuf_ref.at[j], sem_ref.at[j])
                for j, w in enumerate(w_refs)]

    @pl.when(i == 0)
    def _():
        for c in copies(te_ref[0]):
            c.start()

    first = jnp.logical_or(i == 0, te_ref[i] != te_ref[jnp.maximum(i - 1, 0)])

    @pl.when(jnp.logical_and(first, i < nt_ref[0]))
    def _():
        for c in copies(te_ref[i]):
            c.wait()
        for j, wb in enumerate(wb_refs):
            wb[...] = wbuf_ref[j].astype(BF16)

        @pl.when(nx_ref[i] >= 0)
        def _():
            for c in copies(nx_ref[i]):
                c.start()


def _issue_row_copies(idx_ref, base, src_ref, dst_ref, sem, n_rows, stride=1):
    def body(g, carry):
        r0 = pl.multiple_of(g * SUBLANES, SUBLANES)
        for k in range(SUBLANES):
            row = idx_ref[base + stride * (r0 + k)]
            pltpu.make_async_copy(src_ref.at[pl.ds(row, 1)], dst_ref.at[pl.ds(r0 + k, 1)], sem).start()
        return carry
    lax.fori_loop(0, n_rows // SUBLANES, body, 0)


def _gather_kernel(idx_ref, nt_ref, src_ref, out_ref, buf_ref, sem_ref):
    i = pl.program_id(0)
    n = nt_ref[0]
    tg = out_ref.shape[0]

    def start_tile(tile, slot):
        _issue_row_copies(idx_ref, tile * tg, src_ref, buf_ref.at[slot], sem_ref.at[slot], tg)

    @pl.when(i == 0)
    def _():
        start_tile(0, 0)

    @pl.when(i + 1 < n)
    def _():
        start_tile(i + 1, (i + 1) % 2)

    @pl.when(i < n)
    def _():
        slot = i % 2
        pltpu.make_async_copy(src_ref.at[pl.ds(0, tg)], buf_ref.at[slot], sem_ref.at[slot]).wait()
        out_ref[...] = buf_ref[slot].astype(out_ref.dtype)

    @pl.when(i >= n)
    def _():
        out_ref[...] = jnp.zeros_like(out_ref)


def _gather_rows(src, idx, n_tiles, out_dtype, name):
    m = idx.shape[0]
    width = src.shape[1]
    return pl.pallas_call(
        _gather_kernel,
        grid_spec=pltpu.PrefetchScalarGridSpec(
            num_scalar_prefetch=2,
            grid=(m // MOE_TILE,),
            in_specs=[pl.BlockSpec(memory_space=pl.ANY)],
            out_specs=pl.BlockSpec((MOE_TILE, width), lambda i, idx, nt: (i, 0)),
            scratch_shapes=[pltpu.VMEM((2, MOE_TILE, width), src.dtype),
                            pltpu.SemaphoreType.DMA((2,))]),
        out_shape=jax.ShapeDtypeStruct((m, width), out_dtype),
        compiler_params=_params("arbitrary"),
        name=name,
    )(idx, n_tiles, src)


def _moe_up_kernel(layer, te_ref, nx_ref, nt_ref, xs_ref, w1_ref, w3_ref, hid_ref,
                   wbuf_ref, w1b_ref, w3b_ref, sem_ref):
    _stream_expert_weights(layer, (te_ref, nx_ref, nt_ref), (w1_ref, w3_ref), wbuf_ref,
                           (w1b_ref, w3b_ref), sem_ref)
    i = pl.program_id(0)

    @pl.when(i < nt_ref[0])
    def _():
        xs = xs_ref[...]
        h1 = _dot_nn(xs, w1b_ref[...])
        h3 = _dot_nn(xs, w3b_ref[...])
        hid_ref[...] = (_silu(h1) * h3).astype(BF16)

    @pl.when(i >= nt_ref[0])
    def _():
        hid_ref[...] = jnp.zeros_like(hid_ref)


def _moe_up(xs, w1, w3, layer, tables):
    last = lambda i, te, nx, nt: (jnp.minimum(i, nt[0] - 1), 0)
    return pl.pallas_call(
        functools.partial(_moe_up_kernel, layer),
        grid_spec=pltpu.PrefetchScalarGridSpec(
            num_scalar_prefetch=3,
            grid=(MOE_MAX_TILES,),
            in_specs=[pl.BlockSpec((MOE_TILE, D_MODEL), last),
                      pl.BlockSpec(memory_space=pl.ANY),
                      pl.BlockSpec(memory_space=pl.ANY)],
            out_specs=pl.BlockSpec((MOE_TILE, D_FF), lambda i, te, nx, nt: (i, 0)),
            scratch_shapes=[pltpu.VMEM((2, D_MODEL, D_FF), F32),
                            pltpu.VMEM((D_MODEL, D_FF), BF16), pltpu.VMEM((D_MODEL, D_FF), BF16),
                            pltpu.SemaphoreType.DMA((2,))]),
        out_shape=jax.ShapeDtypeStruct((N_ASSIGN_PAD, D_FF), BF16),
        compiler_params=_params("arbitrary"),
        name="moe_up",
    )(*tables, xs, w1, w3)


def _moe_down_kernel(layer, te_ref, nx_ref, nt_ref, hid_ref, w2_ref, ys_ref,
                     wbuf_ref, w2b_ref, sem_ref):
    _stream_expert_weights(layer, (te_ref, nx_ref, nt_ref), (w2_ref,), wbuf_ref,
                           (w2b_ref,), sem_ref)
    i = pl.program_id(0)

    @pl.when(i < nt_ref[0])
    def _():
        ys_ref[...] = _dot_nn(hid_ref[...], w2b_ref[...])

    @pl.when(i >= nt_ref[0])
    def _():
        ys_ref[...] = jnp.zeros_like(ys_ref)


def _moe_down(hid, w2, layer, tables):
    last = lambda i, te, nx, nt: (jnp.minimum(i, nt[0] - 1), 0)
    return pl.pallas_call(
        functools.partial(_moe_down_kernel, layer),
        grid_spec=pltpu.PrefetchScalarGridSpec(
            num_scalar_prefetch=3,
            grid=(MOE_MAX_TILES,),
            in_specs=[pl.BlockSpec((MOE_TILE, D_FF), last),
                      pl.BlockSpec(memory_space=pl.ANY)],
            out_specs=pl.BlockSpec((MOE_TILE, D_MODEL), lambda i, te, nx, nt: (i, 0)),
            scratch_shapes=[pltpu.VMEM((1, D_FF, D_MODEL), F32),
                            pltpu.VMEM((D_FF, D_MODEL), BF16),
                            pltpu.SemaphoreType.DMA((1,))]),
        out_shape=jax.ShapeDtypeStruct((N_ASSIGN_PAD, D_MODEL), F32),
        compiler_params=_params("arbitrary"),
        name="moe_down",
    )(*tables, hid, w2)


def _combine_norm_kernel(final, pos_ref, x_ref, ys_ref, gt_ref, lg_ref, lb_ref, o0_ref, o1_ref,
                         buf_ref, sem_ref):
    i = pl.program_id(0)
    n = pl.num_programs(0)
    tm = x_ref.shape[0]

    def start_tile(tile, slot):
        for k in range(TOP_K):
            _issue_row_copies(pos_ref, tile * tm * TOP_K + k, ys_ref, buf_ref.at[slot, k],
                              sem_ref.at[slot], tm, stride=TOP_K)

    @pl.when(i == 0)
    def _():
        start_tile(0, 0)

    @pl.when(i + 1 < n)
    def _():
        start_tile(i + 1, (i + 1) % 2)

    slot = i % 2
    for k in range(TOP_K):
        pltpu.make_async_copy(ys_ref.at[pl.ds(0, tm)], buf_ref.at[slot, k], sem_ref.at[slot]).wait()
    gt = gt_ref[...]
    moe = buf_ref[slot, 0] * gt[:, 0:1] + buf_ref[slot, 1] * gt[:, 1:2]
    xn = _layer_norm(ALPHA * x_ref[...] + moe, lg_ref[...], lb_ref[...])
    if final:
        @pl.when(i < n - 1)
        def _():
            o0_ref[...] = xn

        @pl.when(i == n - 1)
        def _():
            o1_ref[...] = xn
    else:
        o0_ref[...] = xn
        o1_ref[...] = xn.astype(BF16)


def _combine_norm(x, ys, pos, gates, ln_g, ln_b, final):
    tm = N_SAMPLE if final else ROW_TILE
    const = lambda i, pos: (0, 0)
    tile = lambda i, pos: (i, 0)
    if final:
        n_prompt_tiles = N_PROMPT // tm
        out_specs = [pl.BlockSpec((tm, D_MODEL), lambda i, pos: (jnp.minimum(i, n_prompt_tiles - 1), 0)),
                     pl.BlockSpec((tm, D_MODEL), const)]
        out_shape = [jax.ShapeDtypeStruct((N_PROMPT, D_MODEL), F32),
                     jax.ShapeDtypeStruct((N_SAMPLE, D_MODEL), F32)]
    else:
        out_specs = [pl.BlockSpec((tm, D_MODEL), tile), pl.BlockSpec((tm, D_MODEL), tile)]
        out_shape = [jax.ShapeDtypeStruct((N_TOK, D_MODEL), F32),
                     jax.ShapeDtypeStruct((N_TOK, D_MODEL), BF16)]
    return pl.pallas_call(
        functools.partial(_combine_norm_kernel, final),
        grid_spec=pltpu.PrefetchScalarGridSpec(
            num_scalar_prefetch=1,
            grid=(N_TOK // tm,),
            in_specs=[pl.BlockSpec((tm, D_MODEL), tile),
                      pl.BlockSpec(memory_space=pl.ANY),
                      pl.BlockSpec((tm, TOP_K), tile),
                      pl.BlockSpec((1, D_MODEL), const),
                      pl.BlockSpec((1, D_MODEL), const)],
            out_specs=out_specs,
            scratch_shapes=[pltpu.VMEM((2, TOP_K, tm, D_MODEL), F32),
                            pltpu.SemaphoreType.DMA((2,))]),
        out_shape=out_shape,
        compiler_params=_params("arbitrary"),
        name="combine_norm",
    )(pos, x, ys, gates, ln_g.reshape(1, D_MODEL), ln_b.reshape(1, D_MODEL))


def _routing_tables(ids):
    e = ids.reshape(-1)
    onehot = (e[:, None] == jnp.arange(N_EXPERTS, dtype=jnp.int32)[None, :]).astype(jnp.int32)
    csum = jnp.cumsum(onehot, axis=0)
    rank = jnp.sum(onehot * csum, axis=1) - 1
    sizes = csum[-1]
    tiles_per = (sizes + MOE_TILE - 1) // MOE_TILE
    tile_end = jnp.cumsum(tiles_per)
    tile_start = tile_end - tiles_per
    n_tiles = tile_end[-1]
    pos = (jnp.sum(onehot * (tile_start * MOE_TILE)[None, :], axis=1) + rank).astype(jnp.int32)
    tile = jnp.arange(MOE_MAX_TILES, dtype=jnp.int32)
    owner = jnp.sum((tile_end[None, :] <= jnp.minimum(tile, n_tiles - 1)[:, None]).astype(jnp.int32), axis=1)
    order = jnp.argsort(e, stable=True).astype(jnp.int32)
    owner_hot = (owner[:, None] == jnp.arange(N_EXPERTS, dtype=jnp.int32)[None, :]).astype(jnp.int32)
    size_start = jnp.cumsum(sizes) - sizes
    tile_first = jnp.sum(owner_hot * (size_start - tile_start * MOE_TILE)[None, :], axis=1)
    tile_limit = jnp.sum(owner_hot * (size_start + sizes)[None, :], axis=1)
    row = jnp.arange(N_ASSIGN_PAD, dtype=jnp.int32).reshape(MOE_MAX_TILES, MOE_TILE)
    src = tile_first[:, None] + row
    valid = src < tile_limit[:, None]
    row_token = jnp.where(valid, order[jnp.clip(src, 0, N_ASSIGN - 1)] // TOP_K, row % N_TOK).reshape(-1)
    experts = jnp.arange(N_EXPERTS, dtype=jnp.int32)
    later = lax.cummin(jnp.where(tiles_per > 0, experts, N_EXPERTS), reverse=True)
    next_expert = jnp.concatenate([later[1:], jnp.full((1,), N_EXPERTS, jnp.int32)])
    next_expert = jnp.where(next_expert >= N_EXPERTS, -1, next_expert)
    tile_next = jnp.sum(owner_hot * next_expert[None, :], axis=1)
    n_tiles = n_tiles.reshape(1).astype(jnp.int32)
    tables = (owner.astype(jnp.int32), tile_next.astype(jnp.int32), n_tiles)
    return pos, row_token.astype(jnp.int32), tables


def _moe_block(x, ids, gates, w1, w3, w2, layer, ln_g, ln_b, final):
    pos, row_token, tables = _routing_tables(ids)
    xs = _gather_rows(x, row_token, tables[2], BF16, "moe_gather")
    hid = _moe_up(xs, w1, w3, layer, tables)
    ys = _moe_down(hid, w2, layer, tables)
    return _combine_norm(x, ys, pos, gates, ln_g, ln_b, final)


def _conv_in_kernel(x_ref, wb_ref, wc_ref, wh_ref, cw_ref, s0_ref, s1_ref, gbz_ref, u_ref,
                    wbb_ref, wcb_ref, whb_ref, ubuf_ref):
    i = pl.program_id(1)
    tm = x_ref.shape[0]
    pad = SUBLANES

    @pl.when(i == 0)
    def _():
        wbb_ref[...] = wb_ref[...].astype(BF16)
        wcb_ref[...] = wc_ref[...].astype(BF16)
        whb_ref[...] = wh_ref[...].astype(BF16)
        ubuf_ref[0:pad, :] = jnp.zeros((pad, ubuf_ref.shape[1]), F32)

    x = x_ref[...]
    gb = _dot_nn(x, wbb_ref[...])
    u = _dot_nn(x, wcb_ref[...]) * _dot_nn(x, whb_ref[...])
    u_ref[...] = u
    ubuf_ref[pad:pad + tm, :] = u
    t = (i * tm + lax.broadcasted_iota(jnp.int32, (tm, 1), 0)) & (SEQ - 1)
    u_m1 = jnp.where(t >= 1, ubuf_ref[pad - 1:pad - 1 + tm, :], 0.0)
    u_m2 = jnp.where(t >= 2, ubuf_ref[pad - 2:pad - 2 + tm, :], 0.0)
    w0 = cw_ref[0:1, :]
    w1 = cw_ref[1:2, :]
    w2 = cw_ref[2:3, :]
    z = u_m2 * w0 + u_m1 * w1 + u * w2
    gbz_ref[...] = (gb * z).astype(BF16)
    ubuf_ref[0:pad, :] = ubuf_ref[tm:tm + pad, :]

    @pl.when(i == pl.num_programs(1) - 1)
    def _():
        lo = tm - N_SAMPLE
        z_s = s0_ref[...] * w0 + s1_ref[...] * w1 + u[lo:tm, :] * w2
        gbz_ref[lo:tm, :] = (gb[lo:tm, :] * z_s).astype(BF16)


def _conv_in(xb, w_in, conv_w, state):
    tn = 512
    nb = D_MODEL // tn
    s0 = state[:, 0, :]
    s1 = state[:, 1, :]
    return pl.pallas_call(
        _conv_in_kernel,
        grid=(nb, N_TOK // TOKEN_TILE),
        in_specs=[pl.BlockSpec((TOKEN_TILE, D_MODEL), lambda j, i: (i, 0)),
                  pl.BlockSpec((D_MODEL, tn), lambda j, i: (0, j)),
                  pl.BlockSpec((D_MODEL, tn), lambda j, i: (0, nb + j)),
                  pl.BlockSpec((D_MODEL, tn), lambda j, i: (0, 2 * nb + j)),
                  pl.BlockSpec((CONV_W, tn), lambda j, i: (0, j)),
                  pl.BlockSpec((N_SAMPLE, tn), lambda j, i: (0, j)),
                  pl.BlockSpec((N_SAMPLE, tn), lambda j, i: (0, j))],
        out_specs=[pl.BlockSpec((TOKEN_TILE, tn), lambda j, i: (i, j)),
                   pl.BlockSpec((TOKEN_TILE, tn), lambda j, i: (i, j))],
        out_shape=[jax.ShapeDtypeStruct((N_TOK, D_MODEL), BF16),
                   jax.ShapeDtypeStruct((N_TOK, D_MODEL), F32)],
        scratch_shapes=[pltpu.VMEM((D_MODEL, tn), BF16), pltpu.VMEM((D_MODEL, tn), BF16),
                        pltpu.VMEM((D_MODEL, tn), BF16),
                        pltpu.VMEM((TOKEN_TILE + 2 * SUBLANES, tn), F32)],
        compiler_params=_params("arbitrary", "arbitrary"),
        name="conv_in",
    )(xb, w_in, w_in, w_in, conv_w, s0, s1)


def kernel(x_prompt, x_sample, state_gla, state_conv, router_w, router_b, gla_w_in, gla_w_gate, gla_b_gate,
           gla_norm_g, gla_w_out, conv_w_in, conv_w, conv_w_out, ln_mix_g, ln_mix_b, ln_ffn_g, ln_ffn_b,
           moe_w1, moe_w3, moe_w2):
    x0 = jnp.concatenate([x_prompt.reshape(N_PROMPT, D_MODEL), x_sample.reshape(N_SAMPLE, D_MODEL)], axis=0)
    router = _router_operands(router_w, router_b)

    qkvr = _matmul_cols(x0, gla_w_in[0], GLA_QKVR_WIDTH, 1024, F32, "gla_in_proj")
    g = _gla_gate(x0, gla_w_in[0][:, GLA_QKVR_WIDTH:], gla_w_gate[0], gla_b_gate[0])
    norm_g = gla_norm_g[0].reshape(1, DV_TOTAL)
    og, s_prompt = _gla_prompt(qkvr, g, norm_g)
    og, s_sample = _gla_sample(qkvr, g, norm_g, state_gla[0], og)
    x1, ids, gates = _proj_norm_route(og, x0, gla_w_out[0], ln_mix_g[0], ln_mix_b[0], router)
    x2, x2b = _moe_block(x1, ids, gates, moe_w1, moe_w3, moe_w2, 0, ln_ffn_g[0], ln_ffn_b[0], False)

    gbz, u = _conv_in(x2b, conv_w_in[0], conv_w[0], state_conv[0])
    x3, ids, gates = _proj_norm_route(gbz, x2, conv_w_out[0], ln_mix_g[1], ln_mix_b[1], router)
    y_p, y_s = _moe_block(x3, ids, gates, moe_w1, moe_w3, moe_w2, 1, ln_ffn_g[1], ln_ffn_b[1], True)

    y_prompt = y_p.reshape(N_PROMPT_SEQ, SEQ, D_MODEL)
    y_sample = y_s.reshape(N_SAMPLE, 1, D_MODEL)
    conv_prompt = jnp.stack([u[(b + 1) * SEQ - (CONV_W - 1):(b + 1) * SEQ] for b in range(N_PROMPT_SEQ)])
    conv_sample = jnp.concatenate([state_conv[0][:, 1:, :], u[N_PROMPT:][:, None, :]], axis=1)
    return (y_prompt, y_sample, s_prompt[None], conv_prompt[None], s_sample[None], conv_sample[None])
```

```python
import functools

import jax
import jax.numpy as jnp
from jax import lax
from jax.experimental import pallas as pl
from jax.experimental.pallas import tpu as pltpu

F32 = jnp.float32
BF16 = jnp.bfloat16

D_MODEL = 2048
N_PROMPT_SEQ = 4
SEQ = 2048
N_PROMPT = N_PROMPT_SEQ * SEQ
N_SAMPLE = 128
N_TOK = N_PROMPT + N_SAMPLE
DEPTH = 2

GLA_HEADS = 4
DK_TOTAL = D_MODEL // 2
DV_TOTAL = D_MODEL
DK_HEAD = DK_TOTAL // GLA_HEADS
DV_HEAD = DV_TOTAL // GLA_HEADS
GATE_RANK = 16
GATE_NORMALIZER = 16.0
GLA_QKVR_WIDTH = 2 * DK_TOTAL + 2 * DV_TOTAL
CONV_W = 3
N_EXPERTS = 16
N_GROUPS = 4
EXPERTS_PER_GROUP = N_EXPERTS // N_GROUPS
TOP_K = 2
D_FF = D_MODEL // 2
ALPHA = (2.0 * DEPTH) ** 0.25
LN_EPS = 1e-5
RMS_EPS = 1e-6

LANES = 128
SUBLANES = 8
VMEM_LIMIT_BYTES = 56 * 1024 * 1024

TOKEN_TILE = 640
ROW_TILE = 320
GLA_CHUNK = 64
GLA_SAMPLE_BATCH = 16
MOE_TILE = 256
N_ASSIGN = N_TOK * TOP_K
MOE_MAX_TILES = N_ASSIGN // MOE_TILE + N_EXPERTS
N_ASSIGN_PAD = MOE_MAX_TILES * MOE_TILE


def _params(*semantics):
    return pltpu.CompilerParams(dimension_semantics=semantics, vmem_limit_bytes=VMEM_LIMIT_BYTES)


def _split3(x):
    hi = x.astype(BF16)
    r1 = x - hi.astype(F32)
    mid = r1.astype(BF16)
    lo = (r1 - mid.astype(F32)).astype(BF16)
    return hi, mid, lo


def _dot_nn(a, b):
    return jnp.dot(a, b, preferred_element_type=F32)


def _dot_tn(a, b):
    return lax.dot_general(a, b, (((0,), (0,)), ((), ())), preferred_element_type=F32)


def _dot_nt(a, b):
    return lax.dot_general(a, b, (((1,), (1,)), ((), ())), preferred_element_type=F32)


def _col_bcast(rows, n):
    ones = jnp.ones((rows.shape[0], LANES), BF16)
    hi, mid, lo = _split3(rows)
    col = _dot_tn(hi, ones) + _dot_tn(mid, ones) + _dot_tn(lo, ones)
    return jnp.concatenate([col] * (n // LANES), axis=1)


def _silu(r):
    return r / (1.0 + jnp.exp(-r))


def _layer_norm(h, g, b):
    mu = jnp.mean(h, axis=-1, keepdims=True)
    d = h - mu
    var = jnp.mean(d * d, axis=-1, keepdims=True)
    return d * lax.rsqrt(var + LN_EPS) * g + b


def _matmul_kernel(x_ref, w_ref, o_ref, wb_ref):
    @pl.when(pl.program_id(1) == 0)
    def _():
        wb_ref[...] = w_ref[...].astype(BF16)

    for s in range(TOKEN_TILE // ROW_TILE):
        rows = slice(s * ROW_TILE, (s + 1) * ROW_TILE)
        o_ref[rows, :] = _dot_nn(x_ref[rows, :].astype(BF16), wb_ref[...]).astype(o_ref.dtype)


def _matmul_cols(x, w, n_cols, tn, out_dtype, name):
    m, k = x.shape
    return pl.pallas_call(
        _matmul_kernel,
        grid=(n_cols // tn, m // TOKEN_TILE),
        in_specs=[pl.BlockSpec((TOKEN_TILE, k), lambda j, i: (i, 0)),
                  pl.BlockSpec((k, tn), lambda j, i: (0, j))],
        out_specs=pl.BlockSpec((TOKEN_TILE, tn), lambda j, i: (i, j)),
        out_shape=jax.ShapeDtypeStruct((m, n_cols), out_dtype),
        scratch_shapes=[pltpu.VMEM((k, tn), BF16)],
        compiler_params=_params("arbitrary", "arbitrary"),
        name=name,
    )(x, w)


def _gate_kernel(x_ref, wgl_ref, wg_ref, bg_ref, g_ref):
    gl = _dot_nn(x_ref[...].astype(BF16), wgl_ref[...])
    z = _dot_nn(gl.astype(BF16), wg_ref[...]) + bg_ref[...]
    log_sig = jnp.minimum(z, 0.0) - jnp.log1p(jnp.exp(-jnp.abs(z)))
    g_ref[...] = log_sig * (1.0 / GATE_NORMALIZER)


def _gla_gate(xb, w_gl, w_gate, b_gate):
    wgl = jnp.pad(w_gl, ((0, 0), (0, LANES - GATE_RANK))).astype(BF16)
    wg = jnp.pad(w_gate, ((0, LANES - GATE_RANK), (0, 0))).astype(BF16)
    return pl.pallas_call(
        _gate_kernel,
        grid=(N_TOK // TOKEN_TILE,),
        in_specs=[pl.BlockSpec((TOKEN_TILE, D_MODEL), lambda i: (i, 0)),
                  pl.BlockSpec((D_MODEL, LANES), lambda i: (0, 0)),
                  pl.BlockSpec((LANES, DK_TOTAL), lambda i: (0, 0)),
                  pl.BlockSpec((1, DK_TOTAL), lambda i: (0, 0))],
        out_specs=pl.BlockSpec((TOKEN_TILE, DK_TOTAL), lambda i: (i, 0)),
        out_shape=jax.ShapeDtypeStruct((N_TOK, DK_TOTAL), F32),
        compiler_params=_params("arbitrary"),
        name="gla_gate",
    )(xb, wgl, wg, b_gate.reshape(1, DK_TOTAL))


def _rms_gate(o, norm_g, r):
    o = o * lax.rsqrt(jnp.mean(o * o, axis=-1, keepdims=True) + RMS_EPS)
    return (o * norm_g) * _silu(r)


def _gla_prompt_kernel(q_ref, k_ref, v_ref, r_ref, g_ref, ng_ref, og_in_ref, og_ref, s_out_ref, s_ref):
    del og_in_ref
    c = pl.program_id(1)
    cc = q_ref.shape[0]

    @pl.when(c == 0)
    def _():
        s_ref[...] = jnp.zeros_like(s_ref)

    row = lax.broadcasted_iota(jnp.int32, (cc, cc), 0)
    col = lax.broadcasted_iota(jnp.int32, (cc, cc), 1)
    causal = col <= row
    tri = jnp.where(causal, 1.0, 0.0).astype(BF16)
    b = _dot_nn(jnp.concatenate([tri] * 3, axis=1),
                jnp.concatenate(_split3(g_ref[...]), axis=0))
    b_last = b[cc - 1:cc, :]
    b_mid = b[cc // 2 - 1:cc // 2, :]

    q = q_ref[...]
    k = k_ref[...] * (DK_HEAD ** -0.5)
    q_in = (q * jnp.exp(b)).astype(BF16)
    qh, qm, _ = _split3(q * jnp.exp(b - b_mid))
    kh, km, _ = _split3(k * jnp.exp(b_mid - b))
    k_d = (k * jnp.exp(b_last - b)).astype(BF16)

    rows = 2 * SUBLANES
    sub = lax.broadcasted_iota(jnp.int32, (rows, DK_TOTAL), 0)
    decay_rows = jnp.where(sub == 0, jnp.broadcast_to(jnp.exp(b_last), (rows, DK_TOTAL)), 0.0)
    decay_col = _dot_tn(jnp.concatenate(_split3(decay_rows), axis=0),
                        jnp.ones((3 * rows, LANES), BF16))

    for h in range(GLA_HEADS):
        dk = slice(h * DK_HEAD, (h + 1) * DK_HEAD)
        dv = slice(h * DV_HEAD, (h + 1) * DV_HEAD)
        q3 = jnp.concatenate([qh[:, dk], qh[:, dk], qm[:, dk]], axis=1)
        k3 = jnp.concatenate([kh[:, dk], km[:, dk], kh[:, dk]], axis=1)
        scores = jnp.where(causal, _dot_nt(q3, k3), 0.0).astype(BF16)
        v = v_ref[:, dv].astype(BF16)
        s_old = s_ref[h]
        o = _dot_nn(jnp.concatenate([q_in[:, dk], scores], axis=1),
                    jnp.concatenate([s_old.astype(BF16), v], axis=0))
        decay = jnp.concatenate([decay_col[dk, :]] * (DV_HEAD // LANES), axis=1)
        s_new = decay * s_old + _dot_tn(k_d[:, dk], v)
        s_ref[h] = s_new

        @pl.when(c == pl.num_programs(1) - 1)
        def _():
            s_out_ref[0, h] = s_new

        og_ref[:, dv] = _rms_gate(o, ng_ref[:, dv], r_ref[:, dv]).astype(BF16)


def _gla_prompt(qkvr, g, norm_g):
    nc = SEQ // GLA_CHUNK
    row = lambda b, c: b * nc + c
    return pl.pallas_call(
        _gla_prompt_kernel,
        grid=(N_PROMPT_SEQ, nc),
        in_specs=[pl.BlockSpec((GLA_CHUNK, DK_TOTAL), lambda b, c: (row(b, c), 0)),
                  pl.BlockSpec((GLA_CHUNK, DK_TOTAL), lambda b, c: (row(b, c), 1)),
                  pl.BlockSpec((GLA_CHUNK, DV_TOTAL), lambda b, c: (row(b, c), 1)),
                  pl.BlockSpec((GLA_CHUNK, DV_TOTAL), lambda b, c: (row(b, c), 2)),
                  pl.BlockSpec((GLA_CHUNK, DK_TOTAL), lambda b, c: (row(b, c), 0)),
                  pl.BlockSpec((1, DV_TOTAL), lambda b, c: (0, 0)),
                  pl.BlockSpec(memory_space=pl.ANY)],
        out_specs=[pl.BlockSpec((GLA_CHUNK, DV_TOTAL), lambda b, c: (row(b, c), 0)),
                   pl.BlockSpec((1, GLA_HEADS, DK_HEAD, DV_HEAD), lambda b, c: (b, 0, 0, 0))],
        out_shape=[jax.ShapeDtypeStruct((N_TOK, DV_TOTAL), BF16),
                   jax.ShapeDtypeStruct((N_PROMPT_SEQ, GLA_HEADS, DK_HEAD, DV_HEAD), F32)],
        scratch_shapes=[pltpu.VMEM((GLA_HEADS, DK_HEAD, DV_HEAD), F32)],
        input_output_aliases={6: 0},
        compiler_params=_params("arbitrary", "arbitrary"),
        name="gla_prompt",
    )(qkvr, qkvr, qkvr, qkvr, g, norm_g, jnp.zeros((N_TOK, DV_TOTAL), BF16))


def _gla_sample_kernel(q_ref, k_ref, v_ref, r_ref, g_ref, ng_ref, s_ref, og_in_ref, og_ref, s_out_ref,
                       o_scr):
    del og_in_ref
    bb = q_ref.shape[0]
    q = q_ref[...]
    k = k_ref[...] * (DK_HEAD ** -0.5)
    v = v_ref[...]
    eg = jnp.exp(g_ref[...])
    qe = q * eg
    qk = jnp.sum(q * k, axis=-1, keepdims=True)
    sub = lax.broadcasted_iota(jnp.int32, (bb, DK_HEAD), 0)
    for bi in range(bb):
        sel = sub == bi
        s_old = s_ref[bi, 0]
        decay = _col_bcast(jnp.where(sel, eg, 0.0), DV_HEAD)
        k_col = _col_bcast(jnp.where(sel, k, 0.0), DV_HEAD)
        qe_col = _col_bcast(jnp.where(sel, qe, 0.0), DV_HEAD)
        s_out_ref[bi, 0] = decay * s_old + k_col * v[bi:bi + 1, :]
        o_scr[bi:bi + 1, :] = jnp.sum(qe_col * s_old, axis=0, keepdims=True)
    o = qk * v + o_scr[...]
    og_ref[...] = _rms_gate(o, ng_ref[...], r_ref[...]).astype(BF16)


def _gla_sample(qkvr, g, norm_g, state, og):
    bb = GLA_SAMPLE_BATCH
    r0 = N_PROMPT // bb
    hk = DK_TOTAL // DK_HEAD
    hv = 2 * DK_TOTAL // DV_HEAD
    return pl.pallas_call(
        _gla_sample_kernel,
        grid=(N_SAMPLE // bb, GLA_HEADS),
        in_specs=[pl.BlockSpec((bb, DK_HEAD), lambda i, h: (r0 + i, h)),
                  pl.BlockSpec((bb, DK_HEAD), lambda i, h: (r0 + i, hk + h)),
                  pl.BlockSpec((bb, DV_HEAD), lambda i, h: (r0 + i, hv + h)),
                  pl.BlockSpec((bb, DV_HEAD), lambda i, h: (r0 + i, hv + GLA_HEADS + h)),
                  pl.BlockSpec((bb, DK_HEAD), lambda i, h: (r0 + i, h)),
                  pl.BlockSpec((1, DV_HEAD), lambda i, h: (0, h)),
                  pl.BlockSpec((bb, 1, DK_HEAD, DV_HEAD), lambda i, h: (i, h, 0, 0)),
                  pl.BlockSpec(memory_space=pl.ANY)],
        out_specs=[pl.BlockSpec((bb, DV_HEAD), lambda i, h: (r0 + i, h)),
                   pl.BlockSpec((bb, 1, DK_HEAD, DV_HEAD), lambda i, h: (i, h, 0, 0))],
        out_shape=[jax.ShapeDtypeStruct((N_TOK, DV_TOTAL), BF16),
                   jax.ShapeDtypeStruct((N_SAMPLE, GLA_HEADS, DK_HEAD, DV_HEAD), F32)],
        scratch_shapes=[pltpu.VMEM((bb, DV_HEAD), F32)],
        input_output_aliases={7: 0},
        compiler_params=_params("arbitrary", "arbitrary"),
        name="gla_sample",
    )(qkvr, qkvr, qkvr, qkvr, g, norm_g, state, og)


def _top2_of4(p):
    ranks = []
    for j in range(4):
        rk = jnp.zeros(p[j].shape, jnp.int32)
        for i in range(4):
            if i == j:
                continue
            beats = (p[i] >= p[j]) if i < j else (p[i] > p[j])
            rk = rk + jnp.where(beats, 1, 0)
        ranks.append(rk)

    def pick(rank):
        val = jnp.zeros(p[0].shape, F32)
        idx = jnp.zeros(p[0].shape, jnp.int32)
        for j in range(4):
            hit = ranks[j] == rank
            val = jnp.where(hit, p[j], val)
            idx = jnp.where(hit, j, idx)
        return val, idx

    v1, i1 = pick(0)
    v2, i2 = pick(1)
    return v1, i1, v2, i2


def _route(x, rw_ref, rb_ref):
    logits = _dot_nn(x.astype(BF16), rw_ref[...])
    n = x.shape[0]
    n_pad = -n % LANES
    if n_pad:
        logits = jnp.concatenate([logits, jnp.zeros((n_pad, LANES), F32)], axis=0)
    lt = logits.T[0:N_EXPERTS, 0:n] + rb_ref[...]
    e = jnp.exp(lt - jnp.max(lt, axis=0, keepdims=True))
    probs = e / jnp.sum(e, axis=0, keepdims=True)
    best = None
    for grp in range(N_GROUPS):
        rows = [probs[grp * EXPERTS_PER_GROUP + j:grp * EXPERTS_PER_GROUP + j + 1, :]
                for j in range(EXPERTS_PER_GROUP)]
        v1, i1, v2, i2 = _top2_of4(rows)
        score = v1 + v2
        cand = (score, v1, i1 + grp * EXPERTS_PER_GROUP, v2, i2 + grp * EXPERTS_PER_GROUP)
        if best is None:
            best = cand
        else:
            better = score > best[0]
            best = tuple(jnp.where(better, n, o) for n, o in zip(cand, best))
    _, v1, e1, v2, e2 = best
    denom = v1 + v2
    return (e1, e2), (v1 / denom, v2 / denom)


def _proj_norm_route_kernel(a_ref, x_ref, w_ref, lg_ref, lb_ref, rw_ref, rb_ref,
                            xo_ref, id_ref, gt_ref):
    sub = id_ref.shape[2]
    for s in range(id_ref.shape[0]):
        rows = slice(s * sub, (s + 1) * sub)
        y = _dot_nn(a_ref[rows, :], w_ref[...])
        xn = _layer_norm(ALPHA * x_ref[rows, :] + y, lg_ref[...], lb_ref[...])
        xo_ref[rows, :] = xn
        (e1, e2), (g1, g2) = _route(xn, rw_ref, rb_ref)
        id_ref[s, 0:1, :] = e1
        id_ref[s, 1:2, :] = e2
        gt_ref[s, 0:1, :] = g1
        gt_ref[s, 1:2, :] = g2


def _router_operands(router_w, router_b):
    rw = jnp.pad(router_w, ((0, 0), (0, LANES - N_EXPERTS))).astype(BF16)
    return rw, router_b.reshape(N_EXPERTS, 1)


def _proj_norm_route(a, x, w_out, ln_g, ln_b, router):
    nt = N_TOK // ROW_TILE
    n_sub = TOKEN_TILE // ROW_TILE
    rw, rb = router
    const = lambda i: (0, 0)
    x1, ids, gates = pl.pallas_call(
        _proj_norm_route_kernel,
        grid=(N_TOK // TOKEN_TILE,),
        in_specs=[pl.BlockSpec((TOKEN_TILE, D_MODEL), lambda i: (i, 0)),
                  pl.BlockSpec((TOKEN_TILE, D_MODEL), lambda i: (i, 0)),
                  pl.BlockSpec((D_MODEL, D_MODEL), const, pipeline_mode=pl.Buffered(1)),
                  pl.BlockSpec((1, D_MODEL), const),
                  pl.BlockSpec((1, D_MODEL), const),
                  pl.BlockSpec((D_MODEL, LANES), const),
                  pl.BlockSpec((N_EXPERTS, 1), const)],
        out_specs=[pl.BlockSpec((TOKEN_TILE, D_MODEL), lambda i: (i, 0)),
                   pl.BlockSpec((n_sub, TOP_K, ROW_TILE), lambda i: (i, 0, 0)),
                   pl.BlockSpec((n_sub, TOP_K, ROW_TILE), lambda i: (i, 0, 0))],
        out_shape=[jax.ShapeDtypeStruct((N_TOK, D_MODEL), F32),
                   jax.ShapeDtypeStruct((nt, TOP_K, ROW_TILE), jnp.int32),
                   jax.ShapeDtypeStruct((nt, TOP_K, ROW_TILE), F32)],
        compiler_params=_params("arbitrary"),
        name="proj_norm_route",
    )(a, x, w_out.astype(BF16), ln_g.reshape(1, D_MODEL), ln_b.reshape(1, D_MODEL), rw, rb)
    ids = ids.transpose(0, 2, 1).reshape(N_TOK, TOP_K)
    gates = gates.transpose(0, 2, 1).reshape(N_TOK, TOP_K)
    return x1, ids, gates


def _stream_expert_weights(layer, tables, w_refs, wbuf_ref, wb_refs, sem_ref):
    te_ref, nx_ref, nt_ref = tables
    i = pl.program_id(0)

    def copies(expert):
        return [pltpu.make_async_copy(w.at[layer, expert], wbuf_ref.at[j], sem_ref.at[j])
                for j, w in enumerate(w_refs)]

    @pl.when(i == 0)
    def _():
        for c in copies(te_ref[0]):
            c.start()

    first = jnp.logical_or(i == 0, te_ref[i] != te_ref[jnp.maximum(i - 1, 0)])

    @pl.when(jnp.logical_and(first, i < nt_ref[0]))
    def _():
        for c in copies(te_ref[i]):
            c.wait()
        for j, wb in enumerate(wb_refs):
            wb[...] = wbuf_ref[j].astype(BF16)

        @pl.when(nx_ref[i] >= 0)
        def _():
            for c in copies(nx_ref[i]):
                c.start()


def _issue_row_copies(idx_ref, base, src_ref, dst_ref, sem, n_rows, stride=1):
    def body(g, carry):
        r0 = pl.multiple_of(g * SUBLANES, SUBLANES)
        for k in range(SUBLANES):
            row = idx_ref[base + stride * (r0 + k)]
            pltpu.make_async_copy(src_ref.at[pl.ds(row, 1)], dst_ref.at[pl.ds(r0 + k, 1)], sem).start()
        return carry
    lax.fori_loop(0, n_rows // SUBLANES, body, 0)


def _gather_kernel(idx_ref, nt_ref, src_ref, out_ref, buf_ref, sem_ref):
    i = pl.program_id(0)
    n = nt_ref[0]
    tg = out_ref.shape[0]

    def start_tile(tile, slot):
        _issue_row_copies(idx_ref, tile * tg, src_ref, buf_ref.at[slot], sem_ref.at[slot], tg)

    @pl.when(i == 0)
    def _():
        start_tile(0, 0)

    @pl.when(i + 1 < n)
    def _():
        start_tile(i + 1, (i + 1) % 2)

    @pl.when(i < n)
    def _():
        slot = i % 2
        pltpu.make_async_copy(src_ref.at[pl.ds(0, tg)], buf_ref.at[slot], sem_ref.at[slot]).wait()
        out_ref[...] = buf_ref[slot].astype(out_ref.dtype)

    @pl.when(i >= n)
    def _():
        out_ref[...] = jnp.zeros_like(out_ref)


def _gather_rows(src, idx, n_tiles, out_dtype, name):
    m = idx.shape[0]
    width = src.shape[1]
    return pl.pallas_call(
        _gather_kernel,
        grid_spec=pltpu.PrefetchScalarGridSpec(
            num_scalar_prefetch=2,
            grid=(m // MOE_TILE,),
            in_specs=[pl.BlockSpec(memory_space=pl.ANY)],
            out_specs=pl.BlockSpec((MOE_TILE, width), lambda i, idx, nt: (i, 0)),
            scratch_shapes=[pltpu.VMEM((2, MOE_TILE, width), src.dtype),
                            pltpu.SemaphoreType.DMA((2,))]),
        out_shape=jax.ShapeDtypeStruct((m, width), out_dtype),
        compiler_params=_params("arbitrary"),
        name=name,
    )(idx, n_tiles, src)


def _moe_up_kernel(layer, te_ref, nx_ref, nt_ref, xs_ref, w1_ref, w3_ref, hid_ref,
                   wbuf_ref, w1b_ref, w3b_ref, sem_ref):
    _stream_expert_weights(layer, (te_ref, nx_ref, nt_ref), (w1_ref, w3_ref), wbuf_ref,
                           (w1b_ref, w3b_ref), sem_ref)
    i = pl.program_id(0)

    @pl.when(i < nt_ref[0])
    def _():
        xs = xs_ref[...]
        h1 = _dot_nn(xs, w1b_ref[...])
        h3 = _dot_nn(xs, w3b_ref[...])
        hid_ref[...] = (_silu(h1) * h3).astype(BF16)

    @pl.when(i >= nt_ref[0])
    def _():
        hid_ref[...] = jnp.zeros_like(hid_ref)


def _moe_up(xs, w1, w3, layer, tables):
    last = lambda i, te, nx, nt: (jnp.minimum(i, nt[0] - 1), 0)
    return pl.pallas_call(
        functools.partial(_moe_up_kernel, layer),
        grid_spec=pltpu.PrefetchScalarGridSpec(
            num_scalar_prefetch=3,
            grid=(MOE_MAX_TILES,),
            in_specs=[pl.BlockSpec((MOE_TILE, D_MODEL), last),
                      pl.BlockSpec(memory_space=pl.ANY),
                      pl.BlockSpec(memory_space=pl.ANY)],
            out_specs=pl.BlockSpec((MOE_TILE, D_FF), lambda i, te, nx, nt: (i, 0)),
            scratch_shapes=[pltpu.VMEM((2, D_MODEL, D_FF), F32),
                            pltpu.VMEM((D_MODEL, D_FF), BF16), pltpu.VMEM((D_MODEL, D_FF), BF16),
                            pltpu.SemaphoreType.DMA((2,))]),
        out_shape=jax.ShapeDtypeStruct((N_ASSIGN_PAD, D_FF), BF16),
        compiler_params=_params("arbitrary"),
        name="moe_up",
    )(*tables, xs, w1, w3)


def _moe_down_kernel(layer, te_ref, nx_ref, nt_ref, hid_ref, w2_ref, ys_ref,
                     wbuf_ref, w2b_ref, sem_ref):
    _stream_expert_weights(layer, (te_ref, nx_ref, nt_ref), (w2_ref,), wbuf_ref,
                           (w2b_ref,), sem_ref)
    i = pl.program_id(0)

    @pl.when(i < nt_ref[0])
    def _():
        ys_ref[...] = _dot_nn(hid_ref[...], w2b_ref[...])

    @pl.when(i >= nt_ref[0])
    def _():
        ys_ref[...] = jnp.zeros_like(ys_ref)


def _moe_down(hid, w2, layer, tables):
    last = lambda i, te, nx, nt: (jnp.minimum(i, nt[0] - 1), 0)
    return pl.pallas_call(
        functools.partial(_moe_down_kernel, layer),
        grid_spec=pltpu.PrefetchScalarGridSpec(
            num_scalar_prefetch=3,
            grid=(MOE_MAX_TILES,),
            in_specs=[pl.BlockSpec((MOE_TILE, D_FF), last),
                      pl.BlockSpec(memory_space=pl.ANY)],
            out_specs=pl.BlockSpec((MOE_TILE, D_MODEL), lambda i, te, nx, nt: (i, 0)),
            scratch_shapes=[pltpu.VMEM((1, D_FF, D_MODEL), F32),
                            pltpu.VMEM((D_FF, D_MODEL), BF16),
                            pltpu.SemaphoreType.DMA((1,))]),
        out_shape=jax.ShapeDtypeStruct((N_ASSIGN_PAD, D_MODEL), F32),
        compiler_params=_params("arbitrary"),
        name="moe_down",
    )(*tables, hid, w2)


def _combine_norm_kernel(final, pos_ref, x_ref, ys_ref, gt_ref, lg_ref, lb_ref, o0_ref, o1_ref,
                         buf_ref, sem_ref):
    i = pl.program_id(0)
    n = pl.num_programs(0)
    tm = x_ref.shape[0]

    def start_tile(tile, slot):
        for k in range(TOP_K):
            _issue_row_copies(pos_ref, tile * tm * TOP_K + k, ys_ref, buf_ref.at[slot, k],
                              sem_ref.at[slot], tm, stride=TOP_K)

    @pl.when(i == 0)
    def _():
        start_tile(0, 0)

    @pl.when(i + 1 < n)
    def _():
        start_tile(i + 1, (i + 1) % 2)

    slot = i % 2
    for k in range(TOP_K):
        pltpu.make_async_copy(ys_ref.at[pl.ds(0, tm)], buf_ref.at[slot, k], sem_ref.at[slot]).wait()
    gt = gt_ref[...]
    moe = buf_ref[slot, 0] * gt[:, 0:1] + buf_ref[slot, 1] * gt[:, 1:2]
    xn = _layer_norm(ALPHA * x_ref[...] + moe, lg_ref[...], lb_ref[...])
    if final:
        @pl.when(i < n - 1)
        def _():
            o0_ref[...] = xn

        @pl.when(i == n - 1)
        def _():
            o1_ref[...] = xn
    else:
        o0_ref[...] = xn
        o1_ref[...] = xn.astype(BF16)


def _combine_norm(x, ys, pos, gates, ln_g, ln_b, final):
    tm = N_SAMPLE if final else ROW_TILE
    const = lambda i, pos: (0, 0)
    tile = lambda i, pos: (i, 0)
    if final:
        n_prompt_tiles = N_PROMPT // tm
        out_specs = [pl.BlockSpec((tm, D_MODEL), lambda i, pos: (jnp.minimum(i, n_prompt_tiles - 1), 0)),
                     pl.BlockSpec((tm, D_MODEL), const)]
        out_shape = [jax.ShapeDtypeStruct((N_PROMPT, D_MODEL), F32),
                     jax.ShapeDtypeStruct((N_SAMPLE, D_MODEL), F32)]
    else:
        out_specs = [pl.BlockSpec((tm, D_MODEL), tile), pl.BlockSpec((tm, D_MODEL), tile)]
        out_shape = [jax.ShapeDtypeStruct((N_TOK, D_MODEL), F32),
                     jax.ShapeDtypeStruct((N_TOK, D_MODEL), BF16)]
    return pl.pallas_call(
        functools.partial(_combine_norm_kernel, final),
        grid_spec=pltpu.PrefetchScalarGridSpec(
            num_scalar_prefetch=1,
            grid=(N_TOK // tm,),
            in_specs=[pl.BlockSpec((tm, D_MODEL), tile),
                      pl.BlockSpec(memory_space=pl.ANY),
                      pl.BlockSpec((tm, TOP_K), tile),
                      pl.BlockSpec((1, D_MODEL), const),
                      pl.BlockSpec((1, D_MODEL), const)],
            out_specs=out_specs,
            scratch_shapes=[pltpu.VMEM((2, TOP_K, tm, D_MODEL), F32),
                            pltpu.SemaphoreType.DMA((2,))]),
        out_shape=out_shape,
        compiler_params=_params("arbitrary"),
        name="combine_norm",
    )(pos, x, ys, gates, ln_g.reshape(1, D_MODEL), ln_b.reshape(1, D_MODEL))


def _routing_tables(ids):
    e = ids.reshape(-1)
    onehot = (e[:, None] == jnp.arange(N_EXPERTS, dtype=jnp.int32)[None, :]).astype(jnp.int32)
    csum = jnp.cumsum(onehot, axis=0)
    rank = jnp.sum(onehot * csum, axis=1) - 1
    sizes = csum[-1]
    tiles_per = (sizes + MOE_TILE - 1) // MOE_TILE
    tile_end = jnp.cumsum(tiles_per)
    tile_start = tile_end - tiles_per
    n_tiles = tile_end[-1]
    pos = (jnp.sum(onehot * (tile_start * MOE_TILE)[None, :], axis=1) + rank).astype(jnp.int32)
    tile = jnp.arange(MOE_MAX_TILES, dtype=jnp.int32)
    owner = jnp.sum((tile_end[None, :] <= jnp.minimum(tile, n_tiles - 1)[:, None]).astype(jnp.int32), axis=1)
    order = jnp.argsort(e, stable=True).astype(jnp.int32)
    owner_hot = (owner[:, None] == jnp.arange(N_EXPERTS, dtype=jnp.int32)[None, :]).astype(jnp.int32)
    size_start = jnp.cumsum(sizes) - sizes
    tile_first = jnp.sum(owner_hot * (size_start - tile_start * MOE_TILE)[None, :], axis=1)
    tile_limit = jnp.sum(owner_hot * (size_start + sizes)[None, :], axis=1)
    row = jnp.arange(N_ASSIGN_PAD, dtype=jnp.int32).reshape(MOE_MAX_TILES, MOE_TILE)
    src = tile_first[:, None] + row
    valid = src < tile_limit[:, None]
    row_token = jnp.where(valid, order[jnp.clip(src, 0, N_ASSIGN - 1)] // TOP_K, row % N_TOK).reshape(-1)
    experts = jnp.arange(N_EXPERTS, dtype=jnp.int32)
    later = lax.cummin(jnp.where(tiles_per > 0, experts, N_EXPERTS), reverse=True)
    next_expert = jnp.concatenate([later[1:], jnp.full((1,), N_EXPERTS, jnp.int32)])
    next_expert = jnp.where(next_expert >= N_EXPERTS, -1, next_expert)
    tile_next = jnp.sum(owner_hot * next_expert[None, :], axis=1)
    n_tiles = n_tiles.reshape(1).astype(jnp.int32)
    tables = (owner.astype(jnp.int32), tile_next.astype(jnp.int32), n_tiles)
    return pos, row_token.astype(jnp.int32), tables


def _moe_block(x, ids, gates, w1, w3, w2, layer, ln_g, ln_b, final):
    pos, row_token, tables = _routing_tables(ids)
    xs = _gather_rows(x, row_token, tables[2], BF16, "moe_gather")
    hid = _moe_up(xs, w1, w3, layer, tables)
    ys = _moe_down(hid, w2, layer, tables)
    return _combine_norm(x, ys, pos, gates, ln_g, ln_b, final)


def _conv_in_kernel(x_ref, wb_ref, wc_ref, wh_ref, cw_ref, s0_ref, s1_ref, gbz_ref, u_ref,
                    wbb_ref, wcb_ref, whb_ref, ubuf_ref):
    i = pl.program_id(1)
    tm = x_ref.shape[0]
    pad = SUBLANES

    @pl.when(i == 0)
    def _():
        wbb_ref[...] = wb_ref[...].astype(BF16)
        wcb_ref[...] = wc_ref[...].astype(BF16)
        whb_ref[...] = wh_ref[...].astype(BF16)
        ubuf_ref[0:pad, :] = jnp.zeros((pad, ubuf_ref.shape[1]), F32)

    w0 = cw_ref[0:1, :]
    w1 = cw_ref[1:2, :]
    w2 = cw_ref[2:3, :]
    sub = ROW_TILE
    for s in range(tm // sub):
        lo, hi = s * sub, (s + 1) * sub
        x = x_ref[lo:hi, :]
        gb = _dot_nn(x, wbb_ref[...])
        u = _dot_nn(x, wcb_ref[...]) * _dot_nn(x, whb_ref[...])
        u_ref[lo:hi, :] = u
        ubuf_ref[pad + lo:pad + hi, :] = u
        t = (i * tm + lo + lax.broadcasted_iota(jnp.int32, (sub, 1), 0)) & (SEQ - 1)
        u_m1 = jnp.where(t >= 1, ubuf_ref[pad - 1 + lo:pad - 1 + hi, :], 0.0)
        u_m2 = jnp.where(t >= 2, ubuf_ref[pad - 2 + lo:pad - 2 + hi, :], 0.0)
        z = u_m2 * w0 + u_m1 * w1 + u * w2
        gbz_ref[lo:hi, :] = (gb * z).astype(BF16)
    ubuf_ref[0:pad, :] = ubuf_ref[tm:tm + pad, :]

    @pl.when(i == pl.num_programs(1) - 1)
    def _():
        first = sub - N_SAMPLE
        z_s = s0_ref[...] * w0 + s1_ref[...] * w1 + u[first:sub, :] * w2
        gbz_ref[tm - N_SAMPLE:tm, :] = (gb[first:sub, :] * z_s).astype(BF16)


def _conv_in(xb, w_in, conv_w, state):
    tn = 512
    nb = D_MODEL // tn
    s0 = state[:, 0, :]
    s1 = state[:, 1, :]
    return pl.pallas_call(
        _conv_in_kernel,
        grid=(nb, N_TOK // TOKEN_TILE),
        in_specs=[pl.BlockSpec((TOKEN_TILE, D_MODEL), lambda j, i: (i, 0)),
                  pl.BlockSpec((D_MODEL, tn), lambda j, i: (0, j)),
                  pl.BlockSpec((D_MODEL, tn), lambda j, i: (0, nb + j)),
                  pl.BlockSpec((D_MODEL, tn), lambda j, i: (0, 2 * nb + j)),
                  pl.BlockSpec((CONV_W, tn), lambda j, i: (0, j)),
                  pl.BlockSpec((N_SAMPLE, tn), lambda j, i: (0, j)),
                  pl.BlockSpec((N_SAMPLE, tn), lambda j, i: (0, j))],
        out_specs=[pl.BlockSpec((TOKEN_TILE, tn), lambda j, i: (i, j)),
                   pl.BlockSpec((TOKEN_TILE, tn), lambda j, i: (i, j))],
        out_shape=[jax.ShapeDtypeStruct((N_TOK, D_MODEL), BF16),
                   jax.ShapeDtypeStruct((N_TOK, D_MODEL), F32)],
        scratch_shapes=[pltpu.VMEM((D_MODEL, tn), BF16), pltpu.VMEM((D_MODEL, tn), BF16),
                        pltpu.VMEM((D_MODEL, tn), BF16),
                        pltpu.VMEM((TOKEN_TILE + 2 * SUBLANES, tn), F32)],
        compiler_params=_params("arbitrary", "arbitrary"),
        name="conv_in",
    )(xb, w_in, w_in, w_in, conv_w, s0, s1)


def kernel(x_prompt, x_sample, state_gla, state_conv, router_w, router_b, gla_w_in, gla_w_gate, gla_b_gate,
           gla_norm_g, gla_w_out, conv_w_in, conv_w, conv_w_out, ln_mix_g, ln_mix_b, ln_ffn_g, ln_ffn_b,
           moe_w1, moe_w3, moe_w2):
    x0 = jnp.concatenate([x_prompt.reshape(N_PROMPT, D_MODEL), x_sample.reshape(N_SAMPLE, D_MODEL)], axis=0)
    router = _router_operands(router_w, router_b)

    qkvr = _matmul_cols(x0, gla_w_in[0], GLA_QKVR_WIDTH, 1024, F32, "gla_in_proj")
    g = _gla_gate(x0, gla_w_in[0][:, GLA_QKVR_WIDTH:], gla_w_gate[0], gla_b_gate[0])
    norm_g = gla_norm_g[0].reshape(1, DV_TOTAL)
    og, s_prompt = _gla_prompt(qkvr, g, norm_g)
    og, s_sample = _gla_sample(qkvr, g, norm_g, state_gla[0], og)
    x1, ids, gates = _proj_norm_route(og, x0, gla_w_out[0], ln_mix_g[0], ln_mix_b[0], router)
    x2, x2b = _moe_block(x1, ids, gates, moe_w1, moe_w3, moe_w2, 0, ln_ffn_g[0], ln_ffn_b[0], False)

    gbz, u = _conv_in(x2b, conv_w_in[0], conv_w[0], state_conv[0])
    x3, ids, gates = _proj_norm_route(gbz, x2, conv_w_out[0], ln_mix_g[1], ln_mix_b[1], router)
    y_p, y_s = _moe_block(x3, ids, gates, moe_w1, moe_w3, moe_w2, 1, ln_ffn_g[1], ln_ffn_b[1], True)

    y_prompt = y_p.reshape(N_PROMPT_SEQ, SEQ, D_MODEL)
    y_sample = y_s.reshape(N_SAMPLE, 1, D_MODEL)
    conv_prompt = jnp.stack([u[(b + 1) * SEQ - (CONV_W - 1):(b + 1) * SEQ] for b in range(N_PROMPT_SEQ)])
    conv_sample = jnp.concatenate([state_conv[0][:, 1:, :], u[N_PROMPT:][:, None, :]], axis=1)
    return (y_prompt, y_sample, s_prompt[None], conv_prompt[None], s_sample[None], conv_sample[None])
```

```python
import functools

import jax
import jax.numpy as jnp
from jax import lax
from jax.experimental import pallas as pl
from jax.experimental.pallas import tpu as pltpu

F32 = jnp.float32
BF16 = jnp.bfloat16

D_MODEL = 2048
N_PROMPT_SEQ = 4
SEQ = 2048
N_PROMPT = N_PROMPT_SEQ * SEQ
N_SAMPLE = 128
N_TOK = N_PROMPT + N_SAMPLE
DEPTH = 2

GLA_HEADS = 4
DK_TOTAL = D_MODEL // 2
DV_TOTAL = D_MODEL
DK_HEAD = DK_TOTAL // GLA_HEADS
DV_HEAD = DV_TOTAL // GLA_HEADS
GATE_RANK = 16
GATE_NORMALIZER = 16.0
GLA_QKVR_WIDTH = 2 * DK_TOTAL + 2 * DV_TOTAL
CONV_W = 3
N_EXPERTS = 16
N_GROUPS = 4
EXPERTS_PER_GROUP = N_EXPERTS // N_GROUPS
TOP_K = 2
D_FF = D_MODEL // 2
ALPHA = (2.0 * DEPTH) ** 0.25
LN_EPS = 1e-5
RMS_EPS = 1e-6

LANES = 128
SUBLANES = 8
VMEM_LIMIT_BYTES = 56 * 1024 * 1024

TOKEN_TILE = 640
ROW_TILE = 320
GLA_CHUNK = 64
GLA_SAMPLE_BATCH = 16
MOE_TILE = 256
N_ASSIGN = N_TOK * TOP_K
MOE_MAX_TILES = N_ASSIGN // MOE_TILE + N_EXPERTS
N_ASSIGN_PAD = MOE_MAX_TILES * MOE_TILE


def _params(*semantics):
    return pltpu.CompilerParams(dimension_semantics=semantics, vmem_limit_bytes=VMEM_LIMIT_BYTES)


def _split3(x):
    hi = x.astype(BF16)
    r1 = x - hi.astype(F32)
    mid = r1.astype(BF16)
    lo = (r1 - mid.astype(F32)).astype(BF16)
    return hi, mid, lo


def _dot_nn(a, b):
    return jnp.dot(a, b, preferred_element_type=F32)


def _dot_tn(a, b):
    return lax.dot_general(a, b, (((0,), (0,)), ((), ())), preferred_element_type=F32)


def _dot_nt(a, b):
    return lax.dot_general(a, b, (((1,), (1,)), ((), ())), preferred_element_type=F32)


def _col_bcast(rows, n):
    ones = jnp.ones((rows.shape[0], LANES), BF16)
    hi, mid, lo = _split3(rows)
    col = _dot_tn(hi, ones) + _dot_tn(mid, ones) + _dot_tn(lo, ones)
    return jnp.concatenate([col] * (n // LANES), axis=1)


def _silu(r):
    return r / (1.0 + jnp.exp(-r))


def _layer_norm(h, g, b):
    mu = jnp.mean(h, axis=-1, keepdims=True)
    d = h - mu
    var = jnp.mean(d * d, axis=-1, keepdims=True)
    return d * lax.rsqrt(var + LN_EPS) * g + b


def _matmul_kernel(x_ref, wt_ref, o_ref, wb_ref):
    @pl.when(pl.program_id(1) == 0)
    def _():
        wb_ref[...] = wt_ref[...].astype(BF16)

    for s in range(TOKEN_TILE // ROW_TILE):
        rows = slice(s * ROW_TILE, (s + 1) * ROW_TILE)
        o_ref[rows, :] = _dot_nt(x_ref[rows, :].astype(BF16), wb_ref[...]).astype(o_ref.dtype)


def _matmul_cols(x, w_t, n_cols, tn, out_dtype, name):
    m, k = x.shape
    return pl.pallas_call(
        _matmul_kernel,
        grid=(n_cols // tn, m // TOKEN_TILE),
        in_specs=[pl.BlockSpec((TOKEN_TILE, k), lambda j, i: (i, 0)),
                  pl.BlockSpec((tn, k), lambda j, i: (j, 0))],
        out_specs=pl.BlockSpec((TOKEN_TILE, tn), lambda j, i: (i, j)),
        out_shape=jax.ShapeDtypeStruct((m, n_cols), out_dtype),
        scratch_shapes=[pltpu.VMEM((tn, k), BF16)],
        compiler_params=_params("arbitrary", "arbitrary"),
        name=name,
    )(x, w_t)


def _gate_kernel(x_ref, wgl_ref, wg_ref, bg_ref, g_ref):
    wgl = jnp.concatenate([wgl_ref[...].astype(BF16), jnp.zeros((LANES - GATE_RANK, D_MODEL), BF16)], axis=0)
    gl = _dot_nt(x_ref[...].astype(BF16), wgl)
    z = _dot_nn(gl.astype(BF16), wg_ref[...]) + bg_ref[...]
    log_sig = jnp.minimum(z, 0.0) - jnp.log1p(jnp.exp(-jnp.abs(z)))
    g_ref[...] = log_sig * (1.0 / GATE_NORMALIZER)


def _gla_gate(x, w_in_t, w_gate, b_gate):
    wg = jnp.pad(w_gate, ((0, LANES - GATE_RANK), (0, 0))).astype(BF16)
    return pl.pallas_call(
        _gate_kernel,
        grid=(N_TOK // TOKEN_TILE,),
        in_specs=[pl.BlockSpec((TOKEN_TILE, D_MODEL), lambda i: (i, 0)),
                  pl.BlockSpec((GATE_RANK, D_MODEL), lambda i: (GLA_QKVR_WIDTH // GATE_RANK, 0)),
                  pl.BlockSpec((LANES, DK_TOTAL), lambda i: (0, 0)),
                  pl.BlockSpec((1, DK_TOTAL), lambda i: (0, 0))],
        out_specs=pl.BlockSpec((TOKEN_TILE, DK_TOTAL), lambda i: (i, 0)),
        out_shape=jax.ShapeDtypeStruct((N_TOK, DK_TOTAL), F32),
        compiler_params=_params("arbitrary"),
        name="gla_gate",
    )(x, w_in_t, wg, b_gate.reshape(1, DK_TOTAL))


def _rms_gate(o, norm_g, r):
    o = o * lax.rsqrt(jnp.mean(o * o, axis=-1, keepdims=True) + RMS_EPS)
    return (o * norm_g) * _silu(r)


def _gla_prompt_kernel(q_ref, k_ref, v_ref, r_ref, g_ref, ng_ref, og_in_ref, og_ref, s_out_ref, s_ref):
    del og_in_ref
    c = pl.program_id(1)
    cc = q_ref.shape[0]

    @pl.when(c == 0)
    def _():
        s_ref[...] = jnp.zeros_like(s_ref)

    row = lax.broadcasted_iota(jnp.int32, (cc, cc), 0)
    col = lax.broadcasted_iota(jnp.int32, (cc, cc), 1)
    causal = col <= row
    tri = jnp.where(causal, 1.0, 0.0).astype(BF16)
    b = _dot_nn(jnp.concatenate([tri] * 3, axis=1),
                jnp.concatenate(_split3(g_ref[...]), axis=0))
    b_last = b[cc - 1:cc, :]
    b_mid = b[cc // 2 - 1:cc // 2, :]

    q = q_ref[...]
    k = k_ref[...] * (DK_HEAD ** -0.5)
    q_in = (q * jnp.exp(b)).astype(BF16)
    qh, qm, _ = _split3(q * jnp.exp(b - b_mid))
    kh, km, _ = _split3(k * jnp.exp(b_mid - b))
    k_d = (k * jnp.exp(b_last - b)).astype(BF16)

    rows = 2 * SUBLANES
    sub = lax.broadcasted_iota(jnp.int32, (rows, DK_TOTAL), 0)
    decay_rows = jnp.where(sub == 0, jnp.broadcast_to(jnp.exp(b_last), (rows, DK_TOTAL)), 0.0)
    decay_col = _dot_tn(jnp.concatenate(_split3(decay_rows), axis=0),
                        jnp.ones((3 * rows, LANES), BF16))

    for h in range(GLA_HEADS):
        dk = slice(h * DK_HEAD, (h + 1) * DK_HEAD)
        dv = slice(h * DV_HEAD, (h + 1) * DV_HEAD)
        q3 = jnp.concatenate([qh[:, dk], qh[:, dk], qm[:, dk]], axis=1)
        k3 = jnp.concatenate([kh[:, dk], km[:, dk], kh[:, dk]], axis=1)
        scores = jnp.where(causal, _dot_nt(q3, k3), 0.0).astype(BF16)
        v = v_ref[:, dv].astype(BF16)
        s_old = s_ref[h]
        o = _dot_nn(jnp.concatenate([q_in[:, dk], scores], axis=1),
                    jnp.concatenate([s_old.astype(BF16), v], axis=0))
        decay = jnp.concatenate([decay_col[dk, :]] * (DV_HEAD // LANES), axis=1)
        s_new = decay * s_old + _dot_tn(k_d[:, dk], v)
        s_ref[h] = s_new

        @pl.when(c == pl.num_programs(1) - 1)
        def _():
            s_out_ref[0, h] = s_new

        og_ref[:, dv] = _rms_gate(o, ng_ref[:, dv], r_ref[:, dv]).astype(BF16)


def _gla_prompt(qkvr, g, norm_g):
    nc = SEQ // GLA_CHUNK
    row = lambda b, c: b * nc + c
    return pl.pallas_call(
        _gla_prompt_kernel,
        grid=(N_PROMPT_SEQ, nc),
        in_specs=[pl.BlockSpec((GLA_CHUNK, DK_TOTAL), lambda b, c: (row(b, c), 0)),
                  pl.BlockSpec((GLA_CHUNK, DK_TOTAL), lambda b, c: (row(b, c), 1)),
                  pl.BlockSpec((GLA_CHUNK, DV_TOTAL), lambda b, c: (row(b, c), 1)),
                  pl.BlockSpec((GLA_CHUNK, DV_TOTAL), lambda b, c: (row(b, c), 2)),
                  pl.BlockSpec((GLA_CHUNK, DK_TOTAL), lambda b, c: (row(b, c), 0)),
                  pl.BlockSpec((1, DV_TOTAL), lambda b, c: (0, 0)),
                  pl.BlockSpec(memory_space=pl.ANY)],
        out_specs=[pl.BlockSpec((GLA_CHUNK, DV_TOTAL), lambda b, c: (row(b, c), 0)),
                   pl.BlockSpec((1, GLA_HEADS, DK_HEAD, DV_HEAD), lambda b, c: (b, 0, 0, 0))],
        out_shape=[jax.ShapeDtypeStruct((N_TOK, DV_TOTAL), BF16),
                   jax.ShapeDtypeStruct((N_PROMPT_SEQ, GLA_HEADS, DK_HEAD, DV_HEAD), F32)],
        scratch_shapes=[pltpu.VMEM((GLA_HEADS, DK_HEAD, DV_HEAD), F32)],
        input_output_aliases={6: 0},
        compiler_params=_params("arbitrary", "arbitrary"),
        name="gla_prompt",
    )(qkvr, qkvr, qkvr, qkvr, g, norm_g, jnp.zeros((N_TOK, DV_TOTAL), BF16))


def _gla_sample_kernel(q_ref, k_ref, v_ref, r_ref, g_ref, ng_ref, s_ref, og_in_ref, og_ref, s_out_ref,
                       o_scr):
    del og_in_ref
    bb = q_ref.shape[0]
    q = q_ref[...]
    k = k_ref[...] * (DK_HEAD ** -0.5)
    v = v_ref[...]
    eg = jnp.exp(g_ref[...])
    qe = q * eg
    qk = jnp.sum(q * k, axis=-1, keepdims=True)
    sub = lax.broadcasted_iota(jnp.int32, (bb, DK_HEAD), 0)
    for bi in range(bb):
        sel = sub == bi
        s_old = s_ref[bi, 0]
        decay = _col_bcast(jnp.where(sel, eg, 0.0), DV_HEAD)
        k_col = _col_bcast(jnp.where(sel, k, 0.0), DV_HEAD)
        qe_col = _col_bcast(jnp.where(sel, qe, 0.0), DV_HEAD)
        s_out_ref[bi, 0] = decay * s_old + k_col * v[bi:bi + 1, :]
        o_scr[bi:bi + 1, :] = jnp.sum(qe_col * s_old, axis=0, keepdims=True)
    o = qk * v + o_scr[...]
    og_ref[...] = _rms_gate(o, ng_ref[...], r_ref[...]).astype(BF16)


def _gla_sample(qkvr, g, norm_g, state, og):
    bb = GLA_SAMPLE_BATCH
    r0 = N_PROMPT // bb
    hk = DK_TOTAL // DK_HEAD
    hv = 2 * DK_TOTAL // DV_HEAD
    return pl.pallas_call(
        _gla_sample_kernel,
        grid=(N_SAMPLE // bb, GLA_HEADS),
        in_specs=[pl.BlockSpec((bb, DK_HEAD), lambda i, h: (r0 + i, h)),
                  pl.BlockSpec((bb, DK_HEAD), lambda i, h: (r0 + i, hk + h)),
                  pl.BlockSpec((bb, DV_HEAD), lambda i, h: (r0 + i, hv + h)),
                  pl.BlockSpec((bb, DV_HEAD), lambda i, h: (r0 + i, hv + GLA_HEADS + h)),
                  pl.BlockSpec((bb, DK_HEAD), lambda i, h: (r0 + i, h)),
                  pl.BlockSpec((1, DV_HEAD), lambda i, h: (0, h)),
                  pl.BlockSpec((bb, 1, DK_HEAD, DV_HEAD), lambda i, h: (i, h, 0, 0)),
                  pl.BlockSpec(memory_space=pl.ANY)],
        out_specs=[pl.BlockSpec((bb, DV_HEAD), lambda i, h: (r0 + i, h)),
                   pl.BlockSpec((bb, 1, DK_HEAD, DV_HEAD), lambda i, h: (i, h, 0, 0))],
        out_shape=[jax.ShapeDtypeStruct((N_TOK, DV_TOTAL), BF16),
                   jax.ShapeDtypeStruct((N_SAMPLE, GLA_HEADS, DK_HEAD, DV_HEAD), F32)],
        scratch_shapes=[pltpu.VMEM((bb, DV_HEAD), F32)],
        input_output_aliases={7: 0},
        compiler_params=_params("arbitrary", "arbitrary"),
        name="gla_sample",
    )(qkvr, qkvr, qkvr, qkvr, g, norm_g, state, og)


def _top2_of4(p):
    ranks = []
    for j in range(4):
        rk = jnp.zeros(p[j].shape, jnp.int32)
        for i in range(4):
            if i == j:
                continue
            beats = (p[i] >= p[j]) if i < j else (p[i] > p[j])
            rk = rk + jnp.where(beats, 1, 0)
        ranks.append(rk)

    def pick(rank):
        val = jnp.zeros(p[0].shape, F32)
        idx = jnp.zeros(p[0].shape, jnp.int32)
        for j in range(4):
            hit = ranks[j] == rank
            val = jnp.where(hit, p[j], val)
            idx = jnp.where(hit, j, idx)
        return val, idx

    v1, i1 = pick(0)
    v2, i2 = pick(1)
    return v1, i1, v2, i2


def _route(x, rw_ref, rb_ref):
    logits = _dot_nn(x.astype(BF16), rw_ref[...])
    n = x.shape[0]
    n_pad = -n % LANES
    if n_pad:
        logits = jnp.concatenate([logits, jnp.zeros((n_pad, LANES), F32)], axis=0)
    lt = logits.T[0:N_EXPERTS, 0:n] + rb_ref[...]
    e = jnp.exp(lt - jnp.max(lt, axis=0, keepdims=True))
    probs = e / jnp.sum(e, axis=0, keepdims=True)
    best = None
    for grp in range(N_GROUPS):
        rows = [probs[grp * EXPERTS_PER_GROUP + j:grp * EXPERTS_PER_GROUP + j + 1, :]
                for j in range(EXPERTS_PER_GROUP)]
        v1, i1, v2, i2 = _top2_of4(rows)
        score = v1 + v2
        cand = (score, v1, i1 + grp * EXPERTS_PER_GROUP, v2, i2 + grp * EXPERTS_PER_GROUP)
        if best is None:
            best = cand
        else:
            better = score > best[0]
            best = tuple(jnp.where(better, n, o) for n, o in zip(cand, best))
    _, v1, e1, v2, e2 = best
    denom = v1 + v2
    return (e1, e2), (v1 / denom, v2 / denom)


def _proj_norm_route_kernel(a_ref, x_ref, w_ref, lg_ref, lb_ref, rw_ref, rb_ref,
                            xo_ref, id_ref, gt_ref):
    sub = id_ref.shape[2]
    for s in range(id_ref.shape[0]):
        rows = slice(s * sub, (s + 1) * sub)
        y = _dot_nn(a_ref[rows, :], w_ref[...])
        xn = _layer_norm(ALPHA * x_ref[rows, :] + y, lg_ref[...], lb_ref[...])
        xo_ref[rows, :] = xn
        (e1, e2), (g1, g2) = _route(xn, rw_ref, rb_ref)
        id_ref[s, 0:1, :] = e1
        id_ref[s, 1:2, :] = e2
        gt_ref[s, 0:1, :] = g1
        gt_ref[s, 1:2, :] = g2


def _router_operands(router_w, router_b):
    rw = jnp.pad(router_w, ((0, 0), (0, LANES - N_EXPERTS))).astype(BF16)
    return rw, router_b.reshape(N_EXPERTS, 1)


def _proj_norm_route(a, x, w_out, ln_g, ln_b, router):
    nt = N_TOK // ROW_TILE
    n_sub = TOKEN_TILE // ROW_TILE
    rw, rb = router
    const = lambda i: (0, 0)
    x1, ids, gates = pl.pallas_call(
        _proj_norm_route_kernel,
        grid=(N_TOK // TOKEN_TILE,),
        in_specs=[pl.BlockSpec((TOKEN_TILE, D_MODEL), lambda i: (i, 0)),
                  pl.BlockSpec((TOKEN_TILE, D_MODEL), lambda i: (i, 0)),
                  pl.BlockSpec((D_MODEL, D_MODEL), const, pipeline_mode=pl.Buffered(1)),
                  pl.BlockSpec((1, D_MODEL), const),
                  pl.BlockSpec((1, D_MODEL), const),
                  pl.BlockSpec((D_MODEL, LANES), const),
                  pl.BlockSpec((N_EXPERTS, 1), const)],
        out_specs=[pl.BlockSpec((TOKEN_TILE, D_MODEL), lambda i: (i, 0)),
                   pl.BlockSpec((n_sub, TOP_K, ROW_TILE), lambda i: (i, 0, 0)),
                   pl.BlockSpec((n_sub, TOP_K, ROW_TILE), lambda i: (i, 0, 0))],
        out_shape=[jax.ShapeDtypeStruct((N_TOK, D_MODEL), F32),
                   jax.ShapeDtypeStruct((nt, TOP_K, ROW_TILE), jnp.int32),
                   jax.ShapeDtypeStruct((nt, TOP_K, ROW_TILE), F32)],
        compiler_params=_params("arbitrary"),
        name="proj_norm_route",
    )(a, x, w_out.astype(BF16), ln_g.reshape(1, D_MODEL), ln_b.reshape(1, D_MODEL), rw, rb)
    ids = ids.transpose(0, 2, 1).reshape(N_TOK, TOP_K)
    gates = gates.transpose(0, 2, 1).reshape(N_TOK, TOP_K)
    return x1, ids, gates


def _stream_expert_weights(layer, tables, w_refs, wbuf_ref, wb_refs, sem_ref):
    te_ref, nx_ref, nt_ref = tables
    i = pl.program_id(0)

    def copies(expert):
        return [pltpu.make_async_copy(w.at[layer, expert], wbuf_ref.at[j], sem_ref.at[j])
                for j, w in enumerate(w_refs)]

    @pl.when(i == 0)
    def _():
        for c in copies(te_ref[0]):
            c.start()

    first = jnp.logical_or(i == 0, te_ref[i] != te_ref[jnp.maximum(i - 1, 0)])

    @pl.when(jnp.logical_and(first, i < nt_ref[0]))
    def _():
        for c in copies(te_ref[i]):
            c.wait()
        for j, wb in enumerate(wb_refs):
            wb[...] = wbuf_ref[j].astype(BF16)

        @pl.when(nx_ref[i] >= 0)
        def _():
            for c in copies(nx_ref[i]):
                c.start()


def _issue_row_copies(idx_ref, base, src_ref, dst_ref, sem, n_rows, stride=1):
    def body(g, carry):
        r0 = pl.multiple_of(g * SUBLANES, SUBLANES)
        for k in range(SUBLANES):
            row = idx_ref[base + stride * (r0 + k)]
            pltpu.make_async_copy(src_ref.at[pl.ds(row, 1)], dst_ref.at[pl.ds(r0 + k, 1)],
                                  sem).start(priority=k % 2)
        return carry
    lax.fori_loop(0, n_rows // SUBLANES, body, 0)


def _gather_kernel(idx_ref, nt_ref, src_ref, out_ref, buf_ref, sem_ref):
    i = pl.program_id(0)
    n = nt_ref[0]
    tg = out_ref.shape[0]

    def start_tile(tile, slot):
        _issue_row_copies(idx_ref, tile * tg, src_ref, buf_ref.at[slot], sem_ref.at[slot], tg)

    @pl.when(i == 0)
    def _():
        start_tile(0, 0)

    @pl.when(i + 1 < n)
    def _():
        start_tile(i + 1, (i + 1) % 2)

    @pl.when(i < n)
    def _():
        slot = i % 2
        pltpu.make_async_copy(src_ref.at[pl.ds(0, tg)], buf_ref.at[slot], sem_ref.at[slot]).wait()
        out_ref[...] = buf_ref[slot].astype(out_ref.dtype)

    @pl.when(i >= n)
    def _():
        out_ref[...] = jnp.zeros_like(out_ref)


def _gather_rows(src, idx, n_tiles, out_dtype, name):
    m = idx.shape[0]
    width = src.shape[1]
    return pl.pallas_call(
        _gather_kernel,
        grid_spec=pltpu.PrefetchScalarGridSpec(
            num_scalar_prefetch=2,
            grid=(m // MOE_TILE,),
            in_specs=[pl.BlockSpec(memory_space=pl.ANY)],
            out_specs=pl.BlockSpec((MOE_TILE, width), lambda i, idx, nt: (i, 0)),
            scratch_shapes=[pltpu.VMEM((2, MOE_TILE, width), src.dtype),
                            pltpu.SemaphoreType.DMA((2,))]),
        out_shape=jax.ShapeDtypeStruct((m, width), out_dtype),
        compiler_params=_params("arbitrary"),
        name=name,
    )(idx, n_tiles, src)


def _moe_up_kernel(layer, te_ref, nx_ref, nt_ref, xs_ref, w1_ref, w3_ref, hid_ref,
                   wbuf_ref, w1b_ref, w3b_ref, sem_ref):
    _stream_expert_weights(layer, (te_ref, nx_ref, nt_ref), (w1_ref, w3_ref), wbuf_ref,
                           (w1b_ref, w3b_ref), sem_ref)
    i = pl.program_id(0)

    @pl.when(i < nt_ref[0])
    def _():
        xs = xs_ref[...]
        h1 = _dot_nn(xs, w1b_ref[...])
        h3 = _dot_nn(xs, w3b_ref[...])
        hid_ref[...] = (_silu(h1) * h3).astype(BF16)

    @pl.when(i >= nt_ref[0])
    def _():
        hid_ref[...] = jnp.zeros_like(hid_ref)


def _moe_up(xs, w1, w3, layer, tables):
    last = lambda i, te, nx, nt: (jnp.minimum(i, nt[0] - 1), 0)
    return pl.pallas_call(
        functools.partial(_moe_up_kernel, layer),
        grid_spec=pltpu.PrefetchScalarGridSpec(
            num_scalar_prefetch=3,
            grid=(MOE_MAX_TILES,),
            in_specs=[pl.BlockSpec((MOE_TILE, D_MODEL), last),
                      pl.BlockSpec(memory_space=pl.ANY),
                      pl.BlockSpec(memory_space=pl.ANY)],
            out_specs=pl.BlockSpec((MOE_TILE, D_FF), lambda i, te, nx, nt: (i, 0)),
            scratch_shapes=[pltpu.VMEM((2, D_MODEL, D_FF), F32),
                            pltpu.VMEM((D_MODEL, D_FF), BF16), pltpu.VMEM((D_MODEL, D_FF), BF16),
                            pltpu.SemaphoreType.DMA((2,))]),
        out_shape=jax.ShapeDtypeStruct((N_ASSIGN_PAD, D_FF), BF16),
        compiler_params=_params("arbitrary"),
        name="moe_up",
    )(*tables, xs, w1, w3)


def _moe_down_kernel(layer, te_ref, nx_ref, nt_ref, hid_ref, w2_ref, ys_ref,
                     wbuf_ref, w2b_ref, sem_ref):
    _stream_expert_weights(layer, (te_ref, nx_ref, nt_ref), (w2_ref,), wbuf_ref,
                           (w2b_ref,), sem_ref)
    i = pl.program_id(0)

    @pl.when(i < nt_ref[0])
    def _():
        ys_ref[...] = _dot_nn(hid_ref[...], w2b_ref[...])

    @pl.when(i >= nt_ref[0])
    def _():
        ys_ref[...] = jnp.zeros_like(ys_ref)


def _moe_down(hid, w2, layer, tables):
    last = lambda i, te, nx, nt: (jnp.minimum(i, nt[0] - 1), 0)
    return pl.pallas_call(
        functools.partial(_moe_down_kernel, layer),
        grid_spec=pltpu.PrefetchScalarGridSpec(
            num_scalar_prefetch=3,
            grid=(MOE_MAX_TILES,),
            in_specs=[pl.BlockSpec((MOE_TILE, D_FF), last),
                      pl.BlockSpec(memory_space=pl.ANY)],
            out_specs=pl.BlockSpec((MOE_TILE, D_MODEL), lambda i, te, nx, nt: (i, 0)),
            scratch_shapes=[pltpu.VMEM((1, D_FF, D_MODEL), F32),
                            pltpu.VMEM((D_FF, D_MODEL), BF16),
                            pltpu.SemaphoreType.DMA((1,))]),
        out_shape=jax.ShapeDtypeStruct((N_ASSIGN_PAD, D_MODEL), F32),
        compiler_params=_params("arbitrary"),
        name="moe_down",
    )(*tables, hid, w2)


def _combine_norm_kernel(final, pos_ref, x_ref, ys_ref, gt_ref, lg_ref, lb_ref, o0_ref, o1_ref,
                         buf_ref, sem_ref):
    i = pl.program_id(0)
    n = pl.num_programs(0)
    tm = x_ref.shape[0]

    def start_tile(tile, slot):
        for k in range(TOP_K):
            _issue_row_copies(pos_ref, tile * tm * TOP_K + k, ys_ref, buf_ref.at[slot, k],
                              sem_ref.at[slot], tm, stride=TOP_K)

    @pl.when(i == 0)
    def _():
        start_tile(0, 0)

    @pl.when(i + 1 < n)
    def _():
        start_tile(i + 1, (i + 1) % 2)

    slot = i % 2
    for k in range(TOP_K):
        pltpu.make_async_copy(ys_ref.at[pl.ds(0, tm)], buf_ref.at[slot, k], sem_ref.at[slot]).wait()
    gt = gt_ref[...]
    moe = buf_ref[slot, 0] * gt[:, 0:1] + buf_ref[slot, 1] * gt[:, 1:2]
    xn = _layer_norm(ALPHA * x_ref[...] + moe, lg_ref[...], lb_ref[...])
    if final:
        @pl.when(i < n - 1)
        def _():
            o0_ref[...] = xn

        @pl.when(i == n - 1)
        def _():
            o1_ref[...] = xn
    else:
        o0_ref[...] = xn
        o1_ref[...] = xn.astype(BF16)


def _combine_norm(x, ys, pos, gates, ln_g, ln_b, final):
    tm = N_SAMPLE if final else ROW_TILE
    const = lambda i, pos: (0, 0)
    tile = lambda i, pos: (i, 0)
    if final:
        n_prompt_tiles = N_PROMPT // tm
        out_specs = [pl.BlockSpec((tm, D_MODEL), lambda i, pos: (jnp.minimum(i, n_prompt_tiles - 1), 0)),
                     pl.BlockSpec((tm, D_MODEL), const)]
        out_shape = [jax.ShapeDtypeStruct((N_PROMPT, D_MODEL), F32),
                     jax.ShapeDtypeStruct((N_SAMPLE, D_MODEL), F32)]
    else:
        out_specs = [pl.BlockSpec((tm, D_MODEL), tile), pl.BlockSpec((tm, D_MODEL), tile)]
        out_shape = [jax.ShapeDtypeStruct((N_TOK, D_MODEL), F32),
                     jax.ShapeDtypeStruct((N_TOK, D_MODEL), BF16)]
    return pl.pallas_call(
        functools.partial(_combine_norm_kernel, final),
        grid_spec=pltpu.PrefetchScalarGridSpec(
            num_scalar_prefetch=1,
            grid=(N_TOK // tm,),
            in_specs=[pl.BlockSpec((tm, D_MODEL), tile),
                      pl.BlockSpec(memory_space=pl.ANY),
                      pl.BlockSpec((tm, TOP_K), tile),
                      pl.BlockSpec((1, D_MODEL), const),
                      pl.BlockSpec((1, D_MODEL), const)],
            out_specs=out_specs,
            scratch_shapes=[pltpu.VMEM((2, TOP_K, tm, D_MODEL), F32),
                            pltpu.SemaphoreType.DMA((2,))]),
        out_shape=out_shape,
        compiler_params=_params("arbitrary"),
        name="combine_norm",
    )(pos, x, ys, gates, ln_g.reshape(1, D_MODEL), ln_b.reshape(1, D_MODEL))


def _routing_tables(ids):
    e = ids.reshape(-1)
    onehot = (e[:, None] == jnp.arange(N_EXPERTS, dtype=jnp.int32)[None, :]).astype(jnp.int32)
    csum = jnp.cumsum(onehot, axis=0)
    rank = jnp.sum(onehot * csum, axis=1) - 1
    sizes = csum[-1]
    tiles_per = (sizes + MOE_TILE - 1) // MOE_TILE
    tile_end = jnp.cumsum(tiles_per)
    tile_start = tile_end - tiles_per
    n_tiles = tile_end[-1]
    pos = (jnp.sum(onehot * (tile_start * MOE_TILE)[None, :], axis=1) + rank).astype(jnp.int32)
    tile = jnp.arange(MOE_MAX_TILES, dtype=jnp.int32)
    owner = jnp.sum((tile_end[None, :] <= jnp.minimum(tile, n_tiles - 1)[:, None]).astype(jnp.int32), axis=1)
    order = jnp.argsort(e, stable=True).astype(jnp.int32)
    owner_hot = (owner[:, None] == jnp.arange(N_EXPERTS, dtype=jnp.int32)[None, :]).astype(jnp.int32)
    size_start = jnp.cumsum(sizes) - sizes
    tile_first = jnp.sum(owner_hot * (size_start - tile_start * MOE_TILE)[None, :], axis=1)
    tile_limit = jnp.sum(owner_hot * (size_start + sizes)[None, :], axis=1)
    row = jnp.arange(N_ASSIGN_PAD, dtype=jnp.int32).reshape(MOE_MAX_TILES, MOE_TILE)
    src = tile_first[:, None] + row
    valid = src < tile_limit[:, None]
    row_token = jnp.where(valid, order[jnp.clip(src, 0, N_ASSIGN - 1)] // TOP_K, row % N_TOK).reshape(-1)
    experts = jnp.arange(N_EXPERTS, dtype=jnp.int32)
    later = lax.cummin(jnp.where(tiles_per > 0, experts, N_EXPERTS), reverse=True)
    next_expert = jnp.concatenate([later[1:], jnp.full((1,), N_EXPERTS, jnp.int32)])
    next_expert = jnp.where(next_expert >= N_EXPERTS, -1, next_expert)
    tile_next = jnp.sum(owner_hot * next_expert[None, :], axis=1)
    n_tiles = n_tiles.reshape(1).astype(jnp.int32)
    tables = (owner.astype(jnp.int32), tile_next.astype(jnp.int32), n_tiles)
    return pos, row_token.astype(jnp.int32), tables


def _moe_block(x, ids, gates, w1, w3, w2, layer, ln_g, ln_b, final):
    pos, row_token, tables = _routing_tables(ids)
    xs = _gather_rows(x, row_token, tables[2], BF16, "moe_gather")
    hid = _moe_up(xs, w1, w3, layer, tables)
    ys = _moe_down(hid, w2, layer, tables)
    return _combine_norm(x, ys, pos, gates, ln_g, ln_b, final)


def _conv_in_kernel(x_ref, wb_ref, wc_ref, wh_ref, cw_ref, s0_ref, s1_ref, gbz_ref, u_ref,
                    wbb_ref, wcb_ref, whb_ref, ubuf_ref):
    i = pl.program_id(1)
    tm = x_ref.shape[0]
    pad = SUBLANES

    @pl.when(i == 0)
    def _():
        wbb_ref[...] = wb_ref[...].astype(BF16)
        wcb_ref[...] = wc_ref[...].astype(BF16)
        whb_ref[...] = wh_ref[...].astype(BF16)
        ubuf_ref[0:pad, :] = jnp.zeros((pad, ubuf_ref.shape[1]), F32)

    w0 = cw_ref[0:1, :]
    w1 = cw_ref[1:2, :]
    w2 = cw_ref[2:3, :]
    sub = ROW_TILE
    for s in range(tm // sub):
        lo, hi = s * sub, (s + 1) * sub
        x = x_ref[lo:hi, :]
        gb = _dot_nn(x, wbb_ref[...])
        u = _dot_nn(x, wcb_ref[...]) * _dot_nn(x, whb_ref[...])
        u_ref[lo:hi, :] = u
        ubuf_ref[pad + lo:pad + hi, :] = u
        t = (i * tm + lo + lax.broadcasted_iota(jnp.int32, (sub, 1), 0)) & (SEQ - 1)
        u_m1 = jnp.where(t >= 1, ubuf_ref[pad - 1 + lo:pad - 1 + hi, :], 0.0)
        u_m2 = jnp.where(t >= 2, ubuf_ref[pad - 2 + lo:pad - 2 + hi, :], 0.0)
        z = u_m2 * w0 + u_m1 * w1 + u * w2
        gbz_ref[lo:hi, :] = (gb * z).astype(BF16)
    ubuf_ref[0:pad, :] = ubuf_ref[tm:tm + pad, :]

    @pl.when(i == pl.num_programs(1) - 1)
    def _():
        first = sub - N_SAMPLE
        z_s = s0_ref[...] * w0 + s1_ref[...] * w1 + u[first:sub, :] * w2
        gbz_ref[tm - N_SAMPLE:tm, :] = (gb[first:sub, :] * z_s).astype(BF16)


def _conv_in(xb, w_in, conv_w, state):
    tn = 512
    nb = D_MODEL // tn
    s0 = state[:, 0, :]
    s1 = state[:, 1, :]
    return pl.pallas_call(
        _conv_in_kernel,
        grid=(nb, N_TOK // TOKEN_TILE),
        in_specs=[pl.BlockSpec((TOKEN_TILE, D_MODEL), lambda j, i: (i, 0)),
                  pl.BlockSpec((D_MODEL, tn), lambda j, i: (0, j)),
                  pl.BlockSpec((D_MODEL, tn), lambda j, i: (0, nb + j)),
                  pl.BlockSpec((D_MODEL, tn), lambda j, i: (0, 2 * nb + j)),
                  pl.BlockSpec((CONV_W, tn), lambda j, i: (0, j)),
                  pl.BlockSpec((N_SAMPLE, tn), lambda j, i: (0, j)),
                  pl.BlockSpec((N_SAMPLE, tn), lambda j, i: (0, j))],
        out_specs=[pl.BlockSpec((TOKEN_TILE, tn), lambda j, i: (i, j)),
                   pl.BlockSpec((TOKEN_TILE, tn), lambda j, i: (i, j))],
        out_shape=[jax.ShapeDtypeStruct((N_TOK, D_MODEL), BF16),
                   jax.ShapeDtypeStruct((N_TOK, D_MODEL), F32)],
        scratch_shapes=[pltpu.VMEM((D_MODEL, tn), BF16), pltpu.VMEM((D_MODEL, tn), BF16),
                        pltpu.VMEM((D_MODEL, tn), BF16),
                        pltpu.VMEM((TOKEN_TILE + 2 * SUBLANES, tn), F32)],
        compiler_params=_params("arbitrary", "arbitrary"),
        name="conv_in",
    )(xb, w_in, w_in, w_in, conv_w, s0, s1)


def kernel(x_prompt, x_sample, state_gla, state_conv, router_w, router_b, gla_w_in, gla_w_gate, gla_b_gate,
           gla_norm_g, gla_w_out, conv_w_in, conv_w, conv_w_out, ln_mix_g, ln_mix_b, ln_ffn_g, ln_ffn_b,
           moe_w1, moe_w3, moe_w2):
    x0 = jnp.concatenate([x_prompt.reshape(N_PROMPT, D_MODEL), x_sample.reshape(N_SAMPLE, D_MODEL)], axis=0)
    router = _router_operands(router_w, router_b)

    w_in_t = gla_w_in[0].T
    qkvr = _matmul_cols(x0, w_in_t, GLA_QKVR_WIDTH, 1024, F32, "gla_in_proj")
    g = _gla_gate(x0, w_in_t, gla_w_gate[0], gla_b_gate[0])
    norm_g = gla_norm_g[0].reshape(1, DV_TOTAL)
    og, s_prompt = _gla_prompt(qkvr, g, norm_g)
    og, s_sample = _gla_sample(qkvr, g, norm_g, state_gla[0], og)
    x1, ids, gates = _proj_norm_route(og, x0, gla_w_out[0], ln_mix_g[0], ln_mix_b[0], router)
    x2, x2b = _moe_block(x1, ids, gates, moe_w1, moe_w3, moe_w2, 0, ln_ffn_g[0], ln_ffn_b[0], False)

    gbz, u = _conv_in(x2b, conv_w_in[0], conv_w[0], state_conv[0])
    x3, ids, gates = _proj_norm_route(gbz, x2, conv_w_out[0], ln_mix_g[1], ln_mix_b[1], router)
    y_p, y_s = _moe_block(x3, ids, gates, moe_w1, moe_w3, moe_w2, 1, ln_ffn_g[1], ln_ffn_b[1], True)

    y_prompt = y_p.reshape(N_PROMPT_SEQ, SEQ, D_MODEL)
    y_sample = y_s.reshape(N_SAMPLE, 1, D_MODEL)
    conv_prompt = jnp.stack([u[(b + 1) * SEQ - (CONV_W - 1):(b + 1) * SEQ] for b in range(N_PROMPT_SEQ)])
    conv_sample = jnp.concatenate([state_conv[0][:, 1:, :], u[N_PROMPT:][:, None, :]], axis=1)
    return (y_prompt, y_sample, s_prompt[None], conv_prompt[None], s_sample[None], conv_sample[None])
```

```python
import functools

import jax
import jax.numpy as jnp
from jax import lax
from jax.experimental import pallas as pl
from jax.experimental.pallas import tpu as pltpu

F32 = jnp.float32
BF16 = jnp.bfloat16

D_MODEL = 2048
N_PROMPT_SEQ = 4
SEQ = 2048
N_PROMPT = N_PROMPT_SEQ * SEQ
N_SAMPLE = 128
N_TOK = N_PROMPT + N_SAMPLE
DEPTH = 2

GLA_HEADS = 4
DK_TOTAL = D_MODEL // 2
DV_TOTAL = D_MODEL
DK_HEAD = DK_TOTAL // GLA_HEADS
DV_HEAD = DV_TOTAL // GLA_HEADS
GATE_RANK = 16
GATE_NORMALIZER = 16.0
GLA_QKVR_WIDTH = 2 * DK_TOTAL + 2 * DV_TOTAL
CONV_W = 3
N_EXPERTS = 16
N_GROUPS = 4
EXPERTS_PER_GROUP = N_EXPERTS // N_GROUPS
TOP_K = 2
D_FF = D_MODEL // 2
ALPHA = (2.0 * DEPTH) ** 0.25
LN_EPS = 1e-5
RMS_EPS = 1e-6

LANES = 128
SUBLANES = 8
VMEM_LIMIT_BYTES = 56 * 1024 * 1024

TOKEN_TILE = 640
ROW_TILE = 320
GLA_CHUNK = 64
GLA_SAMPLE_BATCH = 16
MOE_TILE = 256
N_ASSIGN = N_TOK * TOP_K
MOE_MAX_TILES = N_ASSIGN // MOE_TILE + N_EXPERTS
N_ASSIGN_PAD = MOE_MAX_TILES * MOE_TILE


def _params(*semantics):
    return pltpu.CompilerParams(dimension_semantics=semantics, vmem_limit_bytes=VMEM_LIMIT_BYTES)


def _split3(x):
    hi = x.astype(BF16)
    r1 = x - hi.astype(F32)
    mid = r1.astype(BF16)
    lo = (r1 - mid.astype(F32)).astype(BF16)
    return hi, mid, lo


def _dot_nn(a, b):
    return jnp.dot(a, b, preferred_element_type=F32)


def _dot_tn(a, b):
    return lax.dot_general(a, b, (((0,), (0,)), ((), ())), preferred_element_type=F32)


def _dot_nt(a, b):
    return lax.dot_general(a, b, (((1,), (1,)), ((), ())), preferred_element_type=F32)


def _col_bcast(rows, n):
    ones = jnp.ones((rows.shape[0], LANES), BF16)
    hi, mid, lo = _split3(rows)
    col = _dot_tn(hi, ones) + _dot_tn(mid, ones) + _dot_tn(lo, ones)
    return jnp.concatenate([col] * (n // LANES), axis=1)


def _silu(r):
    return r / (1.0 + jnp.exp(-r))


def _layer_norm(h, g, b):
    mu = jnp.mean(h, axis=-1, keepdims=True)
    d = h - mu
    var = jnp.mean(d * d, axis=-1, keepdims=True)
    return d * lax.rsqrt(var + LN_EPS) * g + b


X_TILE_BLOCKS = TOKEN_TILE // N_SAMPLE


def _x_tile_specs(tile_index):
    last = N_PROMPT // N_SAMPLE - 1

    def prompt_spec(k):
        return pl.BlockSpec((N_SAMPLE, D_MODEL),
                            lambda *g: (jnp.minimum(tile_index(*g) * X_TILE_BLOCKS + k, last), 0))
    return ([prompt_spec(k) for k in range(X_TILE_BLOCKS)]
            + [pl.BlockSpec((N_SAMPLE, D_MODEL), lambda *g: (0, 0))])


def _x_tile(i, x_refs):
    *prompt_refs, sample_ref = x_refs
    tail_is_sample = (i + 1) * TOKEN_TILE > N_PROMPT
    blocks = [r[...] for r in prompt_refs[:-1]]
    blocks.append(jnp.where(tail_is_sample, sample_ref[...], prompt_refs[-1][...]))
    return jnp.concatenate(blocks, axis=0)


def _matmul_kernel(*refs):
    *x_refs, wt_ref, o_ref, wb_ref = refs
    i = pl.program_id(1)

    @pl.when(i == 0)
    def _():
        wb_ref[...] = wt_ref[...].astype(BF16)

    x = _x_tile(i, x_refs).astype(BF16)
    for s in range(TOKEN_TILE // ROW_TILE):
        rows = slice(s * ROW_TILE, (s + 1) * ROW_TILE)
        o_ref[rows, :] = _dot_nt(x[rows, :], wb_ref[...]).astype(o_ref.dtype)


def _matmul_cols(x_prompt, x_sample, w_t, n_cols, tn, out_dtype, name):
    k = x_prompt.shape[1]
    return pl.pallas_call(
        _matmul_kernel,
        grid=(n_cols // tn, N_TOK // TOKEN_TILE),
        in_specs=_x_tile_specs(lambda j, i: i) + [pl.BlockSpec((tn, k), lambda j, i: (j, 0))],
        out_specs=pl.BlockSpec((TOKEN_TILE, tn), lambda j, i: (i, j)),
        out_shape=jax.ShapeDtypeStruct((N_TOK, n_cols), out_dtype),
        scratch_shapes=[pltpu.VMEM((tn, k), BF16)],
        compiler_params=_params("arbitrary", "arbitrary"),
        name=name,
    )(*([x_prompt] * X_TILE_BLOCKS), x_sample, w_t)


def _gate_kernel(*refs):
    *x_refs, wgl_ref, wg_ref, bg_ref, g_ref = refs
    wgl = jnp.concatenate([wgl_ref[...].astype(BF16), jnp.zeros((LANES - GATE_RANK, D_MODEL), BF16)], axis=0)
    gl = _dot_nt(_x_tile(pl.program_id(0), x_refs).astype(BF16), wgl)
    z = _dot_nn(gl.astype(BF16), wg_ref[...]) + bg_ref[...]
    log_sig = jnp.minimum(z, 0.0) - jnp.log1p(jnp.exp(-jnp.abs(z)))
    g_ref[...] = log_sig * (1.0 / GATE_NORMALIZER)


def _gla_gate(x_prompt, x_sample, w_in_t, w_gate, b_gate):
    wg = jnp.pad(w_gate, ((0, LANES - GATE_RANK), (0, 0))).astype(BF16)
    return pl.pallas_call(
        _gate_kernel,
        grid=(N_TOK // TOKEN_TILE,),
        in_specs=_x_tile_specs(lambda i: i) + [
            pl.BlockSpec((GATE_RANK, D_MODEL), lambda i: (GLA_QKVR_WIDTH // GATE_RANK, 0)),
            pl.BlockSpec((LANES, DK_TOTAL), lambda i: (0, 0)),
            pl.BlockSpec((1, DK_TOTAL), lambda i: (0, 0))],
        out_specs=pl.BlockSpec((TOKEN_TILE, DK_TOTAL), lambda i: (i, 0)),
        out_shape=jax.ShapeDtypeStruct((N_TOK, DK_TOTAL), F32),
        compiler_params=_params("arbitrary"),
        name="gla_gate",
    )(*([x_prompt] * X_TILE_BLOCKS), x_sample, w_in_t, wg, b_gate.reshape(1, DK_TOTAL))


def _rms_gate(o, norm_g, r):
    o = o * lax.rsqrt(jnp.mean(o * o, axis=-1, keepdims=True) + RMS_EPS)
    return (o * norm_g) * _silu(r)


def _gla_prompt_kernel(q_ref, k_ref, v_ref, r_ref, g_ref, ng_ref, og_in_ref, og_ref, s_out_ref, s_ref):
    del og_in_ref
    c = pl.program_id(1)
    cc = q_ref.shape[0]

    @pl.when(c == 0)
    def _():
        s_ref[...] = jnp.zeros_like(s_ref)

    row = lax.broadcasted_iota(jnp.int32, (cc, cc), 0)
    col = lax.broadcasted_iota(jnp.int32, (cc, cc), 1)
    causal = col <= row
    tri = jnp.where(causal, 1.0, 0.0).astype(BF16)
    b = _dot_nn(jnp.concatenate([tri] * 3, axis=1),
                jnp.concatenate(_split3(g_ref[...]), axis=0))
    b_last = b[cc - 1:cc, :]
    b_mid = b[cc // 2 - 1:cc // 2, :]

    q = q_ref[...]
    k = k_ref[...] * (DK_HEAD ** -0.5)
    q_in = (q * jnp.exp(b)).astype(BF16)
    qh, qm, _ = _split3(q * jnp.exp(b - b_mid))
    kh, km, _ = _split3(k * jnp.exp(b_mid - b))
    k_d = (k * jnp.exp(b_last - b)).astype(BF16)

    rows = 2 * SUBLANES
    sub = lax.broadcasted_iota(jnp.int32, (rows, DK_TOTAL), 0)
    decay_rows = jnp.where(sub == 0, jnp.broadcast_to(jnp.exp(b_last), (rows, DK_TOTAL)), 0.0)
    decay_col = _dot_tn(jnp.concatenate(_split3(decay_rows), axis=0),
                        jnp.ones((3 * rows, LANES), BF16))

    for h in range(GLA_HEADS):
        dk = slice(h * DK_HEAD, (h + 1) * DK_HEAD)
        dv = slice(h * DV_HEAD, (h + 1) * DV_HEAD)
        q3 = jnp.concatenate([qh[:, dk], qh[:, dk], qm[:, dk]], axis=1)
        k3 = jnp.concatenate([kh[:, dk], km[:, dk], kh[:, dk]], axis=1)
        scores = jnp.where(causal, _dot_nt(q3, k3), 0.0).astype(BF16)
        v = v_ref[:, dv].astype(BF16)
        s_old = s_ref[h]
        o = _dot_nn(jnp.concatenate([q_in[:, dk], scores], axis=1),
                    jnp.concatenate([s_old.astype(BF16), v], axis=0))
        decay = jnp.concatenate([decay_col[dk, :]] * (DV_HEAD // LANES), axis=1)
        s_new = decay * s_old + _dot_tn(k_d[:, dk], v)
        s_ref[h] = s_new

        @pl.when(c == pl.num_programs(1) - 1)
        def _():
            s_out_ref[0, h] = s_new

        og_ref[:, dv] = _rms_gate(o, ng_ref[:, dv], r_ref[:, dv]).astype(BF16)


def _gla_prompt(qkvr, g, norm_g):
    nc = SEQ // GLA_CHUNK
    row = lambda b, c: b * nc + c
    return pl.pallas_call(
        _gla_prompt_kernel,
        grid=(N_PROMPT_SEQ, nc),
        in_specs=[pl.BlockSpec((GLA_CHUNK, DK_TOTAL), lambda b, c: (row(b, c), 0)),
                  pl.BlockSpec((GLA_CHUNK, DK_TOTAL), lambda b, c: (row(b, c), 1)),
                  pl.BlockSpec((GLA_CHUNK, DV_TOTAL), lambda b, c: (row(b, c), 1)),
                  pl.BlockSpec((GLA_CHUNK, DV_TOTAL), lambda b, c: (row(b, c), 2)),
                  pl.BlockSpec((GLA_CHUNK, DK_TOTAL), lambda b, c: (row(b, c), 0)),
                  pl.BlockSpec((1, DV_TOTAL), lambda b, c: (0, 0)),
                  pl.BlockSpec(memory_space=pl.ANY)],
        out_specs=[pl.BlockSpec((GLA_CHUNK, DV_TOTAL), lambda b, c: (row(b, c), 0)),
                   pl.BlockSpec((1, GLA_HEADS, DK_HEAD, DV_HEAD), lambda b, c: (b, 0, 0, 0))],
        out_shape=[jax.ShapeDtypeStruct((N_TOK, DV_TOTAL), BF16),
                   jax.ShapeDtypeStruct((N_PROMPT_SEQ, GLA_HEADS, DK_HEAD, DV_HEAD), F32)],
        scratch_shapes=[pltpu.VMEM((GLA_HEADS, DK_HEAD, DV_HEAD), F32)],
        input_output_aliases={6: 0},
        compiler_params=_params("arbitrary", "arbitrary"),
        name="gla_prompt",
    )(qkvr, qkvr, qkvr, qkvr, g, norm_g, jnp.zeros((N_TOK, DV_TOTAL), BF16))


def _gla_sample_kernel(q_ref, k_ref, v_ref, r_ref, g_ref, ng_ref, s_ref, og_in_ref, og_ref, s_out_ref,
                       o_scr):
    del og_in_ref
    bb = q_ref.shape[0]
    q = q_ref[...]
    k = k_ref[...] * (DK_HEAD ** -0.5)
    v = v_ref[...]
    eg = jnp.exp(g_ref[...])
    qe = q * eg
    qk = jnp.sum(q * k, axis=-1, keepdims=True)
    sub = lax.broadcasted_iota(jnp.int32, (bb, DK_HEAD), 0)
    for bi in range(bb):
        sel = sub == bi
        s_old = s_ref[bi, 0]
        decay = _col_bcast(jnp.where(sel, eg, 0.0), DV_HEAD)
        k_col = _col_bcast(jnp.where(sel, k, 0.0), DV_HEAD)
        qe_col = _col_bcast(jnp.where(sel, qe, 0.0), DV_HEAD)
        s_out_ref[bi, 0] = decay * s_old + k_col * v[bi:bi + 1, :]
        o_scr[bi:bi + 1, :] = jnp.sum(qe_col * s_old, axis=0, keepdims=True)
    o = qk * v + o_scr[...]
    og_ref[...] = _rms_gate(o, ng_ref[...], r_ref[...]).astype(BF16)


def _gla_sample(qkvr, g, norm_g, state, og):
    bb = GLA_SAMPLE_BATCH
    r0 = N_PROMPT // bb
    hk = DK_TOTAL // DK_HEAD
    hv = 2 * DK_TOTAL // DV_HEAD
    return pl.pallas_call(
        _gla_sample_kernel,
        grid=(N_SAMPLE // bb, GLA_HEADS),
        in_specs=[pl.BlockSpec((bb, DK_HEAD), lambda i, h: (r0 + i, h)),
                  pl.BlockSpec((bb, DK_HEAD), lambda i, h: (r0 + i, hk + h)),
                  pl.BlockSpec((bb, DV_HEAD), lambda i, h: (r0 + i, hv + h)),
                  pl.BlockSpec((bb, DV_HEAD), lambda i, h: (r0 + i, hv + GLA_HEADS + h)),
                  pl.BlockSpec((bb, DK_HEAD), lambda i, h: (r0 + i, h)),
                  pl.BlockSpec((1, DV_HEAD), lambda i, h: (0, h)),
                  pl.BlockSpec((bb, 1, DK_HEAD, DV_HEAD), lambda i, h: (i, h, 0, 0)),
                  pl.BlockSpec(memory_space=pl.ANY)],
        out_specs=[pl.BlockSpec((bb, DV_HEAD), lambda i, h: (r0 + i, h)),
                   pl.BlockSpec((bb, 1, DK_HEAD, DV_HEAD), lambda i, h: (i, h, 0, 0))],
        out_shape=[jax.ShapeDtypeStruct((N_TOK, DV_TOTAL), BF16),
                   jax.ShapeDtypeStruct((N_SAMPLE, GLA_HEADS, DK_HEAD, DV_HEAD), F32)],
        scratch_shapes=[pltpu.VMEM((bb, DV_HEAD), F32)],
        input_output_aliases={7: 0},
        compiler_params=_params("arbitrary", "arbitrary"),
        name="gla_sample",
    )(qkvr, qkvr, qkvr, qkvr, g, norm_g, state, og)


def _top2_of4(p):
    ranks = []
    for j in range(4):
        rk = jnp.zeros(p[j].shape, jnp.int32)
        for i in range(4):
            if i == j:
                continue
            beats = (p[i] >= p[j]) if i < j else (p[i] > p[j])
            rk = rk + jnp.where(beats, 1, 0)
        ranks.append(rk)

    def pick(rank):
        val = jnp.zeros(p[0].shape, F32)
        idx = jnp.zeros(p[0].shape, jnp.int32)
        for j in range(4):
            hit = ranks[j] == rank
            val = jnp.where(hit, p[j], val)
            idx = jnp.where(hit, j, idx)
        return val, idx

    v1, i1 = pick(0)
    v2, i2 = pick(1)
    return v1, i1, v2, i2


def _route(x, rw_ref, rb_ref):
    logits = _dot_nn(x.astype(BF16), rw_ref[...])
    n = x.shape[0]
    n_pad = -n % LANES
    if n_pad:
        logits = jnp.concatenate([logits, jnp.zeros((n_pad, LANES), F32)], axis=0)
    lt = logits.T[0:N_EXPERTS, 0:n] + rb_ref[...]
    e = jnp.exp(lt - jnp.max(lt, axis=0, keepdims=True))
    probs = e / jnp.sum(e, axis=0, keepdims=True)
    best = None
    for grp in range(N_GROUPS):
        rows = [probs[grp * EXPERTS_PER_GROUP + j:grp * EXPERTS_PER_GROUP + j + 1, :]
                for j in range(EXPERTS_PER_GROUP)]
        v1, i1, v2, i2 = _top2_of4(rows)
        score = v1 + v2
        cand = (score, v1, i1 + grp * EXPERTS_PER_GROUP, v2, i2 + grp * EXPERTS_PER_GROUP)
        if best is None:
            best = cand
        else:
            better = score > best[0]
            best = tuple(jnp.where(better, n, o) for n, o in zip(cand, best))
    _, v1, e1, v2, e2 = best
    denom = v1 + v2
    return (e1, e2), (v1 / denom, v2 / denom)


def _proj_norm_route_kernel(a_ref, *refs):
    *x_refs, w_ref, lg_ref, lb_ref, rw_ref, rb_ref, xo_ref, id_ref, gt_ref = refs
    x = x_refs[0][...] if len(x_refs) == 1 else _x_tile(pl.program_id(0), x_refs)
    sub = id_ref.shape[2]
    for s in range(id_ref.shape[0]):
        rows = slice(s * sub, (s + 1) * sub)
        y = _dot_nn(a_ref[rows, :], w_ref[...])
        xn = _layer_norm(ALPHA * x[rows, :] + y, lg_ref[...], lb_ref[...])
        xo_ref[rows, :] = xn
        (e1, e2), (g1, g2) = _route(xn, rw_ref, rb_ref)
        id_ref[s, 0:1, :] = e1
        id_ref[s, 1:2, :] = e2
        gt_ref[s, 0:1, :] = g1
        gt_ref[s, 1:2, :] = g2


def _router_operands(router_w, router_b):
    rw = jnp.pad(router_w, ((0, 0), (0, LANES - N_EXPERTS))).astype(BF16)
    return rw, router_b.reshape(N_EXPERTS, 1)


def _proj_norm_route(a, x, w_out, ln_g, ln_b, router):
    nt = N_TOK // ROW_TILE
    n_sub = TOKEN_TILE // ROW_TILE
    rw, rb = router
    const = lambda i: (0, 0)
    if isinstance(x, tuple):
        x_specs = _x_tile_specs(lambda i: i)
        x_args = [x[0]] * X_TILE_BLOCKS + [x[1]]
    else:
        x_specs = [pl.BlockSpec((TOKEN_TILE, D_MODEL), lambda i: (i, 0))]
        x_args = [x]
    x1, ids, gates = pl.pallas_call(
        _proj_norm_route_kernel,
        grid=(N_TOK // TOKEN_TILE,),
        in_specs=[pl.BlockSpec((TOKEN_TILE, D_MODEL), lambda i: (i, 0))] + x_specs + [
            pl.BlockSpec((D_MODEL, D_MODEL), const, pipeline_mode=pl.Buffered(1)),
            pl.BlockSpec((1, D_MODEL), const),
            pl.BlockSpec((1, D_MODEL), const),
            pl.BlockSpec((D_MODEL, LANES), const),
            pl.BlockSpec((N_EXPERTS, 1), const)],
        out_specs=[pl.BlockSpec((TOKEN_TILE, D_MODEL), lambda i: (i, 0)),
                   pl.BlockSpec((n_sub, TOP_K, ROW_TILE), lambda i: (i, 0, 0)),
                   pl.BlockSpec((n_sub, TOP_K, ROW_TILE), lambda i: (i, 0, 0))],
        out_shape=[jax.ShapeDtypeStruct((N_TOK, D_MODEL), F32),
                   jax.ShapeDtypeStruct((nt, TOP_K, ROW_TILE), jnp.int32),
                   jax.ShapeDtypeStruct((nt, TOP_K, ROW_TILE), F32)],
        compiler_params=_params("arbitrary"),
        name="proj_norm_route",
    )(a, *x_args, w_out.astype(BF16), ln_g.reshape(1, D_MODEL), ln_b.reshape(1, D_MODEL), rw, rb)
    ids = ids.transpose(0, 2, 1).reshape(N_TOK, TOP_K)
    gates = gates.transpose(0, 2, 1).reshape(N_TOK, TOP_K)
    return x1, ids, gates


def _stream_expert_weights(layer, tables, w_refs, wbuf_ref, wb_refs, sem_ref):
    te_ref, nx_ref, nt_ref = tables
    i = pl.program_id(0)

    def copies(expert):
        return [pltpu.make_async_copy(w.at[layer, expert], wbuf_ref.at[j], sem_ref.at[j])
                for j, w in enumerate(w_refs)]

    @pl.when(i == 0)
    def _():
        for c in copies(te_ref[0]):
            c.start()

    first = jnp.logical_or(i == 0, te_ref[i] != te_ref[jnp.maximum(i - 1, 0)])

    @pl.when(jnp.logical_and(first, i < nt_ref[0]))
    def _():
        for c in copies(te_ref[i]):
            c.wait()
        for j, wb in enumerate(wb_refs):
            wb[...] = wbuf_ref[j].astype(BF16)

        @pl.when(nx_ref[i] >= 0)
        def _():
            for c in copies(nx_ref[i]):
                c.start()


def _issue_row_copies(idx_ref, base, src_ref, dst_ref, sem, n_rows, stride=1):
    def body(g, carry):
        r0 = pl.multiple_of(g * SUBLANES, SUBLANES)
        for k in range(SUBLANES):
            row = idx_ref[base + stride * (r0 + k)]
            pltpu.make_async_copy(src_ref.at[pl.ds(row, 1)], dst_ref.at[pl.ds(r0 + k, 1)],
                                  sem).start(priority=k % 2)
        return carry
    lax.fori_loop(0, n_rows // SUBLANES, body, 0)


def _gather_kernel(idx_ref, nt_ref, src_ref, out_ref, buf_ref, sem_ref):
    i = pl.program_id(0)
    n = nt_ref[0]
    tg = out_ref.shape[0]

    def start_tile(tile, slot):
        _issue_row_copies(idx_ref, tile * tg, src_ref, buf_ref.at[slot], sem_ref.at[slot], tg)

    @pl.when(i == 0)
    def _():
        start_tile(0, 0)

    @pl.when(i + 1 < n)
    def _():
        start_tile(i + 1, (i + 1) % 2)

    @pl.when(i < n)
    def _():
        slot = i % 2
        pltpu.make_async_copy(src_ref.at[pl.ds(0, tg)], buf_ref.at[slot], sem_ref.at[slot]).wait()
        out_ref[...] = buf_ref[slot].astype(out_ref.dtype)

    @pl.when(i >= n)
    def _():
        out_ref[...] = jnp.zeros_like(out_ref)


def _gather_rows(src, idx, n_tiles, out_dtype, name):
    m = idx.shape[0]
    width = src.shape[1]
    return pl.pallas_call(
        _gather_kernel,
        grid_spec=pltpu.PrefetchScalarGridSpec(
            num_scalar_prefetch=2,
            grid=(m // MOE_TILE,),
            in_specs=[pl.BlockSpec(memory_space=pl.ANY)],
            out_specs=pl.BlockSpec((MOE_TILE, width), lambda i, idx, nt: (i, 0)),
            scratch_shapes=[pltpu.VMEM((2, MOE_TILE, width), src.dtype),
                            pltpu.SemaphoreType.DMA((2,))]),
        out_shape=jax.ShapeDtypeStruct((m, width), out_dtype),
        compiler_params=_params("arbitrary"),
        name=name,
    )(idx, n_tiles, src)


def _moe_up_kernel(layer, te_ref, nx_ref, nt_ref, xs_ref, w1_ref, w3_ref, hid_ref,
                   wbuf_ref, w1b_ref, w3b_ref, sem_ref):
    _stream_expert_weights(layer, (te_ref, nx_ref, nt_ref), (w1_ref, w3_ref), wbuf_ref,
                           (w1b_ref, w3b_ref), sem_ref)
    i = pl.program_id(0)

    @pl.when(i < nt_ref[0])
    def _():
        xs = xs_ref[...]
        h1 = _dot_nn(xs, w1b_ref[...])
        h3 = _dot_nn(xs, w3b_ref[...])
        hid_ref[...] = (_silu(h1) * h3).astype(BF16)

    @pl.when(i >= nt_ref[0])
    def _():
        hid_ref[...] = jnp.zeros_like(hid_ref)


def _moe_up(xs, w1, w3, layer, tables):
    last = lambda i, te, nx, nt: (jnp.minimum(i, nt[0] - 1), 0)
    return pl.pallas_call(
        functools.partial(_moe_up_kernel, layer),
        grid_spec=pltpu.PrefetchScalarGridSpec(
            num_scalar_prefetch=3,
            grid=(MOE_MAX_TILES,),
            in_specs=[pl.BlockSpec((MOE_TILE, D_MODEL), last),
                      pl.BlockSpec(memory_space=pl.ANY),
                      pl.BlockSpec(memory_space=pl.ANY)],
            out_specs=pl.BlockSpec((MOE_TILE, D_FF), lambda i, te, nx, nt: (i, 0)),
            scratch_shapes=[pltpu.VMEM((2, D_MODEL, D_FF), F32),
                            pltpu.VMEM((D_MODEL, D_FF), BF16), pltpu.VMEM((D_MODEL, D_FF), BF16),
                            pltpu.SemaphoreType.DMA((2,))]),
        out_shape=jax.ShapeDtypeStruct((N_ASSIGN_PAD, D_FF), BF16),
        compiler_params=_params("arbitrary"),
        name="moe_up",
    )(*tables, xs, w1, w3)


def _moe_down_kernel(layer, te_ref, nx_ref, nt_ref, hid_ref, w2_ref, ys_ref,
                     wbuf_ref, w2b_ref, sem_ref):
    _stream_expert_weights(layer, (te_ref, nx_ref, nt_ref), (w2_ref,), wbuf_ref,
                           (w2b_ref,), sem_ref)
    i = pl.program_id(0)

    @pl.when(i < nt_ref[0])
    def _():
        ys_ref[...] = _dot_nn(hid_ref[...], w2b_ref[...])

    @pl.when(i >= nt_ref[0])
    def _():
        ys_ref[...] = jnp.zeros_like(ys_ref)


def _moe_down(hid, w2, layer, tables):
    last = lambda i, te, nx, nt: (jnp.minimum(i, nt[0] - 1), 0)
    return pl.pallas_call(
        functools.partial(_moe_down_kernel, layer),
        grid_spec=pltpu.PrefetchScalarGridSpec(
            num_scalar_prefetch=3,
            grid=(MOE_MAX_TILES,),
            in_specs=[pl.BlockSpec((MOE_TILE, D_FF), last),
                      pl.BlockSpec(memory_space=pl.ANY)],
            out_specs=pl.BlockSpec((MOE_TILE, D_MODEL), lambda i, te, nx, nt: (i, 0)),
            scratch_shapes=[pltpu.VMEM((1, D_FF, D_MODEL), F32),
                            pltpu.VMEM((D_FF, D_MODEL), BF16),
                            pltpu.SemaphoreType.DMA((1,))]),
        out_shape=jax.ShapeDtypeStruct((N_ASSIGN_PAD, D_MODEL), F32),
        compiler_params=_params("arbitrary"),
        name="moe_down",
    )(*tables, hid, w2)


def _combine_norm_kernel(final, pos_ref, x_ref, ys_ref, gt_ref, lg_ref, lb_ref, o0_ref, o1_ref,
                         buf_ref, sem_ref):
    i = pl.program_id(0)
    n = pl.num_programs(0)
    tm = x_ref.shape[0]

    def start_tile(tile, slot):
        for k in range(TOP_K):
            _issue_row_copies(pos_ref, tile * tm * TOP_K + k, ys_ref, buf_ref.at[slot, k],
                              sem_ref.at[slot], tm, stride=TOP_K)

    @pl.when(i == 0)
    def _():
        start_tile(0, 0)

    @pl.when(i + 1 < n)
    def _():
        start_tile(i + 1, (i + 1) % 2)

    slot = i % 2
    for k in range(TOP_K):
        pltpu.make_async_copy(ys_ref.at[pl.ds(0, tm)], buf_ref.at[slot, k], sem_ref.at[slot]).wait()
    gt = gt_ref[...]
    moe = buf_ref[slot, 0] * gt[:, 0:1] + buf_ref[slot, 1] * gt[:, 1:2]
    xn = _layer_norm(ALPHA * x_ref[...] + moe, lg_ref[...], lb_ref[...])
    if final:
        @pl.when(i < n - 1)
        def _():
            o0_ref[...] = xn

        @pl.when(i == n - 1)
        def _():
            o1_ref[...] = xn
    else:
        o0_ref[...] = xn
        o1_ref[...] = xn.astype(BF16)


def _combine_norm(x, ys, pos, gates, ln_g, ln_b, final):
    tm = N_SAMPLE if final else ROW_TILE
    const = lambda i, pos: (0, 0)
    tile = lambda i, pos: (i, 0)
    if final:
        n_prompt_tiles = N_PROMPT // tm
        out_specs = [pl.BlockSpec((tm, D_MODEL), lambda i, pos: (jnp.minimum(i, n_prompt_tiles - 1), 0)),
                     pl.BlockSpec((tm, D_MODEL), const)]
        out_shape = [jax.ShapeDtypeStruct((N_PROMPT, D_MODEL), F32),
                     jax.ShapeDtypeStruct((N_SAMPLE, D_MODEL), F32)]
    else:
        out_specs = [pl.BlockSpec((tm, D_MODEL), tile), pl.BlockSpec((tm, D_MODEL), tile)]
        out_shape = [jax.ShapeDtypeStruct((N_TOK, D_MODEL), F32),
                     jax.ShapeDtypeStruct((N_TOK, D_MODEL), BF16)]
    return pl.pallas_call(
        functools.partial(_combine_norm_kernel, final),
        grid_spec=pltpu.PrefetchScalarGridSpec(
            num_scalar_prefetch=1,
            grid=(N_TOK // tm,),
            in_specs=[pl.BlockSpec((tm, D_MODEL), tile),
                      pl.BlockSpec(memory_space=pl.ANY),
                      pl.BlockSpec((tm, TOP_K), tile),
                      pl.BlockSpec((1, D_MODEL), const),
                      pl.BlockSpec((1, D_MODEL), const)],
            out_specs=out_specs,
            scratch_shapes=[pltpu.VMEM((2, TOP_K, tm, D_MODEL), F32),
                            pltpu.SemaphoreType.DMA((2,))]),
        out_shape=out_shape,
        compiler_params=_params("arbitrary"),
        name="combine_norm",
    )(pos, x, ys, gates, ln_g.reshape(1, D_MODEL), ln_b.reshape(1, D_MODEL))


def _routing_tables(ids):
    e = ids.reshape(-1)
    onehot = (e[:, None] == jnp.arange(N_EXPERTS, dtype=jnp.int32)[None, :]).astype(jnp.int32)
    csum = jnp.cumsum(onehot, axis=0)
    rank = jnp.sum(onehot * csum, axis=1) - 1
    sizes = csum[-1]
    tiles_per = (sizes + MOE_TILE - 1) // MOE_TILE
    tile_end = jnp.cumsum(tiles_per)
    tile_start = tile_end - tiles_per
    n_tiles = tile_end[-1]
    pos = (jnp.sum(onehot * (tile_start * MOE_TILE)[None, :], axis=1) + rank).astype(jnp.int32)
    tile = jnp.arange(MOE_MAX_TILES, dtype=jnp.int32)
    owner = jnp.sum((tile_end[None, :] <= jnp.minimum(tile, n_tiles - 1)[:, None]).astype(jnp.int32), axis=1)
    order = jnp.argsort(e, stable=True).astype(jnp.int32)
    owner_hot = (owner[:, None] == jnp.arange(N_EXPERTS, dtype=jnp.int32)[None, :]).astype(jnp.int32)
    size_start = jnp.cumsum(sizes) - sizes
    tile_first = jnp.sum(owner_hot * (size_start - tile_start * MOE_TILE)[None, :], axis=1)
    tile_limit = jnp.sum(owner_hot * (size_start + sizes)[None, :], axis=1)
    row = jnp.arange(N_ASSIGN_PAD, dtype=jnp.int32).reshape(MOE_MAX_TILES, MOE_TILE)
    src = tile_first[:, None] + row
    valid = src < tile_limit[:, None]
    row_token = jnp.where(valid, order[jnp.clip(src, 0, N_ASSIGN - 1)] // TOP_K, row % N_TOK).reshape(-1)
    experts = jnp.arange(N_EXPERTS, dtype=jnp.int32)
    later = lax.cummin(jnp.where(tiles_per > 0, experts, N_EXPERTS), reverse=True)
    next_expert = jnp.concatenate([later[1:], jnp.full((1,), N_EXPERTS, jnp.int32)])
    next_expert = jnp.where(next_expert >= N_EXPERTS, -1, next_expert)
    tile_next = jnp.sum(owner_hot * next_expert[None, :], axis=1)
    n_tiles = n_tiles.reshape(1).astype(jnp.int32)
    tables = (owner.astype(jnp.int32), tile_next.astype(jnp.int32), n_tiles)
    return pos, row_token.astype(jnp.int32), tables


def _moe_block(x, ids, gates, w1, w3, w2, layer, ln_g, ln_b, final):
    pos, row_token, tables = _routing_tables(ids)
    xs = _gather_rows(x, row_token, tables[2], BF16, "moe_gather")
    hid = _moe_up(xs, w1, w3, layer, tables)
    ys = _moe_down(hid, w2, layer, tables)
    return _combine_norm(x, ys, pos, gates, ln_g, ln_b, final)


def _conv_in_kernel(x_ref, wb_ref, wc_ref, wh_ref, cw_ref, s0_ref, s1_ref, gbz_ref, u_ref,
                    wbb_ref, wcb_ref, whb_ref, ubuf_ref):
    i = pl.program_id(1)
    tm = x_ref.shape[0]
    pad = SUBLANES

    @pl.when(i == 0)
    def _():
        wbb_ref[...] = wb_ref[...].astype(BF16)
        wcb_ref[...] = wc_ref[...].astype(BF16)
        whb_ref[...] = wh_ref[...].astype(BF16)
        ubuf_ref[0:pad, :] = jnp.zeros((pad, ubuf_ref.shape[1]), F32)

    w0 = cw_ref[0:1, :]
    w1 = cw_ref[1:2, :]
    w2 = cw_ref[2:3, :]
    sub = ROW_TILE
    for s in range(tm // sub):
        lo, hi = s * sub, (s + 1) * sub
        x = x_ref[lo:hi, :]
        gb = _dot_nn(x, wbb_ref[...])
        u = _dot_nn(x, wcb_ref[...]) * _dot_nn(x, whb_ref[...])
        u_ref[lo:hi, :] = u
        ubuf_ref[pad + lo:pad + hi, :] = u
        t = (i * tm + lo + lax.broadcasted_iota(jnp.int32, (sub, 1), 0)) & (SEQ - 1)
        u_m1 = jnp.where(t >= 1, ubuf_ref[pad - 1 + lo:pad - 1 + hi, :], 0.0)
        u_m2 = jnp.where(t >= 2, ubuf_ref[pad - 2 + lo:pad - 2 + hi, :], 0.0)
        z = u_m2 * w0 + u_m1 * w1 + u * w2
        gbz_ref[lo:hi, :] = (gb * z).astype(BF16)
    ubuf_ref[0:pad, :] = ubuf_ref[tm:tm + pad, :]

    @pl.when(i == pl.num_programs(1) - 1)
    def _():
        first = sub - N_SAMPLE
        z_s = s0_ref[...] * w0 + s1_ref[...] * w1 + u[first:sub, :] * w2
        gbz_ref[tm - N_SAMPLE:tm, :] = (gb[first:sub, :] * z_s).astype(BF16)


def _conv_in(xb, w_in, conv_w, state):
    tn = 512
    nb = D_MODEL // tn
    s0 = state[:, 0, :]
    s1 = state[:, 1, :]
    return pl.pallas_call(
        _conv_in_kernel,
        grid=(nb, N_TOK // TOKEN_TILE),
        in_specs=[pl.BlockSpec((TOKEN_TILE, D_MODEL), lambda j, i: (i, 0)),
                  pl.BlockSpec((D_MODEL, tn), lambda j, i: (0, j)),
                  pl.BlockSpec((D_MODEL, tn), lambda j, i: (0, nb + j)),
                  pl.BlockSpec((D_MODEL, tn), lambda j, i: (0, 2 * nb + j)),
                  pl.BlockSpec((CONV_W, tn), lambda j, i: (0, j)),
                  pl.BlockSpec((N_SAMPLE, tn), lambda j, i: (0, j)),
                  pl.BlockSpec((N_SAMPLE, tn), lambda j, i: (0, j))],
        out_specs=[pl.BlockSpec((TOKEN_TILE, tn), lambda j, i: (i, j)),
                   pl.BlockSpec((TOKEN_TILE, tn), lambda j, i: (i, j))],
        out_shape=[jax.ShapeDtypeStruct((N_TOK, D_MODEL), BF16),
                   jax.ShapeDtypeStruct((N_TOK, D_MODEL), F32)],
        scratch_shapes=[pltpu.VMEM((D_MODEL, tn), BF16), pltpu.VMEM((D_MODEL, tn), BF16),
                        pltpu.VMEM((D_MODEL, tn), BF16),
                        pltpu.VMEM((TOKEN_TILE + 2 * SUBLANES, tn), F32)],
        compiler_params=_params("arbitrary", "arbitrary"),
        name="conv_in",
    )(xb, w_in, w_in, w_in, conv_w, s0, s1)


def kernel(x_prompt, x_sample, state_gla, state_conv, router_w, router_b, gla_w_in, gla_w_gate, gla_b_gate,
           gla_norm_g, gla_w_out, conv_w_in, conv_w, conv_w_out, ln_mix_g, ln_mix_b, ln_ffn_g, ln_ffn_b,
           moe_w1, moe_w3, moe_w2):
    xp = x_prompt.reshape(N_PROMPT, D_MODEL)
    xs = x_sample.reshape(N_SAMPLE, D_MODEL)
    router = _router_operands(router_w, router_b)

    w_in_t = gla_w_in[0].T
    qkvr = _matmul_cols(xp, xs, w_in_t, GLA_QKVR_WIDTH, 1024, F32, "gla_in_proj")
    g = _gla_gate(xp, xs, w_in_t, gla_w_gate[0], gla_b_gate[0])
    norm_g = gla_norm_g[0].reshape(1, DV_TOTAL)
    og, s_prompt = _gla_prompt(qkvr, g, norm_g)
    og, s_sample = _gla_sample(qkvr, g, norm_g, state_gla[0], og)
    x1, ids, gates = _proj_norm_route(og, (xp, xs), gla_w_out[0], ln_mix_g[0], ln_mix_b[0], router)
    x2, x2b = _moe_block(x1, ids, gates, moe_w1, moe_w3, moe_w2, 0, ln_ffn_g[0], ln_ffn_b[0], False)

    gbz, u = _conv_in(x2b, conv_w_in[0], conv_w[0], state_conv[0])
    x3, ids, gates = _proj_norm_route(gbz, x2, conv_w_out[0], ln_mix_g[1], ln_mix_b[1], router)
    y_p, y_s = _moe_block(x3, ids, gates, moe_w1, moe_w3, moe_w2, 1, ln_ffn_g[1], ln_ffn_b[1], True)

    y_prompt = y_p.reshape(N_PROMPT_SEQ, SEQ, D_MODEL)
    y_sample = y_s.reshape(N_SAMPLE, 1, D_MODEL)
    conv_prompt = jnp.stack([u[(b + 1) * SEQ - (CONV_W - 1):(b + 1) * SEQ] for b in range(N_PROMPT_SEQ)])
    conv_sample = jnp.concatenate([state_conv[0][:, 1:, :], u[N_PROMPT:][:, None, :]], axis=1)
    return (y_prompt, y_sample, s_prompt[None], conv_prompt[None], s_sample[None], conv_sample[None])
```

```python
import functools

import jax
import jax.numpy as jnp
from jax import lax
from jax.experimental import pallas as pl
from jax.experimental.pallas import tpu as pltpu

F32 = jnp.float32
BF16 = jnp.bfloat16

D_MODEL = 2048
N_PROMPT_SEQ = 4
SEQ = 2048
N_PROMPT = N_PROMPT_SEQ * SEQ
N_SAMPLE = 128
N_TOK = N_PROMPT + N_SAMPLE
DEPTH = 2

GLA_HEADS = 4
DK_TOTAL = D_MODEL // 2
DV_TOTAL = D_MODEL
DK_HEAD = DK_TOTAL // GLA_HEADS
DV_HEAD = DV_TOTAL // GLA_HEADS
GATE_RANK = 16
GATE_NORMALIZER = 16.0
GLA_QKVR_WIDTH = 2 * DK_TOTAL + 2 * DV_TOTAL
CONV_W = 3
N_EXPERTS = 16
N_GROUPS = 4
EXPERTS_PER_GROUP = N_EXPERTS // N_GROUPS
TOP_K = 2
D_FF = D_MODEL // 2
ALPHA = (2.0 * DEPTH) ** 0.25
LN_EPS = 1e-5
RMS_EPS = 1e-6

LANES = 128
SUBLANES = 8
VMEM_LIMIT_BYTES = 56 * 1024 * 1024

TOKEN_TILE = 640
ROW_TILE = 320
GLA_CHUNK = 64
GLA_SAMPLE_BATCH = 16
MOE_TILE = 256
N_ASSIGN = N_TOK * TOP_K
MOE_MAX_TILES = N_ASSIGN // MOE_TILE + N_EXPERTS
N_ASSIGN_PAD = MOE_MAX_TILES * MOE_TILE


def _params(*semantics):
    return pltpu.CompilerParams(dimension_semantics=semantics, vmem_limit_bytes=VMEM_LIMIT_BYTES)


def _split3(x):
    hi = x.astype(BF16)
    r1 = x - hi.astype(F32)
    mid = r1.astype(BF16)
    lo = (r1 - mid.astype(F32)).astype(BF16)
    return hi, mid, lo


def _dot_nn(a, b):
    return jnp.dot(a, b, preferred_element_type=F32)


def _dot_tn(a, b):
    return lax.dot_general(a, b, (((0,), (0,)), ((), ())), preferred_element_type=F32)


def _dot_nt(a, b):
    return lax.dot_general(a, b, (((1,), (1,)), ((), ())), preferred_element_type=F32)


def _col_bcast(rows, n):
    ones = jnp.ones((rows.shape[0], LANES), BF16)
    hi, mid, lo = _split3(rows)
    col = _dot_tn(hi, ones) + _dot_tn(mid, ones) + _dot_tn(lo, ones)
    return jnp.concatenate([col] * (n // LANES), axis=1)


def _silu(r):
    return r / (1.0 + jnp.exp(-r))


def _layer_norm(h, g, b):
    mu = jnp.mean(h, axis=-1, keepdims=True)
    d = h - mu
    var = jnp.mean(d * d, axis=-1, keepdims=True)
    return d * lax.rsqrt(var + LN_EPS) * g + b


X_TILE_BLOCKS = TOKEN_TILE // N_SAMPLE


def _x_tile_specs(tile_index):
    last = N_PROMPT // N_SAMPLE - 1

    def prompt_spec(k):
        return pl.BlockSpec((N_SAMPLE, D_MODEL),
                            lambda *g: (jnp.minimum(tile_index(*g) * X_TILE_BLOCKS + k, last), 0))
    return ([prompt_spec(k) for k in range(X_TILE_BLOCKS)]
            + [pl.BlockSpec((N_SAMPLE, D_MODEL), lambda *g: (0, 0))])


def _x_tile(i, x_refs):
    *prompt_refs, sample_ref = x_refs
    tail_is_sample = (i + 1) * TOKEN_TILE > N_PROMPT
    blocks = [r[...] for r in prompt_refs[:-1]]
    blocks.append(jnp.where(tail_is_sample, sample_ref[...], prompt_refs[-1][...]))
    return jnp.concatenate(blocks, axis=0)


def _gla_in_proj_kernel(*refs):
    *x_refs, wt_ref, wgl_ref, o_ref, gl_ref, wb_ref = refs
    j = pl.program_id(0)
    i = pl.program_id(1)

    @pl.when(i == 0)
    def _():
        wb_ref[...] = wt_ref[...].astype(BF16)

    x = _x_tile(i, x_refs).astype(BF16)
    for s in range(TOKEN_TILE // ROW_TILE):
        rows = slice(s * ROW_TILE, (s + 1) * ROW_TILE)
        o_ref[rows, :] = _dot_nt(x[rows, :], wb_ref[...]).astype(o_ref.dtype)

    @pl.when(j == pl.num_programs(0) - 1)
    def _():
        wgl = jnp.concatenate([wgl_ref[...].astype(BF16), jnp.zeros((LANES - GATE_RANK, D_MODEL), BF16)],
                              axis=0)
        gl_ref[...] = _dot_nt(x, wgl)

    @pl.when(j < pl.num_programs(0) - 1)
    def _():
        gl_ref[...] = jnp.zeros_like(gl_ref)


def _gla_in_proj(x_prompt, x_sample, w_t):
    tn = 1024
    n_pass = GLA_QKVR_WIDTH // tn
    n_tiles = N_TOK // TOKEN_TILE
    return pl.pallas_call(
        _gla_in_proj_kernel,
        grid=(n_pass, n_tiles),
        in_specs=_x_tile_specs(lambda j, i: i) + [
            pl.BlockSpec((tn, D_MODEL), lambda j, i: (j, 0)),
            pl.BlockSpec((GATE_RANK, D_MODEL), lambda j, i: (GLA_QKVR_WIDTH // GATE_RANK, 0))],
        out_specs=[pl.BlockSpec((TOKEN_TILE, tn), lambda j, i: (i, j)),
                   pl.BlockSpec((TOKEN_TILE, LANES), lambda j, i: (jnp.where(j == n_pass - 1, i, n_tiles), 0))],
        out_shape=[jax.ShapeDtypeStruct((N_TOK, GLA_QKVR_WIDTH), F32),
                   jax.ShapeDtypeStruct((N_TOK + TOKEN_TILE, LANES), F32)],
        scratch_shapes=[pltpu.VMEM((tn, D_MODEL), BF16)],
        compiler_params=_params("arbitrary", "arbitrary"),
        name="gla_in_proj",
    )(*([x_prompt] * X_TILE_BLOCKS), x_sample, w_t, w_t)


def _gate_kernel(gl_ref, wg_ref, bg_ref, g_ref):
    z = _dot_nn(gl_ref[...].astype(BF16), wg_ref[...]) + bg_ref[...]
    log_sig = jnp.minimum(z, 0.0) - jnp.log1p(jnp.exp(-jnp.abs(z)))
    g_ref[...] = log_sig * (1.0 / GATE_NORMALIZER)


def _gla_gate(gl, w_gate, b_gate):
    wg = jnp.pad(w_gate, ((0, LANES - GATE_RANK), (0, 0))).astype(BF16)
    return pl.pallas_call(
        _gate_kernel,
        grid=(N_TOK // TOKEN_TILE,),
        in_specs=[pl.BlockSpec((TOKEN_TILE, LANES), lambda i: (i, 0)),
                  pl.BlockSpec((LANES, DK_TOTAL), lambda i: (0, 0)),
                  pl.BlockSpec((1, DK_TOTAL), lambda i: (0, 0))],
        out_specs=pl.BlockSpec((TOKEN_TILE, DK_TOTAL), lambda i: (i, 0)),
        out_shape=jax.ShapeDtypeStruct((N_TOK, DK_TOTAL), F32),
        compiler_params=_params("arbitrary"),
        name="gla_gate",
    )(gl, wg, b_gate.reshape(1, DK_TOTAL))


def _rms_gate(o, norm_g, r):
    o = o * lax.rsqrt(jnp.mean(o * o, axis=-1, keepdims=True) + RMS_EPS)
    return (o * norm_g) * _silu(r)


def _gla_prompt_kernel(q_ref, k_ref, v_ref, r_ref, g_ref, ng_ref, og_in_ref, og_ref, s_out_ref, s_ref):
    del og_in_ref
    c = pl.program_id(1)
    cc = q_ref.shape[0]

    @pl.when(c == 0)
    def _():
        s_ref[...] = jnp.zeros_like(s_ref)

    row = lax.broadcasted_iota(jnp.int32, (cc, cc), 0)
    col = lax.broadcasted_iota(jnp.int32, (cc, cc), 1)
    causal = col <= row
    tri = jnp.where(causal, 1.0, 0.0).astype(BF16)
    b = _dot_nn(jnp.concatenate([tri] * 3, axis=1),
                jnp.concatenate(_split3(g_ref[...]), axis=0))
    b_last = b[cc - 1:cc, :]
    b_mid = b[cc // 2 - 1:cc // 2, :]

    q = q_ref[...]
    k = k_ref[...] * (DK_HEAD ** -0.5)
    q_in = (q * jnp.exp(b)).astype(BF16)
    qh, qm, _ = _split3(q * jnp.exp(b - b_mid))
    kh, km, _ = _split3(k * jnp.exp(b_mid - b))
    k_d = (k * jnp.exp(b_last - b)).astype(BF16)

    rows = 2 * SUBLANES
    sub = lax.broadcasted_iota(jnp.int32, (rows, DK_TOTAL), 0)
    decay_rows = jnp.where(sub == 0, jnp.broadcast_to(jnp.exp(b_last), (rows, DK_TOTAL)), 0.0)
    decay_col = _dot_tn(jnp.concatenate(_split3(decay_rows), axis=0),
                        jnp.ones((3 * rows, LANES), BF16))

    for h in range(GLA_HEADS):
        dk = slice(h * DK_HEAD, (h + 1) * DK_HEAD)
        dv = slice(h * DV_HEAD, (h + 1) * DV_HEAD)
        q3 = jnp.concatenate([qh[:, dk], qh[:, dk], qm[:, dk]], axis=1)
        k3 = jnp.concatenate([kh[:, dk], km[:, dk], kh[:, dk]], axis=1)
        scores = jnp.where(causal, _dot_nt(q3, k3), 0.0).astype(BF16)
        v = v_ref[:, dv].astype(BF16)
        s_old = s_ref[h]
        o = _dot_nn(jnp.concatenate([q_in[:, dk], scores], axis=1),
                    jnp.concatenate([s_old.astype(BF16), v], axis=0))
        decay = jnp.concatenate([decay_col[dk, :]] * (DV_HEAD // LANES), axis=1)
        s_new = decay * s_old + _dot_tn(k_d[:, dk], v)
        s_ref[h] = s_new

        @pl.when(c == pl.num_programs(1) - 1)
        def _():
            s_out_ref[0, h] = s_new

        og_ref[:, dv] = _rms_gate(o, ng_ref[:, dv], r_ref[:, dv]).astype(BF16)


def _gla_prompt(qkvr, g, norm_g):
    nc = SEQ // GLA_CHUNK
    row = lambda b, c: b * nc + c
    return pl.pallas_call(
        _gla_prompt_kernel,
        grid=(N_PROMPT_SEQ, nc),
        in_specs=[pl.BlockSpec((GLA_CHUNK, DK_TOTAL), lambda b, c: (row(b, c), 0)),
                  pl.BlockSpec((GLA_CHUNK, DK_TOTAL), lambda b, c: (row(b, c), 1)),
                  pl.BlockSpec((GLA_CHUNK, DV_TOTAL), lambda b, c: (row(b, c), 1)),
                  pl.BlockSpec((GLA_CHUNK, DV_TOTAL), lambda b, c: (row(b, c), 2)),
                  pl.BlockSpec((GLA_CHUNK, DK_TOTAL), lambda b, c: (row(b, c), 0)),
                  pl.BlockSpec((1, DV_TOTAL), lambda b, c: (0, 0)),
                  pl.BlockSpec(memory_space=pl.ANY)],
        out_specs=[pl.BlockSpec((GLA_CHUNK, DV_TOTAL), lambda b, c: (row(b, c), 0)),
                   pl.BlockSpec((1, GLA_HEADS, DK_HEAD, DV_HEAD), lambda b, c: (b, 0, 0, 0))],
        out_shape=[jax.ShapeDtypeStruct((N_TOK, DV_TOTAL), BF16),
                   jax.ShapeDtypeStruct((N_PROMPT_SEQ, GLA_HEADS, DK_HEAD, DV_HEAD), F32)],
        scratch_shapes=[pltpu.VMEM((GLA_HEADS, DK_HEAD, DV_HEAD), F32)],
        input_output_aliases={6: 0},
        compiler_params=_params("arbitrary", "arbitrary"),
        name="gla_prompt",
    )(qkvr, qkvr, qkvr, qkvr, g, norm_g, jnp.zeros((N_TOK, DV_TOTAL), BF16))


def _gla_sample_kernel(q_ref, k_ref, v_ref, r_ref, g_ref, ng_ref, s_ref, og_in_ref, og_ref, s_out_ref,
                       o_scr):
    del og_in_ref
    bb = q_ref.shape[0]
    q = q_ref[...]
    k = k_ref[...] * (DK_HEAD ** -0.5)
    v = v_ref[...]
    eg = jnp.exp(g_ref[...])
    qe = q * eg
    qk = jnp.sum(q * k, axis=-1, keepdims=True)
    sub = lax.broadcasted_iota(jnp.int32, (bb, DK_HEAD), 0)
    for bi in range(bb):
        sel = sub == bi
        s_old = s_ref[bi, 0]
        decay = _col_bcast(jnp.where(sel, eg, 0.0), DV_HEAD)
        k_col = _col_bcast(jnp.where(sel, k, 0.0), DV_HEAD)
        qe_col = _col_bcast(jnp.where(sel, qe, 0.0), DV_HEAD)
        s_out_ref[bi, 0] = decay * s_old + k_col * v[bi:bi + 1, :]
        o_scr[bi:bi + 1, :] = jnp.sum(qe_col * s_old, axis=0, keepdims=True)
    o = qk * v + o_scr[...]
    og_ref[...] = _rms_gate(o, ng_ref[...], r_ref[...]).astype(BF16)


def _gla_sample(qkvr, g, norm_g, state, og):
    bb = GLA_SAMPLE_BATCH
    r0 = N_PROMPT // bb
    hk = DK_TOTAL // DK_HEAD
    hv = 2 * DK_TOTAL // DV_HEAD
    return pl.pallas_call(
        _gla_sample_kernel,
        grid=(N_SAMPLE // bb, GLA_HEADS),
        in_specs=[pl.BlockSpec((bb, DK_HEAD), lambda i, h: (r0 + i, h)),
                  pl.BlockSpec((bb, DK_HEAD), lambda i, h: (r0 + i, hk + h)),
                  pl.BlockSpec((bb, DV_HEAD), lambda i, h: (r0 + i, hv + h)),
                  pl.BlockSpec((bb, DV_HEAD), lambda i, h: (r0 + i, hv + GLA_HEADS + h)),
                  pl.BlockSpec((bb, DK_HEAD), lambda i, h: (r0 + i, h)),
                  pl.BlockSpec((1, DV_HEAD), lambda i, h: (0, h)),
                  pl.BlockSpec((bb, 1, DK_HEAD, DV_HEAD), lambda i, h: (i, h, 0, 0)),
                  pl.BlockSpec(memory_space=pl.ANY)],
        out_specs=[pl.BlockSpec((bb, DV_HEAD), lambda i, h: (r0 + i, h)),
                   pl.BlockSpec((bb, 1, DK_HEAD, DV_HEAD), lambda i, h: (i, h, 0, 0))],
        out_shape=[jax.ShapeDtypeStruct((N_TOK, DV_TOTAL), BF16),
                   jax.ShapeDtypeStruct((N_SAMPLE, GLA_HEADS, DK_HEAD, DV_HEAD), F32)],
        scratch_shapes=[pltpu.VMEM((bb, DV_HEAD), F32)],
        input_output_aliases={7: 0},
        compiler_params=_params("arbitrary", "arbitrary"),
        name="gla_sample",
    )(qkvr, qkvr, qkvr, qkvr, g, norm_g, state, og)


def _top2_of4(p):
    ranks = []
    for j in range(4):
        rk = jnp.zeros(p[j].shape, jnp.int32)
        for i in range(4):
            if i == j:
                continue
            beats = (p[i] >= p[j]) if i < j else (p[i] > p[j])
            rk = rk + jnp.where(beats, 1, 0)
        ranks.append(rk)

    def pick(rank):
        val = jnp.zeros(p[0].shape, F32)
        idx = jnp.zeros(p[0].shape, jnp.int32)
        for j in range(4):
            hit = ranks[j] == rank
            val = jnp.where(hit, p[j], val)
            idx = jnp.where(hit, j, idx)
        return val, idx

    v1, i1 = pick(0)
    v2, i2 = pick(1)
    return v1, i1, v2, i2


def _route(x, rw_ref, rb_ref):
    logits = _dot_nn(x.astype(BF16), rw_ref[...])
    n = x.shape[0]
    n_pad = -n % LANES
    if n_pad:
        logits = jnp.concatenate([logits, jnp.zeros((n_pad, LANES), F32)], axis=0)
    lt = logits.T[0:N_EXPERTS, 0:n] + rb_ref[...]
    e = jnp.exp(lt - jnp.max(lt, axis=0, keepdims=True))
    probs = e / jnp.sum(e, axis=0, keepdims=True)
    best = None
    for grp in range(N_GROUPS):
        rows = [probs[grp * EXPERTS_PER_GROUP + j:grp * EXPERTS_PER_GROUP + j + 1, :]
                for j in range(EXPERTS_PER_GROUP)]
        v1, i1, v2, i2 = _top2_of4(rows)
        score = v1 + v2
        cand = (score, v1, i1 + grp * EXPERTS_PER_GROUP, v2, i2 + grp * EXPERTS_PER_GROUP)
        if best is None:
            best = cand
        else:
            better = score > best[0]
            best = tuple(jnp.where(better, n, o) for n, o in zip(cand, best))
    _, v1, e1, v2, e2 = best
    denom = v1 + v2
    return (e1, e2), (v1 / denom, v2 / denom)


def _proj_norm_route_kernel(a_ref, *refs):
    *x_refs, w_ref, lg_ref, lb_ref, rw_ref, rb_ref, xo_ref, id_ref, gt_ref = refs
    x = x_refs[0][...] if len(x_refs) == 1 else _x_tile(pl.program_id(0), x_refs)
    sub = id_ref.shape[2]
    for s in range(id_ref.shape[0]):
        rows = slice(s * sub, (s + 1) * sub)
        y = _dot_nn(a_ref[rows, :], w_ref[...])
        xn = _layer_norm(ALPHA * x[rows, :] + y, lg_ref[...], lb_ref[...])
        xo_ref[rows, :] = xn
        (e1, e2), (g1, g2) = _route(xn, rw_ref, rb_ref)
        id_ref[s, 0:1, :] = e1
        id_ref[s, 1:2, :] = e2
        gt_ref[s, 0:1, :] = g1
        gt_ref[s, 1:2, :] = g2


def _router_operands(router_w, router_b):
    rw = jnp.pad(router_w, ((0, 0), (0, LANES - N_EXPERTS))).astype(BF16)
    return rw, router_b.reshape(N_EXPERTS, 1)


def _proj_norm_route(a, x, w_out, ln_g, ln_b, router):
    nt = N_TOK // ROW_TILE
    n_sub = TOKEN_TILE // ROW_TILE
    rw, rb = router
    const = lambda i: (0, 0)
    if isinstance(x, tuple):
        x_specs = _x_tile_specs(lambda i: i)
        x_args = [x[0]] * X_TILE_BLOCKS + [x[1]]
    else:
        x_specs = [pl.BlockSpec((TOKEN_TILE, D_MODEL), lambda i: (i, 0))]
        x_args = [x]
    x1, ids, gates = pl.pallas_call(
        _proj_norm_route_kernel,
        grid=(N_TOK // TOKEN_TILE,),
        in_specs=[pl.BlockSpec((TOKEN_TILE, D_MODEL), lambda i: (i, 0))] + x_specs + [
            pl.BlockSpec((D_MODEL, D_MODEL), const, pipeline_mode=pl.Buffered(1)),
            pl.BlockSpec((1, D_MODEL), const),
            pl.BlockSpec((1, D_MODEL), const),
            pl.BlockSpec((D_MODEL, LANES), const),
            pl.BlockSpec((N_EXPERTS, 1), const)],
        out_specs=[pl.BlockSpec((TOKEN_TILE, D_MODEL), lambda i: (i, 0)),
                   pl.BlockSpec((n_sub, TOP_K, ROW_TILE), lambda i: (i, 0, 0)),
                   pl.BlockSpec((n_sub, TOP_K, ROW_TILE), lambda i: (i, 0, 0))],
        out_shape=[jax.ShapeDtypeStruct((N_TOK, D_MODEL), F32),
                   jax.ShapeDtypeStruct((nt, TOP_K, ROW_TILE), jnp.int32),
                   jax.ShapeDtypeStruct((nt, TOP_K, ROW_TILE), F32)],
        compiler_params=_params("arbitrary"),
        name="proj_norm_route",
    )(a, *x_args, w_out.astype(BF16), ln_g.reshape(1, D_MODEL), ln_b.reshape(1, D_MODEL), rw, rb)
    ids = ids.transpose(0, 2, 1).reshape(N_TOK, TOP_K)
    gates = gates.transpose(0, 2, 1).reshape(N_TOK, TOP_K)
    return x1, ids, gates


def _stream_expert_weights(layer, tables, w_refs, wbuf_ref, wb_refs, sem_ref):
    te_ref, nx_ref, nt_ref = tables
    i = pl.program_id(0)

    def copies(expert):
        return [pltpu.make_async_copy(w.at[layer, expert], wbuf_ref.at[j], sem_ref.at[j])
                for j, w in enumerate(w_refs)]

    @pl.when(i == 0)
    def _():
        for c in copies(te_ref[0]):
            c.start()

    first = jnp.logical_or(i == 0, te_ref[i] != te_ref[jnp.maximum(i - 1, 0)])

    @pl.when(jnp.logical_and(first, i < nt_ref[0]))
    def _():
        for c in copies(te_ref[i]):
            c.wait()
        for j, wb in enumerate(wb_refs):
            wb[...] = wbuf_ref[j].astype(BF16)

        @pl.when(nx_ref[i] >= 0)
        def _():
            for c in copies(nx_ref[i]):
                c.start()


def _issue_row_copies(idx_ref, base, src_ref, dst_ref, sem, n_rows, stride=1):
    def body(g, carry):
        r0 = pl.multiple_of(g * SUBLANES, SUBLANES)
        for k in range(SUBLANES):
            row = idx_ref[base + stride * (r0 + k)]
            pltpu.make_async_copy(src_ref.at[pl.ds(row, 1)], dst_ref.at[pl.ds(r0 + k, 1)],
                                  sem).start(priority=k % 2)
        return carry
    lax.fori_loop(0, n_rows // SUBLANES, body, 0)


def _gather_kernel(idx_ref, nt_ref, src_ref, out_ref, buf_ref, sem_ref):
    i = pl.program_id(0)
    n = nt_ref[0]
    tg = out_ref.shape[0]

    def start_tile(tile, slot):
        _issue_row_copies(idx_ref, tile * tg, src_ref, buf_ref.at[slot], sem_ref.at[slot], tg)

    @pl.when(i == 0)
    def _():
        start_tile(0, 0)

    @pl.when(i + 1 < n)
    def _():
        start_tile(i + 1, (i + 1) % 2)

    @pl.when(i < n)
    def _():
        slot = i % 2
        pltpu.make_async_copy(src_ref.at[pl.ds(0, tg)], buf_ref.at[slot], sem_ref.at[slot]).wait()
        out_ref[...] = buf_ref[slot].astype(out_ref.dtype)

    @pl.when(i >= n)
    def _():
        out_ref[...] = jnp.zeros_like(out_ref)


def _gather_rows(src, idx, n_tiles, out_dtype, name):
    m = idx.shape[0]
    width = src.shape[1]
    return pl.pallas_call(
        _gather_kernel,
        grid_spec=pltpu.PrefetchScalarGridSpec(
            num_scalar_prefetch=2,
            grid=(m // MOE_TILE,),
            in_specs=[pl.BlockSpec(memory_space=pl.ANY)],
            out_specs=pl.BlockSpec((MOE_TILE, width), lambda i, idx, nt: (i, 0)),
            scratch_shapes=[pltpu.VMEM((2, MOE_TILE, width), src.dtype),
                            pltpu.SemaphoreType.DMA((2,))]),
        out_shape=jax.ShapeDtypeStruct((m, width), out_dtype),
        compiler_params=_params("arbitrary"),
        name=name,
    )(idx, n_tiles, src)


def _moe_up_kernel(layer, te_ref, nx_ref, nt_ref, xs_ref, w1_ref, w3_ref, hid_ref,
                   wbuf_ref, w1b_ref, w3b_ref, sem_ref):
    _stream_expert_weights(layer, (te_ref, nx_ref, nt_ref), (w1_ref, w3_ref), wbuf_ref,
                           (w1b_ref, w3b_ref), sem_ref)
    i = pl.program_id(0)

    @pl.when(i < nt_ref[0])
    def _():
        xs = xs_ref[...]
        h1 = _dot_nn(xs, w1b_ref[...])
        h3 = _dot_nn(xs, w3b_ref[...])
        hid_ref[...] = (_silu(h1) * h3).astype(BF16)

    @pl.when(i >= nt_ref[0])
    def _():
        hid_ref[...] = jnp.zeros_like(hid_ref)


def _moe_up(xs, w1, w3, layer, tables):
    last = lambda i, te, nx, nt: (jnp.minimum(i, nt[0] - 1), 0)
    return pl.pallas_call(
        functools.partial(_moe_up_kernel, layer),
        grid_spec=pltpu.PrefetchScalarGridSpec(
            num_scalar_prefetch=3,
            grid=(MOE_MAX_TILES,),
            in_specs=[pl.BlockSpec((MOE_TILE, D_MODEL), last),
                      pl.BlockSpec(memory_space=pl.ANY),
                      pl.BlockSpec(memory_space=pl.ANY)],
            out_specs=pl.BlockSpec((MOE_TILE, D_FF), lambda i, te, nx, nt: (i, 0)),
            scratch_shapes=[pltpu.VMEM((2, D_MODEL, D_FF), F32),
                            pltpu.VMEM((D_MODEL, D_FF), BF16), pltpu.VMEM((D_MODEL, D_FF), BF16),
                            pltpu.SemaphoreType.DMA((2,))]),
        out_shape=jax.ShapeDtypeStruct((N_ASSIGN_PAD, D_FF), BF16),
        compiler_params=_params("arbitrary"),
        name="moe_up",
    )(*tables, xs, w1, w3)


def _moe_down_kernel(layer, te_ref, nx_ref, nt_ref, hid_ref, w2_ref, ys_ref,
                     wbuf_ref, w2b_ref, sem_ref):
    _stream_expert_weights(layer, (te_ref, nx_ref, nt_ref), (w2_ref,), wbuf_ref,
                           (w2b_ref,), sem_ref)
    i = pl.program_id(0)

    @pl.when(i < nt_ref[0])
    def _():
        ys_ref[...] = _dot_nn(hid_ref[...], w2b_ref[...])

    @pl.when(i >= nt_ref[0])
    def _():
        ys_ref[...] = jnp.zeros_like(ys_ref)


def _moe_down(hid, w2, layer, tables):
    last = lambda i, te, nx, nt: (jnp.minimum(i, nt[0] - 1), 0)
    return pl.pallas_call(
        functools.partial(_moe_down_kernel, layer),
        grid_spec=pltpu.PrefetchScalarGridSpec(
            num_scalar_prefetch=3,
            grid=(MOE_MAX_TILES,),
            in_specs=[pl.BlockSpec((MOE_TILE, D_FF), last),
                      pl.BlockSpec(memory_space=pl.ANY)],
            out_specs=pl.BlockSpec((MOE_TILE, D_MODEL), lambda i, te, nx, nt: (i, 0)),
            scratch_shapes=[pltpu.VMEM((1, D_FF, D_MODEL), F32),
                            pltpu.VMEM((D_FF, D_MODEL), BF16),
                            pltpu.SemaphoreType.DMA((1,))]),
        out_shape=jax.ShapeDtypeStruct((N_ASSIGN_PAD, D_MODEL), F32),
        compiler_params=_params("arbitrary"),
        name="moe_down",
    )(*tables, hid, w2)


def _combine_norm_kernel(final, pos_ref, x_ref, ys_ref, gt_ref, lg_ref, lb_ref, o0_ref, o1_ref,
                         buf_ref, sem_ref):
    i = pl.program_id(0)
    n = pl.num_programs(0)
    tm = x_ref.shape[0]

    def start_tile(tile, slot):
        for k in range(TOP_K):
            _issue_row_copies(pos_ref, tile * tm * TOP_K + k, ys_ref, buf_ref.at[slot, k],
                              sem_ref.at[slot], tm, stride=TOP_K)

    @pl.when(i == 0)
    def _():
        start_tile(0, 0)

    @pl.when(i + 1 < n)
    def _():
        start_tile(i + 1, (i + 1) % 2)

    slot = i % 2
    for k in range(TOP_K):
        pltpu.make_async_copy(ys_ref.at[pl.ds(0, tm)], buf_ref.at[slot, k], sem_ref.at[slot]).wait()
    gt = gt_ref[...]
    moe = buf_ref[slot, 0] * gt[:, 0:1] + buf_ref[slot, 1] * gt[:, 1:2]
    xn = _layer_norm(ALPHA * x_ref[...] + moe, lg_ref[...], lb_ref[...])
    if final:
        @pl.when(i < n - 1)
        def _():
            o0_ref[...] = xn

        @pl.when(i == n - 1)
        def _():
            o1_ref[...] = xn
    else:
        o0_ref[...] = xn
        o1_ref[...] = xn.astype(BF16)


def _combine_norm(x, ys, pos, gates, ln_g, ln_b, final):
    tm = N_SAMPLE if final else ROW_TILE
    const = lambda i, pos: (0, 0)
    tile = lambda i, pos: (i, 0)
    if final:
        n_prompt_tiles = N_PROMPT // tm
        out_specs = [pl.BlockSpec((tm, D_MODEL), lambda i, pos: (jnp.minimum(i, n_prompt_tiles - 1), 0)),
                     pl.BlockSpec((tm, D_MODEL), const)]
        out_shape = [jax.ShapeDtypeStruct((N_PROMPT, D_MODEL), F32),
                     jax.ShapeDtypeStruct((N_SAMPLE, D_MODEL), F32)]
    else:
        out_specs = [pl.BlockSpec((tm, D_MODEL), tile), pl.BlockSpec((tm, D_MODEL), tile)]
        out_shape = [jax.ShapeDtypeStruct((N_TOK, D_MODEL), F32),
                     jax.ShapeDtypeStruct((N_TOK, D_MODEL), BF16)]
    return pl.pallas_call(
        functools.partial(_combine_norm_kernel, final),
        grid_spec=pltpu.PrefetchScalarGridSpec(
            num_scalar_prefetch=1,
            grid=(N_TOK // tm,),
            in_specs=[pl.BlockSpec((tm, D_MODEL), tile),
                      pl.BlockSpec(memory_space=pl.ANY),
                      pl.BlockSpec((tm, TOP_K), tile),
                      pl.BlockSpec((1, D_MODEL), const),
                      pl.BlockSpec((1, D_MODEL), const)],
            out_specs=out_specs,
            scratch_shapes=[pltpu.VMEM((2, TOP_K, tm, D_MODEL), F32),
                            pltpu.SemaphoreType.DMA((2,))]),
        out_shape=out_shape,
        compiler_params=_params("arbitrary"),
        name="combine_norm",
    )(pos, x, ys, gates, ln_g.reshape(1, D_MODEL), ln_b.reshape(1, D_MODEL))


def _routing_tables(ids):
    e = ids.reshape(-1)
    onehot = (e[:, None] == jnp.arange(N_EXPERTS, dtype=jnp.int32)[None, :]).astype(jnp.int32)
    csum = jnp.cumsum(onehot, axis=0)
    rank = jnp.sum(onehot * csum, axis=1) - 1
    sizes = csum[-1]
    tiles_per = (sizes + MOE_TILE - 1) // MOE_TILE
    tile_end = jnp.cumsum(tiles_per)
    tile_start = tile_end - tiles_per
    n_tiles = tile_end[-1]
    pos = (jnp.sum(onehot * (tile_start * MOE_TILE)[None, :], axis=1) + rank).astype(jnp.int32)
    tile = jnp.arange(MOE_MAX_TILES, dtype=jnp.int32)
    owner = jnp.sum((tile_end[None, :] <= jnp.minimum(tile, n_tiles - 1)[:, None]).astype(jnp.int32), axis=1)
    order = jnp.argsort(e, stable=True).astype(jnp.int32)
    owner_hot = (owner[:, None] == jnp.arange(N_EXPERTS, dtype=jnp.int32)[None, :]).astype(jnp.int32)
    size_start = jnp.cumsum(sizes) - sizes
    tile_first = jnp.sum(owner_hot * (size_start - tile_start * MOE_TILE)[None, :], axis=1)
    tile_limit = jnp.sum(owner_hot * (size_start + sizes)[None, :], axis=1)
    row = jnp.arange(N_ASSIGN_PAD, dtype=jnp.int32).reshape(MOE_MAX_TILES, MOE_TILE)
    src = tile_first[:, None] + row
    valid = src < tile_limit[:, None]
    row_token = jnp.where(valid, order[jnp.clip(src, 0, N_ASSIGN - 1)] // TOP_K, row % N_TOK).reshape(-1)
    experts = jnp.arange(N_EXPERTS, dtype=jnp.int32)
    later = lax.cummin(jnp.where(tiles_per > 0, experts, N_EXPERTS), reverse=True)
    next_expert = jnp.concatenate([later[1:], jnp.full((1,), N_EXPERTS, jnp.int32)])
    next_expert = jnp.where(next_expert >= N_EXPERTS, -1, next_expert)
    tile_next = jnp.sum(owner_hot * next_expert[None, :], axis=1)
    n_tiles = n_tiles.reshape(1).astype(jnp.int32)
    tables = (owner.astype(jnp.int32), tile_next.astype(jnp.int32), n_tiles)
    return pos, row_token.astype(jnp.int32), tables


def _moe_block(x, ids, gates, w1, w3, w2, layer, ln_g, ln_b, final):
    pos, row_token, tables = _routing_tables(ids)
    xs = _gather_rows(x, row_token, tables[2], BF16, "moe_gather")
    hid = _moe_up(xs, w1, w3, layer, tables)
    ys = _moe_down(hid, w2, layer, tables)
    return _combine_norm(x, ys, pos, gates, ln_g, ln_b, final)


def _conv_in_kernel(x_ref, wb_ref, wc_ref, wh_ref, cw_ref, s0_ref, s1_ref, gbz_ref, u_ref,
                    wbb_ref, wcb_ref, whb_ref, ubuf_ref):
    i = pl.program_id(1)
    tm = x_ref.shape[0]
    pad = SUBLANES

    @pl.when(i == 0)
    def _():
        wbb_ref[...] = wb_ref[...].astype(BF16)
        wcb_ref[...] = wc_ref[...].astype(BF16)
        whb_ref[...] = wh_ref[...].astype(BF16)
        ubuf_ref[0:pad, :] = jnp.zeros((pad, ubuf_ref.shape[1]), F32)

    w0 = cw_ref[0:1, :]
    w1 = cw_ref[1:2, :]
    w2 = cw_ref[2:3, :]
    sub = ROW_TILE
    for s in range(tm // sub):
        lo, hi = s * sub, (s + 1) * sub
        x = x_ref[lo:hi, :]
        gb = _dot_nn(x, wbb_ref[...])
        u = _dot_nn(x, wcb_ref[...]) * _dot_nn(x, whb_ref[...])
        u_ref[lo:hi, :] = u
        ubuf_ref[pad + lo:pad + hi, :] = u
        t = (i * tm + lo + lax.broadcasted_iota(jnp.int32, (sub, 1), 0)) & (SEQ - 1)
        u_m1 = jnp.where(t >= 1, ubuf_ref[pad - 1 + lo:pad - 1 + hi, :], 0.0)
        u_m2 = jnp.where(t >= 2, ubuf_ref[pad - 2 + lo:pad - 2 + hi, :], 0.0)
        z = u_m2 * w0 + u_m1 * w1 + u * w2
        gbz_ref[lo:hi, :] = (gb * z).astype(BF16)
    ubuf_ref[0:pad, :] = ubuf_ref[tm:tm + pad, :]

    @pl.when(i == pl.num_programs(1) - 1)
    def _():
        first = sub - N_SAMPLE
        z_s = s0_ref[...] * w0 + s1_ref[...] * w1 + u[first:sub, :] * w2
        gbz_ref[tm - N_SAMPLE:tm, :] = (gb[first:sub, :] * z_s).astype(BF16)


def _conv_in(xb, w_in, conv_w, state):
    tn = 512
    nb = D_MODEL // tn
    s0 = state[:, 0, :]
    s1 = state[:, 1, :]
    return pl.pallas_call(
        _conv_in_kernel,
        grid=(nb, N_TOK // TOKEN_TILE),
        in_specs=[pl.BlockSpec((TOKEN_TILE, D_MODEL), lambda j, i: (i, 0)),
                  pl.BlockSpec((D_MODEL, tn), lambda j, i: (0, j)),
                  pl.BlockSpec((D_MODEL, tn), lambda j, i: (0, nb + j)),
                  pl.BlockSpec((D_MODEL, tn), lambda j, i: (0, 2 * nb + j)),
                  pl.BlockSpec((CONV_W, tn), lambda j, i: (0, j)),
                  pl.BlockSpec((N_SAMPLE, tn), lambda j, i: (0, j)),
                  pl.BlockSpec((N_SAMPLE, tn), lambda j, i: (0, j))],
        out_specs=[pl.BlockSpec((TOKEN_TILE, tn), lambda j, i: (i, j)),
                   pl.BlockSpec((TOKEN_TILE, tn), lambda j, i: (i, j))],
        out_shape=[jax.ShapeDtypeStruct((N_TOK, D_MODEL), BF16),
                   jax.ShapeDtypeStruct((N_TOK, D_MODEL), F32)],
        scratch_shapes=[pltpu.VMEM((D_MODEL, tn), BF16), pltpu.VMEM((D_MODEL, tn), BF16),
                        pltpu.VMEM((D_MODEL, tn), BF16),
                        pltpu.VMEM((TOKEN_TILE + 2 * SUBLANES, tn), F32)],
        compiler_params=_params("arbitrary", "arbitrary"),
        name="conv_in",
    )(xb, w_in, w_in, w_in, conv_w, s0, s1)


def kernel(x_prompt, x_sample, state_gla, state_conv, router_w, router_b, gla_w_in, gla_w_gate, gla_b_gate,
           gla_norm_g, gla_w_out, conv_w_in, conv_w, conv_w_out, ln_mix_g, ln_mix_b, ln_ffn_g, ln_ffn_b,
           moe_w1, moe_w3, moe_w2):
    xp = x_prompt.reshape(N_PROMPT, D_MODEL)
    xs = x_sample.reshape(N_SAMPLE, D_MODEL)
    router = _router_operands(router_w, router_b)

    w_in_t = gla_w_in[0].T
    qkvr, gl = _gla_in_proj(xp, xs, w_in_t)
    g = _gla_gate(gl, gla_w_gate[0], gla_b_gate[0])
    norm_g = gla_norm_g[0].reshape(1, DV_TOTAL)
    og, s_prompt = _gla_prompt(qkvr, g, norm_g)
    og, s_sample = _gla_sample(qkvr, g, norm_g, state_gla[0], og)
    x1, ids, gates = _proj_norm_route(og, (xp, xs), gla_w_out[0], ln_mix_g[0], ln_mix_b[0], router)
    x2, x2b = _moe_block(x1, ids, gates, moe_w1, moe_w3, moe_w2, 0, ln_ffn_g[0], ln_ffn_b[0], False)

    gbz, u = _conv_in(x2b, conv_w_in[0], conv_w[0], state_conv[0])
    x3, ids, gates = _proj_norm_route(gbz, x2, conv_w_out[0], ln_mix_g[1], ln_mix_b[1], router)
    y_p, y_s = _moe_block(x3, ids, gates, moe_w1, moe_w3, moe_w2, 1, ln_ffn_g[1], ln_ffn_b[1], True)

    y_prompt = y_p.reshape(N_PROMPT_SEQ, SEQ, D_MODEL)
    y_sample = y_s.reshape(N_SAMPLE, 1, D_MODEL)
    conv_prompt = jnp.stack([u[(b + 1) * SEQ - (CONV_W - 1):(b + 1) * SEQ] for b in range(N_PROMPT_SEQ)])
    conv_sample = jnp.concatenate([state_conv[0][:, 1:, :], u[N_PROMPT:][:, None, :]], axis=1)
    return (y_prompt, y_sample, s_prompt[None], conv_prompt[None], s_sample[None], conv_sample[None])
```

```python
import functools

import jax
import jax.numpy as jnp
from jax import lax
from jax.experimental import pallas as pl
from jax.experimental.pallas import tpu as pltpu

F32 = jnp.float32
BF16 = jnp.bfloat16

D_MODEL = 2048
N_PROMPT_SEQ = 4
SEQ = 2048
N_PROMPT = N_PROMPT_SEQ * SEQ
N_SAMPLE = 128
N_TOK = N_PROMPT + N_SAMPLE
DEPTH = 2

GLA_HEADS = 4
DK_TOTAL = D_MODEL // 2
DV_TOTAL = D_MODEL
DK_HEAD = DK_TOTAL // GLA_HEADS
DV_HEAD = DV_TOTAL // GLA_HEADS
GATE_RANK = 16
GATE_NORMALIZER = 16.0
GLA_QKVR_WIDTH = 2 * DK_TOTAL + 2 * DV_TOTAL
CONV_W = 3
N_EXPERTS = 16
N_GROUPS = 4
EXPERTS_PER_GROUP = N_EXPERTS // N_GROUPS
TOP_K = 2
D_FF = D_MODEL // 2
ALPHA = (2.0 * DEPTH) ** 0.25
LN_EPS = 1e-5
RMS_EPS = 1e-6

LANES = 128
SUBLANES = 8
VMEM_LIMIT_BYTES = 56 * 1024 * 1024

TOKEN_TILE = 640
ROW_TILE = 320
GLA_CHUNK = 64
GLA_SAMPLE_BATCH = 16
MOE_TILE = 256
N_ASSIGN = N_TOK * TOP_K
MOE_MAX_TILES = N_ASSIGN // MOE_TILE + N_EXPERTS
N_ASSIGN_PAD = MOE_MAX_TILES * MOE_TILE


def _params(*semantics):
    return pltpu.CompilerParams(dimension_semantics=semantics, vmem_limit_bytes=VMEM_LIMIT_BYTES)


def _split3(x):
    hi = x.astype(BF16)
    r1 = x - hi.astype(F32)
    mid = r1.astype(BF16)
    lo = (r1 - mid.astype(F32)).astype(BF16)
    return hi, mid, lo


def _cast_weight(src_ref, dst_ref):
    rows = 64
    n = src_ref.shape[0] // rows

    def body(c, carry):
        r = pl.multiple_of(c * rows, rows)
        dst_ref[pl.ds(r, rows), :] = src_ref[pl.ds(r, rows), :].astype(BF16)
        return carry
    lax.fori_loop(0, n, body, 0)


def _dot_nn(a, b):
    return jnp.dot(a, b, preferred_element_type=F32)


def _dot_tn(a, b):
    return lax.dot_general(a, b, (((0,), (0,)), ((), ())), preferred_element_type=F32)


def _dot_nt(a, b):
    return lax.dot_general(a, b, (((1,), (1,)), ((), ())), preferred_element_type=F32)


def _col_bcast(rows, n):
    ones = jnp.ones((rows.shape[0], LANES), BF16)
    hi, mid, lo = _split3(rows)
    col = _dot_tn(hi, ones) + _dot_tn(mid, ones) + _dot_tn(lo, ones)
    return jnp.concatenate([col] * (n // LANES), axis=1)


def _silu(r):
    return r / (1.0 + jnp.exp(-r))


def _layer_norm(h, g, b):
    mu = jnp.mean(h, axis=-1, keepdims=True)
    d = h - mu
    var = jnp.mean(d * d, axis=-1, keepdims=True)
    return d * lax.rsqrt(var + LN_EPS) * g + b


X_TILE_BLOCKS = TOKEN_TILE // N_SAMPLE


def _x_tile_specs(tile_index):
    last = N_PROMPT // N_SAMPLE - 1

    def prompt_spec(k):
        return pl.BlockSpec((N_SAMPLE, D_MODEL),
                            lambda *g: (jnp.minimum(tile_index(*g) * X_TILE_BLOCKS + k, last), 0))
    return ([prompt_spec(k) for k in range(X_TILE_BLOCKS)]
            + [pl.BlockSpec((N_SAMPLE, D_MODEL), lambda *g: (0, 0))])


def _x_tile(i, x_refs):
    *prompt_refs, sample_ref = x_refs
    tail_is_sample = (i + 1) * TOKEN_TILE > N_PROMPT
    blocks = [r[...] for r in prompt_refs[:-1]]
    blocks.append(jnp.where(tail_is_sample, sample_ref[...], prompt_refs[-1][...]))
    return jnp.concatenate(blocks, axis=0)


def _gla_in_proj_kernel(*refs):
    *x_refs, wt_ref, wgl_ref, o_ref, gl_ref, wb_ref = refs
    j = pl.program_id(0)
    i = pl.program_id(1)

    @pl.when(i == 0)
    def _():
        _cast_weight(wt_ref, wb_ref)

    x = _x_tile(i, x_refs).astype(BF16)
    for s in range(TOKEN_TILE // ROW_TILE):
        rows = slice(s * ROW_TILE, (s + 1) * ROW_TILE)
        o_ref[rows, :] = _dot_nt(x[rows, :], wb_ref[...]).astype(o_ref.dtype)

    @pl.when(j == pl.num_programs(0) - 1)
    def _():
        wgl = jnp.concatenate([wgl_ref[...].astype(BF16), jnp.zeros((LANES - GATE_RANK, D_MODEL), BF16)],
                              axis=0)
        gl_ref[...] = _dot_nt(x, wgl)

    @pl.when(j < pl.num_programs(0) - 1)
    def _():
        gl_ref[...] = jnp.zeros_like(gl_ref)


def _gla_in_proj(x_prompt, x_sample, w_t):
    tn = 1024
    n_pass = GLA_QKVR_WIDTH // tn
    n_tiles = N_TOK // TOKEN_TILE
    return pl.pallas_call(
        _gla_in_proj_kernel,
        grid=(n_pass, n_tiles),
        in_specs=_x_tile_specs(lambda j, i: i) + [
            pl.BlockSpec((tn, D_MODEL), lambda j, i: (j, 0)),
            pl.BlockSpec((GATE_RANK, D_MODEL), lambda j, i: (GLA_QKVR_WIDTH // GATE_RANK, 0))],
        out_specs=[pl.BlockSpec((TOKEN_TILE, tn), lambda j, i: (i, j)),
                   pl.BlockSpec((TOKEN_TILE, LANES), lambda j, i: (jnp.where(j == n_pass - 1, i, n_tiles), 0))],
        out_shape=[jax.ShapeDtypeStruct((N_TOK, GLA_QKVR_WIDTH), F32),
                   jax.ShapeDtypeStruct((N_TOK + TOKEN_TILE, LANES), F32)],
        scratch_shapes=[pltpu.VMEM((tn, D_MODEL), BF16)],
        compiler_params=_params("arbitrary", "arbitrary"),
        name="gla_in_proj",
    )(*([x_prompt] * X_TILE_BLOCKS), x_sample, w_t, w_t)


def _gate_kernel(gl_ref, wg_ref, bg_ref, g_ref):
    z = _dot_nn(gl_ref[...].astype(BF16), wg_ref[...]) + bg_ref[...]
    log_sig = jnp.minimum(z, 0.0) - jnp.log1p(jnp.exp(-jnp.abs(z)))
    g_ref[...] = log_sig * (1.0 / GATE_NORMALIZER)


def _gla_gate(gl, w_gate, b_gate):
    wg = jnp.pad(w_gate, ((0, LANES - GATE_RANK), (0, 0))).astype(BF16)
    return pl.pallas_call(
        _gate_kernel,
        grid=(N_TOK // TOKEN_TILE,),
        in_specs=[pl.BlockSpec((TOKEN_TILE, LANES), lambda i: (i, 0)),
                  pl.BlockSpec((LANES, DK_TOTAL), lambda i: (0, 0)),
                  pl.BlockSpec((1, DK_TOTAL), lambda i: (0, 0))],
        out_specs=pl.BlockSpec((TOKEN_TILE, DK_TOTAL), lambda i: (i, 0)),
        out_shape=jax.ShapeDtypeStruct((N_TOK, DK_TOTAL), F32),
        compiler_params=_params("arbitrary"),
        name="gla_gate",
    )(gl, wg, b_gate.reshape(1, DK_TOTAL))


def _rms_gate(o, norm_g, r):
    o = o * lax.rsqrt(jnp.mean(o * o, axis=-1, keepdims=True) + RMS_EPS)
    return (o * norm_g) * _silu(r)


def _gla_prompt_kernel(q_ref, k_ref, v_ref, r_ref, g_ref, ng_ref, og_in_ref, og_ref, s_out_ref, s_ref):
    del og_in_ref
    c = pl.program_id(1)
    cc = q_ref.shape[0]

    @pl.when(c == 0)
    def _():
        s_ref[...] = jnp.zeros_like(s_ref)

    row = lax.broadcasted_iota(jnp.int32, (cc, cc), 0)
    col = lax.broadcasted_iota(jnp.int32, (cc, cc), 1)
    causal = col <= row
    tri = jnp.where(causal, 1.0, 0.0).astype(BF16)
    b = _dot_nn(jnp.concatenate([tri] * 3, axis=1),
                jnp.concatenate(_split3(g_ref[...]), axis=0))
    b_last = b[cc - 1:cc, :]
    b_mid = b[cc // 2 - 1:cc // 2, :]

    q = q_ref[...]
    k = k_ref[...] * (DK_HEAD ** -0.5)
    q_in = (q * jnp.exp(b)).astype(BF16)
    qh, qm, _ = _split3(q * jnp.exp(b - b_mid))
    kh, km, _ = _split3(k * jnp.exp(b_mid - b))
    k_d = (k * jnp.exp(b_last - b)).astype(BF16)

    rows = 2 * SUBLANES
    sub = lax.broadcasted_iota(jnp.int32, (rows, DK_TOTAL), 0)
    decay_rows = jnp.where(sub == 0, jnp.broadcast_to(jnp.exp(b_last), (rows, DK_TOTAL)), 0.0)
    decay_col = _dot_tn(jnp.concatenate(_split3(decay_rows), axis=0),
                        jnp.ones((3 * rows, LANES), BF16))

    for h in range(GLA_HEADS):
        dk = slice(h * DK_HEAD, (h + 1) * DK_HEAD)
        dv = slice(h * DV_HEAD, (h + 1) * DV_HEAD)
        q3 = jnp.concatenate([qh[:, dk], qh[:, dk], qm[:, dk]], axis=1)
        k3 = jnp.concatenate([kh[:, dk], km[:, dk], kh[:, dk]], axis=1)
        scores = jnp.where(causal, _dot_nt(q3, k3), 0.0).astype(BF16)
        v = v_ref[:, dv].astype(BF16)
        s_old = s_ref[h]
        o = _dot_nn(jnp.concatenate([q_in[:, dk], scores], axis=1),
                    jnp.concatenate([s_old.astype(BF16), v], axis=0))
        decay = jnp.concatenate([decay_col[dk, :]] * (DV_HEAD // LANES), axis=1)
        s_new = decay * s_old + _dot_tn(k_d[:, dk], v)
        s_ref[h] = s_new

        @pl.when(c == pl.num_programs(1) - 1)
        def _():
            s_out_ref[0, h] = s_new

        og_ref[:, dv] = _rms_gate(o, ng_ref[:, dv], r_ref[:, dv]).astype(BF16)


def _gla_prompt(qkvr, g, norm_g):
    nc = SEQ // GLA_CHUNK
    row = lambda b, c: b * nc + c
    return pl.pallas_call(
        _gla_prompt_kernel,
        grid=(N_PROMPT_SEQ, nc),
        in_specs=[pl.BlockSpec((GLA_CHUNK, DK_TOTAL), lambda b, c: (row(b, c), 0)),
                  pl.BlockSpec((GLA_CHUNK, DK_TOTAL), lambda b, c: (row(b, c), 1)),
                  pl.BlockSpec((GLA_CHUNK, DV_TOTAL), lambda b, c: (row(b, c), 1)),
                  pl.BlockSpec((GLA_CHUNK, DV_TOTAL), lambda b, c: (row(b, c), 2)),
                  pl.BlockSpec((GLA_CHUNK, DK_TOTAL), lambda b, c: (row(b, c), 0)),
                  pl.BlockSpec((1, DV_TOTAL), lambda b, c: (0, 0)),
                  pl.BlockSpec(memory_space=pl.ANY)],
        out_specs=[pl.BlockSpec((GLA_CHUNK, DV_TOTAL), lambda b, c: (row(b, c), 0)),
                   pl.BlockSpec((1, GLA_HEADS, DK_HEAD, DV_HEAD), lambda b, c: (b, 0, 0, 0))],
        out_shape=[jax.ShapeDtypeStruct((N_TOK, DV_TOTAL), BF16),
                   jax.ShapeDtypeStruct((N_PROMPT_SEQ, GLA_HEADS, DK_HEAD, DV_HEAD), F32)],
        scratch_shapes=[pltpu.VMEM((GLA_HEADS, DK_HEAD, DV_HEAD), F32)],
        input_output_aliases={6: 0},
        compiler_params=_params("arbitrary", "arbitrary"),
        name="gla_prompt",
    )(qkvr, qkvr, qkvr, qkvr, g, norm_g, jnp.zeros((N_TOK, DV_TOTAL), BF16))


def _gla_sample_kernel(q_ref, k_ref, v_ref, r_ref, g_ref, ng_ref, s_ref, og_in_ref, og_ref, s_out_ref,
                       o_scr):
    del og_in_ref
    bb = q_ref.shape[0]
    q = q_ref[...]
    k = k_ref[...] * (DK_HEAD ** -0.5)
    v = v_ref[...]
    eg = jnp.exp(g_ref[...])
    qe = q * eg
    qk = jnp.sum(q * k, axis=-1, keepdims=True)
    sub = lax.broadcasted_iota(jnp.int32, (bb, DK_HEAD), 0)
    for bi in range(bb):
        sel = sub == bi
        s_old = s_ref[bi, 0]
        decay = _col_bcast(jnp.where(sel, eg, 0.0), DV_HEAD)
        k_col = _col_bcast(jnp.where(sel, k, 0.0), DV_HEAD)
        qe_col = _col_bcast(jnp.where(sel, qe, 0.0), DV_HEAD)
        s_out_ref[bi, 0] = decay * s_old + k_col * v[bi:bi + 1, :]
        o_scr[bi:bi + 1, :] = jnp.sum(qe_col * s_old, axis=0, keepdims=True)
    o = qk * v + o_scr[...]
    og_ref[...] = _rms_gate(o, ng_ref[...], r_ref[...]).astype(BF16)


def _gla_sample(qkvr, g, norm_g, state, og):
    bb = GLA_SAMPLE_BATCH
    r0 = N_PROMPT // bb
    hk = DK_TOTAL // DK_HEAD
    hv = 2 * DK_TOTAL // DV_HEAD
    return pl.pallas_call(
        _gla_sample_kernel,
        grid=(N_SAMPLE // bb, GLA_HEADS),
        in_specs=[pl.BlockSpec((bb, DK_HEAD), lambda i, h: (r0 + i, h)),
                  pl.BlockSpec((bb, DK_HEAD), lambda i, h: (r0 + i, hk + h)),
                  pl.BlockSpec((bb, DV_HEAD), lambda i, h: (r0 + i, hv + h)),
                  pl.BlockSpec((bb, DV_HEAD), lambda i, h: (r0 + i, hv + GLA_HEADS + h)),
                  pl.BlockSpec((bb, DK_HEAD), lambda i, h: (r0 + i, h)),
                  pl.BlockSpec((1, DV_HEAD), lambda i, h: (0, h)),
                  pl.BlockSpec((bb, 1, DK_HEAD, DV_HEAD), lambda i, h: (i, h, 0, 0)),
                  pl.BlockSpec(memory_space=pl.ANY)],
        out_specs=[pl.BlockSpec((bb, DV_HEAD), lambda i, h: (r0 + i, h)),
                   pl.BlockSpec((bb, 1, DK_HEAD, DV_HEAD), lambda i, h: (i, h, 0, 0))],
        out_shape=[jax.ShapeDtypeStruct((N_TOK, DV_TOTAL), BF16),
                   jax.ShapeDtypeStruct((N_SAMPLE, GLA_HEADS, DK_HEAD, DV_HEAD), F32)],
        scratch_shapes=[pltpu.VMEM((bb, DV_HEAD), F32)],
        input_output_aliases={7: 0},
        compiler_params=_params("arbitrary", "arbitrary"),
        name="gla_sample",
    )(qkvr, qkvr, qkvr, qkvr, g, norm_g, state, og)


def _top2_of4(p):
    ranks = []
    for j in range(4):
        rk = jnp.zeros(p[j].shape, jnp.int32)
        for i in range(4):
            if i == j:
                continue
            beats = (p[i] >= p[j]) if i < j else (p[i] > p[j])
            rk = rk + jnp.where(beats, 1, 0)
        ranks.append(rk)

    def pick(rank):
        val = jnp.zeros(p[0].shape, F32)
        idx = jnp.zeros(p[0].shape, jnp.int32)
        for j in range(4):
            hit = ranks[j] == rank
            val = jnp.where(hit, p[j], val)
            idx = jnp.where(hit, j, idx)
        return val, idx

    v1, i1 = pick(0)
    v2, i2 = pick(1)
    return v1, i1, v2, i2


def _route(x, rw_ref, rb_ref):
    logits = _dot_nn(x.astype(BF16), rw_ref[...])
    n = x.shape[0]
    n_pad = -n % LANES
    if n_pad:
        logits = jnp.concatenate([logits, jnp.zeros((n_pad, LANES), F32)], axis=0)
    lt = logits.T[0:N_EXPERTS, 0:n] + rb_ref[...]
    e = jnp.exp(lt - jnp.max(lt, axis=0, keepdims=True))
    probs = e / jnp.sum(e, axis=0, keepdims=True)
    best = None
    for grp in range(N_GROUPS):
        rows = [probs[grp * EXPERTS_PER_GROUP + j:grp * EXPERTS_PER_GROUP + j + 1, :]
                for j in range(EXPERTS_PER_GROUP)]
        v1, i1, v2, i2 = _top2_of4(rows)
        score = v1 + v2
        cand = (score, v1, i1 + grp * EXPERTS_PER_GROUP, v2, i2 + grp * EXPERTS_PER_GROUP)
        if best is None:
            best = cand
        else:
            better = score > best[0]
            best = tuple(jnp.where(better, n, o) for n, o in zip(cand, best))
    _, v1, e1, v2, e2 = best
    denom = v1 + v2
    return (e1, e2), (v1 / denom, v2 / denom)


def _proj_norm_route_kernel(a_ref, *refs):
    *x_refs, w_ref, lg_ref, lb_ref, rw_ref, rb_ref, xo_ref, id_ref, gt_ref = refs
    x = x_refs[0][...] if len(x_refs) == 1 else _x_tile(pl.program_id(0), x_refs)
    sub = id_ref.shape[2]
    for s in range(id_ref.shape[0]):
        rows = slice(s * sub, (s + 1) * sub)
        y = _dot_nn(a_ref[rows, :], w_ref[...])
        xn = _layer_norm(ALPHA * x[rows, :] + y, lg_ref[...], lb_ref[...])
        xo_ref[rows, :] = xn
        (e1, e2), (g1, g2) = _route(xn, rw_ref, rb_ref)
        id_ref[s, 0:1, :] = e1
        id_ref[s, 1:2, :] = e2
        gt_ref[s, 0:1, :] = g1
        gt_ref[s, 1:2, :] = g2


def _router_operands(router_w, router_b):
    rw = jnp.pad(router_w, ((0, 0), (0, LANES - N_EXPERTS))).astype(BF16)
    return rw, router_b.reshape(N_EXPERTS, 1)


def _proj_norm_route(a, x, w_out, ln_g, ln_b, router):
    nt = N_TOK // ROW_TILE
    n_sub = TOKEN_TILE // ROW_TILE
    rw, rb = router
    const = lambda i: (0, 0)
    if isinstance(x, tuple):
        x_specs = _x_tile_specs(lambda i: i)
        x_args = [x[0]] * X_TILE_BLOCKS + [x[1]]
    else:
        x_specs = [pl.BlockSpec((TOKEN_TILE, D_MODEL), lambda i: (i, 0))]
        x_args = [x]
    x1, ids, gates = pl.pallas_call(
        _proj_norm_route_kernel,
        grid=(N_TOK // TOKEN_TILE,),
        in_specs=[pl.BlockSpec((TOKEN_TILE, D_MODEL), lambda i: (i, 0))] + x_specs + [
            pl.BlockSpec((D_MODEL, D_MODEL), const, pipeline_mode=pl.Buffered(1)),
            pl.BlockSpec((1, D_MODEL), const),
            pl.BlockSpec((1, D_MODEL), const),
            pl.BlockSpec((D_MODEL, LANES), const),
            pl.BlockSpec((N_EXPERTS, 1), const)],
        out_specs=[pl.BlockSpec((TOKEN_TILE, D_MODEL), lambda i: (i, 0)),
                   pl.BlockSpec((n_sub, TOP_K, ROW_TILE), lambda i: (i, 0, 0)),
                   pl.BlockSpec((n_sub, TOP_K, ROW_TILE), lambda i: (i, 0, 0))],
        out_shape=[jax.ShapeDtypeStruct((N_TOK, D_MODEL), F32),
                   jax.ShapeDtypeStruct((nt, TOP_K, ROW_TILE), jnp.int32),
                   jax.ShapeDtypeStruct((nt, TOP_K, ROW_TILE), F32)],
        compiler_params=_params("arbitrary"),
        name="proj_norm_route",
    )(a, *x_args, w_out.astype(BF16), ln_g.reshape(1, D_MODEL), ln_b.reshape(1, D_MODEL), rw, rb)
    ids = ids.transpose(0, 2, 1).reshape(N_TOK, TOP_K)
    gates = gates.transpose(0, 2, 1).reshape(N_TOK, TOP_K)
    return x1, ids, gates


def _stream_expert_weights(layer, tables, w_refs, wbuf_ref, wb_refs, sem_ref):
    te_ref, nx_ref, nt_ref = tables
    i = pl.program_id(0)

    def copies(expert):
        return [pltpu.make_async_copy(w.at[layer, expert], wbuf_ref.at[j], sem_ref.at[j])
                for j, w in enumerate(w_refs)]

    @pl.when(i == 0)
    def _():
        for c in copies(te_ref[0]):
            c.start()

    first = jnp.logical_or(i == 0, te_ref[i] != te_ref[jnp.maximum(i - 1, 0)])

    @pl.when(jnp.logical_and(first, i < nt_ref[0]))
    def _():
        for c in copies(te_ref[i]):
            c.wait()
        for j, wb in enumerate(wb_refs):
            _cast_weight(wbuf_ref.at[j], wb)

        @pl.when(nx_ref[i] >= 0)
        def _():
            for c in copies(nx_ref[i]):
                c.start()


def _issue_row_copies(idx_ref, base, src_ref, dst_ref, sem, n_rows, stride=1):
    def body(g, carry):
        r0 = pl.multiple_of(g * SUBLANES, SUBLANES)
        for k in range(SUBLANES):
            row = idx_ref[base + stride * (r0 + k)]
            pltpu.make_async_copy(src_ref.at[pl.ds(row, 1)], dst_ref.at[pl.ds(r0 + k, 1)],
                                  sem).start(priority=k % 2)
        return carry
    lax.fori_loop(0, n_rows // SUBLANES, body, 0)


def _gather_kernel(idx_ref, nt_ref, src_ref, out_ref, buf_ref, sem_ref):
    i = pl.program_id(0)
    n = nt_ref[0]
    tg = out_ref.shape[0]

    def start_tile(tile, slot):
        _issue_row_copies(idx_ref, tile * tg, src_ref, buf_ref.at[slot], sem_ref.at[slot], tg)

    @pl.when(i == 0)
    def _():
        start_tile(0, 0)

    @pl.when(i + 1 < n)
    def _():
        start_tile(i + 1, (i + 1) % 2)

    @pl.when(i < n)
    def _():
        slot = i % 2
        pltpu.make_async_copy(src_ref.at[pl.ds(0, tg)], buf_ref.at[slot], sem_ref.at[slot]).wait()
        out_ref[...] = buf_ref[slot].astype(out_ref.dtype)

    @pl.when(i >= n)
    def _():
        out_ref[...] = jnp.zeros_like(out_ref)


def _gather_rows(src, idx, n_tiles, out_dtype, name):
    m = idx.shape[0]
    width = src.shape[1]
    return pl.pallas_call(
        _gather_kernel,
        grid_spec=pltpu.PrefetchScalarGridSpec(
            num_scalar_prefetch=2,
            grid=(m // MOE_TILE,),
            in_specs=[pl.BlockSpec(memory_space=pl.ANY)],
            out_specs=pl.BlockSpec((MOE_TILE, width), lambda i, idx, nt: (i, 0)),
            scratch_shapes=[pltpu.VMEM((2, MOE_TILE, width), src.dtype),
                            pltpu.SemaphoreType.DMA((2,))]),
        out_shape=jax.ShapeDtypeStruct((m, width), out_dtype),
        compiler_params=_params("arbitrary"),
        name=name,
    )(idx, n_tiles, src)


def _moe_up_kernel(layer, te_ref, nx_ref, nt_ref, xs_ref, w1_ref, w3_ref, hid_ref,
                   wbuf_ref, w1b_ref, w3b_ref, sem_ref):
    _stream_expert_weights(layer, (te_ref, nx_ref, nt_ref), (w1_ref, w3_ref), wbuf_ref,
                           (w1b_ref, w3b_ref), sem_ref)
    i = pl.program_id(0)

    @pl.when(i < nt_ref[0])
    def _():
        xs = xs_ref[...]
        h1 = _dot_nn(xs, w1b_ref[...])
        h3 = _dot_nn(xs, w3b_ref[...])
        hid_ref[...] = (_silu(h1) * h3).astype(BF16)

    @pl.when(i >= nt_ref[0])
    def _():
        hid_ref[...] = jnp.zeros_like(hid_ref)


def _moe_up(xs, w1, w3, layer, tables):
    last = lambda i, te, nx, nt: (jnp.minimum(i, nt[0] - 1), 0)
    return pl.pallas_call(
        functools.partial(_moe_up_kernel, layer),
        grid_spec=pltpu.PrefetchScalarGridSpec(
            num_scalar_prefetch=3,
            grid=(MOE_MAX_TILES,),
            in_specs=[pl.BlockSpec((MOE_TILE, D_MODEL), last),
                      pl.BlockSpec(memory_space=pl.ANY),
                      pl.BlockSpec(memory_space=pl.ANY)],
            out_specs=pl.BlockSpec((MOE_TILE, D_FF), lambda i, te, nx, nt: (i, 0)),
            scratch_shapes=[pltpu.VMEM((2, D_MODEL, D_FF), F32),
                            pltpu.VMEM((D_MODEL, D_FF), BF16), pltpu.VMEM((D_MODEL, D_FF), BF16),
                            pltpu.SemaphoreType.DMA((2,))]),
        out_shape=jax.ShapeDtypeStruct((N_ASSIGN_PAD, D_FF), BF16),
        compiler_params=_params("arbitrary"),
        name="moe_up",
    )(*tables, xs, w1, w3)


def _moe_down_kernel(layer, te_ref, nx_ref, nt_ref, hid_ref, w2_ref, ys_ref,
                     wbuf_ref, w2b_ref, sem_ref):
    _stream_expert_weights(layer, (te_ref, nx_ref, nt_ref), (w2_ref,), wbuf_ref,
                           (w2b_ref,), sem_ref)
    i = pl.program_id(0)

    @pl.when(i < nt_ref[0])
    def _():
        ys_ref[...] = _dot_nn(hid_ref[...], w2b_ref[...])

    @pl.when(i >= nt_ref[0])
    def _():
        ys_ref[...] = jnp.zeros_like(ys_ref)


def _moe_down(hid, w2, layer, tables):
    last = lambda i, te, nx, nt: (jnp.minimum(i, nt[0] - 1), 0)
    return pl.pallas_call(
        functools.partial(_moe_down_kernel, layer),
        grid_spec=pltpu.PrefetchScalarGridSpec(
            num_scalar_prefetch=3,
            grid=(MOE_MAX_TILES,),
            in_specs=[pl.BlockSpec((MOE_TILE, D_FF), last),
                      pl.BlockSpec(memory_space=pl.ANY)],
            out_specs=pl.BlockSpec((MOE_TILE, D_MODEL), lambda i, te, nx, nt: (i, 0)),
            scratch_shapes=[pltpu.VMEM((1, D_FF, D_MODEL), F32),
                            pltpu.VMEM((D_FF, D_MODEL), BF16),
                            pltpu.SemaphoreType.DMA((1,))]),
        out_shape=jax.ShapeDtypeStruct((N_ASSIGN_PAD, D_MODEL), F32),
        compiler_params=_params("arbitrary"),
        name="moe_down",
    )(*tables, hid, w2)


def _combine_norm_kernel(final, pos_ref, x_ref, ys_ref, gt_ref, lg_ref, lb_ref, o0_ref, o1_ref,
                         buf_ref, sem_ref):
    i = pl.program_id(0)
    n = pl.num_programs(0)
    tm = x_ref.shape[0]

    def start_tile(tile, slot):
        for k in range(TOP_K):
            _issue_row_copies(pos_ref, tile * tm * TOP_K + k, ys_ref, buf_ref.at[slot, k],
                              sem_ref.at[slot], tm, stride=TOP_K)

    @pl.when(i == 0)
    def _():
        start_tile(0, 0)

    @pl.when(i + 1 < n)
    def _():
        start_tile(i + 1, (i + 1) % 2)

    slot = i % 2
    for k in range(TOP_K):
        pltpu.make_async_copy(ys_ref.at[pl.ds(0, tm)], buf_ref.at[slot, k], sem_ref.at[slot]).wait()
    gt = gt_ref[...]
    moe = buf_ref[slot, 0] * gt[:, 0:1] + buf_ref[slot, 1] * gt[:, 1:2]
    xn = _layer_norm(ALPHA * x_ref[...] + moe, lg_ref[...], lb_ref[...])
    if final:
        @pl.when(i < n - 1)
        def _():
            o0_ref[...] = xn

        @pl.when(i == n - 1)
        def _():
            o1_ref[...] = xn
    else:
        o0_ref[...] = xn
        o1_ref[...] = xn.astype(BF16)


def _combine_norm(x, ys, pos, gates, ln_g, ln_b, final):
    tm = N_SAMPLE if final else ROW_TILE
    const = lambda i, pos: (0, 0)
    tile = lambda i, pos: (i, 0)
    if final:
        n_prompt_tiles = N_PROMPT // tm
        out_specs = [pl.BlockSpec((tm, D_MODEL), lambda i, pos: (jnp.minimum(i, n_prompt_tiles - 1), 0)),
                     pl.BlockSpec((tm, D_MODEL), const)]
        out_shape = [jax.ShapeDtypeStruct((N_PROMPT, D_MODEL), F32),
                     jax.ShapeDtypeStruct((N_SAMPLE, D_MODEL), F32)]
    else:
        out_specs = [pl.BlockSpec((tm, D_MODEL), tile), pl.BlockSpec((tm, D_MODEL), tile)]
        out_shape = [jax.ShapeDtypeStruct((N_TOK, D_MODEL), F32),
                     jax.ShapeDtypeStruct((N_TOK, D_MODEL), BF16)]
    return pl.pallas_call(
        functools.partial(_combine_norm_kernel, final),
        grid_spec=pltpu.PrefetchScalarGridSpec(
            num_scalar_prefetch=1,
            grid=(N_TOK // tm,),
            in_specs=[pl.BlockSpec((tm, D_MODEL), tile),
                      pl.BlockSpec(memory_space=pl.ANY),
                      pl.BlockSpec((tm, TOP_K), tile),
                      pl.BlockSpec((1, D_MODEL), const),
                      pl.BlockSpec((1, D_MODEL), const)],
            out_specs=out_specs,
            scratch_shapes=[pltpu.VMEM((2, TOP_K, tm, D_MODEL), F32),
                            pltpu.SemaphoreType.DMA((2,))]),
        out_shape=out_shape,
        compiler_params=_params("arbitrary"),
        name="combine_norm",
    )(pos, x, ys, gates, ln_g.reshape(1, D_MODEL), ln_b.reshape(1, D_MODEL))


def _routing_tables(ids):
    e = ids.reshape(-1)
    onehot = (e[:, None] == jnp.arange(N_EXPERTS, dtype=jnp.int32)[None, :]).astype(jnp.int32)
    csum = jnp.cumsum(onehot, axis=0)
    rank = jnp.sum(onehot * csum, axis=1) - 1
    sizes = csum[-1]
    tiles_per = (sizes + MOE_TILE - 1) // MOE_TILE
    tile_end = jnp.cumsum(tiles_per)
    tile_start = tile_end - tiles_per
    n_tiles = tile_end[-1]
    pos = (jnp.sum(onehot * (tile_start * MOE_TILE)[None, :], axis=1) + rank).astype(jnp.int32)
    tile = jnp.arange(MOE_MAX_TILES, dtype=jnp.int32)
    owner = jnp.sum((tile_end[None, :] <= jnp.minimum(tile, n_tiles - 1)[:, None]).astype(jnp.int32), axis=1)
    order = jnp.argsort(e, stable=True).astype(jnp.int32)
    owner_hot = (owner[:, None] == jnp.arange(N_EXPERTS, dtype=jnp.int32)[None, :]).astype(jnp.int32)
    size_start = jnp.cumsum(sizes) - sizes
    tile_first = jnp.sum(owner_hot * (size_start - tile_start * MOE_TILE)[None, :], axis=1)
    tile_limit = jnp.sum(owner_hot * (size_start + sizes)[None, :], axis=1)
    row = jnp.arange(N_ASSIGN_PAD, dtype=jnp.int32).reshape(MOE_MAX_TILES, MOE_TILE)
    src = tile_first[:, None] + row
    valid = src < tile_limit[:, None]
    row_token = jnp.where(valid, order[jnp.clip(src, 0, N_ASSIGN - 1)] // TOP_K, row % N_TOK).reshape(-1)
    experts = jnp.arange(N_EXPERTS, dtype=jnp.int32)
    later = lax.cummin(jnp.where(tiles_per > 0, experts, N_EXPERTS), reverse=True)
    next_expert = jnp.concatenate([later[1:], jnp.full((1,), N_EXPERTS, jnp.int32)])
    next_expert = jnp.where(next_expert >= N_EXPERTS, -1, next_expert)
    tile_next = jnp.sum(owner_hot * next_expert[None, :], axis=1)
    n_tiles = n_tiles.reshape(1).astype(jnp.int32)
    tables = (owner.astype(jnp.int32), tile_next.astype(jnp.int32), n_tiles)
    return pos, row_token.astype(jnp.int32), tables


def _moe_block(x, ids, gates, w1, w3, w2, layer, ln_g, ln_b, final):
    pos, row_token, tables = _routing_tables(ids)
    xs = _gather_rows(x, row_token, tables[2], BF16, "moe_gather")
    hid = _moe_up(xs, w1, w3, layer, tables)
    ys = _moe_down(hid, w2, layer, tables)
    return _combine_norm(x, ys, pos, gates, ln_g, ln_b, final)


def _conv_in_kernel(x_ref, wb_ref, wc_ref, wh_ref, cw_ref, s0_ref, s1_ref, gbz_ref, u_ref,
                    wbb_ref, wcb_ref, whb_ref, ubuf_ref):
    i = pl.program_id(1)
    tm = x_ref.shape[0]
    pad = SUBLANES

    @pl.when(i == 0)
    def _():
        _cast_weight(wb_ref, wbb_ref)
        _cast_weight(wc_ref, wcb_ref)
        _cast_weight(wh_ref, whb_ref)
        ubuf_ref[0:pad, :] = jnp.zeros((pad, ubuf_ref.shape[1]), F32)

    w0 = cw_ref[0:1, :]
    w1 = cw_ref[1:2, :]
    w2 = cw_ref[2:3, :]
    sub = ROW_TILE
    for s in range(tm // sub):
        lo, hi = s * sub, (s + 1) * sub
        x = x_ref[lo:hi, :]
        gb = _dot_nn(x, wbb_ref[...])
        u = _dot_nn(x, wcb_ref[...]) * _dot_nn(x, whb_ref[...])
        u_ref[lo:hi, :] = u
        ubuf_ref[pad + lo:pad + hi, :] = u
        t = (i * tm + lo + lax.broadcasted_iota(jnp.int32, (sub, 1), 0)) & (SEQ - 1)
        u_m1 = jnp.where(t >= 1, ubuf_ref[pad - 1 + lo:pad - 1 + hi, :], 0.0)
        u_m2 = jnp.where(t >= 2, ubuf_ref[pad - 2 + lo:pad - 2 + hi, :], 0.0)
        z = u_m2 * w0 + u_m1 * w1 + u * w2
        gbz_ref[lo:hi, :] = (gb * z).astype(BF16)
    ubuf_ref[0:pad, :] = ubuf_ref[tm:tm + pad, :]

    @pl.when(i == pl.num_programs(1) - 1)
    def _():
        first = sub - N_SAMPLE
        z_s = s0_ref[...] * w0 + s1_ref[...] * w1 + u[first:sub, :] * w2
        gbz_ref[tm - N_SAMPLE:tm, :] = (gb[first:sub, :] * z_s).astype(BF16)


def _conv_in(xb, w_in, conv_w, state):
    tn = 512
    nb = D_MODEL // tn
    s0 = state[:, 0, :]
    s1 = state[:, 1, :]
    return pl.pallas_call(
        _conv_in_kernel,
        grid=(nb, N_TOK // TOKEN_TILE),
        in_specs=[pl.BlockSpec((TOKEN_TILE, D_MODEL), lambda j, i: (i, 0)),
                  pl.BlockSpec((D_MODEL, tn), lambda j, i: (0, j)),
                  pl.BlockSpec((D_MODEL, tn), lambda j, i: (0, nb + j)),
                  pl.BlockSpec((D_MODEL, tn), lambda j, i: (0, 2 * nb + j)),
                  pl.BlockSpec((CONV_W, tn), lambda j, i: (0, j)),
                  pl.BlockSpec((N_SAMPLE, tn), lambda j, i: (0, j)),
                  pl.BlockSpec((N_SAMPLE, tn), lambda j, i: (0, j))],
        out_specs=[pl.BlockSpec((TOKEN_TILE, tn), lambda j, i: (i, j)),
                   pl.BlockSpec((TOKEN_TILE, tn), lambda j, i: (i, j))],
        out_shape=[jax.ShapeDtypeStruct((N_TOK, D_MODEL), BF16),
                   jax.ShapeDtypeStruct((N_TOK, D_MODEL), F32)],
        scratch_shapes=[pltpu.VMEM((D_MODEL, tn), BF16), pltpu.VMEM((D_MODEL, tn), BF16),
                        pltpu.VMEM((D_MODEL, tn), BF16),
                        pltpu.VMEM((TOKEN_TILE + 2 * SUBLANES, tn), F32)],
        compiler_params=_params("arbitrary", "arbitrary"),
        name="conv_in",
    )(xb, w_in, w_in, w_in, conv_w, s0, s1)


def kernel(x_prompt, x_sample, state_gla, state_conv, router_w, router_b, gla_w_in, gla_w_gate, gla_b_gate,
           gla_norm_g, gla_w_out, conv_w_in, conv_w, conv_w_out, ln_mix_g, ln_mix_b, ln_ffn_g, ln_ffn_b,
           moe_w1, moe_w3, moe_w2):
    xp = x_prompt.reshape(N_PROMPT, D_MODEL)
    xs = x_sample.reshape(N_SAMPLE, D_MODEL)
    router = _router_operands(router_w, router_b)

    w_in_t = gla_w_in[0].T
    qkvr, gl = _gla_in_proj(xp, xs, w_in_t)
    g = _gla_gate(gl, gla_w_gate[0], gla_b_gate[0])
    norm_g = gla_norm_g[0].reshape(1, DV_TOTAL)
    og, s_prompt = _gla_prompt(qkvr, g, norm_g)
    og, s_sample = _gla_sample(qkvr, g, norm_g, state_gla[0], og)
    x1, ids, gates = _proj_norm_route(og, (xp, xs), gla_w_out[0], ln_mix_g[0], ln_mix_b[0], router)
    x2, x2b = _moe_block(x1, ids, gates, moe_w1, moe_w3, moe_w2, 0, ln_ffn_g[0], ln_ffn_b[0], False)

    gbz, u = _conv_in(x2b, conv_w_in[0], conv_w[0], state_conv[0])
    x3, ids, gates = _proj_norm_route(gbz, x2, conv_w_out[0], ln_mix_g[1], ln_mix_b[1], router)
    y_p, y_s = _moe_block(x3, ids, gates, moe_w1, moe_w3, moe_w2, 1, ln_ffn_g[1], ln_ffn_b[1], True)

    y_prompt = y_p.reshape(N_PROMPT_SEQ, SEQ, D_MODEL)
    y_sample = y_s.reshape(N_SAMPLE, 1, D_MODEL)
    conv_prompt = jnp.stack([u[(b + 1) * SEQ - (CONV_W - 1):(b + 1) * SEQ] for b in range(N_PROMPT_SEQ)])
    conv_sample = jnp.concatenate([state_conv[0][:, 1:, :], u[N_PROMPT:][:, None, :]], axis=1)
    return (y_prompt, y_sample, s_prompt[None], conv_prompt[None], s_sample[None], conv_sample[None])
```

```python
import functools

import jax
import jax.numpy as jnp
from jax import lax
from jax.experimental import pallas as pl
from jax.experimental.pallas import tpu as pltpu

F32 = jnp.float32
BF16 = jnp.bfloat16

D_MODEL = 2048
N_PROMPT_SEQ = 4
SEQ = 2048
N_PROMPT = N_PROMPT_SEQ * SEQ
N_SAMPLE = 128
N_TOK = N_PROMPT + N_SAMPLE
DEPTH = 2

GLA_HEADS = 4
DK_TOTAL = D_MODEL // 2
DV_TOTAL = D_MODEL
DK_HEAD = DK_TOTAL // GLA_HEADS
DV_HEAD = DV_TOTAL // GLA_HEADS
GATE_RANK = 16
GATE_NORMALIZER = 16.0
GLA_QKVR_WIDTH = 2 * DK_TOTAL + 2 * DV_TOTAL
CONV_W = 3
N_EXPERTS = 16
N_GROUPS = 4
EXPERTS_PER_GROUP = N_EXPERTS // N_GROUPS
TOP_K = 2
D_FF = D_MODEL // 2
ALPHA = (2.0 * DEPTH) ** 0.25
LN_EPS = 1e-5
RMS_EPS = 1e-6

LANES = 128
SUBLANES = 8
VMEM_LIMIT_BYTES = 56 * 1024 * 1024

TOKEN_TILE = 640
ROW_TILE = 320
GLA_CHUNK = 64
GLA_SAMPLE_BATCH = 16
MOE_TILE = 256
N_ASSIGN = N_TOK * TOP_K
MOE_MAX_TILES = N_ASSIGN // MOE_TILE + N_EXPERTS
N_ASSIGN_PAD = MOE_MAX_TILES * MOE_TILE


def _params(*semantics):
    return pltpu.CompilerParams(dimension_semantics=semantics, vmem_limit_bytes=VMEM_LIMIT_BYTES)


def _split3(x):
    hi = x.astype(BF16)
    r1 = x - hi.astype(F32)
    mid = r1.astype(BF16)
    lo = (r1 - mid.astype(F32)).astype(BF16)
    return hi, mid, lo


def _cast_weight(src_ref, dst_ref):
    rows = 64
    n = src_ref.shape[0] // rows

    def body(c, carry):
        r = pl.multiple_of(c * rows, rows)
        dst_ref[pl.ds(r, rows), :] = src_ref[pl.ds(r, rows), :].astype(BF16)
        return carry
    lax.fori_loop(0, n, body, 0)


def _dot_nn(a, b):
    return jnp.dot(a, b, preferred_element_type=F32)


def _dot_tn(a, b):
    return lax.dot_general(a, b, (((0,), (0,)), ((), ())), preferred_element_type=F32)


def _dot_nt(a, b):
    return lax.dot_general(a, b, (((1,), (1,)), ((), ())), preferred_element_type=F32)


def _col_bcast(rows, n):
    ones = jnp.ones((rows.shape[0], LANES), BF16)
    hi, mid, lo = _split3(rows)
    col = _dot_tn(hi, ones) + _dot_tn(mid, ones) + _dot_tn(lo, ones)
    return jnp.concatenate([col] * (n // LANES), axis=1)


def _silu(r):
    return r / (1.0 + jnp.exp(-r))


def _layer_norm(h, g, b):
    mu = jnp.mean(h, axis=-1, keepdims=True)
    d = h - mu
    var = jnp.mean(d * d, axis=-1, keepdims=True)
    return d * lax.rsqrt(var + LN_EPS) * g + b


X_TILE_BLOCKS = TOKEN_TILE // N_SAMPLE


def _x_tile_specs(tile_index):
    last = N_PROMPT // N_SAMPLE - 1

    def prompt_spec(k):
        return pl.BlockSpec((N_SAMPLE, D_MODEL),
                            lambda *g: (jnp.minimum(tile_index(*g) * X_TILE_BLOCKS + k, last), 0))
    return ([prompt_spec(k) for k in range(X_TILE_BLOCKS)]
            + [pl.BlockSpec((N_SAMPLE, D_MODEL), lambda *g: (0, 0))])


def _x_tile(i, x_refs):
    *prompt_refs, sample_ref = x_refs
    tail_is_sample = (i + 1) * TOKEN_TILE > N_PROMPT
    blocks = [r[...] for r in prompt_refs[:-1]]
    blocks.append(jnp.where(tail_is_sample, sample_ref[...], prompt_refs[-1][...]))
    return jnp.concatenate(blocks, axis=0)


def _gla_in_proj_kernel(*refs):
    *x_refs, wt_ref, wgl_ref, o_ref, gl_ref, wb_ref = refs
    j = pl.program_id(0)
    i = pl.program_id(1)

    @pl.when(i == 0)
    def _():
        _cast_weight(wt_ref, wb_ref)

    x = _x_tile(i, x_refs).astype(BF16)
    for s in range(TOKEN_TILE // ROW_TILE):
        rows = slice(s * ROW_TILE, (s + 1) * ROW_TILE)
        o_ref[rows, :] = _dot_nt(x[rows, :], wb_ref[...]).astype(o_ref.dtype)

    @pl.when(j == pl.num_programs(0) - 1)
    def _():
        wgl = jnp.concatenate([wgl_ref[...].astype(BF16), jnp.zeros((LANES - GATE_RANK, D_MODEL), BF16)],
                              axis=0)
        gl_ref[...] = _dot_nt(x, wgl)

    @pl.when(j < pl.num_programs(0) - 1)
    def _():
        gl_ref[...] = jnp.zeros_like(gl_ref)


def _gla_in_proj(x_prompt, x_sample, w_t):
    tn = 1024
    n_pass = GLA_QKVR_WIDTH // tn
    n_tiles = N_TOK // TOKEN_TILE
    return pl.pallas_call(
        _gla_in_proj_kernel,
        grid=(n_pass, n_tiles),
        in_specs=_x_tile_specs(lambda j, i: i) + [
            pl.BlockSpec((tn, D_MODEL), lambda j, i: (j, 0)),
            pl.BlockSpec((GATE_RANK, D_MODEL), lambda j, i: (GLA_QKVR_WIDTH // GATE_RANK, 0))],
        out_specs=[pl.BlockSpec((TOKEN_TILE, tn), lambda j, i: (i, j)),
                   pl.BlockSpec((TOKEN_TILE, LANES), lambda j, i: (jnp.where(j == n_pass - 1, i, n_tiles), 0))],
        out_shape=[jax.ShapeDtypeStruct((N_TOK, GLA_QKVR_WIDTH), F32),
                   jax.ShapeDtypeStruct((N_TOK + TOKEN_TILE, LANES), F32)],
        scratch_shapes=[pltpu.VMEM((tn, D_MODEL), BF16)],
        compiler_params=_params("arbitrary", "arbitrary"),
        name="gla_in_proj",
    )(*([x_prompt] * X_TILE_BLOCKS), x_sample, w_t, w_t)


def _gate_kernel(gl_ref, wg_ref, bg_ref, g_ref):
    z = _dot_nn(gl_ref[...].astype(BF16), wg_ref[...]) + bg_ref[...]
    log_sig = jnp.minimum(z, 0.0) - jnp.log1p(jnp.exp(-jnp.abs(z)))
    g_ref[...] = log_sig * (1.0 / GATE_NORMALIZER)


def _gla_gate(gl, w_gate, b_gate):
    wg = jnp.pad(w_gate, ((0, LANES - GATE_RANK), (0, 0))).astype(BF16)
    return pl.pallas_call(
        _gate_kernel,
        grid=(N_TOK // TOKEN_TILE,),
        in_specs=[pl.BlockSpec((TOKEN_TILE, LANES), lambda i: (i, 0)),
                  pl.BlockSpec((LANES, DK_TOTAL), lambda i: (0, 0)),
                  pl.BlockSpec((1, DK_TOTAL), lambda i: (0, 0))],
        out_specs=pl.BlockSpec((TOKEN_TILE, DK_TOTAL), lambda i: (i, 0)),
        out_shape=jax.ShapeDtypeStruct((N_TOK, DK_TOTAL), F32),
        compiler_params=_params("arbitrary"),
        name="gla_gate",
    )(gl, wg, b_gate.reshape(1, DK_TOTAL))


def _rms_gate(o, norm_g, r):
    o = o * lax.rsqrt(jnp.mean(o * o, axis=-1, keepdims=True) + RMS_EPS)
    return (o * norm_g) * _silu(r)


def _gla_prompt_kernel(q_ref, k_ref, v_ref, r_ref, g_ref, ng_ref, og_in_ref, og_ref, s_out_ref, s_ref):
    del og_in_ref
    c = pl.program_id(1)
    cc = q_ref.shape[0]

    @pl.when(c == 0)
    def _():
        s_ref[...] = jnp.zeros_like(s_ref)

    row = lax.broadcasted_iota(jnp.int32, (cc, cc), 0)
    col = lax.broadcasted_iota(jnp.int32, (cc, cc), 1)
    causal = col <= row
    tri = jnp.where(causal, 1.0, 0.0).astype(BF16)
    b = _dot_nn(jnp.concatenate([tri] * 3, axis=1),
                jnp.concatenate(_split3(g_ref[...]), axis=0))
    b_last = b[cc - 1:cc, :]
    b_mid = b[cc // 2 - 1:cc // 2, :]

    q = q_ref[...]
    k = k_ref[...] * (DK_HEAD ** -0.5)
    q_in = (q * jnp.exp(b)).astype(BF16)
    qh, qm, _ = _split3(q * jnp.exp(b - b_mid))
    kh, km, _ = _split3(k * jnp.exp(b_mid - b))
    k_d = (k * jnp.exp(b_last - b)).astype(BF16)

    rows = 2 * SUBLANES
    sub = lax.broadcasted_iota(jnp.int32, (rows, DK_TOTAL), 0)
    decay_rows = jnp.where(sub == 0, jnp.broadcast_to(jnp.exp(b_last), (rows, DK_TOTAL)), 0.0)
    decay_col = _dot_tn(jnp.concatenate(_split3(decay_rows), axis=0),
                        jnp.ones((3 * rows, LANES), BF16))

    for h in range(GLA_HEADS):
        dk = slice(h * DK_HEAD, (h + 1) * DK_HEAD)
        dv = slice(h * DV_HEAD, (h + 1) * DV_HEAD)
        q3 = jnp.concatenate([qh[:, dk], qh[:, dk], qm[:, dk]], axis=1)
        k3 = jnp.concatenate([kh[:, dk], km[:, dk], kh[:, dk]], axis=1)
        scores = jnp.where(causal, _dot_nt(q3, k3), 0.0).astype(BF16)
        v = v_ref[:, dv].astype(BF16)
        s_old = s_ref[h]
        o = _dot_nn(jnp.concatenate([q_in[:, dk], scores], axis=1),
                    jnp.concatenate([s_old.astype(BF16), v], axis=0))
        decay = jnp.concatenate([decay_col[dk, :]] * (DV_HEAD // LANES), axis=1)
        s_new = decay * s_old + _dot_tn(k_d[:, dk], v)
        s_ref[h] = s_new

        @pl.when(c == pl.num_programs(1) - 1)
        def _():
            s_out_ref[0, h] = s_new

        og_ref[:, dv] = _rms_gate(o, ng_ref[:, dv], r_ref[:, dv]).astype(BF16)


def _gla_prompt(qkvr, g, norm_g):
    nc = SEQ // GLA_CHUNK
    row = lambda b, c: b * nc + c
    return pl.pallas_call(
        _gla_prompt_kernel,
        grid=(N_PROMPT_SEQ, nc),
        in_specs=[pl.BlockSpec((GLA_CHUNK, DK_TOTAL), lambda b, c: (row(b, c), 0)),
                  pl.BlockSpec((GLA_CHUNK, DK_TOTAL), lambda b, c: (row(b, c), 1)),
                  pl.BlockSpec((GLA_CHUNK, DV_TOTAL), lambda b, c: (row(b, c), 1)),
                  pl.BlockSpec((GLA_CHUNK, DV_TOTAL), lambda b, c: (row(b, c), 2)),
                  pl.BlockSpec((GLA_CHUNK, DK_TOTAL), lambda b, c: (row(b, c), 0)),
                  pl.BlockSpec((1, DV_TOTAL), lambda b, c: (0, 0)),
                  pl.BlockSpec(memory_space=pl.ANY)],
        out_specs=[pl.BlockSpec((GLA_CHUNK, DV_TOTAL), lambda b, c: (row(b, c), 0)),
                   pl.BlockSpec((1, GLA_HEADS, DK_HEAD, DV_HEAD), lambda b, c: (b, 0, 0, 0))],
        out_shape=[jax.ShapeDtypeStruct((N_TOK, DV_TOTAL), BF16),
                   jax.ShapeDtypeStruct((N_PROMPT_SEQ, GLA_HEADS, DK_HEAD, DV_HEAD), F32)],
        scratch_shapes=[pltpu.VMEM((GLA_HEADS, DK_HEAD, DV_HEAD), F32)],
        input_output_aliases={6: 0},
        compiler_params=_params("arbitrary", "arbitrary"),
        name="gla_prompt",
    )(qkvr, qkvr, qkvr, qkvr, g, norm_g, jnp.zeros((N_TOK, DV_TOTAL), BF16))


def _gla_sample_kernel(q_ref, k_ref, v_ref, r_ref, g_ref, ng_ref, s_ref, og_in_ref, og_ref, s_out_ref,
                       o_scr):
    del og_in_ref
    bb = q_ref.shape[0]
    q = q_ref[...]
    k = k_ref[...] * (DK_HEAD ** -0.5)
    v = v_ref[...]
    eg = jnp.exp(g_ref[...])
    qe = q * eg
    qk = jnp.sum(q * k, axis=-1, keepdims=True)
    sub = lax.broadcasted_iota(jnp.int32, (bb, DK_HEAD), 0)
    for bi in range(bb):
        sel = sub == bi
        s_old = s_ref[bi, 0]
        decay = _col_bcast(jnp.where(sel, eg, 0.0), DV_HEAD)
        k_col = _col_bcast(jnp.where(sel, k, 0.0), DV_HEAD)
        qe_col = _col_bcast(jnp.where(sel, qe, 0.0), DV_HEAD)
        s_out_ref[bi, 0] = decay * s_old + k_col * v[bi:bi + 1, :]
        o_scr[bi:bi + 1, :] = jnp.sum(qe_col * s_old, axis=0, keepdims=True)
    o = qk * v + o_scr[...]
    og_ref[...] = _rms_gate(o, ng_ref[...], r_ref[...]).astype(BF16)


def _gla_sample(qkvr, g, norm_g, state, og):
    bb = GLA_SAMPLE_BATCH
    r0 = N_PROMPT // bb
    hk = DK_TOTAL // DK_HEAD
    hv = 2 * DK_TOTAL // DV_HEAD
    return pl.pallas_call(
        _gla_sample_kernel,
        grid=(N_SAMPLE // bb, GLA_HEADS),
        in_specs=[pl.BlockSpec((bb, DK_HEAD), lambda i, h: (r0 + i, h)),
                  pl.BlockSpec((bb, DK_HEAD), lambda i, h: (r0 + i, hk + h)),
                  pl.BlockSpec((bb, DV_HEAD), lambda i, h: (r0 + i, hv + h)),
                  pl.BlockSpec((bb, DV_HEAD), lambda i, h: (r0 + i, hv + GLA_HEADS + h)),
                  pl.BlockSpec((bb, DK_HEAD), lambda i, h: (r0 + i, h)),
                  pl.BlockSpec((1, DV_HEAD), lambda i, h: (0, h)),
                  pl.BlockSpec((bb, 1, DK_HEAD, DV_HEAD), lambda i, h: (i, h, 0, 0)),
                  pl.BlockSpec(memory_space=pl.ANY)],
        out_specs=[pl.BlockSpec((bb, DV_HEAD), lambda i, h: (r0 + i, h)),
                   pl.BlockSpec((bb, 1, DK_HEAD, DV_HEAD), lambda i, h: (i, h, 0, 0))],
        out_shape=[jax.ShapeDtypeStruct((N_TOK, DV_TOTAL), BF16),
                   jax.ShapeDtypeStruct((N_SAMPLE, GLA_HEADS, DK_HEAD, DV_HEAD), F32)],
        scratch_shapes=[pltpu.VMEM((bb, DV_HEAD), F32)],
        input_output_aliases={7: 0},
        compiler_params=_params("arbitrary", "arbitrary"),
        name="gla_sample",
    )(qkvr, qkvr, qkvr, qkvr, g, norm_g, state, og)


def _top2_of4(p):
    ranks = []
    for j in range(4):
        rk = jnp.zeros(p[j].shape, jnp.int32)
        for i in range(4):
            if i == j:
                continue
            beats = (p[i] >= p[j]) if i < j else (p[i] > p[j])
            rk = rk + jnp.where(beats, 1, 0)
        ranks.append(rk)

    def pick(rank):
        val = jnp.zeros(p[0].shape, F32)
        idx = jnp.zeros(p[0].shape, jnp.int32)
        for j in range(4):
            hit = ranks[j] == rank
            val = jnp.where(hit, p[j], val)
            idx = jnp.where(hit, j, idx)
        return val, idx

    v1, i1 = pick(0)
    v2, i2 = pick(1)
    return v1, i1, v2, i2


def _route(x, rw_ref, rb_ref):
    logits = _dot_nn(x.astype(BF16), rw_ref[...])
    n = x.shape[0]
    n_pad = -n % LANES
    if n_pad:
        logits = jnp.concatenate([logits, jnp.zeros((n_pad, LANES), F32)], axis=0)
    lt = logits.T[0:N_EXPERTS, 0:n] + rb_ref[...]
    e = jnp.exp(lt - jnp.max(lt, axis=0, keepdims=True))
    probs = e / jnp.sum(e, axis=0, keepdims=True)
    best = None
    for grp in range(N_GROUPS):
        rows = [probs[grp * EXPERTS_PER_GROUP + j:grp * EXPERTS_PER_GROUP + j + 1, :]
                for j in range(EXPERTS_PER_GROUP)]
        v1, i1, v2, i2 = _top2_of4(rows)
        score = v1 + v2
        cand = (score, v1, i1 + grp * EXPERTS_PER_GROUP, v2, i2 + grp * EXPERTS_PER_GROUP)
        if best is None:
            best = cand
        else:
            better = score > best[0]
            best = tuple(jnp.where(better, n, o) for n, o in zip(cand, best))
    _, v1, e1, v2, e2 = best
    denom = v1 + v2
    return (e1, e2), (v1 / denom, v2 / denom)


def _proj_norm_route_kernel(a_ref, *refs):
    *x_refs, w_ref, lg_ref, lb_ref, rw_ref, rb_ref, xo_ref, id_ref, gt_ref = refs
    x = x_refs[0][...] if len(x_refs) == 1 else _x_tile(pl.program_id(0), x_refs)
    sub = id_ref.shape[2]
    for s in range(id_ref.shape[0]):
        rows = slice(s * sub, (s + 1) * sub)
        y = _dot_nn(a_ref[rows, :], w_ref[...])
        xn = _layer_norm(ALPHA * x[rows, :] + y, lg_ref[...], lb_ref[...])
        xo_ref[rows, :] = xn
        (e1, e2), (g1, g2) = _route(xn, rw_ref, rb_ref)
        id_ref[s, 0:1, :] = e1
        id_ref[s, 1:2, :] = e2
        gt_ref[s, 0:1, :] = g1
        gt_ref[s, 1:2, :] = g2


def _router_operands(router_w, router_b):
    rw = jnp.pad(router_w, ((0, 0), (0, LANES - N_EXPERTS))).astype(BF16)
    return rw, router_b.reshape(N_EXPERTS, 1)


def _proj_norm_route(a, x, w_out, ln_g, ln_b, router):
    nt = N_TOK // ROW_TILE
    n_sub = TOKEN_TILE // ROW_TILE
    rw, rb = router
    const = lambda i: (0, 0)
    if isinstance(x, tuple):
        x_specs = _x_tile_specs(lambda i: i)
        x_args = [x[0]] * X_TILE_BLOCKS + [x[1]]
    else:
        x_specs = [pl.BlockSpec((TOKEN_TILE, D_MODEL), lambda i: (i, 0))]
        x_args = [x]
    x1, ids, gates = pl.pallas_call(
        _proj_norm_route_kernel,
        grid=(N_TOK // TOKEN_TILE,),
        in_specs=[pl.BlockSpec((TOKEN_TILE, D_MODEL), lambda i: (i, 0))] + x_specs + [
            pl.BlockSpec((D_MODEL, D_MODEL), const, pipeline_mode=pl.Buffered(1)),
            pl.BlockSpec((1, D_MODEL), const),
            pl.BlockSpec((1, D_MODEL), const),
            pl.BlockSpec((D_MODEL, LANES), const),
            pl.BlockSpec((N_EXPERTS, 1), const)],
        out_specs=[pl.BlockSpec((TOKEN_TILE, D_MODEL), lambda i: (i, 0)),
                   pl.BlockSpec((n_sub, TOP_K, ROW_TILE), lambda i: (i, 0, 0)),
                   pl.BlockSpec((n_sub, TOP_K, ROW_TILE), lambda i: (i, 0, 0))],
        out_shape=[jax.ShapeDtypeStruct((N_TOK, D_MODEL), F32),
                   jax.ShapeDtypeStruct((nt, TOP_K, ROW_TILE), jnp.int32),
                   jax.ShapeDtypeStruct((nt, TOP_K, ROW_TILE), F32)],
        compiler_params=_params("arbitrary"),
        name="proj_norm_route",
    )(a, *x_args, w_out.astype(BF16), ln_g.reshape(1, D_MODEL), ln_b.reshape(1, D_MODEL), rw, rb)
    ids = ids.transpose(0, 2, 1).reshape(N_TOK, TOP_K)
    gates = gates.transpose(0, 2, 1).reshape(N_TOK, TOP_K)
    return x1, ids, gates


def _stream_expert_weights(layer, tables, w_refs, stage_refs, wb_refs, sem_ref):
    te_ref, nx_ref, nt_ref = tables
    i = pl.program_id(0)

    def copies(expert):
        return [pltpu.make_async_copy(w.at[layer, expert], stage, sem_ref.at[j])
                for j, (w, stage) in enumerate(zip(w_refs, stage_refs))]

    @pl.when(i == 0)
    def _():
        for c in copies(te_ref[0]):
            c.start()

    first = jnp.logical_or(i == 0, te_ref[i] != te_ref[jnp.maximum(i - 1, 0)])

    @pl.when(jnp.logical_and(first, i < nt_ref[0]))
    def _():
        for c in copies(te_ref[i]):
            c.wait()
        for stage, wb in zip(stage_refs, wb_refs):
            _cast_weight(stage, wb)

        @pl.when(nx_ref[i] >= 0)
        def _():
            for c in copies(nx_ref[i]):
                c.start()


def _issue_row_copies(idx_ref, base, src_ref, dst_ref, sem, n_rows, stride=1):
    def body(g, carry):
        r0 = pl.multiple_of(g * SUBLANES, SUBLANES)
        for k in range(SUBLANES):
            row = idx_ref[base + stride * (r0 + k)]
            pltpu.make_async_copy(src_ref.at[pl.ds(row, 1)], dst_ref.at[pl.ds(r0 + k, 1)],
                                  sem).start(priority=k % 2)
        return carry
    lax.fori_loop(0, n_rows // SUBLANES, body, 0)


def _gather_kernel(idx_ref, nt_ref, src_ref, out_ref, buf_ref, sem_ref):
    i = pl.program_id(0)
    n = nt_ref[0]
    tg = out_ref.shape[0]

    def start_tile(tile, slot):
        _issue_row_copies(idx_ref, tile * tg, src_ref, buf_ref.at[slot], sem_ref.at[slot], tg)

    @pl.when(i == 0)
    def _():
        start_tile(0, 0)

    @pl.when(i + 1 < n)
    def _():
        start_tile(i + 1, (i + 1) % 2)

    @pl.when(i < n)
    def _():
        slot = i % 2
        pltpu.make_async_copy(src_ref.at[pl.ds(0, tg)], buf_ref.at[slot], sem_ref.at[slot]).wait()
        out_ref[...] = buf_ref[slot].astype(out_ref.dtype)

    @pl.when(i >= n)
    def _():
        out_ref[...] = jnp.zeros_like(out_ref)


def _gather_rows(src, idx, n_tiles, out_dtype, name):
    m = idx.shape[0]
    width = src.shape[1]
    return pl.pallas_call(
        _gather_kernel,
        grid_spec=pltpu.PrefetchScalarGridSpec(
            num_scalar_prefetch=2,
            grid=(m // MOE_TILE,),
            in_specs=[pl.BlockSpec(memory_space=pl.ANY)],
            out_specs=pl.BlockSpec((MOE_TILE, width), lambda i, idx, nt: (i, 0)),
            scratch_shapes=[pltpu.VMEM((2, MOE_TILE, width), src.dtype),
                            pltpu.SemaphoreType.DMA((2,))]),
        out_shape=jax.ShapeDtypeStruct((m, width), out_dtype),
        compiler_params=_params("arbitrary"),
        name=name,
    )(idx, n_tiles, src)


def _moe_ffn_kernel(layer, te_ref, nx_ref, nt_ref, xs_ref, w1_ref, w3_ref, w2_ref, ys_ref,
                    stage13_ref, stage2_ref, w1b_ref, w3b_ref, w2b_ref, sem_ref):
    _stream_expert_weights(layer, (te_ref, nx_ref, nt_ref), (w1_ref, w3_ref, w2_ref),
                           (stage13_ref.at[0], stage13_ref.at[1], stage2_ref),
                           (w1b_ref, w3b_ref, w2b_ref), sem_ref)
    i = pl.program_id(0)

    @pl.when(i < nt_ref[0])
    def _():
        xs = xs_ref[...]
        h1 = _dot_nn(xs, w1b_ref[...])
        h3 = _dot_nn(xs, w3b_ref[...])
        hid = (_silu(h1) * h3).astype(BF16)
        ys_ref[...] = _dot_nn(hid, w2b_ref[...])

    @pl.when(i >= nt_ref[0])
    def _():
        ys_ref[...] = jnp.zeros_like(ys_ref)


def _moe_ffn(xs, w1, w3, w2, layer, tables):
    last = lambda i, te, nx, nt: (jnp.minimum(i, nt[0] - 1), 0)
    return pl.pallas_call(
        functools.partial(_moe_ffn_kernel, layer),
        grid_spec=pltpu.PrefetchScalarGridSpec(
            num_scalar_prefetch=3,
            grid=(MOE_MAX_TILES,),
            in_specs=[pl.BlockSpec((MOE_TILE, D_MODEL), last),
                      pl.BlockSpec(memory_space=pl.ANY),
                      pl.BlockSpec(memory_space=pl.ANY),
                      pl.BlockSpec(memory_space=pl.ANY)],
            out_specs=pl.BlockSpec((MOE_TILE, D_MODEL), lambda i, te, nx, nt: (i, 0)),
            scratch_shapes=[pltpu.VMEM((2, D_MODEL, D_FF), F32),
                            pltpu.VMEM((D_FF, D_MODEL), F32),
                            pltpu.VMEM((D_MODEL, D_FF), BF16), pltpu.VMEM((D_MODEL, D_FF), BF16),
                            pltpu.VMEM((D_FF, D_MODEL), BF16),
                            pltpu.SemaphoreType.DMA((3,))]),
        out_shape=jax.ShapeDtypeStruct((N_ASSIGN_PAD, D_MODEL), F32),
        compiler_params=_params("arbitrary"),
        name="moe_ffn",
    )(*tables, xs, w1, w3, w2)


def _combine_norm_kernel(final, pos_ref, x_ref, ys_ref, gt_ref, lg_ref, lb_ref, o0_ref, o1_ref,
                         buf_ref, sem_ref):
    i = pl.program_id(0)
    n = pl.num_programs(0)
    tm = x_ref.shape[0]

    def start_tile(tile, slot):
        for k in range(TOP_K):
            _issue_row_copies(pos_ref, tile * tm * TOP_K + k, ys_ref, buf_ref.at[slot, k],
                              sem_ref.at[slot], tm, stride=TOP_K)

    @pl.when(i == 0)
    def _():
        start_tile(0, 0)

    @pl.when(i + 1 < n)
    def _():
        start_tile(i + 1, (i + 1) % 2)

    slot = i % 2
    for k in range(TOP_K):
        pltpu.make_async_copy(ys_ref.at[pl.ds(0, tm)], buf_ref.at[slot, k], sem_ref.at[slot]).wait()
    gt = gt_ref[...]
    moe = buf_ref[slot, 0] * gt[:, 0:1] + buf_ref[slot, 1] * gt[:, 1:2]
    xn = _layer_norm(ALPHA * x_ref[...] + moe, lg_ref[...], lb_ref[...])
    if final:
        @pl.when(i < n - 1)
        def _():
            o0_ref[...] = xn

        @pl.when(i == n - 1)
        def _():
            o1_ref[...] = xn
    else:
        o0_ref[...] = xn
        o1_ref[...] = xn.astype(BF16)


def _combine_norm(x, ys, pos, gates, ln_g, ln_b, final):
    tm = N_SAMPLE if final else ROW_TILE
    const = lambda i, pos: (0, 0)
    tile = lambda i, pos: (i, 0)
    if final:
        n_prompt_tiles = N_PROMPT // tm
        out_specs = [pl.BlockSpec((tm, D_MODEL), lambda i, pos: (jnp.minimum(i, n_prompt_tiles - 1), 0)),
                     pl.BlockSpec((tm, D_MODEL), const)]
        out_shape = [jax.ShapeDtypeStruct((N_PROMPT, D_MODEL), F32),
                     jax.ShapeDtypeStruct((N_SAMPLE, D_MODEL), F32)]
    else:
        out_specs = [pl.BlockSpec((tm, D_MODEL), tile), pl.BlockSpec((tm, D_MODEL), tile)]
        out_shape = [jax.ShapeDtypeStruct((N_TOK, D_MODEL), F32),
                     jax.ShapeDtypeStruct((N_TOK, D_MODEL), BF16)]
    return pl.pallas_call(
        functools.partial(_combine_norm_kernel, final),
        grid_spec=pltpu.PrefetchScalarGridSpec(
            num_scalar_prefetch=1,
            grid=(N_TOK // tm,),
            in_specs=[pl.BlockSpec((tm, D_MODEL), tile),
                      pl.BlockSpec(memory_space=pl.ANY),
                      pl.BlockSpec((tm, TOP_K), tile),
                      pl.BlockSpec((1, D_MODEL), const),
                      pl.BlockSpec((1, D_MODEL), const)],
            out_specs=out_specs,
            scratch_shapes=[pltpu.VMEM((2, TOP_K, tm, D_MODEL), F32),
                            pltpu.SemaphoreType.DMA((2,))]),
        out_shape=out_shape,
        compiler_params=_params("arbitrary"),
        name="combine_norm",
    )(pos, x, ys, gates, ln_g.reshape(1, D_MODEL), ln_b.reshape(1, D_MODEL))


def _routing_tables(ids):
    e = ids.reshape(-1)
    onehot = (e[:, None] == jnp.arange(N_EXPERTS, dtype=jnp.int32)[None, :]).astype(jnp.int32)
    csum = jnp.cumsum(onehot, axis=0)
    rank = jnp.sum(onehot * csum, axis=1) - 1
    sizes = csum[-1]
    tiles_per = (sizes + MOE_TILE - 1) // MOE_TILE
    tile_end = jnp.cumsum(tiles_per)
    tile_start = tile_end - tiles_per
    n_tiles = tile_end[-1]
    pos = (jnp.sum(onehot * (tile_start * MOE_TILE)[None, :], axis=1) + rank).astype(jnp.int32)
    tile = jnp.arange(MOE_MAX_TILES, dtype=jnp.int32)
    owner = jnp.sum((tile_end[None, :] <= jnp.minimum(tile, n_tiles - 1)[:, None]).astype(jnp.int32), axis=1)
    order = jnp.argsort(e, stable=True).astype(jnp.int32)
    owner_hot = (owner[:, None] == jnp.arange(N_EXPERTS, dtype=jnp.int32)[None, :]).astype(jnp.int32)
    size_start = jnp.cumsum(sizes) - sizes
    tile_first = jnp.sum(owner_hot * (size_start - tile_start * MOE_TILE)[None, :], axis=1)
    tile_limit = jnp.sum(owner_hot * (size_start + sizes)[None, :], axis=1)
    row = jnp.arange(N_ASSIGN_PAD, dtype=jnp.int32).reshape(MOE_MAX_TILES, MOE_TILE)
    src = tile_first[:, None] + row
    valid = src < tile_limit[:, None]
    row_token = jnp.where(valid, order[jnp.clip(src, 0, N_ASSIGN - 1)] // TOP_K, row % N_TOK).reshape(-1)
    experts = jnp.arange(N_EXPERTS, dtype=jnp.int32)
    later = lax.cummin(jnp.where(tiles_per > 0, experts, N_EXPERTS), reverse=True)
    next_expert = jnp.concatenate([later[1:], jnp.full((1,), N_EXPERTS, jnp.int32)])
    next_expert = jnp.where(next_expert >= N_EXPERTS, -1, next_expert)
    tile_next = jnp.sum(owner_hot * next_expert[None, :], axis=1)
    n_tiles = n_tiles.reshape(1).astype(jnp.int32)
    tables = (owner.astype(jnp.int32), tile_next.astype(jnp.int32), n_tiles)
    return pos, row_token.astype(jnp.int32), tables


def _moe_block(x, ids, gates, w1, w3, w2, layer, ln_g, ln_b, final):
    pos, row_token, tables = _routing_tables(ids)
    xs = _gather_rows(x, row_token, tables[2], BF16, "moe_gather")
    ys = _moe_ffn(xs, w1, w3, w2, layer, tables)
    return _combine_norm(x, ys, pos, gates, ln_g, ln_b, final)


def _conv_in_kernel(x_ref, wb_ref, wc_ref, wh_ref, cw_ref, s0_ref, s1_ref, gbz_ref, u_ref,
                    wbb_ref, wcb_ref, whb_ref, ubuf_ref):
    i = pl.program_id(1)
    tm = x_ref.shape[0]
    pad = SUBLANES

    @pl.when(i == 0)
    def _():
        _cast_weight(wb_ref, wbb_ref)
        _cast_weight(wc_ref, wcb_ref)
        _cast_weight(wh_ref, whb_ref)
        ubuf_ref[0:pad, :] = jnp.zeros((pad, ubuf_ref.shape[1]), F32)

    w0 = cw_ref[0:1, :]
    w1 = cw_ref[1:2, :]
    w2 = cw_ref[2:3, :]
    sub = ROW_TILE
    for s in range(tm // sub):
        lo, hi = s * sub, (s + 1) * sub
        x = x_ref[lo:hi, :]
        gb = _dot_nn(x, wbb_ref[...])
        u = _dot_nn(x, wcb_ref[...]) * _dot_nn(x, whb_ref[...])
        u_ref[lo:hi, :] = u
        ubuf_ref[pad + lo:pad + hi, :] = u
        t = (i * tm + lo + lax.broadcasted_iota(jnp.int32, (sub, 1), 0)) & (SEQ - 1)
        u_m1 = jnp.where(t >= 1, ubuf_ref[pad - 1 + lo:pad - 1 + hi, :], 0.0)
        u_m2 = jnp.where(t >= 2, ubuf_ref[pad - 2 + lo:pad - 2 + hi, :], 0.0)
        z = u_m2 * w0 + u_m1 * w1 + u * w2
        gbz_ref[lo:hi, :] = (gb * z).astype(BF16)
    ubuf_ref[0:pad, :] = ubuf_ref[tm:tm + pad, :]

    @pl.when(i == pl.num_programs(1) - 1)
    def _():
        first = sub - N_SAMPLE
        z_s = s0_ref[...] * w0 + s1_ref[...] * w1 + u[first:sub, :] * w2
        gbz_ref[tm - N_SAMPLE:tm, :] = (gb[first:sub, :] * z_s).astype(BF16)


def _conv_in(xb, w_in, conv_w, state):
    tn = 512
    nb = D_MODEL // tn
    s0 = state[:, 0, :]
    s1 = state[:, 1, :]
    return pl.pallas_call(
        _conv_in_kernel,
        grid=(nb, N_TOK // TOKEN_TILE),
        in_specs=[pl.BlockSpec((TOKEN_TILE, D_MODEL), lambda j, i: (i, 0)),
                  pl.BlockSpec((D_MODEL, tn), lambda j, i: (0, j)),
                  pl.BlockSpec((D_MODEL, tn), lambda j, i: (0, nb + j)),
                  pl.BlockSpec((D_MODEL, tn), lambda j, i: (0, 2 * nb + j)),
                  pl.BlockSpec((CONV_W, tn), lambda j, i: (0, j)),
                  pl.BlockSpec((N_SAMPLE, tn), lambda j, i: (0, j)),
                  pl.BlockSpec((N_SAMPLE, tn), lambda j, i: (0, j))],
        out_specs=[pl.BlockSpec((TOKEN_TILE, tn), lambda j, i: (i, j)),
                   pl.BlockSpec((TOKEN_TILE, tn), lambda j, i: (i, j))],
        out_shape=[jax.ShapeDtypeStruct((N_TOK, D_MODEL), BF16),
                   jax.ShapeDtypeStruct((N_TOK, D_MODEL), F32)],
        scratch_shapes=[pltpu.VMEM((D_MODEL, tn), BF16), pltpu.VMEM((D_MODEL, tn), BF16),
                        pltpu.VMEM((D_MODEL, tn), BF16),
                        pltpu.VMEM((TOKEN_TILE + 2 * SUBLANES, tn), F32)],
        compiler_params=_params("arbitrary", "arbitrary"),
        name="conv_in",
    )(xb, w_in, w_in, w_in, conv_w, s0, s1)


def kernel(x_prompt, x_sample, state_gla, state_conv, router_w, router_b, gla_w_in, gla_w_gate, gla_b_gate,
           gla_norm_g, gla_w_out, conv_w_in, conv_w, conv_w_out, ln_mix_g, ln_mix_b, ln_ffn_g, ln_ffn_b,
           moe_w1, moe_w3, moe_w2):
    xp = x_prompt.reshape(N_PROMPT, D_MODEL)
    xs = x_sample.reshape(N_SAMPLE, D_MODEL)
    router = _router_operands(router_w, router_b)

    w_in_t = gla_w_in[0].T
    qkvr, gl = _gla_in_proj(xp, xs, w_in_t)
    g = _gla_gate(gl, gla_w_gate[0], gla_b_gate[0])
    norm_g = gla_norm_g[0].reshape(1, DV_TOTAL)
    og, s_prompt = _gla_prompt(qkvr, g, norm_g)
    og, s_sample = _gla_sample(qkvr, g, norm_g, state_gla[0], og)
    x1, ids, gates = _proj_norm_route(og, (xp, xs), gla_w_out[0], ln_mix_g[0], ln_mix_b[0], router)
    x2, x2b = _moe_block(x1, ids, gates, moe_w1, moe_w3, moe_w2, 0, ln_ffn_g[0], ln_ffn_b[0], False)

    gbz, u = _conv_in(x2b, conv_w_in[0], conv_w[0], state_conv[0])
    x3, ids, gates = _proj_norm_route(gbz, x2, conv_w_out[0], ln_mix_g[1], ln_mix_b[1], router)
    y_p, y_s = _moe_block(x3, ids, gates, moe_w1, moe_w3, moe_w2, 1, ln_ffn_g[1], ln_ffn_b[1], True)

    y_prompt = y_p.reshape(N_PROMPT_SEQ, SEQ, D_MODEL)
    y_sample = y_s.reshape(N_SAMPLE, 1, D_MODEL)
    conv_prompt = jnp.stack([u[(b + 1) * SEQ - (CONV_W - 1):(b + 1) * SEQ] for b in range(N_PROMPT_SEQ)])
    conv_sample = jnp.concatenate([state_conv[0][:, 1:, :], u[N_PROMPT:][:, None, :]], axis=1)
    return (y_prompt, y_sample, s_prompt[None], conv_prompt[None], s_sample[None], conv_sample[None])
```

```python
import functools

import jax
import jax.numpy as jnp
from jax import lax
from jax.experimental import pallas as pl
from jax.experimental.pallas import tpu as pltpu

F32 = jnp.float32
BF16 = jnp.bfloat16

D_MODEL = 2048
N_PROMPT_SEQ = 4
SEQ = 2048
N_PROMPT = N_PROMPT_SEQ * SEQ
N_SAMPLE = 128
N_TOK = N_PROMPT + N_SAMPLE
DEPTH = 2

GLA_HEADS = 4
DK_TOTAL = D_MODEL // 2
DV_TOTAL = D_MODEL
DK_HEAD = DK_TOTAL // GLA_HEADS
DV_HEAD = DV_TOTAL // GLA_HEADS
GATE_RANK = 16
GATE_NORMALIZER = 16.0
GLA_QKVR_WIDTH = 2 * DK_TOTAL + 2 * DV_TOTAL
CONV_W = 3
N_EXPERTS = 16
N_GROUPS = 4
EXPERTS_PER_GROUP = N_EXPERTS // N_GROUPS
TOP_K = 2
D_FF = D_MODEL // 2
ALPHA = (2.0 * DEPTH) ** 0.25
LN_EPS = 1e-5
RMS_EPS = 1e-6

LANES = 128
SUBLANES = 8
VMEM_LIMIT_BYTES = 56 * 1024 * 1024

TOKEN_TILE = 640
ROW_TILE = 320
GLA_CHUNK = 64
GLA_STEP_CHUNKS = 2
GLA_SAMPLE_BATCH = 16
MOE_TILE = 256
N_ASSIGN = N_TOK * TOP_K
GATHER_TILES = 2
MOE_MAX_TILES = -(-(N_ASSIGN // MOE_TILE + N_EXPERTS) // GATHER_TILES) * GATHER_TILES
N_ASSIGN_PAD = MOE_MAX_TILES * MOE_TILE


def _params(*semantics):
    return pltpu.CompilerParams(dimension_semantics=semantics, vmem_limit_bytes=VMEM_LIMIT_BYTES)


def _split3(x):
    hi = x.astype(BF16)
    r1 = x - hi.astype(F32)
    mid = r1.astype(BF16)
    lo = (r1 - mid.astype(F32)).astype(BF16)
    return hi, mid, lo


def _cast_weight(src_ref, dst_ref):
    rows = 64
    n = src_ref.shape[0] // rows

    def body(c, carry):
        r = pl.multiple_of(c * rows, rows)
        dst_ref[pl.ds(r, rows), :] = src_ref[pl.ds(r, rows), :].astype(BF16)
        return carry
    lax.fori_loop(0, n, body, 0)


def _dot_nn(a, b):
    return jnp.dot(a, b, preferred_element_type=F32)


def _dot_tn(a, b):
    return lax.dot_general(a, b, (((0,), (0,)), ((), ())), preferred_element_type=F32)


def _dot_nt(a, b):
    return lax.dot_general(a, b, (((1,), (1,)), ((), ())), preferred_element_type=F32)


def _col_bcast(rows, n):
    ones = jnp.ones((rows.shape[0], LANES), BF16)
    hi, mid, lo = _split3(rows)
    col = _dot_tn(hi, ones) + _dot_tn(mid, ones) + _dot_tn(lo, ones)
    return jnp.concatenate([col] * (n // LANES), axis=1)


def _silu(r):
    return r / (1.0 + jnp.exp(-r))


def _layer_norm(h, g, b):
    mu = jnp.mean(h, axis=-1, keepdims=True)
    d = h - mu
    var = jnp.mean(d * d, axis=-1, keepdims=True)
    return d * lax.rsqrt(var + LN_EPS) * g + b


X_TILE_BLOCKS = TOKEN_TILE // N_SAMPLE


def _x_tile_specs(tile_index):
    last = N_PROMPT // N_SAMPLE - 1

    def prompt_spec(k):
        return pl.BlockSpec((N_SAMPLE, D_MODEL),
                            lambda *g: (jnp.minimum(tile_index(*g) * X_TILE_BLOCKS + k, last), 0))
    return ([prompt_spec(k) for k in range(X_TILE_BLOCKS)]
            + [pl.BlockSpec((N_SAMPLE, D_MODEL), lambda *g: (0, 0))])


def _x_tile(i, x_refs):
    *prompt_refs, sample_ref = x_refs
    tail_is_sample = (i + 1) * TOKEN_TILE > N_PROMPT
    blocks = [r[...] for r in prompt_refs[:-1]]
    blocks.append(jnp.where(tail_is_sample, sample_ref[...], prompt_refs[-1][...]))
    return jnp.concatenate(blocks, axis=0)


def _gla_in_proj_kernel(*refs):
    *x_refs, wt_ref, wgl_ref, o_ref, gl_ref, wb_ref = refs
    j = pl.program_id(0)
    i = pl.program_id(1)

    @pl.when(i == 0)
    def _():
        _cast_weight(wt_ref, wb_ref)

    x = _x_tile(i, x_refs).astype(BF16)
    for s in range(TOKEN_TILE // ROW_TILE):
        rows = slice(s * ROW_TILE, (s + 1) * ROW_TILE)
        o_ref[rows, :] = _dot_nt(x[rows, :], wb_ref[...]).astype(o_ref.dtype)

    @pl.when(j == pl.num_programs(0) - 1)
    def _():
        wgl = jnp.concatenate([wgl_ref[...].astype(BF16), jnp.zeros((LANES - GATE_RANK, D_MODEL), BF16)],
                              axis=0)
        gl_ref[...] = _dot_nt(x, wgl)

    @pl.when(j < pl.num_programs(0) - 1)
    def _():
        gl_ref[...] = jnp.zeros_like(gl_ref)


def _gla_in_proj(x_prompt, x_sample, w_t):
    tn = 1024
    n_pass = GLA_QKVR_WIDTH // tn
    n_tiles = N_TOK // TOKEN_TILE
    return pl.pallas_call(
        _gla_in_proj_kernel,
        grid=(n_pass, n_tiles),
        in_specs=_x_tile_specs(lambda j, i: i) + [
            pl.BlockSpec((tn, D_MODEL), lambda j, i: (j, 0)),
            pl.BlockSpec((GATE_RANK, D_MODEL), lambda j, i: (GLA_QKVR_WIDTH // GATE_RANK, 0))],
        out_specs=[pl.BlockSpec((TOKEN_TILE, tn), lambda j, i: (i, j)),
                   pl.BlockSpec((TOKEN_TILE, LANES), lambda j, i: (jnp.where(j == n_pass - 1, i, n_tiles), 0))],
        out_shape=[jax.ShapeDtypeStruct((N_TOK, GLA_QKVR_WIDTH), F32),
                   jax.ShapeDtypeStruct((N_TOK + TOKEN_TILE, LANES), F32)],
        scratch_shapes=[pltpu.VMEM((tn, D_MODEL), BF16)],
        compiler_params=_params("arbitrary", "arbitrary"),
        name="gla_in_proj",
    )(*([x_prompt] * X_TILE_BLOCKS), x_sample, w_t, w_t)


def _gate_kernel(gl_ref, wg_ref, bg_ref, g_ref):
    z = _dot_nn(gl_ref[...].astype(BF16), wg_ref[...]) + bg_ref[...]
    log_sig = jnp.minimum(z, 0.0) - jnp.log1p(jnp.exp(-jnp.abs(z)))
    g_ref[...] = log_sig * (1.0 / GATE_NORMALIZER)


def _gla_gate(gl, w_gate, b_gate):
    wg = jnp.pad(w_gate, ((0, LANES - GATE_RANK), (0, 0))).astype(BF16)
    return pl.pallas_call(
        _gate_kernel,
        grid=(N_TOK // TOKEN_TILE,),
        in_specs=[pl.BlockSpec((TOKEN_TILE, LANES), lambda i: (i, 0)),
                  pl.BlockSpec((LANES, DK_TOTAL), lambda i: (0, 0)),
                  pl.BlockSpec((1, DK_TOTAL), lambda i: (0, 0))],
        out_specs=pl.BlockSpec((TOKEN_TILE, DK_TOTAL), lambda i: (i, 0)),
        out_shape=jax.ShapeDtypeStruct((N_TOK, DK_TOTAL), F32),
        compiler_params=_params("arbitrary"),
        name="gla_gate",
    )(gl, wg, b_gate.reshape(1, DK_TOTAL))


def _rms_gate(o, norm_g, r):
    o = o * lax.rsqrt(jnp.mean(o * o, axis=-1, keepdims=True) + RMS_EPS)
    return (o * norm_g) * _silu(r)


def _gla_prompt_kernel(q_ref, k_ref, v_ref, r_ref, g_ref, ng_ref, og_in_ref, og_ref, s_out_ref, s_ref):
    del og_in_ref
    c = pl.program_id(1)
    cc = GLA_CHUNK

    @pl.when(c == 0)
    def _():
        s_ref[...] = jnp.zeros_like(s_ref)

    for ci in range(q_ref.shape[0] // cc):
        last_chunk = ci == q_ref.shape[0] // cc - 1
        _gla_chunk(slice(ci * cc, (ci + 1) * cc), last_chunk, c, q_ref, k_ref, v_ref, r_ref, g_ref,
                   ng_ref, og_ref, s_out_ref, s_ref)


def _gla_chunk(rows, last_chunk, c, q_ref, k_ref, v_ref, r_ref, g_ref, ng_ref, og_ref, s_out_ref, s_ref):
    cc = GLA_CHUNK
    row = lax.broadcasted_iota(jnp.int32, (cc, cc), 0)
    col = lax.broadcasted_iota(jnp.int32, (cc, cc), 1)
    causal = col <= row
    tri = jnp.where(causal, 1.0, 0.0).astype(BF16)
    b = _dot_nn(jnp.concatenate([tri] * 3, axis=1),
                jnp.concatenate(_split3(g_ref[rows, :]), axis=0))
    b_last = b[cc - 1:cc, :]
    b_mid = b[cc // 2 - 1:cc // 2, :]

    q = q_ref[rows, :]
    k = k_ref[rows, :] * (DK_HEAD ** -0.5)
    q_in = (q * jnp.exp(b)).astype(BF16)
    qh, qm, _ = _split3(q * jnp.exp(b - b_mid))
    kh, km, _ = _split3(k * jnp.exp(b_mid - b))
    k_d = (k * jnp.exp(b_last - b)).astype(BF16)

    pad = 2 * SUBLANES
    sub = lax.broadcasted_iota(jnp.int32, (pad, DK_TOTAL), 0)
    decay_rows = jnp.where(sub == 0, jnp.broadcast_to(jnp.exp(b_last), (pad, DK_TOTAL)), 0.0)
    decay_col = _dot_tn(jnp.concatenate(_split3(decay_rows), axis=0),
                        jnp.ones((3 * pad, LANES), BF16))

    for h in range(GLA_HEADS):
        dk = slice(h * DK_HEAD, (h + 1) * DK_HEAD)
        dv = slice(h * DV_HEAD, (h + 1) * DV_HEAD)
        q3 = jnp.concatenate([qh[:, dk], qh[:, dk], qm[:, dk]], axis=1)
        k3 = jnp.concatenate([kh[:, dk], km[:, dk], kh[:, dk]], axis=1)
        scores = jnp.where(causal, _dot_nt(q3, k3), 0.0).astype(BF16)
        v = v_ref[rows, dv].astype(BF16)
        s_old = s_ref[h]
        o = _dot_nn(jnp.concatenate([q_in[:, dk], scores], axis=1),
                    jnp.concatenate([s_old.astype(BF16), v], axis=0))
        decay = jnp.concatenate([decay_col[dk, :]] * (DV_HEAD // LANES), axis=1)
        s_new = decay * s_old + _dot_tn(k_d[:, dk], v)
        s_ref[h] = s_new

        if last_chunk:
            @pl.when(c == pl.num_programs(1) - 1)
            def _():
                s_out_ref[0, h] = s_new

        og_ref[rows, dv] = _rms_gate(o, ng_ref[:, dv], r_ref[rows, dv]).astype(BF16)


def _gla_prompt(qkvr, g, norm_g):
    step = GLA_STEP_CHUNKS * GLA_CHUNK
    nc = SEQ // step
    row = lambda b, c: b * nc + c
    return pl.pallas_call(
        _gla_prompt_kernel,
        grid=(N_PROMPT_SEQ, nc),
        in_specs=[pl.BlockSpec((step, DK_TOTAL), lambda b, c: (row(b, c), 0)),
                  pl.BlockSpec((step, DK_TOTAL), lambda b, c: (row(b, c), 1)),
                  pl.BlockSpec((step, DV_TOTAL), lambda b, c: (row(b, c), 1)),
                  pl.BlockSpec((step, DV_TOTAL), lambda b, c: (row(b, c), 2)),
                  pl.BlockSpec((step, DK_TOTAL), lambda b, c: (row(b, c), 0)),
                  pl.BlockSpec((1, DV_TOTAL), lambda b, c: (0, 0)),
                  pl.BlockSpec(memory_space=pl.ANY)],
        out_specs=[pl.BlockSpec((step, DV_TOTAL), lambda b, c: (row(b, c), 0)),
                   pl.BlockSpec((1, GLA_HEADS, DK_HEAD, DV_HEAD), lambda b, c: (b, 0, 0, 0))],
        out_shape=[jax.ShapeDtypeStruct((N_TOK, DV_TOTAL), BF16),
                   jax.ShapeDtypeStruct((N_PROMPT_SEQ, GLA_HEADS, DK_HEAD, DV_HEAD), F32)],
        scratch_shapes=[pltpu.VMEM((GLA_HEADS, DK_HEAD, DV_HEAD), F32)],
        input_output_aliases={6: 0},
        compiler_params=_params("arbitrary", "arbitrary"),
        name="gla_prompt",
    )(qkvr, qkvr, qkvr, qkvr, g, norm_g, jnp.zeros((N_TOK, DV_TOTAL), BF16))


def _gla_sample_kernel(q_ref, k_ref, v_ref, r_ref, g_ref, ng_ref, s_ref, og_in_ref, og_ref, s_out_ref,
                       o_scr):
    del og_in_ref
    bb = q_ref.shape[0]
    q = q_ref[...]
    k = k_ref[...] * (DK_HEAD ** -0.5)
    v = v_ref[...]
    eg = jnp.exp(g_ref[...])
    qe = q * eg
    qk = jnp.sum(q * k, axis=-1, keepdims=True)
    sub = lax.broadcasted_iota(jnp.int32, (bb, DK_HEAD), 0)
    for bi in range(bb):
        sel = sub == bi
        s_old = s_ref[bi, 0]
        decay = _col_bcast(jnp.where(sel, eg, 0.0), DV_HEAD)
        k_col = _col_bcast(jnp.where(sel, k, 0.0), DV_HEAD)
        qe_col = _col_bcast(jnp.where(sel, qe, 0.0), DV_HEAD)
        s_out_ref[bi, 0] = decay * s_old + k_col * v[bi:bi + 1, :]
        o_scr[bi:bi + 1, :] = jnp.sum(qe_col * s_old, axis=0, keepdims=True)
    o = qk * v + o_scr[...]
    og_ref[...] = _rms_gate(o, ng_ref[...], r_ref[...]).astype(BF16)


def _gla_sample(qkvr, g, norm_g, state, og):
    bb = GLA_SAMPLE_BATCH
    r0 = N_PROMPT // bb
    hk = DK_TOTAL // DK_HEAD
    hv = 2 * DK_TOTAL // DV_HEAD
    return pl.pallas_call(
        _gla_sample_kernel,
        grid=(N_SAMPLE // bb, GLA_HEADS),
        in_specs=[pl.BlockSpec((bb, DK_HEAD), lambda i, h: (r0 + i, h)),
                  pl.BlockSpec((bb, DK_HEAD), lambda i, h: (r0 + i, hk + h)),
                  pl.BlockSpec((bb, DV_HEAD), lambda i, h: (r0 + i, hv + h)),
                  pl.BlockSpec((bb, DV_HEAD), lambda i, h: (r0 + i, hv + GLA_HEADS + h)),
                  pl.BlockSpec((bb, DK_HEAD), lambda i, h: (r0 + i, h)),
                  pl.BlockSpec((1, DV_HEAD), lambda i, h: (0, h)),
                  pl.BlockSpec((bb, 1, DK_HEAD, DV_HEAD), lambda i, h: (i, h, 0, 0)),
                  pl.BlockSpec(memory_space=pl.ANY)],
        out_specs=[pl.BlockSpec((bb, DV_HEAD), lambda i, h: (r0 + i, h)),
                   pl.BlockSpec((bb, 1, DK_HEAD, DV_HEAD), lambda i, h: (i, h, 0, 0))],
        out_shape=[jax.ShapeDtypeStruct((N_TOK, DV_TOTAL), BF16),
                   jax.ShapeDtypeStruct((N_SAMPLE, GLA_HEADS, DK_HEAD, DV_HEAD), F32)],
        scratch_shapes=[pltpu.VMEM((bb, DV_HEAD), F32)],
        input_output_aliases={7: 0},
        compiler_params=_params("arbitrary", "arbitrary"),
        name="gla_sample",
    )(qkvr, qkvr, qkvr, qkvr, g, norm_g, state, og)


def _top2_of4(p):
    ranks = []
    for j in range(4):
        rk = jnp.zeros(p[j].shape, jnp.int32)
        for i in range(4):
            if i == j:
                continue
            beats = (p[i] >= p[j]) if i < j else (p[i] > p[j])
            rk = rk + jnp.where(beats, 1, 0)
        ranks.append(rk)

    def pick(rank):
        val = jnp.zeros(p[0].shape, F32)
        idx = jnp.zeros(p[0].shape, jnp.int32)
        for j in range(4):
            hit = ranks[j] == rank
            val = jnp.where(hit, p[j], val)
            idx = jnp.where(hit, j, idx)
        return val, idx

    v1, i1 = pick(0)
    v2, i2 = pick(1)
    return v1, i1, v2, i2


def _route(x, rw_ref, rb_ref):
    logits = _dot_nn(x.astype(BF16), rw_ref[...])
    n = x.shape[0]
    n_pad = -n % LANES
    if n_pad:
        logits = jnp.concatenate([logits, jnp.zeros((n_pad, LANES), F32)], axis=0)
    lt = logits.T[0:N_EXPERTS, 0:n] + rb_ref[...]
    e = jnp.exp(lt - jnp.max(lt, axis=0, keepdims=True))
    probs = e / jnp.sum(e, axis=0, keepdims=True)
    best = None
    for grp in range(N_GROUPS):
        rows = [probs[grp * EXPERTS_PER_GROUP + j:grp * EXPERTS_PER_GROUP + j + 1, :]
                for j in range(EXPERTS_PER_GROUP)]
        v1, i1, v2, i2 = _top2_of4(rows)
        score = v1 + v2
        cand = (score, v1, i1 + grp * EXPERTS_PER_GROUP, v2, i2 + grp * EXPERTS_PER_GROUP)
        if best is None:
            best = cand
        else:
            better = score > best[0]
            best = tuple(jnp.where(better, n, o) for n, o in zip(cand, best))
    _, v1, e1, v2, e2 = best
    denom = v1 + v2
    return (e1, e2), (v1 / denom, v2 / denom)


def _proj_norm_route_kernel(a_ref, *refs):
    *x_refs, w_ref, lg_ref, lb_ref, rw_ref, rb_ref, xo_ref, id_ref, gt_ref = refs
    x = x_refs[0][...] if len(x_refs) == 1 else _x_tile(pl.program_id(0), x_refs)
    sub = id_ref.shape[2]
    for s in range(id_ref.shape[0]):
        rows = slice(s * sub, (s + 1) * sub)
        y = _dot_nn(a_ref[rows, :], w_ref[...])
        xn = _layer_norm(ALPHA * x[rows, :] + y, lg_ref[...], lb_ref[...])
        xo_ref[rows, :] = xn
        (e1, e2), (g1, g2) = _route(xn, rw_ref, rb_ref)
        id_ref[s, 0:1, :] = e1
        id_ref[s, 1:2, :] = e2
        gt_ref[s, 0:1, :] = g1
        gt_ref[s, 1:2, :] = g2


def _router_operands(router_w, router_b):
    rw = jnp.pad(router_w, ((0, 0), (0, LANES - N_EXPERTS))).astype(BF16)
    return rw, router_b.reshape(N_EXPERTS, 1)


def _proj_norm_route(a, x, w_out, ln_g, ln_b, router):
    nt = N_TOK // ROW_TILE
    n_sub = TOKEN_TILE // ROW_TILE
    rw, rb = router
    const = lambda i: (0, 0)
    if isinstance(x, tuple):
        x_specs = _x_tile_specs(lambda i: i)
        x_args = [x[0]] * X_TILE_BLOCKS + [x[1]]
    else:
        x_specs = [pl.BlockSpec((TOKEN_TILE, D_MODEL), lambda i: (i, 0))]
        x_args = [x]
    x1, ids, gates = pl.pallas_call(
        _proj_norm_route_kernel,
        grid=(N_TOK // TOKEN_TILE,),
        in_specs=[pl.BlockSpec((TOKEN_TILE, D_MODEL), lambda i: (i, 0))] + x_specs + [
            pl.BlockSpec((D_MODEL, D_MODEL), const, pipeline_mode=pl.Buffered(1)),
            pl.BlockSpec((1, D_MODEL), const),
            pl.BlockSpec((1, D_MODEL), const),
            pl.BlockSpec((D_MODEL, LANES), const),
            pl.BlockSpec((N_EXPERTS, 1), const)],
        out_specs=[pl.BlockSpec((TOKEN_TILE, D_MODEL), lambda i: (i, 0)),
                   pl.BlockSpec((n_sub, TOP_K, ROW_TILE), lambda i: (i, 0, 0)),
                   pl.BlockSpec((n_sub, TOP_K, ROW_TILE), lambda i: (i, 0, 0))],
        out_shape=[jax.ShapeDtypeStruct((N_TOK, D_MODEL), F32),
                   jax.ShapeDtypeStruct((nt, TOP_K, ROW_TILE), jnp.int32),
                   jax.ShapeDtypeStruct((nt, TOP_K, ROW_TILE), F32)],
        compiler_params=_params("arbitrary"),
        name="proj_norm_route",
    )(a, *x_args, w_out.astype(BF16), ln_g.reshape(1, D_MODEL), ln_b.reshape(1, D_MODEL), rw, rb)
    ids = ids.transpose(0, 2, 1).reshape(N_TOK, TOP_K)
    gates = gates.transpose(0, 2, 1).reshape(N_TOK, TOP_K)
    return x1, ids, gates


def _stream_expert_weights(layer, tables, w_refs, stage_refs, wb_refs, sem_ref):
    te_ref, nx_ref, nt_ref = tables
    i = pl.program_id(0)

    def copy(expert, j):
        return pltpu.make_async_copy(w_refs[j].at[layer, expert], stage_refs[j], sem_ref.at[j])

    @pl.when(i == 0)
    def _():
        for j in range(len(w_refs)):
            copy(te_ref[0], j).start()

    first = jnp.logical_or(i == 0, te_ref[i] != te_ref[jnp.maximum(i - 1, 0)])

    @pl.when(jnp.logical_and(first, i < nt_ref[0]))
    def _():
        for j, (stage, wb) in enumerate(zip(stage_refs, wb_refs)):
            copy(te_ref[i], j).wait()
            _cast_weight(stage, wb)

            @pl.when(nx_ref[i] >= 0)
            def _():
                copy(nx_ref[i], j).start()


def _issue_row_copies(idx_ref, base, src_ref, dst_ref, sem, n_rows, stride=1):
    def body(g, carry):
        r0 = pl.multiple_of(g * SUBLANES, SUBLANES)
        for k in range(SUBLANES):
            row = idx_ref[base + stride * (r0 + k)]
            pltpu.make_async_copy(src_ref.at[pl.ds(row, 1)], dst_ref.at[pl.ds(r0 + k, 1)],
                                  sem).start(priority=k % 2)
        return carry
    lax.fori_loop(0, n_rows // SUBLANES, body, 0)


def _gather_kernel(idx_ref, nt_ref, src_ref, out_ref, buf_ref, sem_ref):
    i = pl.program_id(0)
    tg = out_ref.shape[0]
    n = (nt_ref[0] * MOE_TILE + tg - 1) // tg

    def start_tile(tile, slot):
        _issue_row_copies(idx_ref, tile * tg, src_ref, buf_ref.at[slot], sem_ref.at[slot], tg)

    @pl.when(i == 0)
    def _():
        start_tile(0, 0)

    @pl.when(i + 1 < n)
    def _():
        start_tile(i + 1, (i + 1) % 2)

    @pl.when(i < n)
    def _():
        slot = i % 2
        pltpu.make_async_copy(src_ref.at[pl.ds(0, tg)], buf_ref.at[slot], sem_ref.at[slot]).wait()
        out_ref[...] = buf_ref[slot].astype(out_ref.dtype)

    @pl.when(i >= n)
    def _():
        out_ref[...] = jnp.zeros_like(out_ref)


def _gather_rows(src, idx, n_tiles, out_dtype, name):
    m = idx.shape[0]
    width = src.shape[1]
    tg = GATHER_TILES * MOE_TILE
    return pl.pallas_call(
        _gather_kernel,
        grid_spec=pltpu.PrefetchScalarGridSpec(
            num_scalar_prefetch=2,
            grid=(m // tg,),
            in_specs=[pl.BlockSpec(memory_space=pl.ANY)],
            out_specs=pl.BlockSpec((tg, width), lambda i, idx, nt: (i, 0)),
            scratch_shapes=[pltpu.VMEM((2, tg, width), src.dtype),
                            pltpu.SemaphoreType.DMA((2,))]),
        out_shape=jax.ShapeDtypeStruct((m, width), out_dtype),
        compiler_params=_params("arbitrary"),
        name=name,
    )(idx, n_tiles, src)


def _moe_ffn_kernel(layer, te_ref, nx_ref, nt_ref, xs_ref, w1_ref, w3_ref, w2_ref, ys_ref,
                    stage13_ref, stage2_ref, w1b_ref, w3b_ref, w2b_ref, sem_ref):
    _stream_expert_weights(layer, (te_ref, nx_ref, nt_ref), (w1_ref, w3_ref, w2_ref),
                           (stage13_ref.at[0], stage13_ref.at[1], stage2_ref),
                           (w1b_ref, w3b_ref, w2b_ref), sem_ref)
    i = pl.program_id(0)

    @pl.when(i < nt_ref[0])
    def _():
        xs = xs_ref[...]
        h1 = _dot_nn(xs, w1b_ref[...])
        h3 = _dot_nn(xs, w3b_ref[...])
        hid = (_silu(h1) * h3).astype(BF16)
        ys_ref[...] = _dot_nn(hid, w2b_ref[...])

    @pl.when(i >= nt_ref[0])
    def _():
        ys_ref[...] = jnp.zeros_like(ys_ref)


def _moe_ffn(xs, w1, w3, w2, layer, tables):
    last = lambda i, te, nx, nt: (jnp.minimum(i, nt[0] - 1), 0)
    return pl.pallas_call(
        functools.partial(_moe_ffn_kernel, layer),
        grid_spec=pltpu.PrefetchScalarGridSpec(
            num_scalar_prefetch=3,
            grid=(MOE_MAX_TILES,),
            in_specs=[pl.BlockSpec((MOE_TILE, D_MODEL), last),
                      pl.BlockSpec(memory_space=pl.ANY),
                      pl.BlockSpec(memory_space=pl.ANY),
                      pl.BlockSpec(memory_space=pl.ANY)],
            out_specs=pl.BlockSpec((MOE_TILE, D_MODEL), lambda i, te, nx, nt: (i, 0)),
            scratch_shapes=[pltpu.VMEM((2, D_MODEL, D_FF), F32),
                            pltpu.VMEM((D_FF, D_MODEL), F32),
                            pltpu.VMEM((D_MODEL, D_FF), BF16), pltpu.VMEM((D_MODEL, D_FF), BF16),
                            pltpu.VMEM((D_FF, D_MODEL), BF16),
                            pltpu.SemaphoreType.DMA((3,))]),
        out_shape=jax.ShapeDtypeStruct((N_ASSIGN_PAD, D_MODEL), F32),
        compiler_params=_params("arbitrary"),
        name="moe_ffn",
    )(*tables, xs, w1, w3, w2)


def _combine_norm_kernel(final, pos_ref, x_ref, ys_ref, gt_ref, lg_ref, lb_ref, o0_ref, o1_ref,
                         buf_ref, sem_ref):
    i = pl.program_id(0)
    n = pl.num_programs(0)
    tm = x_ref.shape[0]

    def start_tile(tile, slot):
        for k in range(TOP_K):
            _issue_row_copies(pos_ref, tile * tm * TOP_K + k, ys_ref, buf_ref.at[slot, k],
                              sem_ref.at[slot], tm, stride=TOP_K)

    @pl.when(i == 0)
    def _():
        start_tile(0, 0)

    @pl.when(i + 1 < n)
    def _():
        start_tile(i + 1, (i + 1) % 2)

    slot = i % 2
    for k in range(TOP_K):
        pltpu.make_async_copy(ys_ref.at[pl.ds(0, tm)], buf_ref.at[slot, k], sem_ref.at[slot]).wait()
    gt = gt_ref[...]
    moe = buf_ref[slot, 0] * gt[:, 0:1] + buf_ref[slot, 1] * gt[:, 1:2]
    xn = _layer_norm(ALPHA * x_ref[...] + moe, lg_ref[...], lb_ref[...])
    if final:
        @pl.when(i < n - 1)
        def _():
            o0_ref[...] = xn

        @pl.when(i == n - 1)
        def _():
            o1_ref[...] = xn
    else:
        o0_ref[...] = xn
        o1_ref[...] = xn.astype(BF16)


def _combine_norm(x, ys, pos, gates, ln_g, ln_b, final):
    tm = N_SAMPLE if final else ROW_TILE
    const = lambda i, pos: (0, 0)
    tile = lambda i, pos: (i, 0)
    if final:
        n_prompt_tiles = N_PROMPT // tm
        out_specs = [pl.BlockSpec((tm, D_MODEL), lambda i, pos: (jnp.minimum(i, n_prompt_tiles - 1), 0)),
                     pl.BlockSpec((tm, D_MODEL), const)]
        out_shape = [jax.ShapeDtypeStruct((N_PROMPT, D_MODEL), F32),
                     jax.ShapeDtypeStruct((N_SAMPLE, D_MODEL), F32)]
    else:
        out_specs = [pl.BlockSpec((tm, D_MODEL), tile), pl.BlockSpec((tm, D_MODEL), tile)]
        out_shape = [jax.ShapeDtypeStruct((N_TOK, D_MODEL), F32),
                     jax.ShapeDtypeStruct((N_TOK, D_MODEL), BF16)]
    return pl.pallas_call(
        functools.partial(_combine_norm_kernel, final),
        grid_spec=pltpu.PrefetchScalarGridSpec(
            num_scalar_prefetch=1,
            grid=(N_TOK // tm,),
            in_specs=[pl.BlockSpec((tm, D_MODEL), tile),
                      pl.BlockSpec(memory_space=pl.ANY),
                      pl.BlockSpec((tm, TOP_K), tile),
                      pl.BlockSpec((1, D_MODEL), const),
                      pl.BlockSpec((1, D_MODEL), const)],
            out_specs=out_specs,
            scratch_shapes=[pltpu.VMEM((2, TOP_K, tm, D_MODEL), F32),
                            pltpu.SemaphoreType.DMA((2,))]),
        out_shape=out_shape,
        compiler_params=_params("arbitrary"),
        name="combine_norm",
    )(pos, x, ys, gates, ln_g.reshape(1, D_MODEL), ln_b.reshape(1, D_MODEL))


def _routing_tables(ids):
    e = ids.reshape(-1)
    onehot = (e[:, None] == jnp.arange(N_EXPERTS, dtype=jnp.int32)[None, :]).astype(jnp.int32)
    csum = jnp.cumsum(onehot, axis=0)
    rank = jnp.sum(onehot * csum, axis=1) - 1
    sizes = csum[-1]
    tiles_per = (sizes + MOE_TILE - 1) // MOE_TILE
    tile_end = jnp.cumsum(tiles_per)
    tile_start = tile_end - tiles_per
    n_tiles = tile_end[-1]
    pos = (jnp.sum(onehot * (tile_start * MOE_TILE)[None, :], axis=1) + rank).astype(jnp.int32)
    tile = jnp.arange(MOE_MAX_TILES, dtype=jnp.int32)
    owner = jnp.sum((tile_end[None, :] <= jnp.minimum(tile, n_tiles - 1)[:, None]).astype(jnp.int32), axis=1)
    order = jnp.argsort(e, stable=True).astype(jnp.int32)
    owner_hot = (owner[:, None] == jnp.arange(N_EXPERTS, dtype=jnp.int32)[None, :]).astype(jnp.int32)
    size_start = jnp.cumsum(sizes) - sizes
    tile_first = jnp.sum(owner_hot * (size_start - tile_start * MOE_TILE)[None, :], axis=1)
    tile_limit = jnp.sum(owner_hot * (size_start + sizes)[None, :], axis=1)
    row = jnp.arange(N_ASSIGN_PAD, dtype=jnp.int32).reshape(MOE_MAX_TILES, MOE_TILE)
    src = tile_first[:, None] + row
    valid = src < tile_limit[:, None]
    row_token = jnp.where(valid, order[jnp.clip(src, 0, N_ASSIGN - 1)] // TOP_K, row % N_TOK).reshape(-1)
    experts = jnp.arange(N_EXPERTS, dtype=jnp.int32)
    later = lax.cummin(jnp.where(tiles_per > 0, experts, N_EXPERTS), reverse=True)
    next_expert = jnp.concatenate([later[1:], jnp.full((1,), N_EXPERTS, jnp.int32)])
    next_expert = jnp.where(next_expert >= N_EXPERTS, -1, next_expert)
    tile_next = jnp.sum(owner_hot * next_expert[None, :], axis=1)
    n_tiles = n_tiles.reshape(1).astype(jnp.int32)
    tables = (owner.astype(jnp.int32), tile_next.astype(jnp.int32), n_tiles)
    return pos, row_token.astype(jnp.int32), tables


def _moe_block(x, ids, gates, w1, w3, w2, layer, ln_g, ln_b, final):
    pos, row_token, tables = _routing_tables(ids)
    xs = _gather_rows(x, row_token, tables[2], BF16, "moe_gather")
    ys = _moe_ffn(xs, w1, w3, w2, layer, tables)
    return _combine_norm(x, ys, pos, gates, ln_g, ln_b, final)


def _conv_in_kernel(x_ref, wb_ref, wc_ref, wh_ref, cw_ref, s0_ref, s1_ref, gbz_ref, u_ref,
                    wbb_ref, wcb_ref, whb_ref, ubuf_ref):
    i = pl.program_id(1)
    tm = x_ref.shape[0]
    pad = SUBLANES

    @pl.when(i == 0)
    def _():
        _cast_weight(wb_ref, wbb_ref)
        _cast_weight(wc_ref, wcb_ref)
        _cast_weight(wh_ref, whb_ref)
        ubuf_ref[0:pad, :] = jnp.zeros((pad, ubuf_ref.shape[1]), F32)

    w0 = cw_ref[0:1, :]
    w1 = cw_ref[1:2, :]
    w2 = cw_ref[2:3, :]
    sub = ROW_TILE
    for s in range(tm // sub):
        lo, hi = s * sub, (s + 1) * sub
        x = x_ref[lo:hi, :]
        gb = _dot_nn(x, wbb_ref[...])
        u = _dot_nn(x, wcb_ref[...]) * _dot_nn(x, whb_ref[...])
        u_ref[lo:hi, :] = u
        ubuf_ref[pad + lo:pad + hi, :] = u
        t = (i * tm + lo + lax.broadcasted_iota(jnp.int32, (sub, 1), 0)) & (SEQ - 1)
        u_m1 = jnp.where(t >= 1, ubuf_ref[pad - 1 + lo:pad - 1 + hi, :], 0.0)
        u_m2 = jnp.where(t >= 2, ubuf_ref[pad - 2 + lo:pad - 2 + hi, :], 0.0)
        z = u_m2 * w0 + u_m1 * w1 + u * w2
        gbz_ref[lo:hi, :] = (gb * z).astype(BF16)
    ubuf_ref[0:pad, :] = ubuf_ref[tm:tm + pad, :]

    @pl.when(i == pl.num_programs(1) - 1)
    def _():
        first = sub - N_SAMPLE
        z_s = s0_ref[...] * w0 + s1_ref[...] * w1 + u[first:sub, :] * w2
        gbz_ref[tm - N_SAMPLE:tm, :] = (gb[first:sub, :] * z_s).astype(BF16)


def _conv_in(xb, w_in, conv_w, state):
    tn = 512
    nb = D_MODEL // tn
    s0 = state[:, 0, :]
    s1 = state[:, 1, :]
    return pl.pallas_call(
        _conv_in_kernel,
        grid=(nb, N_TOK // TOKEN_TILE),
        in_specs=[pl.BlockSpec((TOKEN_TILE, D_MODEL), lambda j, i: (i, 0)),
                  pl.BlockSpec((D_MODEL, tn), lambda j, i: (0, j)),
                  pl.BlockSpec((D_MODEL, tn), lambda j, i: (0, nb + j)),
                  pl.BlockSpec((D_MODEL, tn), lambda j, i: (0, 2 * nb + j)),
                  pl.BlockSpec((CONV_W, tn), lambda j, i: (0, j)),
                  pl.BlockSpec((N_SAMPLE, tn), lambda j, i: (0, j)),
                  pl.BlockSpec((N_SAMPLE, tn), lambda j, i: (0, j))],
        out_specs=[pl.BlockSpec((TOKEN_TILE, tn), lambda j, i: (i, j)),
                   pl.BlockSpec((TOKEN_TILE, tn), lambda j, i: (i, j))],
        out_shape=[jax.ShapeDtypeStruct((N_TOK, D_MODEL), BF16),
                   jax.ShapeDtypeStruct((N_TOK, D_MODEL), F32)],
        scratch_shapes=[pltpu.VMEM((D_MODEL, tn), BF16), pltpu.VMEM((D_MODEL, tn), BF16),
                        pltpu.VMEM((D_MODEL, tn), BF16),
                        pltpu.VMEM((TOKEN_TILE + 2 * SUBLANES, tn), F32)],
        compiler_params=_params("arbitrary", "arbitrary"),
        name="conv_in",
    )(xb, w_in, w_in, w_in, conv_w, s0, s1)


def kernel(x_prompt, x_sample, state_gla, state_conv, router_w, router_b, gla_w_in, gla_w_gate, gla_b_gate,
           gla_norm_g, gla_w_out, conv_w_in, conv_w, conv_w_out, ln_mix_g, ln_mix_b, ln_ffn_g, ln_ffn_b,
           moe_w1, moe_w3, moe_w2):
    xp = x_prompt.reshape(N_PROMPT, D_MODEL)
    xs = x_sample.reshape(N_SAMPLE, D_MODEL)
    router = _router_operands(router_w, router_b)

    w_in_t = gla_w_in[0].T
    qkvr, gl = _gla_in_proj(xp, xs, w_in_t)
    g = _gla_gate(gl, gla_w_gate[0], gla_b_gate[0])
    norm_g = gla_norm_g[0].reshape(1, DV_TOTAL)
    og, s_prompt = _gla_prompt(qkvr, g, norm_g)
    og, s_sample = _gla_sample(qkvr, g, norm_g, state_gla[0], og)
    x1, ids, gates = _proj_norm_route(og, (xp, xs), gla_w_out[0], ln_mix_g[0], ln_mix_b[0], router)
    x2, x2b = _moe_block(x1, ids, gates, moe_w1, moe_w3, moe_w2, 0, ln_ffn_g[0], ln_ffn_b[0], False)

    gbz, u = _conv_in(x2b, conv_w_in[0], conv_w[0], state_conv[0])
    x3, ids, gates = _proj_norm_route(gbz, x2, conv_w_out[0], ln_mix_g[1], ln_mix_b[1], router)
    y_p, y_s = _moe_block(x3, ids, gates, moe_w1, moe_w3, moe_w2, 1, ln_ffn_g[1], ln_ffn_b[1], True)

    y_prompt = y_p.reshape(N_PROMPT_SEQ, SEQ, D_MODEL)
    y_sample = y_s.reshape(N_SAMPLE, 1, D_MODEL)
    conv_prompt = jnp.stack([u[(b + 1) * SEQ - (CONV_W - 1):(b + 1) * SEQ] for b in range(N_PROMPT_SEQ)])
    conv_sample = jnp.concatenate([state_conv[0][:, 1:, :], u[N_PROMPT:][:, None, :]], axis=1)
    return (y_prompt, y_sample, s_prompt[None], conv_prompt[None], s_sample[None], conv_sample[None])
```

```python
import functools

import jax
import jax.numpy as jnp
from jax import lax
from jax.experimental import pallas as pl
from jax.experimental.pallas import tpu as pltpu

F32 = jnp.float32
BF16 = jnp.bfloat16

D_MODEL = 2048
N_PROMPT_SEQ = 4
SEQ = 2048
N_PROMPT = N_PROMPT_SEQ * SEQ
N_SAMPLE = 128
N_TOK = N_PROMPT + N_SAMPLE
DEPTH = 2

GLA_HEADS = 4
DK_TOTAL = D_MODEL // 2
DV_TOTAL = D_MODEL
DK_HEAD = DK_TOTAL // GLA_HEADS
DV_HEAD = DV_TOTAL // GLA_HEADS
GATE_RANK = 16
GATE_NORMALIZER = 16.0
GLA_QKVR_WIDTH = 2 * DK_TOTAL + 2 * DV_TOTAL
CONV_W = 3
N_EXPERTS = 16
N_GROUPS = 4
EXPERTS_PER_GROUP = N_EXPERTS // N_GROUPS
TOP_K = 2
D_FF = D_MODEL // 2
ALPHA = (2.0 * DEPTH) ** 0.25
LN_EPS = 1e-5
RMS_EPS = 1e-6

LANES = 128
SUBLANES = 8
VMEM_LIMIT_BYTES = 56 * 1024 * 1024

TOKEN_TILE = 640
ROW_TILE = 320
GLA_CHUNK = 64
GLA_STEP_CHUNKS = 8
GLA_SAMPLE_BATCH = 16
MOE_TILE = 256
N_ASSIGN = N_TOK * TOP_K
GATHER_TILES = 2
MOE_MAX_TILES = -(-(N_ASSIGN // MOE_TILE + N_EXPERTS) // GATHER_TILES) * GATHER_TILES
N_ASSIGN_PAD = MOE_MAX_TILES * MOE_TILE


def _params(*semantics):
    return pltpu.CompilerParams(dimension_semantics=semantics, vmem_limit_bytes=VMEM_LIMIT_BYTES)


def _split3(x):
    hi = x.astype(BF16)
    r1 = x - hi.astype(F32)
    mid = r1.astype(BF16)
    lo = (r1 - mid.astype(F32)).astype(BF16)
    return hi, mid, lo


def _cast_weight(src_ref, dst_ref):
    rows = 64
    n = src_ref.shape[0] // rows

    def body(c, carry):
        r = pl.multiple_of(c * rows, rows)
        dst_ref[pl.ds(r, rows), :] = src_ref[pl.ds(r, rows), :].astype(BF16)
        return carry
    lax.fori_loop(0, n, body, 0)


def _dot_nn(a, b):
    return jnp.dot(a, b, preferred_element_type=F32)


def _dot_tn(a, b):
    return lax.dot_general(a, b, (((0,), (0,)), ((), ())), preferred_element_type=F32)


def _dot_nt(a, b):
    return lax.dot_general(a, b, (((1,), (1,)), ((), ())), preferred_element_type=F32)


def _col_bcast(rows, n):
    ones = jnp.ones((rows.shape[0], LANES), BF16)
    hi, mid, lo = _split3(rows)
    col = _dot_tn(hi, ones) + _dot_tn(mid, ones) + _dot_tn(lo, ones)
    return jnp.concatenate([col] * (n // LANES), axis=1)


def _silu(r):
    return r / (1.0 + jnp.exp(-r))


def _layer_norm(h, g, b):
    mu = jnp.mean(h, axis=-1, keepdims=True)
    d = h - mu
    var = jnp.mean(d * d, axis=-1, keepdims=True)
    return d * lax.rsqrt(var + LN_EPS) * g + b


X_TILE_BLOCKS = TOKEN_TILE // N_SAMPLE


def _x_tile_specs(tile_index):
    last = N_PROMPT // N_SAMPLE - 1

    def prompt_spec(k):
        return pl.BlockSpec((N_SAMPLE, D_MODEL),
                            lambda *g: (jnp.minimum(tile_index(*g) * X_TILE_BLOCKS + k, last), 0))
    return ([prompt_spec(k) for k in range(X_TILE_BLOCKS)]
            + [pl.BlockSpec((N_SAMPLE, D_MODEL), lambda *g: (0, 0))])


def _x_tile(i, x_refs):
    *prompt_refs, sample_ref = x_refs
    tail_is_sample = (i + 1) * TOKEN_TILE > N_PROMPT
    blocks = [r[...] for r in prompt_refs[:-1]]
    blocks.append(jnp.where(tail_is_sample, sample_ref[...], prompt_refs[-1][...]))
    return jnp.concatenate(blocks, axis=0)


def _gla_in_proj_kernel(*refs):
    *x_refs, wt_ref, wgl_ref, o_ref, gl_ref, wb_ref = refs
    j = pl.program_id(0)
    i = pl.program_id(1)

    @pl.when(i == 0)
    def _():
        _cast_weight(wt_ref, wb_ref)

    x = _x_tile(i, x_refs).astype(BF16)
    for s in range(TOKEN_TILE // ROW_TILE):
        rows = slice(s * ROW_TILE, (s + 1) * ROW_TILE)
        o_ref[rows, :] = _dot_nt(x[rows, :], wb_ref[...]).astype(o_ref.dtype)

    @pl.when(j == pl.num_programs(0) - 1)
    def _():
        wgl = jnp.concatenate([wgl_ref[...].astype(BF16), jnp.zeros((LANES - GATE_RANK, D_MODEL), BF16)],
                              axis=0)
        gl_ref[...] = _dot_nt(x, wgl)

    @pl.when(j < pl.num_programs(0) - 1)
    def _():
        gl_ref[...] = jnp.zeros_like(gl_ref)


def _gla_in_proj(x_prompt, x_sample, w_t):
    tn = 1024
    n_pass = GLA_QKVR_WIDTH // tn
    n_tiles = N_TOK // TOKEN_TILE
    return pl.pallas_call(
        _gla_in_proj_kernel,
        grid=(n_pass, n_tiles),
        in_specs=_x_tile_specs(lambda j, i: i) + [
            pl.BlockSpec((tn, D_MODEL), lambda j, i: (j, 0)),
            pl.BlockSpec((GATE_RANK, D_MODEL), lambda j, i: (GLA_QKVR_WIDTH // GATE_RANK, 0))],
        out_specs=[pl.BlockSpec((TOKEN_TILE, tn), lambda j, i: (i, j)),
                   pl.BlockSpec((TOKEN_TILE, LANES), lambda j, i: (jnp.where(j == n_pass - 1, i, n_tiles), 0))],
        out_shape=[jax.ShapeDtypeStruct((N_TOK, GLA_QKVR_WIDTH), F32),
                   jax.ShapeDtypeStruct((N_TOK + TOKEN_TILE, LANES), F32)],
        scratch_shapes=[pltpu.VMEM((tn, D_MODEL), BF16)],
        compiler_params=_params("arbitrary", "arbitrary"),
        name="gla_in_proj",
    )(*([x_prompt] * X_TILE_BLOCKS), x_sample, w_t, w_t)


def _gate_kernel(gl_ref, wg_ref, bg_ref, g_ref):
    z = _dot_nn(gl_ref[...].astype(BF16), wg_ref[...]) + bg_ref[...]
    log_sig = jnp.minimum(z, 0.0) - jnp.log1p(jnp.exp(-jnp.abs(z)))
    g_ref[...] = log_sig * (1.0 / GATE_NORMALIZER)


def _gla_gate(gl, w_gate, b_gate):
    wg = jnp.pad(w_gate, ((0, LANES - GATE_RANK), (0, 0))).astype(BF16)
    return pl.pallas_call(
        _gate_kernel,
        grid=(N_TOK // TOKEN_TILE,),
        in_specs=[pl.BlockSpec((TOKEN_TILE, LANES), lambda i: (i, 0)),
                  pl.BlockSpec((LANES, DK_TOTAL), lambda i: (0, 0)),
                  pl.BlockSpec((1, DK_TOTAL), lambda i: (0, 0))],
        out_specs=pl.BlockSpec((TOKEN_TILE, DK_TOTAL), lambda i: (i, 0)),
        out_shape=jax.ShapeDtypeStruct((N_TOK, DK_TOTAL), F32),
        compiler_params=_params("arbitrary"),
        name="gla_gate",
    )(gl, wg, b_gate.reshape(1, DK_TOTAL))


def _rms_gate(o, norm_g, r):
    o = o * lax.rsqrt(jnp.mean(o * o, axis=-1, keepdims=True) + RMS_EPS)
    return (o * norm_g) * _silu(r)


def _gla_prompt_kernel(q_ref, k_ref, v_ref, r_ref, g_ref, ng_ref, og_in_ref, og_ref, s_out_ref, s_ref):
    del og_in_ref
    c = pl.program_id(1)
    cc = GLA_CHUNK

    @pl.when(c == 0)
    def _():
        s_ref[...] = jnp.zeros_like(s_ref)

    for ci in range(q_ref.shape[0] // cc):
        last_chunk = ci == q_ref.shape[0] // cc - 1
        _gla_chunk(slice(ci * cc, (ci + 1) * cc), last_chunk, c, q_ref, k_ref, v_ref, r_ref, g_ref,
                   ng_ref, og_ref, s_out_ref, s_ref)


def _gla_chunk(rows, last_chunk, c, q_ref, k_ref, v_ref, r_ref, g_ref, ng_ref, og_ref, s_out_ref, s_ref):
    cc = GLA_CHUNK
    row = lax.broadcasted_iota(jnp.int32, (cc, cc), 0)
    col = lax.broadcasted_iota(jnp.int32, (cc, cc), 1)
    causal = col <= row
    tri = jnp.where(causal, 1.0, 0.0).astype(BF16)
    b = _dot_nn(jnp.concatenate([tri] * 3, axis=1),
                jnp.concatenate(_split3(g_ref[rows, :]), axis=0))
    b_last = b[cc - 1:cc, :]
    b_mid = b[cc // 2 - 1:cc // 2, :]

    q = q_ref[rows, :]
    k = k_ref[rows, :] * (DK_HEAD ** -0.5)
    q_in = (q * jnp.exp(b)).astype(BF16)
    qh, qm, _ = _split3(q * jnp.exp(b - b_mid))
    kh, km, _ = _split3(k * jnp.exp(b_mid - b))
    k_d = (k * jnp.exp(b_last - b)).astype(BF16)

    pad = 2 * SUBLANES
    sub = lax.broadcasted_iota(jnp.int32, (pad, DK_TOTAL), 0)
    decay_rows = jnp.where(sub == 0, jnp.broadcast_to(jnp.exp(b_last), (pad, DK_TOTAL)), 0.0)
    decay_col = _dot_tn(jnp.concatenate(_split3(decay_rows), axis=0),
                        jnp.ones((3 * pad, LANES), BF16))

    for h in range(GLA_HEADS):
        dk = slice(h * DK_HEAD, (h + 1) * DK_HEAD)
        dv = slice(h * DV_HEAD, (h + 1) * DV_HEAD)
        q3 = jnp.concatenate([qh[:, dk], qh[:, dk], qm[:, dk]], axis=1)
        k3 = jnp.concatenate([kh[:, dk], km[:, dk], kh[:, dk]], axis=1)
        scores = jnp.where(causal, _dot_nt(q3, k3), 0.0).astype(BF16)
        v = v_ref[rows, dv].astype(BF16)
        s_old = s_ref[h]
        o = _dot_nn(jnp.concatenate([q_in[:, dk], scores], axis=1),
                    jnp.concatenate([s_old.astype(BF16), v], axis=0))
        decay = jnp.concatenate([decay_col[dk, :]] * (DV_HEAD // LANES), axis=1)
        s_new = decay * s_old + _dot_tn(k_d[:, dk], v)
        s_ref[h] = s_new

        if last_chunk:
            @pl.when(c == pl.num_programs(1) - 1)
            def _():
                s_out_ref[0, h] = s_new

        og_ref[rows, dv] = _rms_gate(o, ng_ref[:, dv], r_ref[rows, dv]).astype(BF16)


def _gla_prompt(qkvr, g, norm_g):
    step = GLA_STEP_CHUNKS * GLA_CHUNK
    nc = SEQ // step
    row = lambda b, c: b * nc + c
    return pl.pallas_call(
        _gla_prompt_kernel,
        grid=(N_PROMPT_SEQ, nc),
        in_specs=[pl.BlockSpec((step, DK_TOTAL), lambda b, c: (row(b, c), 0)),
                  pl.BlockSpec((step, DK_TOTAL), lambda b, c: (row(b, c), 1)),
                  pl.BlockSpec((step, DV_TOTAL), lambda b, c: (row(b, c), 1)),
                  pl.BlockSpec((step, DV_TOTAL), lambda b, c: (row(b, c), 2)),
                  pl.BlockSpec((step, DK_TOTAL), lambda b, c: (row(b, c), 0)),
                  pl.BlockSpec((1, DV_TOTAL), lambda b, c: (0, 0)),
                  pl.BlockSpec(memory_space=pl.ANY)],
        out_specs=[pl.BlockSpec((step, DV_TOTAL), lambda b, c: (row(b, c), 0)),
                   pl.BlockSpec((1, GLA_HEADS, DK_HEAD, DV_HEAD), lambda b, c: (b, 0, 0, 0))],
        out_shape=[jax.ShapeDtypeStruct((N_TOK, DV_TOTAL), BF16),
                   jax.ShapeDtypeStruct((N_PROMPT_SEQ, GLA_HEADS, DK_HEAD, DV_HEAD), F32)],
        scratch_shapes=[pltpu.VMEM((GLA_HEADS, DK_HEAD, DV_HEAD), F32)],
        input_output_aliases={6: 0},
        compiler_params=_params("arbitrary", "arbitrary"),
        name="gla_prompt",
    )(qkvr, qkvr, qkvr, qkvr, g, norm_g, jnp.zeros((N_TOK, DV_TOTAL), BF16))


def _gla_sample_kernel(q_ref, k_ref, v_ref, r_ref, g_ref, ng_ref, s_ref, og_in_ref, og_ref, s_out_ref,
                       o_scr):
    del og_in_ref
    bb = q_ref.shape[0]
    q = q_ref[...]
    k = k_ref[...] * (DK_HEAD ** -0.5)
    v = v_ref[...]
    eg = jnp.exp(g_ref[...])
    qe = q * eg
    qk = jnp.sum(q * k, axis=-1, keepdims=True)
    sub = lax.broadcasted_iota(jnp.int32, (bb, DK_HEAD), 0)
    for bi in range(bb):
        sel = sub == bi
        s_old = s_ref[bi, 0]
        decay = _col_bcast(jnp.where(sel, eg, 0.0), DV_HEAD)
        k_col = _col_bcast(jnp.where(sel, k, 0.0), DV_HEAD)
        qe_col = _col_bcast(jnp.where(sel, qe, 0.0), DV_HEAD)
        s_out_ref[bi, 0] = decay * s_old + k_col * v[bi:bi + 1, :]
        o_scr[bi:bi + 1, :] = jnp.sum(qe_col * s_old, axis=0, keepdims=True)
    o = qk * v + o_scr[...]
    og_ref[...] = _rms_gate(o, ng_ref[...], r_ref[...]).astype(BF16)


def _gla_sample(qkvr, g, norm_g, state, og):
    bb = GLA_SAMPLE_BATCH
    r0 = N_PROMPT // bb
    hk = DK_TOTAL // DK_HEAD
    hv = 2 * DK_TOTAL // DV_HEAD
    return pl.pallas_call(
        _gla_sample_kernel,
        grid=(N_SAMPLE // bb, GLA_HEADS),
        in_specs=[pl.BlockSpec((bb, DK_HEAD), lambda i, h: (r0 + i, h)),
                  pl.BlockSpec((bb, DK_HEAD), lambda i, h: (r0 + i, hk + h)),
                  pl.BlockSpec((bb, DV_HEAD), lambda i, h: (r0 + i, hv + h)),
                  pl.BlockSpec((bb, DV_HEAD), lambda i, h: (r0 + i, hv + GLA_HEADS + h)),
                  pl.BlockSpec((bb, DK_HEAD), lambda i, h: (r0 + i, h)),
                  pl.BlockSpec((1, DV_HEAD), lambda i, h: (0, h)),
                  pl.BlockSpec((bb, 1, DK_HEAD, DV_HEAD), lambda i, h: (i, h, 0, 0)),
                  pl.BlockSpec(memory_space=pl.ANY)],
        out_specs=[pl.BlockSpec((bb, DV_HEAD), lambda i, h: (r0 + i, h)),
                   pl.BlockSpec((bb, 1, DK_HEAD, DV_HEAD), lambda i, h: (i, h, 0, 0))],
        out_shape=[jax.ShapeDtypeStruct((N_TOK, DV_TOTAL), BF16),
                   jax.ShapeDtypeStruct((N_SAMPLE, GLA_HEADS, DK_HEAD, DV_HEAD), F32)],
        scratch_shapes=[pltpu.VMEM((bb, DV_HEAD), F32)],
        input_output_aliases={7: 0},
        compiler_params=_params("arbitrary", "arbitrary"),
        name="gla_sample",
    )(qkvr, qkvr, qkvr, qkvr, g, norm_g, state, og)


def _top2_of4(p):
    ranks = []
    for j in range(4):
        rk = jnp.zeros(p[j].shape, jnp.int32)
        for i in range(4):
            if i == j:
                continue
            beats = (p[i] >= p[j]) if i < j else (p[i] > p[j])
            rk = rk + jnp.where(beats, 1, 0)
        ranks.append(rk)

    def pick(rank):
        val = jnp.zeros(p[0].shape, F32)
        idx = jnp.zeros(p[0].shape, jnp.int32)
        for j in range(4):
            hit = ranks[j] == rank
            val = jnp.where(hit, p[j], val)
            idx = jnp.where(hit, j, idx)
        return val, idx

    v1, i1 = pick(0)
    v2, i2 = pick(1)
    return v1, i1, v2, i2


def _route(x, rw_ref, rb_ref):
    logits = _dot_nn(x.astype(BF16), rw_ref[...])
    n = x.shape[0]
    n_pad = -n % LANES
    if n_pad:
        logits = jnp.concatenate([logits, jnp.zeros((n_pad, LANES), F32)], axis=0)
    lt = logits.T[0:N_EXPERTS, 0:n] + rb_ref[...]
    e = jnp.exp(lt - jnp.max(lt, axis=0, keepdims=True))
    probs = e / jnp.sum(e, axis=0, keepdims=True)
    best = None
    for grp in range(N_GROUPS):
        rows = [probs[grp * EXPERTS_PER_GROUP + j:grp * EXPERTS_PER_GROUP + j + 1, :]
                for j in range(EXPERTS_PER_GROUP)]
        v1, i1, v2, i2 = _top2_of4(rows)
        score = v1 + v2
        cand = (score, v1, i1 + grp * EXPERTS_PER_GROUP, v2, i2 + grp * EXPERTS_PER_GROUP)
        if best is None:
            best = cand
        else:
            better = score > best[0]
            best = tuple(jnp.where(better, n, o) for n, o in zip(cand, best))
    _, v1, e1, v2, e2 = best
    denom = v1 + v2
    return (e1, e2), (v1 / denom, v2 / denom)


def _proj_norm_route_kernel(a_ref, *refs):
    *x_refs, w_ref, lg_ref, lb_ref, rw_ref, rb_ref, xo_ref, id_ref, gt_ref = refs
    x = x_refs[0][...] if len(x_refs) == 1 else _x_tile(pl.program_id(0), x_refs)
    sub = id_ref.shape[2]
    for s in range(id_ref.shape[0]):
        rows = slice(s * sub, (s + 1) * sub)
        y = _dot_nn(a_ref[rows, :], w_ref[...])
        xn = _layer_norm(ALPHA * x[rows, :] + y, lg_ref[...], lb_ref[...])
        xo_ref[rows, :] = xn
        (e1, e2), (g1, g2) = _route(xn, rw_ref, rb_ref)
        id_ref[s, 0:1, :] = e1
        id_ref[s, 1:2, :] = e2
        gt_ref[s, 0:1, :] = g1
        gt_ref[s, 1:2, :] = g2


def _router_operands(router_w, router_b):
    rw = jnp.pad(router_w, ((0, 0), (0, LANES - N_EXPERTS))).astype(BF16)
    return rw, router_b.reshape(N_EXPERTS, 1)


def _proj_norm_route(a, x, w_out, ln_g, ln_b, router):
    nt = N_TOK // ROW_TILE
    n_sub = TOKEN_TILE // ROW_TILE
    rw, rb = router
    const = lambda i: (0, 0)
    if isinstance(x, tuple):
        x_specs = _x_tile_specs(lambda i: i)
        x_args = [x[0]] * X_TILE_BLOCKS + [x[1]]
    else:
        x_specs = [pl.BlockSpec((TOKEN_TILE, D_MODEL), lambda i: (i, 0))]
        x_args = [x]
    x1, ids, gates = pl.pallas_call(
        _proj_norm_route_kernel,
        grid=(N_TOK // TOKEN_TILE,),
        in_specs=[pl.BlockSpec((TOKEN_TILE, D_MODEL), lambda i: (i, 0))] + x_specs + [
            pl.BlockSpec((D_MODEL, D_MODEL), const, pipeline_mode=pl.Buffered(1)),
            pl.BlockSpec((1, D_MODEL), const),
            pl.BlockSpec((1, D_MODEL), const),
            pl.BlockSpec((D_MODEL, LANES), const),
            pl.BlockSpec((N_EXPERTS, 1), const)],
        out_specs=[pl.BlockSpec((TOKEN_TILE, D_MODEL), lambda i: (i, 0)),
                   pl.BlockSpec((n_sub, TOP_K, ROW_TILE), lambda i: (i, 0, 0)),
                   pl.BlockSpec((n_sub, TOP_K, ROW_TILE), lambda i: (i, 0, 0))],
        out_shape=[jax.ShapeDtypeStruct((N_TOK, D_MODEL), F32),
                   jax.ShapeDtypeStruct((nt, TOP_K, ROW_TILE), jnp.int32),
                   jax.ShapeDtypeStruct((nt, TOP_K, ROW_TILE), F32)],
        compiler_params=_params("arbitrary"),
        name="proj_norm_route",
    )(a, *x_args, w_out.astype(BF16), ln_g.reshape(1, D_MODEL), ln_b.reshape(1, D_MODEL), rw, rb)
    ids = ids.transpose(0, 2, 1).reshape(N_TOK, TOP_K)
    gates = gates.transpose(0, 2, 1).reshape(N_TOK, TOP_K)
    return x1, ids, gates


def _stream_expert_weights(layer, tables, w_refs, stage_refs, wb_refs, sem_ref):
    te_ref, nx_ref, nt_ref = tables
    i = pl.program_id(0)

    def copy(expert, j):
        return pltpu.make_async_copy(w_refs[j].at[layer, expert], stage_refs[j], sem_ref.at[j])

    @pl.when(i == 0)
    def _():
        for j in range(len(w_refs)):
            copy(te_ref[0], j).start()

    first = jnp.logical_or(i == 0, te_ref[i] != te_ref[jnp.maximum(i - 1, 0)])

    @pl.when(jnp.logical_and(first, i < nt_ref[0]))
    def _():
        for j, (stage, wb) in enumerate(zip(stage_refs, wb_refs)):
            copy(te_ref[i], j).wait()
            _cast_weight(stage, wb)

            @pl.when(nx_ref[i] >= 0)
            def _():
                copy(nx_ref[i], j).start()


def _issue_row_copies(idx_ref, base, src_ref, dst_ref, sem, n_rows, stride=1):
    def body(g, carry):
        r0 = pl.multiple_of(g * SUBLANES, SUBLANES)
        for k in range(SUBLANES):
            row = idx_ref[base + stride * (r0 + k)]
            pltpu.make_async_copy(src_ref.at[pl.ds(row, 1)], dst_ref.at[pl.ds(r0 + k, 1)],
                                  sem).start(priority=k % 2)
        return carry
    lax.fori_loop(0, n_rows // SUBLANES, body, 0)


def _gather_kernel(idx_ref, nt_ref, src_ref, out_ref, buf_ref, sem_ref):
    i = pl.program_id(0)
    tg = out_ref.shape[0]
    n = (nt_ref[0] * MOE_TILE + tg - 1) // tg

    def start_tile(tile, slot):
        _issue_row_copies(idx_ref, tile * tg, src_ref, buf_ref.at[slot], sem_ref.at[slot], tg)

    @pl.when(i == 0)
    def _():
        start_tile(0, 0)

    @pl.when(i + 1 < n)
    def _():
        start_tile(i + 1, (i + 1) % 2)

    @pl.when(i < n)
    def _():
        slot = i % 2
        pltpu.make_async_copy(src_ref.at[pl.ds(0, tg)], buf_ref.at[slot], sem_ref.at[slot]).wait()
        out_ref[...] = buf_ref[slot].astype(out_ref.dtype)

    @pl.when(i >= n)
    def _():
        out_ref[...] = jnp.zeros_like(out_ref)


def _gather_rows(src, idx, n_tiles, out_dtype, name):
    m = idx.shape[0]
    width = src.shape[1]
    tg = GATHER_TILES * MOE_TILE
    return pl.pallas_call(
        _gather_kernel,
        grid_spec=pltpu.PrefetchScalarGridSpec(
            num_scalar_prefetch=2,
            grid=(m // tg,),
            in_specs=[pl.BlockSpec(memory_space=pl.ANY)],
            out_specs=pl.BlockSpec((tg, width), lambda i, idx, nt: (i, 0)),
            scratch_shapes=[pltpu.VMEM((2, tg, width), src.dtype),
                            pltpu.SemaphoreType.DMA((2,))]),
        out_shape=jax.ShapeDtypeStruct((m, width), out_dtype),
        compiler_params=_params("arbitrary"),
        name=name,
    )(idx, n_tiles, src)


def _moe_ffn_kernel(layer, te_ref, nx_ref, nt_ref, xs_ref, w1_ref, w3_ref, w2_ref, ys_ref,
                    stage13_ref, stage2_ref, w1b_ref, w3b_ref, w2b_ref, sem_ref):
    _stream_expert_weights(layer, (te_ref, nx_ref, nt_ref), (w1_ref, w3_ref, w2_ref),
                           (stage13_ref.at[0], stage13_ref.at[1], stage2_ref),
                           (w1b_ref, w3b_ref, w2b_ref), sem_ref)
    i = pl.program_id(0)

    @pl.when(i < nt_ref[0])
    def _():
        xs = xs_ref[...]
        h1 = _dot_nn(xs, w1b_ref[...])
        h3 = _dot_nn(xs, w3b_ref[...])
        hid = (_silu(h1) * h3).astype(BF16)
        ys_ref[...] = _dot_nn(hid, w2b_ref[...])

    @pl.when(i >= nt_ref[0])
    def _():
        ys_ref[...] = jnp.zeros_like(ys_ref)


def _moe_ffn(xs, w1, w3, w2, layer, tables):
    last = lambda i, te, nx, nt: (jnp.minimum(i, nt[0] - 1), 0)
    return pl.pallas_call(
        functools.partial(_moe_ffn_kernel, layer),
        grid_spec=pltpu.PrefetchScalarGridSpec(
            num_scalar_prefetch=3,
            grid=(MOE_MAX_TILES,),
            in_specs=[pl.BlockSpec((MOE_TILE, D_MODEL), last),
                      pl.BlockSpec(memory_space=pl.ANY),
                      pl.BlockSpec(memory_space=pl.ANY),
                      pl.BlockSpec(memory_space=pl.ANY)],
            out_specs=pl.BlockSpec((MOE_TILE, D_MODEL), lambda i, te, nx, nt: (i, 0)),
            scratch_shapes=[pltpu.VMEM((2, D_MODEL, D_FF), F32),
                            pltpu.VMEM((D_FF, D_MODEL), F32),
                            pltpu.VMEM((D_MODEL, D_FF), BF16), pltpu.VMEM((D_MODEL, D_FF), BF16),
                            pltpu.VMEM((D_FF, D_MODEL), BF16),
                            pltpu.SemaphoreType.DMA((3,))]),
        out_shape=jax.ShapeDtypeStruct((N_ASSIGN_PAD, D_MODEL), F32),
        compiler_params=_params("arbitrary"),
        name="moe_ffn",
    )(*tables, xs, w1, w3, w2)


def _combine_norm_kernel(final, pos_ref, x_ref, ys_ref, gt_ref, lg_ref, lb_ref, o0_ref, o1_ref,
                         buf_ref, sem_ref):
    i = pl.program_id(0)
    n = pl.num_programs(0)
    tm = x_ref.shape[0]

    def start_tile(tile, slot):
        for k in range(TOP_K):
            _issue_row_copies(pos_ref, tile * tm * TOP_K + k, ys_ref, buf_ref.at[slot, k],
                              sem_ref.at[slot], tm, stride=TOP_K)

    @pl.when(i == 0)
    def _():
        start_tile(0, 0)

    @pl.when(i + 1 < n)
    def _():
        start_tile(i + 1, (i + 1) % 2)

    slot = i % 2
    for k in range(TOP_K):
        pltpu.make_async_copy(ys_ref.at[pl.ds(0, tm)], buf_ref.at[slot, k], sem_ref.at[slot]).wait()
    gt = gt_ref[...]
    moe = buf_ref[slot, 0] * gt[:, 0:1] + buf_ref[slot, 1] * gt[:, 1:2]
    xn = _layer_norm(ALPHA * x_ref[...] + moe, lg_ref[...], lb_ref[...])
    if final:
        @pl.when(i < n - 1)
        def _():
            o0_ref[...] = xn

        @pl.when(i == n - 1)
        def _():
            o1_ref[...] = xn
    else:
        o0_ref[...] = xn
        o1_ref[...] = xn.astype(BF16)


def _combine_norm(x, ys, pos, gates, ln_g, ln_b, final):
    tm = N_SAMPLE if final else ROW_TILE
    const = lambda i, pos: (0, 0)
    tile = lambda i, pos: (i, 0)
    if final:
        n_prompt_tiles = N_PROMPT // tm
        out_specs = [pl.BlockSpec((tm, D_MODEL), lambda i, pos: (jnp.minimum(i, n_prompt_tiles - 1), 0)),
                     pl.BlockSpec((tm, D_MODEL), const)]
        out_shape = [jax.ShapeDtypeStruct((N_PROMPT, D_MODEL), F32),
                     jax.ShapeDtypeStruct((N_SAMPLE, D_MODEL), F32)]
    else:
        out_specs = [pl.BlockSpec((tm, D_MODEL), tile), pl.BlockSpec((tm, D_MODEL), tile)]
        out_shape = [jax.ShapeDtypeStruct((N_TOK, D_MODEL), F32),
                     jax.ShapeDtypeStruct((N_TOK, D_MODEL), BF16)]
    return pl.pallas_call(
        functools.partial(_combine_norm_kernel, final),
        grid_spec=pltpu.PrefetchScalarGridSpec(
            num_scalar_prefetch=1,
            grid=(N_TOK // tm,),
            in_specs=[pl.BlockSpec((tm, D_MODEL), tile),
                      pl.BlockSpec(memory_space=pl.ANY),
                      pl.BlockSpec((tm, TOP_K), tile),
                      pl.BlockSpec((1, D_MODEL), const),
                      pl.BlockSpec((1, D_MODEL), const)],
            out_specs=out_specs,
            scratch_shapes=[pltpu.VMEM((2, TOP_K, tm, D_MODEL), F32),
                            pltpu.SemaphoreType.DMA((2,))]),
        out_shape=out_shape,
        compiler_params=_params("arbitrary"),
        name="combine_norm",
    )(pos, x, ys, gates, ln_g.reshape(1, D_MODEL), ln_b.reshape(1, D_MODEL))


def _routing_tables(ids):
    e = ids.reshape(-1)
    onehot = (e[:, None] == jnp.arange(N_EXPERTS, dtype=jnp.int32)[None, :]).astype(jnp.int32)
    csum = jnp.cumsum(onehot, axis=0)
    rank = jnp.sum(onehot * csum, axis=1) - 1
    sizes = csum[-1]
    tiles_per = (sizes + MOE_TILE - 1) // MOE_TILE
    tile_end = jnp.cumsum(tiles_per)
    tile_start = tile_end - tiles_per
    n_tiles = tile_end[-1]
    pos = (jnp.sum(onehot * (tile_start * MOE_TILE)[None, :], axis=1) + rank).astype(jnp.int32)
    tile = jnp.arange(MOE_MAX_TILES, dtype=jnp.int32)
    owner = jnp.sum((tile_end[None, :] <= jnp.minimum(tile, n_tiles - 1)[:, None]).astype(jnp.int32), axis=1)
    order = jnp.argsort(e, stable=True).astype(jnp.int32)
    owner_hot = (owner[:, None] == jnp.arange(N_EXPERTS, dtype=jnp.int32)[None, :]).astype(jnp.int32)
    size_start = jnp.cumsum(sizes) - sizes
    tile_first = jnp.sum(owner_hot * (size_start - tile_start * MOE_TILE)[None, :], axis=1)
    tile_limit = jnp.sum(owner_hot * (size_start + sizes)[None, :], axis=1)
    row = jnp.arange(N_ASSIGN_PAD, dtype=jnp.int32).reshape(MOE_MAX_TILES, MOE_TILE)
    src = tile_first[:, None] + row
    valid = src < tile_limit[:, None]
    row_token = jnp.where(valid, order[jnp.clip(src, 0, N_ASSIGN - 1)] // TOP_K, row % N_TOK).reshape(-1)
    experts = jnp.arange(N_EXPERTS, dtype=jnp.int32)
    later = lax.cummin(jnp.where(tiles_per > 0, experts, N_EXPERTS), reverse=True)
    next_expert = jnp.concatenate([later[1:], jnp.full((1,), N_EXPERTS, jnp.int32)])
    next_expert = jnp.where(next_expert >= N_EXPERTS, -1, next_expert)
    tile_next = jnp.sum(owner_hot * next_expert[None, :], axis=1)
    n_tiles = n_tiles.reshape(1).astype(jnp.int32)
    tables = (owner.astype(jnp.int32), tile_next.astype(jnp.int32), n_tiles)
    return pos, row_token.astype(jnp.int32), tables


def _moe_block(x, ids, gates, w1, w3, w2, layer, ln_g, ln_b, final):
    pos, row_token, tables = _routing_tables(ids)
    xs = _gather_rows(x, row_token, tables[2], BF16, "moe_gather")
    ys = _moe_ffn(xs, w1, w3, w2, layer, tables)
    return _combine_norm(x, ys, pos, gates, ln_g, ln_b, final)


def _conv_in_kernel(x_ref, wb_ref, wc_ref, wh_ref, cw_ref, s0_ref, s1_ref, gbz_ref, u_ref,
                    wbb_ref, wcb_ref, whb_ref, ubuf_ref):
    i = pl.program_id(1)
    tm = x_ref.shape[0]
    pad = SUBLANES

    @pl.when(i == 0)
    def _():
        _cast_weight(wb_ref, wbb_ref)
        _cast_weight(wc_ref, wcb_ref)
        _cast_weight(wh_ref, whb_ref)
        ubuf_ref[0:pad, :] = jnp.zeros((pad, ubuf_ref.shape[1]), F32)

    w0 = cw_ref[0:1, :]
    w1 = cw_ref[1:2, :]
    w2 = cw_ref[2:3, :]
    sub = ROW_TILE
    for s in range(tm // sub):
        lo, hi = s * sub, (s + 1) * sub
        x = x_ref[lo:hi, :]
        gb = _dot_nn(x, wbb_ref[...])
        u = _dot_nn(x, wcb_ref[...]) * _dot_nn(x, whb_ref[...])
        u_ref[lo:hi, :] = u
        ubuf_ref[pad + lo:pad + hi, :] = u
        t = (i * tm + lo + lax.broadcasted_iota(jnp.int32, (sub, 1), 0)) & (SEQ - 1)
        u_m1 = jnp.where(t >= 1, ubuf_ref[pad - 1 + lo:pad - 1 + hi, :], 0.0)
        u_m2 = jnp.where(t >= 2, ubuf_ref[pad - 2 + lo:pad - 2 + hi, :], 0.0)
        z = u_m2 * w0 + u_m1 * w1 + u * w2
        gbz_ref[lo:hi, :] = (gb * z).astype(BF16)
    ubuf_ref[0:pad, :] = ubuf_ref[tm:tm + pad, :]

    @pl.when(i == pl.num_programs(1) - 1)
    def _():
        first = sub - N_SAMPLE
        z_s = s0_ref[...] * w0 + s1_ref[...] * w1 + u[first:sub, :] * w2
        gbz_ref[tm - N_SAMPLE:tm, :] = (gb[first:sub, :] * z_s).astype(BF16)


def _conv_in(xb, w_in, conv_w, state):
    tn = 512
    nb = D_MODEL // tn
    s0 = state[:, 0, :]
    s1 = state[:, 1, :]
    return pl.pallas_call(
        _conv_in_kernel,
        grid=(nb, N_TOK // TOKEN_TILE),
        in_specs=[pl.BlockSpec((TOKEN_TILE, D_MODEL), lambda j, i: (i, 0)),
                  pl.BlockSpec((D_MODEL, tn), lambda j, i: (0, j)),
                  pl.BlockSpec((D_MODEL, tn), lambda j, i: (0, nb + j)),
                  pl.BlockSpec((D_MODEL, tn), lambda j, i: (0, 2 * nb + j)),
                  pl.BlockSpec((CONV_W, tn), lambda j, i: (0, j)),
                  pl.BlockSpec((N_SAMPLE, tn), lambda j, i: (0, j)),
                  pl.BlockSpec((N_SAMPLE, tn), lambda j, i: (0, j))],
        out_specs=[pl.BlockSpec((TOKEN_TILE, tn), lambda j, i: (i, j)),
                   pl.BlockSpec((TOKEN_TILE, tn), lambda j, i: (i, j))],
        out_shape=[jax.ShapeDtypeStruct((N_TOK, D_MODEL), BF16),
                   jax.ShapeDtypeStruct((N_TOK, D_MODEL), F32)],
        scratch_shapes=[pltpu.VMEM((D_MODEL, tn), BF16), pltpu.VMEM((D_MODEL, tn), BF16),
                        pltpu.VMEM((D_MODEL, tn), BF16),
                        pltpu.VMEM((TOKEN_TILE + 2 * SUBLANES, tn), F32)],
        compiler_params=_params("arbitrary", "arbitrary"),
        name="conv_in",
    )(xb, w_in, w_in, w_in, conv_w, s0, s1)


def kernel(x_prompt, x_sample, state_gla, state_conv, router_w, router_b, gla_w_in, gla_w_gate, gla_b_gate,
           gla_norm_g, gla_w_out, conv_w_in, conv_w, conv_w_out, ln_mix_g, ln_mix_b, ln_ffn_g, ln_ffn_b,
           moe_w1, moe_w3, moe_w2):
    xp = x_prompt.reshape(N_PROMPT, D_MODEL)
    xs = x_sample.reshape(N_SAMPLE, D_MODEL)
    router = _router_operands(router_w, router_b)

    w_in_t = gla_w_in[0].T
    qkvr, gl = _gla_in_proj(xp, xs, w_in_t)
    g = _gla_gate(gl, gla_w_gate[0], gla_b_gate[0])
    norm_g = gla_norm_g[0].reshape(1, DV_TOTAL)
    og, s_prompt = _gla_prompt(qkvr, g, norm_g)
    og, s_sample = _gla_sample(qkvr, g, norm_g, state_gla[0], og)
    x1, ids, gates = _proj_norm_route(og, (xp, xs), gla_w_out[0], ln_mix_g[0], ln_mix_b[0], router)
    x2, x2b = _moe_block(x1, ids, gates, moe_w1, moe_w3, moe_w2, 0, ln_ffn_g[0], ln_ffn_b[0], False)

    gbz, u = _conv_in(x2b, conv_w_in[0], conv_w[0], state_conv[0])
    x3, ids, gates = _proj_norm_route(gbz, x2, conv_w_out[0], ln_mix_g[1], ln_mix_b[1], router)
    y_p, y_s = _moe_block(x3, ids, gates, moe_w1, moe_w3, moe_w2, 1, ln_ffn_g[1], ln_ffn_b[1], True)

    y_prompt = y_p.reshape(N_PROMPT_SEQ, SEQ, D_MODEL)
    y_sample = y_s.reshape(N_SAMPLE, 1, D_MODEL)
    conv_prompt = jnp.stack([u[(b + 1) * SEQ - (CONV_W - 1):(b + 1) * SEQ] for b in range(N_PROMPT_SEQ)])
    conv_sample = jnp.concatenate([state_conv[0][:, 1:, :], u[N_PROMPT:][:, None, :]], axis=1)
    return (y_prompt, y_sample, s_prompt[None], conv_prompt[None], s_sample[None], conv_sample[None])
```

```python
import functools

import jax
import jax.numpy as jnp
from jax import lax
from jax.experimental import pallas as pl
from jax.experimental.pallas import tpu as pltpu

F32 = jnp.float32
BF16 = jnp.bfloat16

D_MODEL = 2048
N_PROMPT_SEQ = 4
SEQ = 2048
N_PROMPT = N_PROMPT_SEQ * SEQ
N_SAMPLE = 128
N_TOK = N_PROMPT + N_SAMPLE
DEPTH = 2

GLA_HEADS = 4
DK_TOTAL = D_MODEL // 2
DV_TOTAL = D_MODEL
DK_HEAD = DK_TOTAL // GLA_HEADS
DV_HEAD = DV_TOTAL // GLA_HEADS
GATE_RANK = 16
GATE_NORMALIZER = 16.0
GLA_QKVR_WIDTH = 2 * DK_TOTAL + 2 * DV_TOTAL
CONV_W = 3
N_EXPERTS = 16
N_GROUPS = 4
EXPERTS_PER_GROUP = N_EXPERTS // N_GROUPS
TOP_K = 2
D_FF = D_MODEL // 2
ALPHA = (2.0 * DEPTH) ** 0.25
LN_EPS = 1e-5
RMS_EPS = 1e-6

LANES = 128
SUBLANES = 8
VMEM_LIMIT_BYTES = 56 * 1024 * 1024

TOKEN_TILE = 640
ROW_TILE = 320
GLA_CHUNK = 64
GLA_STEP_CHUNKS = 8
GLA_SAMPLE_BATCH = 16
MOE_TILE = 256
N_ASSIGN = N_TOK * TOP_K
GATHER_TILES = 2
MOE_MAX_TILES = -(-(N_ASSIGN // MOE_TILE + N_EXPERTS) // GATHER_TILES) * GATHER_TILES
N_ASSIGN_PAD = MOE_MAX_TILES * MOE_TILE


def _params(*semantics):
    return pltpu.CompilerParams(dimension_semantics=semantics, vmem_limit_bytes=VMEM_LIMIT_BYTES)


def _split3(x):
    hi = x.astype(BF16)
    r1 = x - hi.astype(F32)
    mid = r1.astype(BF16)
    lo = (r1 - mid.astype(F32)).astype(BF16)
    return hi, mid, lo


def _cast_weight(src_ref, dst_ref):
    rows = 64
    n = src_ref.shape[0] // rows

    def body(c, carry):
        r = pl.multiple_of(c * rows, rows)
        dst_ref[pl.ds(r, rows), :] = src_ref[pl.ds(r, rows), :].astype(BF16)
        return carry
    lax.fori_loop(0, n, body, 0)


def _dot_nn(a, b):
    return jnp.dot(a, b, preferred_element_type=F32)


def _dot_tn(a, b):
    return lax.dot_general(a, b, (((0,), (0,)), ((), ())), preferred_element_type=F32)


def _dot_nt(a, b):
    return lax.dot_general(a, b, (((1,), (1,)), ((), ())), preferred_element_type=F32)


def _col_bcast(rows, n):
    ones = jnp.ones((rows.shape[0], LANES), BF16)
    hi, mid, lo = _split3(rows)
    col = _dot_tn(hi, ones) + _dot_tn(mid, ones) + _dot_tn(lo, ones)
    return jnp.concatenate([col] * (n // LANES), axis=1)


def _silu(r):
    return r / (1.0 + jnp.exp(-r))


def _layer_norm(h, g, b):
    mu = jnp.mean(h, axis=-1, keepdims=True)
    d = h - mu
    var = jnp.mean(d * d, axis=-1, keepdims=True)
    return d * lax.rsqrt(var + LN_EPS) * g + b


X_TILE_BLOCKS = TOKEN_TILE // N_SAMPLE


def _x_tile_specs(tile_index):
    last = N_PROMPT // N_SAMPLE - 1

    def prompt_spec(k):
        return pl.BlockSpec((N_SAMPLE, D_MODEL),
                            lambda *g: (jnp.minimum(tile_index(*g) * X_TILE_BLOCKS + k, last), 0))
    return ([prompt_spec(k) for k in range(X_TILE_BLOCKS)]
            + [pl.BlockSpec((N_SAMPLE, D_MODEL), lambda *g: (0, 0))])


def _x_tile(i, x_refs):
    *prompt_refs, sample_ref = x_refs
    tail_is_sample = (i + 1) * TOKEN_TILE > N_PROMPT
    blocks = [r[...] for r in prompt_refs[:-1]]
    blocks.append(jnp.where(tail_is_sample, sample_ref[...], prompt_refs[-1][...]))
    return jnp.concatenate(blocks, axis=0)


def _gla_in_proj_kernel(*refs):
    *x_refs, wt_ref, wgl_ref, o_ref, gl_ref, wb_ref = refs
    j = pl.program_id(0)
    i = pl.program_id(1)

    @pl.when(i == 0)
    def _():
        _cast_weight(wt_ref, wb_ref)

    x = _x_tile(i, x_refs).astype(BF16)
    for s in range(TOKEN_TILE // ROW_TILE):
        rows = slice(s * ROW_TILE, (s + 1) * ROW_TILE)
        o_ref[rows, :] = _dot_nt(x[rows, :], wb_ref[...]).astype(o_ref.dtype)

    @pl.when(j == pl.num_programs(0) - 1)
    def _():
        wgl = jnp.concatenate([wgl_ref[...].astype(BF16), jnp.zeros((LANES - GATE_RANK, D_MODEL), BF16)],
                              axis=0)
        gl_ref[...] = _dot_nt(x, wgl)

    @pl.when(j < pl.num_programs(0) - 1)
    def _():
        gl_ref[...] = jnp.zeros_like(gl_ref)


def _gla_in_proj(x_prompt, x_sample, w_t):
    tn = 2048
    n_pass = GLA_QKVR_WIDTH // tn
    n_tiles = N_TOK // TOKEN_TILE
    return pl.pallas_call(
        _gla_in_proj_kernel,
        grid=(n_pass, n_tiles),
        in_specs=_x_tile_specs(lambda j, i: i) + [
            pl.BlockSpec((tn, D_MODEL), lambda j, i: (j, 0), pipeline_mode=pl.Buffered(1)),
            pl.BlockSpec((GATE_RANK, D_MODEL), lambda j, i: (GLA_QKVR_WIDTH // GATE_RANK, 0))],
        out_specs=[pl.BlockSpec((TOKEN_TILE, tn), lambda j, i: (i, j)),
                   pl.BlockSpec((TOKEN_TILE, LANES), lambda j, i: (jnp.where(j == n_pass - 1, i, n_tiles), 0))],
        out_shape=[jax.ShapeDtypeStruct((N_TOK, GLA_QKVR_WIDTH), F32),
                   jax.ShapeDtypeStruct((N_TOK + TOKEN_TILE, LANES), F32)],
        scratch_shapes=[pltpu.VMEM((tn, D_MODEL), BF16)],
        compiler_params=_params("arbitrary", "arbitrary"),
        name="gla_in_proj",
    )(*([x_prompt] * X_TILE_BLOCKS), x_sample, w_t, w_t)


def _gate_kernel(gl_ref, wg_ref, bg_ref, g_ref):
    z = _dot_nn(gl_ref[...].astype(BF16), wg_ref[...]) + bg_ref[...]
    log_sig = jnp.minimum(z, 0.0) - jnp.log1p(jnp.exp(-jnp.abs(z)))
    g_ref[...] = log_sig * (1.0 / GATE_NORMALIZER)


def _gla_gate(gl, w_gate, b_gate):
    wg = jnp.pad(w_gate, ((0, LANES - GATE_RANK), (0, 0))).astype(BF16)
    return pl.pallas_call(
        _gate_kernel,
        grid=(N_TOK // TOKEN_TILE,),
        in_specs=[pl.BlockSpec((TOKEN_TILE, LANES), lambda i: (i, 0)),
                  pl.BlockSpec((LANES, DK_TOTAL), lambda i: (0, 0)),
                  pl.BlockSpec((1, DK_TOTAL), lambda i: (0, 0))],
        out_specs=pl.BlockSpec((TOKEN_TILE, DK_TOTAL), lambda i: (i, 0)),
        out_shape=jax.ShapeDtypeStruct((N_TOK, DK_TOTAL), F32),
        compiler_params=_params("arbitrary"),
        name="gla_gate",
    )(gl, wg, b_gate.reshape(1, DK_TOTAL))


def _rms_gate(o, norm_g, r):
    o = o * lax.rsqrt(jnp.mean(o * o, axis=-1, keepdims=True) + RMS_EPS)
    return (o * norm_g) * _silu(r)


def _gla_prompt_kernel(q_ref, k_ref, v_ref, r_ref, g_ref, ng_ref, og_in_ref, og_ref, s_out_ref, s_ref):
    del og_in_ref
    c = pl.program_id(1)
    cc = GLA_CHUNK

    @pl.when(c == 0)
    def _():
        s_ref[...] = jnp.zeros_like(s_ref)

    for ci in range(q_ref.shape[0] // cc):
        last_chunk = ci == q_ref.shape[0] // cc - 1
        _gla_chunk(slice(ci * cc, (ci + 1) * cc), last_chunk, c, q_ref, k_ref, v_ref, r_ref, g_ref,
                   ng_ref, og_ref, s_out_ref, s_ref)


def _gla_chunk(rows, last_chunk, c, q_ref, k_ref, v_ref, r_ref, g_ref, ng_ref, og_ref, s_out_ref, s_ref):
    cc = GLA_CHUNK
    row = lax.broadcasted_iota(jnp.int32, (cc, cc), 0)
    col = lax.broadcasted_iota(jnp.int32, (cc, cc), 1)
    causal = col <= row
    tri = jnp.where(causal, 1.0, 0.0).astype(BF16)
    b = _dot_nn(jnp.concatenate([tri] * 3, axis=1),
                jnp.concatenate(_split3(g_ref[rows, :]), axis=0))
    b_last = b[cc - 1:cc, :]
    b_mid = b[cc // 2 - 1:cc // 2, :]

    q = q_ref[rows, :]
    k = k_ref[rows, :] * (DK_HEAD ** -0.5)
    q_in = (q * jnp.exp(b)).astype(BF16)
    qh, qm, _ = _split3(q * jnp.exp(b - b_mid))
    kh, km, _ = _split3(k * jnp.exp(b_mid - b))
    k_d = (k * jnp.exp(b_last - b)).astype(BF16)

    pad = 2 * SUBLANES
    sub = lax.broadcasted_iota(jnp.int32, (pad, DK_TOTAL), 0)
    decay_rows = jnp.where(sub == 0, jnp.broadcast_to(jnp.exp(b_last), (pad, DK_TOTAL)), 0.0)
    decay_col = _dot_tn(jnp.concatenate(_split3(decay_rows), axis=0),
                        jnp.ones((3 * pad, LANES), BF16))

    for h in range(GLA_HEADS):
        dk = slice(h * DK_HEAD, (h + 1) * DK_HEAD)
        dv = slice(h * DV_HEAD, (h + 1) * DV_HEAD)
        q3 = jnp.concatenate([qh[:, dk], qh[:, dk], qm[:, dk]], axis=1)
        k3 = jnp.concatenate([kh[:, dk], km[:, dk], kh[:, dk]], axis=1)
        scores = jnp.where(causal, _dot_nt(q3, k3), 0.0).astype(BF16)
        v = v_ref[rows, dv].astype(BF16)
        s_old = s_ref[h]
        o = _dot_nn(jnp.concatenate([q_in[:, dk], scores], axis=1),
                    jnp.concatenate([s_old.astype(BF16), v], axis=0))
        decay = jnp.concatenate([decay_col[dk, :]] * (DV_HEAD // LANES), axis=1)
        s_new = decay * s_old + _dot_tn(k_d[:, dk], v)
        s_ref[h] = s_new

        if last_chunk:
            @pl.when(c == pl.num_programs(1) - 1)
            def _():
                s_out_ref[0, h] = s_new

        og_ref[rows, dv] = _rms_gate(o, ng_ref[:, dv], r_ref[rows, dv]).astype(BF16)


def _gla_prompt(qkvr, g, norm_g):
    step = GLA_STEP_CHUNKS * GLA_CHUNK
    nc = SEQ // step
    row = lambda b, c: b * nc + c
    return pl.pallas_call(
        _gla_prompt_kernel,
        grid=(N_PROMPT_SEQ, nc),
        in_specs=[pl.BlockSpec((step, DK_TOTAL), lambda b, c: (row(b, c), 0)),
                  pl.BlockSpec((step, DK_TOTAL), lambda b, c: (row(b, c), 1)),
                  pl.BlockSpec((step, DV_TOTAL), lambda b, c: (row(b, c), 1)),
                  pl.BlockSpec((step, DV_TOTAL), lambda b, c: (row(b, c), 2)),
                  pl.BlockSpec((step, DK_TOTAL), lambda b, c: (row(b, c), 0)),
                  pl.BlockSpec((1, DV_TOTAL), lambda b, c: (0, 0)),
                  pl.BlockSpec(memory_space=pl.ANY)],
        out_specs=[pl.BlockSpec((step, DV_TOTAL), lambda b, c: (row(b, c), 0)),
                   pl.BlockSpec((1, GLA_HEADS, DK_HEAD, DV_HEAD), lambda b, c: (b, 0, 0, 0))],
        out_shape=[jax.ShapeDtypeStruct((N_TOK, DV_TOTAL), BF16),
                   jax.ShapeDtypeStruct((N_PROMPT_SEQ, GLA_HEADS, DK_HEAD, DV_HEAD), F32)],
        scratch_shapes=[pltpu.VMEM((GLA_HEADS, DK_HEAD, DV_HEAD), F32)],
        input_output_aliases={6: 0},
        compiler_params=_params("arbitrary", "arbitrary"),
        name="gla_prompt",
    )(qkvr, qkvr, qkvr, qkvr, g, norm_g, jnp.zeros((N_TOK, DV_TOTAL), BF16))


def _gla_sample_kernel(q_ref, k_ref, v_ref, r_ref, g_ref, ng_ref, s_ref, og_in_ref, og_ref, s_out_ref,
                       o_scr):
    del og_in_ref
    bb = q_ref.shape[0]
    q = q_ref[...]
    k = k_ref[...] * (DK_HEAD ** -0.5)
    v = v_ref[...]
    eg = jnp.exp(g_ref[...])
    qe = q * eg
    qk = jnp.sum(q * k, axis=-1, keepdims=True)
    sub = lax.broadcasted_iota(jnp.int32, (bb, DK_HEAD), 0)
    for bi in range(bb):
        sel = sub == bi
        s_old = s_ref[bi, 0]
        decay = _col_bcast(jnp.where(sel, eg, 0.0), DV_HEAD)
        k_col = _col_bcast(jnp.where(sel, k, 0.0), DV_HEAD)
        qe_col = _col_bcast(jnp.where(sel, qe, 0.0), DV_HEAD)
        s_out_ref[bi, 0] = decay * s_old + k_col * v[bi:bi + 1, :]
        o_scr[bi:bi + 1, :] = jnp.sum(qe_col * s_old, axis=0, keepdims=True)
    o = qk * v + o_scr[...]
    og_ref[...] = _rms_gate(o, ng_ref[...], r_ref[...]).astype(BF16)


def _gla_sample(qkvr, g, norm_g, state, og):
    bb = GLA_SAMPLE_BATCH
    r0 = N_PROMPT // bb
    hk = DK_TOTAL // DK_HEAD
    hv = 2 * DK_TOTAL // DV_HEAD
    return pl.pallas_call(
        _gla_sample_kernel,
        grid=(N_SAMPLE // bb, GLA_HEADS),
        in_specs=[pl.BlockSpec((bb, DK_HEAD), lambda i, h: (r0 + i, h)),
                  pl.BlockSpec((bb, DK_HEAD), lambda i, h: (r0 + i, hk + h)),
                  pl.BlockSpec((bb, DV_HEAD), lambda i, h: (r0 + i, hv + h)),
                  pl.BlockSpec((bb, DV_HEAD), lambda i, h: (r0 + i, hv + GLA_HEADS + h)),
                  pl.BlockSpec((bb, DK_HEAD), lambda i, h: (r0 + i, h)),
                  pl.BlockSpec((1, DV_HEAD), lambda i, h: (0, h)),
                  pl.BlockSpec((bb, 1, DK_HEAD, DV_HEAD), lambda i, h: (i, h, 0, 0)),
                  pl.BlockSpec(memory_space=pl.ANY)],
        out_specs=[pl.BlockSpec((bb, DV_HEAD), lambda i, h: (r0 + i, h)),
                   pl.BlockSpec((bb, 1, DK_HEAD, DV_HEAD), lambda i, h: (i, h, 0, 0))],
        out_shape=[jax.ShapeDtypeStruct((N_TOK, DV_TOTAL), BF16),
                   jax.ShapeDtypeStruct((N_SAMPLE, GLA_HEADS, DK_HEAD, DV_HEAD), F32)],
        scratch_shapes=[pltpu.VMEM((bb, DV_HEAD), F32)],
        input_output_aliases={7: 0},
        compiler_params=_params("arbitrary", "arbitrary"),
        name="gla_sample",
    )(qkvr, qkvr, qkvr, qkvr, g, norm_g, state, og)


def _top2_of4(p):
    ranks = []
    for j in range(4):
        rk = jnp.zeros(p[j].shape, jnp.int32)
        for i in range(4):
            if i == j:
                continue
            beats = (p[i] >= p[j]) if i < j else (p[i] > p[j])
            rk = rk + jnp.where(beats, 1, 0)
        ranks.append(rk)

    def pick(rank):
        val = jnp.zeros(p[0].shape, F32)
        idx = jnp.zeros(p[0].shape, jnp.int32)
        for j in range(4):
            hit = ranks[j] == rank
            val = jnp.where(hit, p[j], val)
            idx = jnp.where(hit, j, idx)
        return val, idx

    v1, i1 = pick(0)
    v2, i2 = pick(1)
    return v1, i1, v2, i2


def _route(x, rw_ref, rb_ref):
    logits = _dot_nn(x.astype(BF16), rw_ref[...])
    n = x.shape[0]
    n_pad = -n % LANES
    if n_pad:
        logits = jnp.concatenate([logits, jnp.zeros((n_pad, LANES), F32)], axis=0)
    lt = logits.T[0:N_EXPERTS, 0:n] + rb_ref[...]
    e = jnp.exp(lt - jnp.max(lt, axis=0, keepdims=True))
    probs = e / jnp.sum(e, axis=0, keepdims=True)
    best = None
    for grp in range(N_GROUPS):
        rows = [probs[grp * EXPERTS_PER_GROUP + j:grp * EXPERTS_PER_GROUP + j + 1, :]
                for j in range(EXPERTS_PER_GROUP)]
        v1, i1, v2, i2 = _top2_of4(rows)
        score = v1 + v2
        cand = (score, v1, i1 + grp * EXPERTS_PER_GROUP, v2, i2 + grp * EXPERTS_PER_GROUP)
        if best is None:
            best = cand
        else:
            better = score > best[0]
            best = tuple(jnp.where(better, n, o) for n, o in zip(cand, best))
    _, v1, e1, v2, e2 = best
    denom = v1 + v2
    return (e1, e2), (v1 / denom, v2 / denom)


def _proj_norm_route_kernel(a_ref, *refs):
    *x_refs, w_ref, lg_ref, lb_ref, rw_ref, rb_ref, xo_ref, id_ref, gt_ref = refs
    x = x_refs[0][...] if len(x_refs) == 1 else _x_tile(pl.program_id(0), x_refs)
    sub = id_ref.shape[2]
    for s in range(id_ref.shape[0]):
        rows = slice(s * sub, (s + 1) * sub)
        y = _dot_nn(a_ref[rows, :], w_ref[...])
        xn = _layer_norm(ALPHA * x[rows, :] + y, lg_ref[...], lb_ref[...])
        xo_ref[rows, :] = xn
        (e1, e2), (g1, g2) = _route(xn, rw_ref, rb_ref)
        id_ref[s, 0:1, :] = e1
        id_ref[s, 1:2, :] = e2
        gt_ref[s, 0:1, :] = g1
        gt_ref[s, 1:2, :] = g2


def _router_operands(router_w, router_b):
    rw = jnp.pad(router_w, ((0, 0), (0, LANES - N_EXPERTS))).astype(BF16)
    return rw, router_b.reshape(N_EXPERTS, 1)


def _proj_norm_route(a, x, w_out, ln_g, ln_b, router):
    nt = N_TOK // ROW_TILE
    n_sub = TOKEN_TILE // ROW_TILE
    rw, rb = router
    const = lambda i: (0, 0)
    if isinstance(x, tuple):
        x_specs = _x_tile_specs(lambda i: i)
        x_args = [x[0]] * X_TILE_BLOCKS + [x[1]]
    else:
        x_specs = [pl.BlockSpec((TOKEN_TILE, D_MODEL), lambda i: (i, 0))]
        x_args = [x]
    x1, ids, gates = pl.pallas_call(
        _proj_norm_route_kernel,
        grid=(N_TOK // TOKEN_TILE,),
        in_specs=[pl.BlockSpec((TOKEN_TILE, D_MODEL), lambda i: (i, 0))] + x_specs + [
            pl.BlockSpec((D_MODEL, D_MODEL), const, pipeline_mode=pl.Buffered(1)),
            pl.BlockSpec((1, D_MODEL), const),
            pl.BlockSpec((1, D_MODEL), const),
            pl.BlockSpec((D_MODEL, LANES), const),
            pl.BlockSpec((N_EXPERTS, 1), const)],
        out_specs=[pl.BlockSpec((TOKEN_TILE, D_MODEL), lambda i: (i, 0)),
                   pl.BlockSpec((n_sub, TOP_K, ROW_TILE), lambda i: (i, 0, 0)),
                   pl.BlockSpec((n_sub, TOP_K, ROW_TILE), lambda i: (i, 0, 0))],
        out_shape=[jax.ShapeDtypeStruct((N_TOK, D_MODEL), F32),
                   jax.ShapeDtypeStruct((nt, TOP_K, ROW_TILE), jnp.int32),
                   jax.ShapeDtypeStruct((nt, TOP_K, ROW_TILE), F32)],
        compiler_params=_params("arbitrary"),
        name="proj_norm_route",
    )(a, *x_args, w_out.astype(BF16), ln_g.reshape(1, D_MODEL), ln_b.reshape(1, D_MODEL), rw, rb)
    ids = ids.transpose(0, 2, 1).reshape(N_TOK, TOP_K)
    gates = gates.transpose(0, 2, 1).reshape(N_TOK, TOP_K)
    return x1, ids, gates


def _stream_expert_weights(layer, tables, w_refs, stage_refs, wb_refs, sem_ref):
    te_ref, nx_ref, nt_ref = tables
    i = pl.program_id(0)

    def copy(expert, j):
        return pltpu.make_async_copy(w_refs[j].at[layer, expert], stage_refs[j], sem_ref.at[j])

    @pl.when(i == 0)
    def _():
        for j in range(len(w_refs)):
            copy(te_ref[0], j).start()

    first = jnp.logical_or(i == 0, te_ref[i] != te_ref[jnp.maximum(i - 1, 0)])

    @pl.when(jnp.logical_and(first, i < nt_ref[0]))
    def _():
        for j, (stage, wb) in enumerate(zip(stage_refs, wb_refs)):
            copy(te_ref[i], j).wait()
            _cast_weight(stage, wb)

            @pl.when(nx_ref[i] >= 0)
            def _():
                copy(nx_ref[i], j).start()


def _issue_row_copies(idx_ref, base, src_ref, dst_ref, sem, n_rows, stride=1):
    def body(g, carry):
        r0 = pl.multiple_of(g * SUBLANES, SUBLANES)
        for k in range(SUBLANES):
            row = idx_ref[base + stride * (r0 + k)]
            pltpu.make_async_copy(src_ref.at[pl.ds(row, 1)], dst_ref.at[pl.ds(r0 + k, 1)],
                                  sem).start(priority=k % 2)
        return carry
    lax.fori_loop(0, n_rows // SUBLANES, body, 0)


def _gather_kernel(idx_ref, nt_ref, src_ref, out_ref, buf_ref, sem_ref):
    i = pl.program_id(0)
    tg = out_ref.shape[0]
    n = (nt_ref[0] * MOE_TILE + tg - 1) // tg

    def start_tile(tile, slot):
        _issue_row_copies(idx_ref, tile * tg, src_ref, buf_ref.at[slot], sem_ref.at[slot], tg)

    @pl.when(i == 0)
    def _():
        start_tile(0, 0)

    @pl.when(i + 1 < n)
    def _():
        start_tile(i + 1, (i + 1) % 2)

    @pl.when(i < n)
    def _():
        slot = i % 2
        pltpu.make_async_copy(src_ref.at[pl.ds(0, tg)], buf_ref.at[slot], sem_ref.at[slot]).wait()
        out_ref[...] = buf_ref[slot].astype(out_ref.dtype)

    @pl.when(i >= n)
    def _():
        out_ref[...] = jnp.zeros_like(out_ref)


def _gather_rows(src, idx, n_tiles, out_dtype, name):
    m = idx.shape[0]
    width = src.shape[1]
    tg = GATHER_TILES * MOE_TILE
    return pl.pallas_call(
        _gather_kernel,
        grid_spec=pltpu.PrefetchScalarGridSpec(
            num_scalar_prefetch=2,
            grid=(m // tg,),
            in_specs=[pl.BlockSpec(memory_space=pl.ANY)],
            out_specs=pl.BlockSpec((tg, width), lambda i, idx, nt: (i, 0)),
            scratch_shapes=[pltpu.VMEM((2, tg, width), src.dtype),
                            pltpu.SemaphoreType.DMA((2,))]),
        out_shape=jax.ShapeDtypeStruct((m, width), out_dtype),
        compiler_params=_params("arbitrary"),
        name=name,
    )(idx, n_tiles, src)


def _moe_ffn_kernel(layer, te_ref, nx_ref, nt_ref, xs_ref, w1_ref, w3_ref, w2_ref, ys_ref,
                    stage13_ref, stage2_ref, w1b_ref, w3b_ref, w2b_ref, sem_ref):
    _stream_expert_weights(layer, (te_ref, nx_ref, nt_ref), (w1_ref, w3_ref, w2_ref),
                           (stage13_ref.at[0], stage13_ref.at[1], stage2_ref),
                           (w1b_ref, w3b_ref, w2b_ref), sem_ref)
    i = pl.program_id(0)

    @pl.when(i < nt_ref[0])
    def _():
        xs = xs_ref[...]
        h1 = _dot_nn(xs, w1b_ref[...])
        h3 = _dot_nn(xs, w3b_ref[...])
        hid = (_silu(h1) * h3).astype(BF16)
        ys_ref[...] = _dot_nn(hid, w2b_ref[...])

    @pl.when(i >= nt_ref[0])
    def _():
        ys_ref[...] = jnp.zeros_like(ys_ref)


def _moe_ffn(xs, w1, w3, w2, layer, tables):
    last = lambda i, te, nx, nt: (jnp.minimum(i, nt[0] - 1), 0)
    return pl.pallas_call(
        functools.partial(_moe_ffn_kernel, layer),
        grid_spec=pltpu.PrefetchScalarGridSpec(
            num_scalar_prefetch=3,
            grid=(MOE_MAX_TILES,),
            in_specs=[pl.BlockSpec((MOE_TILE, D_MODEL), last),
                      pl.BlockSpec(memory_space=pl.ANY),
                      pl.BlockSpec(memory_space=pl.ANY),
                      pl.BlockSpec(memory_space=pl.ANY)],
            out_specs=pl.BlockSpec((MOE_TILE, D_MODEL), lambda i, te, nx, nt: (i, 0)),
            scratch_shapes=[pltpu.VMEM((2, D_MODEL, D_FF), F32),
                            pltpu.VMEM((D_FF, D_MODEL), F32),
                            pltpu.VMEM((D_MODEL, D_FF), BF16), pltpu.VMEM((D_MODEL, D_FF), BF16),
                            pltpu.VMEM((D_FF, D_MODEL), BF16),
                            pltpu.SemaphoreType.DMA((3,))]),
        out_shape=jax.ShapeDtypeStruct((N_ASSIGN_PAD, D_MODEL), F32),
        compiler_params=_params("arbitrary"),
        name="moe_ffn",
    )(*tables, xs, w1, w3, w2)


def _combine_norm_kernel(final, pos_ref, x_ref, ys_ref, gt_ref, lg_ref, lb_ref, o0_ref, o1_ref,
                         buf_ref, sem_ref):
    i = pl.program_id(0)
    n = pl.num_programs(0)
    tm = x_ref.shape[0]

    def start_tile(tile, slot):
        for k in range(TOP_K):
            _issue_row_copies(pos_ref, tile * tm * TOP_K + k, ys_ref, buf_ref.at[slot, k],
                              sem_ref.at[slot], tm, stride=TOP_K)

    @pl.when(i == 0)
    def _():
        start_tile(0, 0)

    @pl.when(i + 1 < n)
    def _():
        start_tile(i + 1, (i + 1) % 2)

    slot = i % 2
    for k in range(TOP_K):
        pltpu.make_async_copy(ys_ref.at[pl.ds(0, tm)], buf_ref.at[slot, k], sem_ref.at[slot]).wait()
    gt = gt_ref[...]
    moe = buf_ref[slot, 0] * gt[:, 0:1] + buf_ref[slot, 1] * gt[:, 1:2]
    xn = _layer_norm(ALPHA * x_ref[...] + moe, lg_ref[...], lb_ref[...])
    if final:
        @pl.when(i < n - 1)
        def _():
            o0_ref[...] = xn

        @pl.when(i == n - 1)
        def _():
            o1_ref[...] = xn
    else:
        o0_ref[...] = xn
        o1_ref[...] = xn.astype(BF16)


def _combine_norm(x, ys, pos, gates, ln_g, ln_b, final):
    tm = N_SAMPLE if final else ROW_TILE
    const = lambda i, pos: (0, 0)
    tile = lambda i, pos: (i, 0)
    if final:
        n_prompt_tiles = N_PROMPT // tm
        out_specs = [pl.BlockSpec((tm, D_MODEL), lambda i, pos: (jnp.minimum(i, n_prompt_tiles - 1), 0)),
                     pl.BlockSpec((tm, D_MODEL), const)]
        out_shape = [jax.ShapeDtypeStruct((N_PROMPT, D_MODEL), F32),
                     jax.ShapeDtypeStruct((N_SAMPLE, D_MODEL), F32)]
    else:
        out_specs = [pl.BlockSpec((tm, D_MODEL), tile), pl.BlockSpec((tm, D_MODEL), tile)]
        out_shape = [jax.ShapeDtypeStruct((N_TOK, D_MODEL), F32),
                     jax.ShapeDtypeStruct((N_TOK, D_MODEL), BF16)]
    return pl.pallas_call(
        functools.partial(_combine_norm_kernel, final),
        grid_spec=pltpu.PrefetchScalarGridSpec(
            num_scalar_prefetch=1,
            grid=(N_TOK // tm,),
            in_specs=[pl.BlockSpec((tm, D_MODEL), tile),
                      pl.BlockSpec(memory_space=pl.ANY),
                      pl.BlockSpec((tm, TOP_K), tile),
                      pl.BlockSpec((1, D_MODEL), const),
                      pl.BlockSpec((1, D_MODEL), const)],
            out_specs=out_specs,
            scratch_shapes=[pltpu.VMEM((2, TOP_K, tm, D_MODEL), F32),
                            pltpu.SemaphoreType.DMA((2,))]),
        out_shape=out_shape,
        compiler_params=_params("arbitrary"),
        name="combine_norm",
    )(pos, x, ys, gates, ln_g.reshape(1, D_MODEL), ln_b.reshape(1, D_MODEL))


def _routing_tables(ids):
    e = ids.reshape(-1)
    onehot = (e[:, None] == jnp.arange(N_EXPERTS, dtype=jnp.int32)[None, :]).astype(jnp.int32)
    csum = jnp.cumsum(onehot, axis=0)
    rank = jnp.sum(onehot * csum, axis=1) - 1
    sizes = csum[-1]
    tiles_per = (sizes + MOE_TILE - 1) // MOE_TILE
    tile_end = jnp.cumsum(tiles_per)
    tile_start = tile_end - tiles_per
    n_tiles = tile_end[-1]
    pos = (jnp.sum(onehot * (tile_start * MOE_TILE)[None, :], axis=1) + rank).astype(jnp.int32)
    tile = jnp.arange(MOE_MAX_TILES, dtype=jnp.int32)
    owner = jnp.sum((tile_end[None, :] <= jnp.minimum(tile, n_tiles - 1)[:, None]).astype(jnp.int32), axis=1)
    order = jnp.argsort(e, stable=True).astype(jnp.int32)
    owner_hot = (owner[:, None] == jnp.arange(N_EXPERTS, dtype=jnp.int32)[None, :]).astype(jnp.int32)
    size_start = jnp.cumsum(sizes) - sizes
    tile_first = jnp.sum(owner_hot * (size_start - tile_start * MOE_TILE)[None, :], axis=1)
    tile_limit = jnp.sum(owner_hot * (size_start + sizes)[None, :], axis=1)
    row = jnp.arange(N_ASSIGN_PAD, dtype=jnp.int32).reshape(MOE_MAX_TILES, MOE_TILE)
    src = tile_first[:, None] + row
    valid = src < tile_limit[:, None]
    row_token = jnp.where(valid, order[jnp.clip(src, 0, N_ASSIGN - 1)] // TOP_K, row % N_TOK).reshape(-1)
    experts = jnp.arange(N_EXPERTS, dtype=jnp.int32)
    later = lax.cummin(jnp.where(tiles_per > 0, experts, N_EXPERTS), reverse=True)
    next_expert = jnp.concatenate([later[1:], jnp.full((1,), N_EXPERTS, jnp.int32)])
    next_expert = jnp.where(next_expert >= N_EXPERTS, -1, next_expert)
    tile_next = jnp.sum(owner_hot * next_expert[None, :], axis=1)
    n_tiles = n_tiles.reshape(1).astype(jnp.int32)
    tables = (owner.astype(jnp.int32), tile_next.astype(jnp.int32), n_tiles)
    return pos, row_token.astype(jnp.int32), tables


def _moe_block(x, ids, gates, w1, w3, w2, layer, ln_g, ln_b, final):
    pos, row_token, tables = _routing_tables(ids)
    xs = _gather_rows(x, row_token, tables[2], BF16, "moe_gather")
    ys = _moe_ffn(xs, w1, w3, w2, layer, tables)
    return _combine_norm(x, ys, pos, gates, ln_g, ln_b, final)


def _conv_in_kernel(x_ref, wb_ref, wc_ref, wh_ref, cw_ref, s0_ref, s1_ref, gbz_ref, u_ref,
                    wbb_ref, wcb_ref, whb_ref, ubuf_ref):
    i = pl.program_id(1)
    tm = x_ref.shape[0]
    pad = SUBLANES

    @pl.when(i == 0)
    def _():
        _cast_weight(wb_ref, wbb_ref)
        _cast_weight(wc_ref, wcb_ref)
        _cast_weight(wh_ref, whb_ref)
        ubuf_ref[0:pad, :] = jnp.zeros((pad, ubuf_ref.shape[1]), F32)

    w0 = cw_ref[0:1, :]
    w1 = cw_ref[1:2, :]
    w2 = cw_ref[2:3, :]
    sub = ROW_TILE
    for s in range(tm // sub):
        lo, hi = s * sub, (s + 1) * sub
        x = x_ref[lo:hi, :]
        gb = _dot_nn(x, wbb_ref[...])
        u = _dot_nn(x, wcb_ref[...]) * _dot_nn(x, whb_ref[...])
        u_ref[lo:hi, :] = u
        ubuf_ref[pad + lo:pad + hi, :] = u
        t = (i * tm + lo + lax.broadcasted_iota(jnp.int32, (sub, 1), 0)) & (SEQ - 1)
        u_m1 = jnp.where(t >= 1, ubuf_ref[pad - 1 + lo:pad - 1 + hi, :], 0.0)
        u_m2 = jnp.where(t >= 2, ubuf_ref[pad - 2 + lo:pad - 2 + hi, :], 0.0)
        z = u_m2 * w0 + u_m1 * w1 + u * w2
        gbz_ref[lo:hi, :] = (gb * z).astype(BF16)
    ubuf_ref[0:pad, :] = ubuf_ref[tm:tm + pad, :]

    @pl.when(i == pl.num_programs(1) - 1)
    def _():
        first = sub - N_SAMPLE
        z_s = s0_ref[...] * w0 + s1_ref[...] * w1 + u[first:sub, :] * w2
        gbz_ref[tm - N_SAMPLE:tm, :] = (gb[first:sub, :] * z_s).astype(BF16)


def _conv_in(xb, w_in, conv_w, state):
    tn = 512
    nb = D_MODEL // tn
    s0 = state[:, 0, :]
    s1 = state[:, 1, :]
    return pl.pallas_call(
        _conv_in_kernel,
        grid=(nb, N_TOK // TOKEN_TILE),
        in_specs=[pl.BlockSpec((TOKEN_TILE, D_MODEL), lambda j, i: (i, 0)),
                  pl.BlockSpec((D_MODEL, tn), lambda j, i: (0, j)),
                  pl.BlockSpec((D_MODEL, tn), lambda j, i: (0, nb + j)),
                  pl.BlockSpec((D_MODEL, tn), lambda j, i: (0, 2 * nb + j)),
                  pl.BlockSpec((CONV_W, tn), lambda j, i: (0, j)),
                  pl.BlockSpec((N_SAMPLE, tn), lambda j, i: (0, j)),
                  pl.BlockSpec((N_SAMPLE, tn), lambda j, i: (0, j))],
        out_specs=[pl.BlockSpec((TOKEN_TILE, tn), lambda j, i: (i, j)),
                   pl.BlockSpec((TOKEN_TILE, tn), lambda j, i: (i, j))],
        out_shape=[jax.ShapeDtypeStruct((N_TOK, D_MODEL), BF16),
                   jax.ShapeDtypeStruct((N_TOK, D_MODEL), F32)],
        scratch_shapes=[pltpu.VMEM((D_MODEL, tn), BF16), pltpu.VMEM((D_MODEL, tn), BF16),
                        pltpu.VMEM((D_MODEL, tn), BF16),
                        pltpu.VMEM((TOKEN_TILE + 2 * SUBLANES, tn), F32)],
        compiler_params=_params("arbitrary", "arbitrary"),
        name="conv_in",
    )(xb, w_in, w_in, w_in, conv_w, s0, s1)


def kernel(x_prompt, x_sample, state_gla, state_conv, router_w, router_b, gla_w_in, gla_w_gate, gla_b_gate,
           gla_norm_g, gla_w_out, conv_w_in, conv_w, conv_w_out, ln_mix_g, ln_mix_b, ln_ffn_g, ln_ffn_b,
           moe_w1, moe_w3, moe_w2):
    xp = x_prompt.reshape(N_PROMPT, D_MODEL)
    xs = x_sample.reshape(N_SAMPLE, D_MODEL)
    router = _router_operands(router_w, router_b)

    w_in_t = gla_w_in[0].T
    qkvr, gl = _gla_in_proj(xp, xs, w_in_t)
    g = _gla_gate(gl, gla_w_gate[0], gla_b_gate[0])
    norm_g = gla_norm_g[0].reshape(1, DV_TOTAL)
    og, s_prompt = _gla_prompt(qkvr, g, norm_g)
    og, s_sample = _gla_sample(qkvr, g, norm_g, state_gla[0], og)
    x1, ids, gates = _proj_norm_route(og, (xp, xs), gla_w_out[0], ln_mix_g[0], ln_mix_b[0], router)
    x2, x2b = _moe_block(x1, ids, gates, moe_w1, moe_w3, moe_w2, 0, ln_ffn_g[0], ln_ffn_b[0], False)

    gbz, u = _conv_in(x2b, conv_w_in[0], conv_w[0], state_conv[0])
    x3, ids, gates = _proj_norm_route(gbz, x2, conv_w_out[0], ln_mix_g[1], ln_mix_b[1], router)
    y_p, y_s = _moe_block(x3, ids, gates, moe_w1, moe_w3, moe_w2, 1, ln_ffn_g[1], ln_ffn_b[1], True)

    y_prompt = y_p.reshape(N_PROMPT_SEQ, SEQ, D_MODEL)
    y_sample = y_s.reshape(N_SAMPLE, 1, D_MODEL)
    conv_prompt = jnp.stack([u[(b + 1) * SEQ - (CONV_W - 1):(b + 1) * SEQ] for b in range(N_PROMPT_SEQ)])
    conv_sample = jnp.concatenate([state_conv[0][:, 1:, :], u[N_PROMPT:][:, None, :]], axis=1)
    return (y_prompt, y_sample, s_prompt[None], conv_prompt[None], s_sample[None], conv_sample[None])
```

```python
import functools

import jax
import jax.numpy as jnp
from jax import lax
from jax.experimental import pallas as pl
from jax.experimental.pallas import tpu as pltpu

F32 = jnp.float32
BF16 = jnp.bfloat16

D_MODEL = 2048
N_PROMPT_SEQ = 4
SEQ = 2048
N_PROMPT = N_PROMPT_SEQ * SEQ
N_SAMPLE = 128
N_TOK = N_PROMPT + N_SAMPLE
DEPTH = 2

GLA_HEADS = 4
DK_TOTAL = D_MODEL // 2
DV_TOTAL = D_MODEL
DK_HEAD = DK_TOTAL // GLA_HEADS
DV_HEAD = DV_TOTAL // GLA_HEADS
GATE_RANK = 16
GATE_NORMALIZER = 16.0
GLA_QKVR_WIDTH = 2 * DK_TOTAL + 2 * DV_TOTAL
CONV_W = 3
N_EXPERTS = 16
N_GROUPS = 4
EXPERTS_PER_GROUP = N_EXPERTS // N_GROUPS
TOP_K = 2
D_FF = D_MODEL // 2
ALPHA = (2.0 * DEPTH) ** 0.25
LN_EPS = 1e-5
RMS_EPS = 1e-6

LANES = 128
SUBLANES = 8
VMEM_LIMIT_BYTES = 56 * 1024 * 1024

TOKEN_TILE = 640
ROW_TILE = 320
GLA_CHUNK = 64
GLA_STEP_CHUNKS = 8
GLA_SAMPLE_BATCH = 16
MOE_TILE = 256
N_ASSIGN = N_TOK * TOP_K
GATHER_TILES = 2
MOE_MAX_TILES = -(-(N_ASSIGN // MOE_TILE + N_EXPERTS) // GATHER_TILES) * GATHER_TILES
N_ASSIGN_PAD = MOE_MAX_TILES * MOE_TILE


def _params(*semantics):
    return pltpu.CompilerParams(dimension_semantics=semantics, vmem_limit_bytes=VMEM_LIMIT_BYTES)


def _split3(x):
    hi = x.astype(BF16)
    r1 = x - hi.astype(F32)
    mid = r1.astype(BF16)
    lo = (r1 - mid.astype(F32)).astype(BF16)
    return hi, mid, lo


def _cast_weight(src_ref, dst_ref):
    rows = 64
    n = src_ref.shape[0] // rows

    def body(c, carry):
        r = pl.multiple_of(c * rows, rows)
        dst_ref[pl.ds(r, rows), :] = src_ref[pl.ds(r, rows), :].astype(BF16)
        return carry
    lax.fori_loop(0, n, body, 0)


def _dot_nn(a, b):
    return jnp.dot(a, b, preferred_element_type=F32)


def _dot_tn(a, b):
    return lax.dot_general(a, b, (((0,), (0,)), ((), ())), preferred_element_type=F32)


def _dot_nt(a, b):
    return lax.dot_general(a, b, (((1,), (1,)), ((), ())), preferred_element_type=F32)


def _col_bcast(rows, n):
    ones = jnp.ones((rows.shape[0], LANES), BF16)
    hi, mid, lo = _split3(rows)
    col = _dot_tn(hi, ones) + _dot_tn(mid, ones) + _dot_tn(lo, ones)
    return jnp.concatenate([col] * (n // LANES), axis=1)


def _silu(r):
    return r / (1.0 + jnp.exp(-r))


def _layer_norm(h, g, b):
    mu = jnp.mean(h, axis=-1, keepdims=True)
    d = h - mu
    var = jnp.mean(d * d, axis=-1, keepdims=True)
    return d * lax.rsqrt(var + LN_EPS) * g + b


X_TILE_BLOCKS = TOKEN_TILE // N_SAMPLE


def _x_tile_specs(tile_index):
    last = N_PROMPT // N_SAMPLE - 1

    def prompt_spec(k):
        return pl.BlockSpec((N_SAMPLE, D_MODEL),
                            lambda *g: (jnp.minimum(tile_index(*g) * X_TILE_BLOCKS + k, last), 0))
    return ([prompt_spec(k) for k in range(X_TILE_BLOCKS)]
            + [pl.BlockSpec((N_SAMPLE, D_MODEL), lambda *g: (0, 0))])


def _x_tile(i, x_refs):
    *prompt_refs, sample_ref = x_refs
    tail_is_sample = (i + 1) * TOKEN_TILE > N_PROMPT
    blocks = [r[...] for r in prompt_refs[:-1]]
    blocks.append(jnp.where(tail_is_sample, sample_ref[...], prompt_refs[-1][...]))
    return jnp.concatenate(blocks, axis=0)


def _gla_in_proj_kernel(*refs):
    *x_refs, wt_ref, wgl_ref, o_ref, gl_ref, wb_ref = refs
    j = pl.program_id(0)
    i = pl.program_id(1)

    @pl.when(i == 0)
    def _():
        _cast_weight(wt_ref, wb_ref)

    x = _x_tile(i, x_refs).astype(BF16)
    for s in range(TOKEN_TILE // ROW_TILE):
        rows = slice(s * ROW_TILE, (s + 1) * ROW_TILE)
        o_ref[rows, :] = _dot_nt(x[rows, :], wb_ref[...]).astype(o_ref.dtype)

    @pl.when(j == pl.num_programs(0) - 1)
    def _():
        wgl = jnp.concatenate([wgl_ref[...].astype(BF16), jnp.zeros((LANES - GATE_RANK, D_MODEL), BF16)],
                              axis=0)
        gl_ref[...] = _dot_nt(x, wgl)

    @pl.when(j < pl.num_programs(0) - 1)
    def _():
        gl_ref[...] = jnp.zeros_like(gl_ref)


def _gla_in_proj(x_prompt, x_sample, w_t):
    tn = 2048
    n_pass = GLA_QKVR_WIDTH // tn
    n_tiles = N_TOK // TOKEN_TILE
    return pl.pallas_call(
        _gla_in_proj_kernel,
        grid=(n_pass, n_tiles),
        in_specs=_x_tile_specs(lambda j, i: i) + [
            pl.BlockSpec((tn, D_MODEL), lambda j, i: (j, 0), pipeline_mode=pl.Buffered(1)),
            pl.BlockSpec((GATE_RANK, D_MODEL), lambda j, i: (GLA_QKVR_WIDTH // GATE_RANK, 0))],
        out_specs=[pl.BlockSpec((TOKEN_TILE, tn), lambda j, i: (i, j)),
                   pl.BlockSpec((TOKEN_TILE, LANES), lambda j, i: (jnp.where(j == n_pass - 1, i, n_tiles), 0))],
        out_shape=[jax.ShapeDtypeStruct((N_TOK, GLA_QKVR_WIDTH), F32),
                   jax.ShapeDtypeStruct((N_TOK + TOKEN_TILE, LANES), F32)],
        scratch_shapes=[pltpu.VMEM((tn, D_MODEL), BF16)],
        compiler_params=_params("arbitrary", "arbitrary"),
        name="gla_in_proj",
    )(*([x_prompt] * X_TILE_BLOCKS), x_sample, w_t, w_t)


def _gate_kernel(gl_ref, wg_ref, bg_ref, g_ref):
    z = _dot_nn(gl_ref[...].astype(BF16), wg_ref[...]) + bg_ref[...]
    log_sig = jnp.minimum(z, 0.0) - jnp.log1p(jnp.exp(-jnp.abs(z)))
    g_ref[...] = log_sig * (1.0 / GATE_NORMALIZER)


def _gla_gate(gl, w_gate, b_gate):
    wg = jnp.pad(w_gate, ((0, LANES - GATE_RANK), (0, 0))).astype(BF16)
    return pl.pallas_call(
        _gate_kernel,
        grid=(N_TOK // TOKEN_TILE,),
        in_specs=[pl.BlockSpec((TOKEN_TILE, LANES), lambda i: (i, 0)),
                  pl.BlockSpec((LANES, DK_TOTAL), lambda i: (0, 0)),
                  pl.BlockSpec((1, DK_TOTAL), lambda i: (0, 0))],
        out_specs=pl.BlockSpec((TOKEN_TILE, DK_TOTAL), lambda i: (i, 0)),
        out_shape=jax.ShapeDtypeStruct((N_TOK, DK_TOTAL), F32),
        compiler_params=_params("arbitrary"),
        name="gla_gate",
    )(gl, wg, b_gate.reshape(1, DK_TOTAL))


def _rms_gate(o, norm_g, r):
    o = o * lax.rsqrt(jnp.mean(o * o, axis=-1, keepdims=True) + RMS_EPS)
    return (o * norm_g) * _silu(r)


def _gla_prompt_kernel(q_ref, k_ref, v_ref, r_ref, g_ref, ng_ref, og_in_ref, og_ref, s_out_ref, s_ref):
    del og_in_ref
    c = pl.program_id(1)
    cc = GLA_CHUNK

    @pl.when(c == 0)
    def _():
        s_ref[...] = jnp.zeros_like(s_ref)

    for ci in range(q_ref.shape[0] // cc):
        last_chunk = ci == q_ref.shape[0] // cc - 1
        _gla_chunk(slice(ci * cc, (ci + 1) * cc), last_chunk, c, q_ref, k_ref, v_ref, r_ref, g_ref,
                   ng_ref, og_ref, s_out_ref, s_ref)


def _gla_chunk(rows, last_chunk, c, q_ref, k_ref, v_ref, r_ref, g_ref, ng_ref, og_ref, s_out_ref, s_ref):
    cc = GLA_CHUNK
    row = lax.broadcasted_iota(jnp.int32, (cc, cc), 0)
    col = lax.broadcasted_iota(jnp.int32, (cc, cc), 1)
    causal = col <= row
    tri = jnp.where(causal, 1.0, 0.0).astype(BF16)
    b = _dot_nn(jnp.concatenate([tri] * 3, axis=1),
                jnp.concatenate(_split3(g_ref[rows, :]), axis=0))
    b_last = b[cc - 1:cc, :]
    b_mid = b[cc // 2 - 1:cc // 2, :]

    q = q_ref[rows, :]
    k = k_ref[rows, :] * (DK_HEAD ** -0.5)
    q_in = (q * jnp.exp(b)).astype(BF16)
    qh, qm, _ = _split3(q * jnp.exp(b - b_mid))
    kh, km, _ = _split3(k * jnp.exp(b_mid - b))
    k_d = (k * jnp.exp(b_last - b)).astype(BF16)

    pad = 2 * SUBLANES
    sub = lax.broadcasted_iota(jnp.int32, (pad, DK_TOTAL), 0)
    decay_rows = jnp.where(sub == 0, jnp.broadcast_to(jnp.exp(b_last), (pad, DK_TOTAL)), 0.0)
    decay_col = _dot_tn(jnp.concatenate(_split3(decay_rows), axis=0),
                        jnp.ones((3 * pad, LANES), BF16))

    for h in range(GLA_HEADS):
        dk = slice(h * DK_HEAD, (h + 1) * DK_HEAD)
        dv = slice(h * DV_HEAD, (h + 1) * DV_HEAD)
        q3 = jnp.concatenate([qh[:, dk], qh[:, dk], qm[:, dk]], axis=1)
        k3 = jnp.concatenate([kh[:, dk], km[:, dk], kh[:, dk]], axis=1)
        scores = jnp.where(causal, _dot_nt(q3, k3), 0.0).astype(BF16)
        v = v_ref[rows, dv].astype(BF16)
        s_old = s_ref[h]
        o = _dot_nn(jnp.concatenate([q_in[:, dk], scores], axis=1),
                    jnp.concatenate([s_old.astype(BF16), v], axis=0))
        decay = jnp.concatenate([decay_col[dk, :]] * (DV_HEAD // LANES), axis=1)
        s_new = decay * s_old + _dot_tn(k_d[:, dk], v)
        s_ref[h] = s_new

        if last_chunk:
            @pl.when(c == pl.num_programs(1) - 1)
            def _():
                s_out_ref[0, h] = s_new

        og_ref[rows, dv] = _rms_gate(o, ng_ref[:, dv], r_ref[rows, dv]).astype(BF16)


def _gla_prompt(qkvr, g, norm_g):
    step = GLA_STEP_CHUNKS * GLA_CHUNK
    nc = SEQ // step
    row = lambda b, c: b * nc + c
    return pl.pallas_call(
        _gla_prompt_kernel,
        grid=(N_PROMPT_SEQ, nc),
        in_specs=[pl.BlockSpec((step, DK_TOTAL), lambda b, c: (row(b, c), 0)),
                  pl.BlockSpec((step, DK_TOTAL), lambda b, c: (row(b, c), 1)),
                  pl.BlockSpec((step, DV_TOTAL), lambda b, c: (row(b, c), 1)),
                  pl.BlockSpec((step, DV_TOTAL), lambda b, c: (row(b, c), 2)),
                  pl.BlockSpec((step, DK_TOTAL), lambda b, c: (row(b, c), 0)),
                  pl.BlockSpec((1, DV_TOTAL), lambda b, c: (0, 0)),
                  pl.BlockSpec(memory_space=pl.ANY)],
        out_specs=[pl.BlockSpec((step, DV_TOTAL), lambda b, c: (row(b, c), 0)),
                   pl.BlockSpec((1, GLA_HEADS, DK_HEAD, DV_HEAD), lambda b, c: (b, 0, 0, 0))],
        out_shape=[jax.ShapeDtypeStruct((N_TOK, DV_TOTAL), BF16),
                   jax.ShapeDtypeStruct((N_PROMPT_SEQ, GLA_HEADS, DK_HEAD, DV_HEAD), F32)],
        scratch_shapes=[pltpu.VMEM((GLA_HEADS, DK_HEAD, DV_HEAD), F32)],
        input_output_aliases={6: 0},
        compiler_params=_params("arbitrary", "arbitrary"),
        name="gla_prompt",
    )(qkvr, qkvr, qkvr, qkvr, g, norm_g, jnp.zeros((N_TOK, DV_TOTAL), BF16))


def _gla_sample_kernel(q_ref, k_ref, v_ref, r_ref, g_ref, ng_ref, s_ref, og_in_ref, og_ref, s_out_ref,
                       o_scr):
    del og_in_ref
    bb = q_ref.shape[0]
    q = q_ref[...]
    k = k_ref[...] * (DK_HEAD ** -0.5)
    v = v_ref[...]
    eg = jnp.exp(g_ref[...])
    qe = q * eg
    qk = jnp.sum(q * k, axis=-1, keepdims=True)
    sub = lax.broadcasted_iota(jnp.int32, (bb, DK_HEAD), 0)
    for bi in range(bb):
        sel = sub == bi
        s_old = s_ref[bi, 0]
        decay = _col_bcast(jnp.where(sel, eg, 0.0), DV_HEAD)
        k_col = _col_bcast(jnp.where(sel, k, 0.0), DV_HEAD)
        qe_col = _col_bcast(jnp.where(sel, qe, 0.0), DV_HEAD)
        s_out_ref[bi, 0] = decay * s_old + k_col * v[bi:bi + 1, :]
        o_scr[bi:bi + 1, :] = jnp.sum(qe_col * s_old, axis=0, keepdims=True)
    o = qk * v + o_scr[...]
    og_ref[...] = _rms_gate(o, ng_ref[...], r_ref[...]).astype(BF16)


def _gla_sample(qkvr, g, norm_g, state, og):
    bb = GLA_SAMPLE_BATCH
    r0 = N_PROMPT // bb
    hk = DK_TOTAL // DK_HEAD
    hv = 2 * DK_TOTAL // DV_HEAD
    return pl.pallas_call(
        _gla_sample_kernel,
        grid=(N_SAMPLE // bb, GLA_HEADS),
        in_specs=[pl.BlockSpec((bb, DK_HEAD), lambda i, h: (r0 + i, h)),
                  pl.BlockSpec((bb, DK_HEAD), lambda i, h: (r0 + i, hk + h)),
                  pl.BlockSpec((bb, DV_HEAD), lambda i, h: (r0 + i, hv + h)),
                  pl.BlockSpec((bb, DV_HEAD), lambda i, h: (r0 + i, hv + GLA_HEADS + h)),
                  pl.BlockSpec((bb, DK_HEAD), lambda i, h: (r0 + i, h)),
                  pl.BlockSpec((1, DV_HEAD), lambda i, h: (0, h)),
                  pl.BlockSpec((bb, 1, DK_HEAD, DV_HEAD), lambda i, h: (i, h, 0, 0)),
                  pl.BlockSpec(memory_space=pl.ANY)],
        out_specs=[pl.BlockSpec((bb, DV_HEAD), lambda i, h: (r0 + i, h)),
                   pl.BlockSpec((bb, 1, DK_HEAD, DV_HEAD), lambda i, h: (i, h, 0, 0))],
        out_shape=[jax.ShapeDtypeStruct((N_TOK, DV_TOTAL), BF16),
                   jax.ShapeDtypeStruct((N_SAMPLE, GLA_HEADS, DK_HEAD, DV_HEAD), F32)],
        scratch_shapes=[pltpu.VMEM((bb, DV_HEAD), F32)],
        input_output_aliases={7: 0},
        compiler_params=_params("arbitrary", "arbitrary"),
        name="gla_sample",
    )(qkvr, qkvr, qkvr, qkvr, g, norm_g, state, og)


def _top2_of4(p):
    ranks = []
    for j in range(4):
        rk = jnp.zeros(p[j].shape, jnp.int32)
        for i in range(4):
            if i == j:
                continue
            beats = (p[i] >= p[j]) if i < j else (p[i] > p[j])
            rk = rk + jnp.where(beats, 1, 0)
        ranks.append(rk)

    def pick(rank):
        val = jnp.zeros(p[0].shape, F32)
        idx = jnp.zeros(p[0].shape, jnp.int32)
        for j in range(4):
            hit = ranks[j] == rank
            val = jnp.where(hit, p[j], val)
            idx = jnp.where(hit, j, idx)
        return val, idx

    v1, i1 = pick(0)
    v2, i2 = pick(1)
    return v1, i1, v2, i2


def _route(x, rw_ref, rb_ref):
    logits = _dot_nn(x.astype(BF16), rw_ref[...])
    n = x.shape[0]
    n_pad = -n % LANES
    if n_pad:
        logits = jnp.concatenate([logits, jnp.zeros((n_pad, LANES), F32)], axis=0)
    lt = logits.T[0:N_EXPERTS, 0:n] + rb_ref[...]
    e = jnp.exp(lt - jnp.max(lt, axis=0, keepdims=True))
    probs = e / jnp.sum(e, axis=0, keepdims=True)
    best = None
    for grp in range(N_GROUPS):
        rows = [probs[grp * EXPERTS_PER_GROUP + j:grp * EXPERTS_PER_GROUP + j + 1, :]
                for j in range(EXPERTS_PER_GROUP)]
        v1, i1, v2, i2 = _top2_of4(rows)
        score = v1 + v2
        cand = (score, v1, i1 + grp * EXPERTS_PER_GROUP, v2, i2 + grp * EXPERTS_PER_GROUP)
        if best is None:
            best = cand
        else:
            better = score > best[0]
            best = tuple(jnp.where(better, n, o) for n, o in zip(cand, best))
    _, v1, e1, v2, e2 = best
    denom = v1 + v2
    return (e1, e2), (v1 / denom, v2 / denom)


def _proj_norm_route_kernel(a_ref, *refs):
    *x_refs, w_ref, lg_ref, lb_ref, rw_ref, rb_ref, xo_ref, id_ref, gt_ref = refs
    x = x_refs[0][...] if len(x_refs) == 1 else _x_tile(pl.program_id(0), x_refs)
    sub = id_ref.shape[2]
    for s in range(id_ref.shape[0]):
        rows = slice(s * sub, (s + 1) * sub)
        y = _dot_nn(a_ref[rows, :], w_ref[...])
        xn = _layer_norm(ALPHA * x[rows, :] + y, lg_ref[...], lb_ref[...])
        xo_ref[rows, :] = xn
        (e1, e2), (g1, g2) = _route(xn, rw_ref, rb_ref)
        id_ref[s, 0:1, :] = e1
        id_ref[s, 1:2, :] = e2
        gt_ref[s, 0:1, :] = g1
        gt_ref[s, 1:2, :] = g2


def _router_operands(router_w, router_b):
    rw = jnp.pad(router_w, ((0, 0), (0, LANES - N_EXPERTS))).astype(BF16)
    return rw, router_b.reshape(N_EXPERTS, 1)


def _proj_norm_route(a, x, w_out, ln_g, ln_b, router):
    nt = N_TOK // ROW_TILE
    n_sub = TOKEN_TILE // ROW_TILE
    rw, rb = router
    const = lambda i: (0, 0)
    if isinstance(x, tuple):
        x_specs = _x_tile_specs(lambda i: i)
        x_args = [x[0]] * X_TILE_BLOCKS + [x[1]]
    else:
        x_specs = [pl.BlockSpec((TOKEN_TILE, D_MODEL), lambda i: (i, 0))]
        x_args = [x]
    x1, ids, gates = pl.pallas_call(
        _proj_norm_route_kernel,
        grid=(N_TOK // TOKEN_TILE,),
        in_specs=[pl.BlockSpec((TOKEN_TILE, D_MODEL), lambda i: (i, 0))] + x_specs + [
            pl.BlockSpec((D_MODEL, D_MODEL), const, pipeline_mode=pl.Buffered(1)),
            pl.BlockSpec((1, D_MODEL), const),
            pl.BlockSpec((1, D_MODEL), const),
            pl.BlockSpec((D_MODEL, LANES), const),
            pl.BlockSpec((N_EXPERTS, 1), const)],
        out_specs=[pl.BlockSpec((TOKEN_TILE, D_MODEL), lambda i: (i, 0)),
                   pl.BlockSpec((n_sub, TOP_K, ROW_TILE), lambda i: (i, 0, 0)),
                   pl.BlockSpec((n_sub, TOP_K, ROW_TILE), lambda i: (i, 0, 0))],
        out_shape=[jax.ShapeDtypeStruct((N_TOK, D_MODEL), F32),
                   jax.ShapeDtypeStruct((nt, TOP_K, ROW_TILE), jnp.int32),
                   jax.ShapeDtypeStruct((nt, TOP_K, ROW_TILE), F32)],
        compiler_params=_params("arbitrary"),
        name="proj_norm_route",
    )(a, *x_args, w_out.astype(BF16), ln_g.reshape(1, D_MODEL), ln_b.reshape(1, D_MODEL), rw, rb)
    ids = ids.transpose(0, 2, 1).reshape(N_TOK, TOP_K)
    gates = gates.transpose(0, 2, 1).reshape(N_TOK, TOP_K)
    return x1, ids, gates


def _stream_expert_weights(layer, tables, w_refs, stage_refs, wb_refs, sem_ref):
    te_ref, nx_ref, nt_ref = tables
    i = pl.program_id(0)

    def copy(expert, j):
        return pltpu.make_async_copy(w_refs[j].at[layer, expert], stage_refs[j], sem_ref.at[j])

    @pl.when(i == 0)
    def _():
        for j in range(len(w_refs)):
            copy(te_ref[0], j).start()

    first = jnp.logical_or(i == 0, te_ref[i] != te_ref[jnp.maximum(i - 1, 0)])

    @pl.when(jnp.logical_and(first, i < nt_ref[0]))
    def _():
        for j, (stage, wb) in enumerate(zip(stage_refs, wb_refs)):
            copy(te_ref[i], j).wait()
            _cast_weight(stage, wb)

            @pl.when(nx_ref[i] >= 0)
            def _():
                copy(nx_ref[i], j).start()


def _issue_row_copies(idx_ref, base, src_ref, dst_ref, sem, n_rows, stride=1, shift=0):
    def body(g, carry):
        r0 = pl.multiple_of(g * SUBLANES, SUBLANES)
        for k in range(SUBLANES):
            row = idx_ref[base + stride * (r0 + k)]
            if shift:
                row = lax.shift_right_logical(row, shift)
            pltpu.make_async_copy(src_ref.at[pl.ds(row, 1)], dst_ref.at[pl.ds(r0 + k, 1)],
                                  sem).start(priority=k % 2)
        return carry
    lax.fori_loop(0, n_rows // SUBLANES, body, 0)


def _gather_kernel(order_ref, first_ref, nt_ref, src_ref, out_ref, buf_ref, sem_ref):
    i = pl.program_id(0)
    tg = out_ref.shape[0]
    n = (nt_ref[0] * MOE_TILE + tg - 1) // tg

    def start_tile(step, slot):
        for t in range(tg // MOE_TILE):
            _issue_row_copies(order_ref, first_ref[step * (tg // MOE_TILE) + t], src_ref,
                              buf_ref.at[slot, pl.ds(t * MOE_TILE, MOE_TILE)], sem_ref.at[slot],
                              MOE_TILE, shift=TOP_K.bit_length() - 1)

    @pl.when(i == 0)
    def _():
        start_tile(0, 0)

    @pl.when(i + 1 < n)
    def _():
        start_tile(i + 1, (i + 1) % 2)

    @pl.when(i < n)
    def _():
        slot = i % 2
        pltpu.make_async_copy(src_ref.at[pl.ds(0, tg)], buf_ref.at[slot], sem_ref.at[slot]).wait()
        out_ref[...] = buf_ref[slot].astype(out_ref.dtype)

    @pl.when(i >= n)
    def _():
        out_ref[...] = jnp.zeros_like(out_ref)


def _gather_rows(src, order, tile_first, n_tiles, out_dtype, name):
    width = src.shape[1]
    tg = GATHER_TILES * MOE_TILE
    return pl.pallas_call(
        _gather_kernel,
        grid_spec=pltpu.PrefetchScalarGridSpec(
            num_scalar_prefetch=3,
            grid=(N_ASSIGN_PAD // tg,),
            in_specs=[pl.BlockSpec(memory_space=pl.ANY)],
            out_specs=pl.BlockSpec((tg, width), lambda i, order, first, nt: (i, 0)),
            scratch_shapes=[pltpu.VMEM((2, tg, width), src.dtype),
                            pltpu.SemaphoreType.DMA((2,))]),
        out_shape=jax.ShapeDtypeStruct((N_ASSIGN_PAD, width), out_dtype),
        compiler_params=_params("arbitrary"),
        name=name,
    )(order, tile_first, n_tiles, src)


def _moe_ffn_kernel(layer, te_ref, nx_ref, nt_ref, xs_ref, w1_ref, w3_ref, w2_ref, ys_ref,
                    stage13_ref, stage2_ref, w1b_ref, w3b_ref, w2b_ref, sem_ref):
    _stream_expert_weights(layer, (te_ref, nx_ref, nt_ref), (w1_ref, w3_ref, w2_ref),
                           (stage13_ref.at[0], stage13_ref.at[1], stage2_ref),
                           (w1b_ref, w3b_ref, w2b_ref), sem_ref)
    i = pl.program_id(0)

    @pl.when(i < nt_ref[0])
    def _():
        xs = xs_ref[...]
        h1 = _dot_nn(xs, w1b_ref[...])
        h3 = _dot_nn(xs, w3b_ref[...])
        hid = (_silu(h1) * h3).astype(BF16)
        ys_ref[...] = _dot_nn(hid, w2b_ref[...])

    @pl.when(i >= nt_ref[0])
    def _():
        ys_ref[...] = jnp.zeros_like(ys_ref)


def _moe_ffn(xs, w1, w3, w2, layer, tables):
    last = lambda i, te, nx, nt: (jnp.minimum(i, nt[0] - 1), 0)
    return pl.pallas_call(
        functools.partial(_moe_ffn_kernel, layer),
        grid_spec=pltpu.PrefetchScalarGridSpec(
            num_scalar_prefetch=3,
            grid=(MOE_MAX_TILES,),
            in_specs=[pl.BlockSpec((MOE_TILE, D_MODEL), last),
                      pl.BlockSpec(memory_space=pl.ANY),
                      pl.BlockSpec(memory_space=pl.ANY),
                      pl.BlockSpec(memory_space=pl.ANY)],
            out_specs=pl.BlockSpec((MOE_TILE, D_MODEL), lambda i, te, nx, nt: (i, 0)),
            scratch_shapes=[pltpu.VMEM((2, D_MODEL, D_FF), F32),
                            pltpu.VMEM((D_FF, D_MODEL), F32),
                            pltpu.VMEM((D_MODEL, D_FF), BF16), pltpu.VMEM((D_MODEL, D_FF), BF16),
                            pltpu.VMEM((D_FF, D_MODEL), BF16),
                            pltpu.SemaphoreType.DMA((3,))]),
        out_shape=jax.ShapeDtypeStruct((N_ASSIGN_PAD, D_MODEL), F32),
        compiler_params=_params("arbitrary"),
        name="moe_ffn",
    )(*tables, xs, w1, w3, w2)


def _combine_norm_kernel(final, pos_ref, x_ref, ys_ref, gt_ref, lg_ref, lb_ref, o0_ref, o1_ref,
                         buf_ref, sem_ref):
    i = pl.program_id(0)
    n = pl.num_programs(0)
    tm = x_ref.shape[0]

    def start_tile(tile, slot):
        for k in range(TOP_K):
            _issue_row_copies(pos_ref, tile * tm * TOP_K + k, ys_ref, buf_ref.at[slot, k],
                              sem_ref.at[slot], tm, stride=TOP_K)

    @pl.when(i == 0)
    def _():
        start_tile(0, 0)

    @pl.when(i + 1 < n)
    def _():
        start_tile(i + 1, (i + 1) % 2)

    slot = i % 2
    for k in range(TOP_K):
        pltpu.make_async_copy(ys_ref.at[pl.ds(0, tm)], buf_ref.at[slot, k], sem_ref.at[slot]).wait()
    gt = gt_ref[...]
    moe = buf_ref[slot, 0] * gt[:, 0:1] + buf_ref[slot, 1] * gt[:, 1:2]
    xn = _layer_norm(ALPHA * x_ref[...] + moe, lg_ref[...], lb_ref[...])
    if final:
        @pl.when(i < n - 1)
        def _():
            o0_ref[...] = xn

        @pl.when(i == n - 1)
        def _():
            o1_ref[...] = xn
    else:
        o0_ref[...] = xn
        o1_ref[...] = xn.astype(BF16)


def _combine_norm(x, ys, pos, gates, ln_g, ln_b, final):
    tm = N_SAMPLE if final else ROW_TILE
    const = lambda i, pos: (0, 0)
    tile = lambda i, pos: (i, 0)
    if final:
        n_prompt_tiles = N_PROMPT // tm
        out_specs = [pl.BlockSpec((tm, D_MODEL), lambda i, pos: (jnp.minimum(i, n_prompt_tiles - 1), 0)),
                     pl.BlockSpec((tm, D_MODEL), const)]
        out_shape = [jax.ShapeDtypeStruct((N_PROMPT, D_MODEL), F32),
                     jax.ShapeDtypeStruct((N_SAMPLE, D_MODEL), F32)]
    else:
        out_specs = [pl.BlockSpec((tm, D_MODEL), tile), pl.BlockSpec((tm, D_MODEL), tile)]
        out_shape = [jax.ShapeDtypeStruct((N_TOK, D_MODEL), F32),
                     jax.ShapeDtypeStruct((N_TOK, D_MODEL), BF16)]
    return pl.pallas_call(
        functools.partial(_combine_norm_kernel, final),
        grid_spec=pltpu.PrefetchScalarGridSpec(
            num_scalar_prefetch=1,
            grid=(N_TOK // tm,),
            in_specs=[pl.BlockSpec((tm, D_MODEL), tile),
                      pl.BlockSpec(memory_space=pl.ANY),
                      pl.BlockSpec((tm, TOP_K), tile),
                      pl.BlockSpec((1, D_MODEL), const),
                      pl.BlockSpec((1, D_MODEL), const)],
            out_specs=out_specs,
            scratch_shapes=[pltpu.VMEM((2, TOP_K, tm, D_MODEL), F32),
                            pltpu.SemaphoreType.DMA((2,))]),
        out_shape=out_shape,
        compiler_params=_params("arbitrary"),
        name="combine_norm",
    )(pos, x, ys, gates, ln_g.reshape(1, D_MODEL), ln_b.reshape(1, D_MODEL))


def _routing_tables(ids):
    e = ids.reshape(-1)
    onehot = (e[:, None] == jnp.arange(N_EXPERTS, dtype=jnp.int32)[None, :]).astype(jnp.int32)
    csum = jnp.cumsum(onehot, axis=0)
    rank = jnp.sum(onehot * csum, axis=1) - 1
    sizes = csum[-1]
    tiles_per = (sizes + MOE_TILE - 1) // MOE_TILE
    tile_end = jnp.cumsum(tiles_per)
    tile_start = tile_end - tiles_per
    n_tiles = tile_end[-1]
    pos = (jnp.sum(onehot * (tile_start * MOE_TILE)[None, :], axis=1) + rank).astype(jnp.int32)
    tile = jnp.arange(MOE_MAX_TILES, dtype=jnp.int32)
    owner = jnp.sum((tile_end[None, :] <= jnp.minimum(tile, n_tiles - 1)[:, None]).astype(jnp.int32), axis=1)
    order = jnp.argsort(e, stable=True).astype(jnp.int32)
    order = jnp.concatenate([order, jnp.arange(MOE_TILE, dtype=jnp.int32)])
    owner_hot = (owner[:, None] == jnp.arange(N_EXPERTS, dtype=jnp.int32)[None, :]).astype(jnp.int32)
    size_start = jnp.cumsum(sizes) - sizes
    tile_first = jnp.sum(owner_hot * (size_start - tile_start * MOE_TILE)[None, :], axis=1) + tile * MOE_TILE
    tile_first = jnp.clip(tile_first, 0, N_ASSIGN).astype(jnp.int32)
    experts = jnp.arange(N_EXPERTS, dtype=jnp.int32)
    later = lax.cummin(jnp.where(tiles_per > 0, experts, N_EXPERTS), reverse=True)
    next_expert = jnp.concatenate([later[1:], jnp.full((1,), N_EXPERTS, jnp.int32)])
    next_expert = jnp.where(next_expert >= N_EXPERTS, -1, next_expert)
    tile_next = jnp.sum(owner_hot * next_expert[None, :], axis=1)
    n_tiles = n_tiles.reshape(1).astype(jnp.int32)
    tables = (owner.astype(jnp.int32), tile_next.astype(jnp.int32), n_tiles)
    return pos, order, tile_first, tables


def _moe_block(x, ids, gates, w1, w3, w2, layer, ln_g, ln_b, final):
    pos, order, tile_first, tables = _routing_tables(ids)
    xs = _gather_rows(x, order, tile_first, tables[2], BF16, "moe_gather")
    ys = _moe_ffn(xs, w1, w3, w2, layer, tables)
    return _combine_norm(x, ys, pos, gates, ln_g, ln_b, final)


def _conv_in_kernel(x_ref, wb_ref, wc_ref, wh_ref, cw_ref, s0_ref, s1_ref, gbz_ref, u_ref,
                    wbb_ref, wcb_ref, whb_ref, ubuf_ref):
    i = pl.program_id(1)
    tm = x_ref.shape[0]
    pad = SUBLANES

    @pl.when(i == 0)
    def _():
        _cast_weight(wb_ref, wbb_ref)
        _cast_weight(wc_ref, wcb_ref)
        _cast_weight(wh_ref, whb_ref)
        ubuf_ref[0:pad, :] = jnp.zeros((pad, ubuf_ref.shape[1]), F32)

    w0 = cw_ref[0:1, :]
    w1 = cw_ref[1:2, :]
    w2 = cw_ref[2:3, :]
    sub = ROW_TILE
    for s in range(tm // sub):
        lo, hi = s * sub, (s + 1) * sub
        x = x_ref[lo:hi, :]
        gb = _dot_nn(x, wbb_ref[...])
        u = _dot_nn(x, wcb_ref[...]) * _dot_nn(x, whb_ref[...])
        u_ref[lo:hi, :] = u
        ubuf_ref[pad + lo:pad + hi, :] = u
        t = (i * tm + lo + lax.broadcasted_iota(jnp.int32, (sub, 1), 0)) & (SEQ - 1)
        u_m1 = jnp.where(t >= 1, ubuf_ref[pad - 1 + lo:pad - 1 + hi, :], 0.0)
        u_m2 = jnp.where(t >= 2, ubuf_ref[pad - 2 + lo:pad - 2 + hi, :], 0.0)
        z = u_m2 * w0 + u_m1 * w1 + u * w2
        gbz_ref[lo:hi, :] = (gb * z).astype(BF16)
    ubuf_ref[0:pad, :] = ubuf_ref[tm:tm + pad, :]

    @pl.when(i == pl.num_programs(1) - 1)
    def _():
        first = sub - N_SAMPLE
        z_s = s0_ref[...] * w0 + s1_ref[...] * w1 + u[first:sub, :] * w2
        gbz_ref[tm - N_SAMPLE:tm, :] = (gb[first:sub, :] * z_s).astype(BF16)


def _conv_in(xb, w_in, conv_w, state):
    tn = 512
    nb = D_MODEL // tn
    s0 = state[:, 0, :]
    s1 = state[:, 1, :]
    return pl.pallas_call(
        _conv_in_kernel,
        grid=(nb, N_TOK // TOKEN_TILE),
        in_specs=[pl.BlockSpec((TOKEN_TILE, D_MODEL), lambda j, i: (i, 0)),
                  pl.BlockSpec((D_MODEL, tn), lambda j, i: (0, j)),
                  pl.BlockSpec((D_MODEL, tn), lambda j, i: (0, nb + j)),
                  pl.BlockSpec((D_MODEL, tn), lambda j, i: (0, 2 * nb + j)),
                  pl.BlockSpec((CONV_W, tn), lambda j, i: (0, j)),
                  pl.BlockSpec((N_SAMPLE, tn), lambda j, i: (0, j)),
                  pl.BlockSpec((N_SAMPLE, tn), lambda j, i: (0, j))],
        out_specs=[pl.BlockSpec((TOKEN_TILE, tn), lambda j, i: (i, j)),
                   pl.BlockSpec((TOKEN_TILE, tn), lambda j, i: (i, j))],
        out_shape=[jax.ShapeDtypeStruct((N_TOK, D_MODEL), BF16),
                   jax.ShapeDtypeStruct((N_TOK, D_MODEL), F32)],
        scratch_shapes=[pltpu.VMEM((D_MODEL, tn), BF16), pltpu.VMEM((D_MODEL, tn), BF16),
                        pltpu.VMEM((D_MODEL, tn), BF16),
                        pltpu.VMEM((TOKEN_TILE + 2 * SUBLANES, tn), F32)],
        compiler_params=_params("arbitrary", "arbitrary"),
        name="conv_in",
    )(xb, w_in, w_in, w_in, conv_w, s0, s1)


def kernel(x_prompt, x_sample, state_gla, state_conv, router_w, router_b, gla_w_in, gla_w_gate, gla_b_gate,
           gla_norm_g, gla_w_out, conv_w_in, conv_w, conv_w_out, ln_mix_g, ln_mix_b, ln_ffn_g, ln_ffn_b,
           moe_w1, moe_w3, moe_w2):
    xp = x_prompt.reshape(N_PROMPT, D_MODEL)
    xs = x_sample.reshape(N_SAMPLE, D_MODEL)
    router = _router_operands(router_w, router_b)

    w_in_t = gla_w_in[0].T
    qkvr, gl = _gla_in_proj(xp, xs, w_in_t)
    g = _gla_gate(gl, gla_w_gate[0], gla_b_gate[0])
    norm_g = gla_norm_g[0].reshape(1, DV_TOTAL)
    og, s_prompt = _gla_prompt(qkvr, g, norm_g)
    og, s_sample = _gla_sample(qkvr, g, norm_g, state_gla[0], og)
    x1, ids, gates = _proj_norm_route(og, (xp, xs), gla_w_out[0], ln_mix_g[0], ln_mix_b[0], router)
    x2, x2b = _moe_block(x1, ids, gates, moe_w1, moe_w3, moe_w2, 0, ln_ffn_g[0], ln_ffn_b[0], False)

    gbz, u = _conv_in(x2b, conv_w_in[0], conv_w[0], state_conv[0])
    x3, ids, gates = _proj_norm_route(gbz, x2, conv_w_out[0], ln_mix_g[1], ln_mix_b[1], router)
    y_p, y_s = _moe_block(x3, ids, gates, moe_w1, moe_w3, moe_w2, 1, ln_ffn_g[1], ln_ffn_b[1], True)

    y_prompt = y_p.reshape(N_PROMPT_SEQ, SEQ, D_MODEL)
    y_sample = y_s.reshape(N_SAMPLE, 1, D_MODEL)
    conv_prompt = jnp.stack([u[(b + 1) * SEQ - (CONV_W - 1):(b + 1) * SEQ] for b in range(N_PROMPT_SEQ)])
    conv_sample = jnp.concatenate([state_conv[0][:, 1:, :], u[N_PROMPT:][:, None, :]], axis=1)
    return (y_prompt, y_sample, s_prompt[None], conv_prompt[None], s_sample[None], conv_sample[None])
```

```python
import functools

import jax
import jax.numpy as jnp
from jax import lax
from jax.experimental import pallas as pl
from jax.experimental.pallas import tpu as pltpu

F32 = jnp.float32
BF16 = jnp.bfloat16

D_MODEL = 2048
N_PROMPT_SEQ = 4
SEQ = 2048
N_PROMPT = N_PROMPT_SEQ * SEQ
N_SAMPLE = 128
N_TOK = N_PROMPT + N_SAMPLE
DEPTH = 2

GLA_HEADS = 4
DK_TOTAL = D_MODEL // 2
DV_TOTAL = D_MODEL
DK_HEAD = DK_TOTAL // GLA_HEADS
DV_HEAD = DV_TOTAL // GLA_HEADS
GATE_RANK = 16
GATE_NORMALIZER = 16.0
GLA_QKVR_WIDTH = 2 * DK_TOTAL + 2 * DV_TOTAL
CONV_W = 3
N_EXPERTS = 16
N_GROUPS = 4
EXPERTS_PER_GROUP = N_EXPERTS // N_GROUPS
TOP_K = 2
D_FF = D_MODEL // 2
ALPHA = (2.0 * DEPTH) ** 0.25
LN_EPS = 1e-5
RMS_EPS = 1e-6

LANES = 128
SUBLANES = 8
VMEM_LIMIT_BYTES = 56 * 1024 * 1024

TOKEN_TILE = 640
ROW_TILE = 320
GLA_CHUNK = 64
GLA_STEP_CHUNKS = 8
GLA_SAMPLE_BATCH = 16
MOE_TILE = 256
N_ASSIGN = N_TOK * TOP_K
GATHER_TILES = 2
MOE_MAX_TILES = -(-(N_ASSIGN // MOE_TILE + N_EXPERTS) // GATHER_TILES) * GATHER_TILES
N_ASSIGN_PAD = MOE_MAX_TILES * MOE_TILE


def _params(*semantics):
    return pltpu.CompilerParams(dimension_semantics=semantics, vmem_limit_bytes=VMEM_LIMIT_BYTES)


def _split3(x):
    hi = x.astype(BF16)
    r1 = x - hi.astype(F32)
    mid = r1.astype(BF16)
    lo = (r1 - mid.astype(F32)).astype(BF16)
    return hi, mid, lo


def _cast_weight(src_ref, dst_ref):
    rows = 64
    n = src_ref.shape[0] // rows

    def body(c, carry):
        r = pl.multiple_of(c * rows, rows)
        dst_ref[pl.ds(r, rows), :] = src_ref[pl.ds(r, rows), :].astype(BF16)
        return carry
    lax.fori_loop(0, n, body, 0)


def _dot_nn(a, b):
    return jnp.dot(a, b, preferred_element_type=F32)


def _dot_tn(a, b):
    return lax.dot_general(a, b, (((0,), (0,)), ((), ())), preferred_element_type=F32)


def _dot_nt(a, b):
    return lax.dot_general(a, b, (((1,), (1,)), ((), ())), preferred_element_type=F32)


def _col_bcast(rows, n):
    ones = jnp.ones((rows.shape[0], LANES), BF16)
    hi, mid, lo = _split3(rows)
    col = _dot_tn(hi, ones) + _dot_tn(mid, ones) + _dot_tn(lo, ones)
    return jnp.concatenate([col] * (n // LANES), axis=1)


def _silu(r):
    return r / (1.0 + jnp.exp(-r))


def _layer_norm(h, g, b):
    mu = jnp.mean(h, axis=-1, keepdims=True)
    d = h - mu
    var = jnp.mean(d * d, axis=-1, keepdims=True)
    return d * lax.rsqrt(var + LN_EPS) * g + b


X_TILE_BLOCKS = TOKEN_TILE // N_SAMPLE


def _x_tile_specs(tile_index):
    last = N_PROMPT // N_SAMPLE - 1

    def prompt_spec(k):
        return pl.BlockSpec((N_SAMPLE, D_MODEL),
                            lambda *g: (jnp.minimum(tile_index(*g) * X_TILE_BLOCKS + k, last), 0))
    return ([prompt_spec(k) for k in range(X_TILE_BLOCKS)]
            + [pl.BlockSpec((N_SAMPLE, D_MODEL), lambda *g: (0, 0))])


def _x_tile(i, x_refs):
    *prompt_refs, sample_ref = x_refs
    tail_is_sample = (i + 1) * TOKEN_TILE > N_PROMPT
    blocks = [r[...] for r in prompt_refs[:-1]]
    blocks.append(jnp.where(tail_is_sample, sample_ref[...], prompt_refs[-1][...]))
    return jnp.concatenate(blocks, axis=0)


def _gla_in_proj_kernel(*refs):
    *x_refs, wt_ref, wgl_ref, wg_ref, bg_ref, o_ref, g_ref, wb_ref = refs
    j = pl.program_id(0)
    i = pl.program_id(1)

    @pl.when(i == 0)
    def _():
        _cast_weight(wt_ref, wb_ref)

    x = _x_tile(i, x_refs).astype(BF16)
    for s in range(TOKEN_TILE // ROW_TILE):
        rows = slice(s * ROW_TILE, (s + 1) * ROW_TILE)
        o_ref[rows, :] = _dot_nt(x[rows, :], wb_ref[...]).astype(o_ref.dtype)

    @pl.when(j == pl.num_programs(0) - 1)
    def _():
        wgl = jnp.concatenate([wgl_ref[...].astype(BF16), jnp.zeros((LANES - GATE_RANK, D_MODEL), BF16)],
                              axis=0)
        gl = _dot_nt(x, wgl)
        z = _dot_nn(gl.astype(BF16), wg_ref[...]) + bg_ref[...]
        log_sig = jnp.minimum(z, 0.0) - jnp.log1p(jnp.exp(-jnp.abs(z)))
        g_ref[...] = log_sig * (1.0 / GATE_NORMALIZER)

    @pl.when(j < pl.num_programs(0) - 1)
    def _():
        g_ref[...] = jnp.zeros_like(g_ref)


def _gla_in_proj(x_prompt, x_sample, w_t, w_gate, b_gate):
    tn = 2048
    n_pass = GLA_QKVR_WIDTH // tn
    n_tiles = N_TOK // TOKEN_TILE
    wg = jnp.pad(w_gate, ((0, LANES - GATE_RANK), (0, 0))).astype(BF16)
    return pl.pallas_call(
        _gla_in_proj_kernel,
        grid=(n_pass, n_tiles),
        in_specs=_x_tile_specs(lambda j, i: i) + [
            pl.BlockSpec((tn, D_MODEL), lambda j, i: (j, 0), pipeline_mode=pl.Buffered(1)),
            pl.BlockSpec((GATE_RANK, D_MODEL), lambda j, i: (GLA_QKVR_WIDTH // GATE_RANK, 0)),
            pl.BlockSpec((LANES, DK_TOTAL), lambda j, i: (0, 0)),
            pl.BlockSpec((1, DK_TOTAL), lambda j, i: (0, 0))],
        out_specs=[pl.BlockSpec((TOKEN_TILE, tn), lambda j, i: (i, j)),
                   pl.BlockSpec((TOKEN_TILE, DK_TOTAL),
                                lambda j, i: (jnp.where(j == n_pass - 1, i, n_tiles), 0))],
        out_shape=[jax.ShapeDtypeStruct((N_TOK, GLA_QKVR_WIDTH), F32),
                   jax.ShapeDtypeStruct((N_TOK + TOKEN_TILE, DK_TOTAL), F32)],
        scratch_shapes=[pltpu.VMEM((tn, D_MODEL), BF16)],
        compiler_params=_params("arbitrary", "arbitrary"),
        name="gla_in_proj",
    )(*([x_prompt] * X_TILE_BLOCKS), x_sample, w_t, w_t, wg, b_gate.reshape(1, DK_TOTAL))


def _rms_gate(o, norm_g, r):
    o = o * lax.rsqrt(jnp.mean(o * o, axis=-1, keepdims=True) + RMS_EPS)
    return (o * norm_g) * _silu(r)


def _gla_prompt_kernel(q_ref, k_ref, v_ref, r_ref, g_ref, ng_ref, og_in_ref, og_ref, s_out_ref, s_ref):
    del og_in_ref
    c = pl.program_id(1)
    cc = GLA_CHUNK

    @pl.when(c == 0)
    def _():
        s_ref[...] = jnp.zeros_like(s_ref)

    for ci in range(q_ref.shape[0] // cc):
        last_chunk = ci == q_ref.shape[0] // cc - 1
        _gla_chunk(slice(ci * cc, (ci + 1) * cc), last_chunk, c, q_ref, k_ref, v_ref, r_ref, g_ref,
                   ng_ref, og_ref, s_out_ref, s_ref)


def _gla_chunk(rows, last_chunk, c, q_ref, k_ref, v_ref, r_ref, g_ref, ng_ref, og_ref, s_out_ref, s_ref):
    cc = GLA_CHUNK
    row = lax.broadcasted_iota(jnp.int32, (cc, cc), 0)
    col = lax.broadcasted_iota(jnp.int32, (cc, cc), 1)
    causal = col <= row
    tri = jnp.where(causal, 1.0, 0.0).astype(BF16)
    b = _dot_nn(jnp.concatenate([tri] * 3, axis=1),
                jnp.concatenate(_split3(g_ref[rows, :]), axis=0))
    b_last = b[cc - 1:cc, :]
    b_mid = b[cc // 2 - 1:cc // 2, :]

    q = q_ref[rows, :]
    k = k_ref[rows, :] * (DK_HEAD ** -0.5)
    q_in = (q * jnp.exp(b)).astype(BF16)
    qh, qm, _ = _split3(q * jnp.exp(b - b_mid))
    kh, km, _ = _split3(k * jnp.exp(b_mid - b))
    k_d = (k * jnp.exp(b_last - b)).astype(BF16)

    pad = 2 * SUBLANES
    sub = lax.broadcasted_iota(jnp.int32, (pad, DK_TOTAL), 0)
    decay_rows = jnp.where(sub == 0, jnp.broadcast_to(jnp.exp(b_last), (pad, DK_TOTAL)), 0.0)
    decay_col = _dot_tn(jnp.concatenate(_split3(decay_rows), axis=0),
                        jnp.ones((3 * pad, LANES), BF16))

    for h in range(GLA_HEADS):
        dk = slice(h * DK_HEAD, (h + 1) * DK_HEAD)
        dv = slice(h * DV_HEAD, (h + 1) * DV_HEAD)
        q3 = jnp.concatenate([qh[:, dk], qh[:, dk], qm[:, dk]], axis=1)
        k3 = jnp.concatenate([kh[:, dk], km[:, dk], kh[:, dk]], axis=1)
        scores = jnp.where(causal, _dot_nt(q3, k3), 0.0).astype(BF16)
        v = v_ref[rows, dv].astype(BF16)
        s_old = s_ref[h]
        o = _dot_nn(jnp.concatenate([q_in[:, dk], scores], axis=1),
                    jnp.concatenate([s_old.astype(BF16), v], axis=0))
        decay = jnp.concatenate([decay_col[dk, :]] * (DV_HEAD // LANES), axis=1)
        s_new = decay * s_old + _dot_tn(k_d[:, dk], v)
        s_ref[h] = s_new

        if last_chunk:
            @pl.when(c == pl.num_programs(1) - 1)
            def _():
                s_out_ref[0, h] = s_new

        og_ref[rows, dv] = _rms_gate(o, ng_ref[:, dv], r_ref[rows, dv]).astype(BF16)


def _gla_prompt(qkvr, g, norm_g):
    step = GLA_STEP_CHUNKS * GLA_CHUNK
    nc = SEQ // step
    row = lambda b, c: b * nc + c
    return pl.pallas_call(
        _gla_prompt_kernel,
        grid=(N_PROMPT_SEQ, nc),
        in_specs=[pl.BlockSpec((step, DK_TOTAL), lambda b, c: (row(b, c), 0)),
                  pl.BlockSpec((step, DK_TOTAL), lambda b, c: (row(b, c), 1)),
                  pl.BlockSpec((step, DV_TOTAL), lambda b, c: (row(b, c), 1)),
                  pl.BlockSpec((step, DV_TOTAL), lambda b, c: (row(b, c), 2)),
                  pl.BlockSpec((step, DK_TOTAL), lambda b, c: (row(b, c), 0)),
                  pl.BlockSpec((1, DV_TOTAL), lambda b, c: (0, 0)),
                  pl.BlockSpec(memory_space=pl.ANY)],
        out_specs=[pl.BlockSpec((step, DV_TOTAL), lambda b, c: (row(b, c), 0)),
                   pl.BlockSpec((1, GLA_HEADS, DK_HEAD, DV_HEAD), lambda b, c: (b, 0, 0, 0))],
        out_shape=[jax.ShapeDtypeStruct((N_TOK, DV_TOTAL), BF16),
                   jax.ShapeDtypeStruct((N_PROMPT_SEQ, GLA_HEADS, DK_HEAD, DV_HEAD), F32)],
        scratch_shapes=[pltpu.VMEM((GLA_HEADS, DK_HEAD, DV_HEAD), F32)],
        input_output_aliases={6: 0},
        compiler_params=_params("arbitrary", "arbitrary"),
        name="gla_prompt",
    )(qkvr, qkvr, qkvr, qkvr, g, norm_g, jnp.zeros((N_TOK, DV_TOTAL), BF16))


def _gla_sample_kernel(q_ref, k_ref, v_ref, r_ref, g_ref, ng_ref, s_ref, og_in_ref, og_ref, s_out_ref,
                       o_scr):
    del og_in_ref
    bb = q_ref.shape[0]
    q = q_ref[...]
    k = k_ref[...] * (DK_HEAD ** -0.5)
    v = v_ref[...]
    eg = jnp.exp(g_ref[...])
    qe = q * eg
    qk = jnp.sum(q * k, axis=-1, keepdims=True)
    sub = lax.broadcasted_iota(jnp.int32, (bb, DK_HEAD), 0)
    for bi in range(bb):
        sel = sub == bi
        s_old = s_ref[bi, 0]
        decay = _col_bcast(jnp.where(sel, eg, 0.0), DV_HEAD)
        k_col = _col_bcast(jnp.where(sel, k, 0.0), DV_HEAD)
        qe_col = _col_bcast(jnp.where(sel, qe, 0.0), DV_HEAD)
        s_out_ref[bi, 0] = decay * s_old + k_col * v[bi:bi + 1, :]
        o_scr[bi:bi + 1, :] = jnp.sum(qe_col * s_old, axis=0, keepdims=True)
    o = qk * v + o_scr[...]
    og_ref[...] = _rms_gate(o, ng_ref[...], r_ref[...]).astype(BF16)


def _gla_sample(qkvr, g, norm_g, state, og):
    bb = GLA_SAMPLE_BATCH
    r0 = N_PROMPT // bb
    hk = DK_TOTAL // DK_HEAD
    hv = 2 * DK_TOTAL // DV_HEAD
    return pl.pallas_call(
        _gla_sample_kernel,
        grid=(N_SAMPLE // bb, GLA_HEADS),
        in_specs=[pl.BlockSpec((bb, DK_HEAD), lambda i, h: (r0 + i, h)),
                  pl.BlockSpec((bb, DK_HEAD), lambda i, h: (r0 + i, hk + h)),
                  pl.BlockSpec((bb, DV_HEAD), lambda i, h: (r0 + i, hv + h)),
                  pl.BlockSpec((bb, DV_HEAD), lambda i, h: (r0 + i, hv + GLA_HEADS + h)),
                  pl.BlockSpec((bb, DK_HEAD), lambda i, h: (r0 + i, h)),
                  pl.BlockSpec((1, DV_HEAD), lambda i, h: (0, h)),
                  pl.BlockSpec((bb, 1, DK_HEAD, DV_HEAD), lambda i, h: (i, h, 0, 0)),
                  pl.BlockSpec(memory_space=pl.ANY)],
        out_specs=[pl.BlockSpec((bb, DV_HEAD), lambda i, h: (r0 + i, h)),
                   pl.BlockSpec((bb, 1, DK_HEAD, DV_HEAD), lambda i, h: (i, h, 0, 0))],
        out_shape=[jax.ShapeDtypeStruct((N_TOK, DV_TOTAL), BF16),
                   jax.ShapeDtypeStruct((N_SAMPLE, GLA_HEADS, DK_HEAD, DV_HEAD), F32)],
        scratch_shapes=[pltpu.VMEM((bb, DV_HEAD), F32)],
        input_output_aliases={7: 0},
        compiler_params=_params("arbitrary", "arbitrary"),
        name="gla_sample",
    )(qkvr, qkvr, qkvr, qkvr, g, norm_g, state, og)


def _top2_of4(p):
    ranks = []
    for j in range(4):
        rk = jnp.zeros(p[j].shape, jnp.int32)
        for i in range(4):
            if i == j:
                continue
            beats = (p[i] >= p[j]) if i < j else (p[i] > p[j])
            rk = rk + jnp.where(beats, 1, 0)
        ranks.append(rk)

    def pick(rank):
        val = jnp.zeros(p[0].shape, F32)
        idx = jnp.zeros(p[0].shape, jnp.int32)
        for j in range(4):
            hit = ranks[j] == rank
            val = jnp.where(hit, p[j], val)
            idx = jnp.where(hit, j, idx)
        return val, idx

    v1, i1 = pick(0)
    v2, i2 = pick(1)
    return v1, i1, v2, i2


def _route(x, rw_ref, rb_ref):
    logits = _dot_nn(x.astype(BF16), rw_ref[...])
    n = x.shape[0]
    n_pad = -n % LANES
    if n_pad:
        logits = jnp.concatenate([logits, jnp.zeros((n_pad, LANES), F32)], axis=0)
    lt = logits.T[0:N_EXPERTS, 0:n] + rb_ref[...]
    e = jnp.exp(lt - jnp.max(lt, axis=0, keepdims=True))
    probs = e / jnp.sum(e, axis=0, keepdims=True)
    best = None
    for grp in range(N_GROUPS):
        rows = [probs[grp * EXPERTS_PER_GROUP + j:grp * EXPERTS_PER_GROUP + j + 1, :]
                for j in range(EXPERTS_PER_GROUP)]
        v1, i1, v2, i2 = _top2_of4(rows)
        score = v1 + v2
        cand = (score, v1, i1 + grp * EXPERTS_PER_GROUP, v2, i2 + grp * EXPERTS_PER_GROUP)
        if best is None:
            best = cand
        else:
            better = score > best[0]
            best = tuple(jnp.where(better, n, o) for n, o in zip(cand, best))
    _, v1, e1, v2, e2 = best
    denom = v1 + v2
    return (e1, e2), (v1 / denom, v2 / denom)


def _proj_norm_route_kernel(a_ref, *refs):
    *x_refs, w_ref, lg_ref, lb_ref, rw_ref, rb_ref, xo_ref, id_ref, gt_ref = refs
    x = x_refs[0][...] if len(x_refs) == 1 else _x_tile(pl.program_id(0), x_refs)
    sub = id_ref.shape[2]
    for s in range(id_ref.shape[0]):
        rows = slice(s * sub, (s + 1) * sub)
        y = _dot_nn(a_ref[rows, :], w_ref[...])
        xn = _layer_norm(ALPHA * x[rows, :] + y, lg_ref[...], lb_ref[...])
        xo_ref[rows, :] = xn
        (e1, e2), (g1, g2) = _route(xn, rw_ref, rb_ref)
        id_ref[s, 0:1, :] = e1
        id_ref[s, 1:2, :] = e2
        gt_ref[s, 0:1, :] = g1
        gt_ref[s, 1:2, :] = g2


def _router_operands(router_w, router_b):
    rw = jnp.pad(router_w, ((0, 0), (0, LANES - N_EXPERTS))).astype(BF16)
    return rw, router_b.reshape(N_EXPERTS, 1)


def _proj_norm_route(a, x, w_out, ln_g, ln_b, router):
    nt = N_TOK // ROW_TILE
    n_sub = TOKEN_TILE // ROW_TILE
    rw, rb = router
    const = lambda i: (0, 0)
    if isinstance(x, tuple):
        x_specs = _x_tile_specs(lambda i: i)
        x_args = [x[0]] * X_TILE_BLOCKS + [x[1]]
    else:
        x_specs = [pl.BlockSpec((TOKEN_TILE, D_MODEL), lambda i: (i, 0))]
        x_args = [x]
    x1, ids, gates = pl.pallas_call(
        _proj_norm_route_kernel,
        grid=(N_TOK // TOKEN_TILE,),
        in_specs=[pl.BlockSpec((TOKEN_TILE, D_MODEL), lambda i: (i, 0))] + x_specs + [
            pl.BlockSpec((D_MODEL, D_MODEL), const, pipeline_mode=pl.Buffered(1)),
            pl.BlockSpec((1, D_MODEL), const),
            pl.BlockSpec((1, D_MODEL), const),
            pl.BlockSpec((D_MODEL, LANES), const),
            pl.BlockSpec((N_EXPERTS, 1), const)],
        out_specs=[pl.BlockSpec((TOKEN_TILE, D_MODEL), lambda i: (i, 0)),
                   pl.BlockSpec((n_sub, TOP_K, ROW_TILE), lambda i: (i, 0, 0)),
                   pl.BlockSpec((n_sub, TOP_K, ROW_TILE), lambda i: (i, 0, 0))],
        out_shape=[jax.ShapeDtypeStruct((N_TOK, D_MODEL), F32),
                   jax.ShapeDtypeStruct((nt, TOP_K, ROW_TILE), jnp.int32),
                   jax.ShapeDtypeStruct((nt, TOP_K, ROW_TILE), F32)],
        compiler_params=_params("arbitrary"),
        name="proj_norm_route",
    )(a, *x_args, w_out.astype(BF16), ln_g.reshape(1, D_MODEL), ln_b.reshape(1, D_MODEL), rw, rb)
    ids = ids.transpose(0, 2, 1).reshape(N_TOK, TOP_K)
    gates = gates.transpose(0, 2, 1).reshape(N_TOK, TOP_K)
    return x1, ids, gates


def _stream_expert_weights(layer, tables, w_refs, stage_refs, wb_refs, sem_ref):
    te_ref, nx_ref, nt_ref = tables
    i = pl.program_id(0)

    def copy(expert, j):
        return pltpu.make_async_copy(w_refs[j].at[layer, expert], stage_refs[j], sem_ref.at[j])

    @pl.when(i == 0)
    def _():
        for j in range(len(w_refs)):
            copy(te_ref[0], j).start()

    first = jnp.logical_or(i == 0, te_ref[i] != te_ref[jnp.maximum(i - 1, 0)])

    @pl.when(jnp.logical_and(first, i < nt_ref[0]))
    def _():
        for j, (stage, wb) in enumerate(zip(stage_refs, wb_refs)):
            copy(te_ref[i], j).wait()
            _cast_weight(stage, wb)

            @pl.when(nx_ref[i] >= 0)
            def _():
                copy(nx_ref[i], j).start()


def _issue_row_copies(idx_ref, base, src_ref, dst_ref, sem, n_rows, stride=1):
    def body(g, carry):
        r0 = pl.multiple_of(g * SUBLANES, SUBLANES)
        for k in range(SUBLANES):
            row = idx_ref[base + stride * (r0 + k)]
            pltpu.make_async_copy(src_ref.at[pl.ds(row, 1)], dst_ref.at[pl.ds(r0 + k, 1)],
                                  sem).start(priority=k % 2)
        return carry
    lax.fori_loop(0, n_rows // SUBLANES, body, 0)


def _gather_kernel(idx_ref, nt_ref, src_ref, out_ref, buf_ref, sem_ref):
    i = pl.program_id(0)
    tg = out_ref.shape[0]
    n = (nt_ref[0] * MOE_TILE + tg - 1) // tg

    def start_tile(tile, slot):
        _issue_row_copies(idx_ref, tile * tg, src_ref, buf_ref.at[slot], sem_ref.at[slot], tg)

    @pl.when(i == 0)
    def _():
        start_tile(0, 0)

    @pl.when(i + 1 < n)
    def _():
        start_tile(i + 1, (i + 1) % 2)

    @pl.when(i < n)
    def _():
        slot = i % 2
        pltpu.make_async_copy(src_ref.at[pl.ds(0, tg)], buf_ref.at[slot], sem_ref.at[slot]).wait()
        out_ref[...] = buf_ref[slot].astype(out_ref.dtype)

    @pl.when(i >= n)
    def _():
        out_ref[...] = jnp.zeros_like(out_ref)


def _gather_rows(src, idx, n_tiles, out_dtype, name):
    m = idx.shape[0]
    width = src.shape[1]
    tg = GATHER_TILES * MOE_TILE
    return pl.pallas_call(
        _gather_kernel,
        grid_spec=pltpu.PrefetchScalarGridSpec(
            num_scalar_prefetch=2,
            grid=(m // tg,),
            in_specs=[pl.BlockSpec(memory_space=pl.ANY)],
            out_specs=pl.BlockSpec((tg, width), lambda i, idx, nt: (i, 0)),
            scratch_shapes=[pltpu.VMEM((2, tg, width), src.dtype),
                            pltpu.SemaphoreType.DMA((2,))]),
        out_shape=jax.ShapeDtypeStruct((m, width), out_dtype),
        compiler_params=_params("arbitrary"),
        name=name,
    )(idx, n_tiles, src)


def _moe_ffn_kernel(layer, te_ref, nx_ref, nt_ref, xs_ref, w1_ref, w3_ref, w2_ref, ys_ref,
                    stage13_ref, stage2_ref, w1b_ref, w3b_ref, w2b_ref, sem_ref):
    _stream_expert_weights(layer, (te_ref, nx_ref, nt_ref), (w1_ref, w3_ref, w2_ref),
                           (stage13_ref.at[0], stage13_ref.at[1], stage2_ref),
                           (w1b_ref, w3b_ref, w2b_ref), sem_ref)
    i = pl.program_id(0)

    @pl.when(i < nt_ref[0])
    def _():
        xs = xs_ref[...]
        h1 = _dot_nn(xs, w1b_ref[...])
        h3 = _dot_nn(xs, w3b_ref[...])
        hid = (_silu(h1) * h3).astype(BF16)
        ys_ref[...] = _dot_nn(hid, w2b_ref[...])

    @pl.when(i >= nt_ref[0])
    def _():
        ys_ref[...] = jnp.zeros_like(ys_ref)


def _moe_ffn(xs, w1, w3, w2, layer, tables):
    last = lambda i, te, nx, nt: (jnp.minimum(i, nt[0] - 1), 0)
    return pl.pallas_call(
        functools.partial(_moe_ffn_kernel, layer),
        grid_spec=pltpu.PrefetchScalarGridSpec(
            num_scalar_prefetch=3,
            grid=(MOE_MAX_TILES,),
            in_specs=[pl.BlockSpec((MOE_TILE, D_MODEL), last),
                      pl.BlockSpec(memory_space=pl.ANY),
                      pl.BlockSpec(memory_space=pl.ANY),
                      pl.BlockSpec(memory_space=pl.ANY)],
            out_specs=pl.BlockSpec((MOE_TILE, D_MODEL), lambda i, te, nx, nt: (i, 0)),
            scratch_shapes=[pltpu.VMEM((2, D_MODEL, D_FF), F32),
                            pltpu.VMEM((D_FF, D_MODEL), F32),
                            pltpu.VMEM((D_MODEL, D_FF), BF16), pltpu.VMEM((D_MODEL, D_FF), BF16),
                            pltpu.VMEM((D_FF, D_MODEL), BF16),
                            pltpu.SemaphoreType.DMA((3,))]),
        out_shape=jax.ShapeDtypeStruct((N_ASSIGN_PAD, D_MODEL), F32),
        compiler_params=_params("arbitrary"),
        name="moe_ffn",
    )(*tables, xs, w1, w3, w2)


def _combine_norm_kernel(final, pos_ref, x_ref, ys_ref, gt_ref, lg_ref, lb_ref, o0_ref, o1_ref,
                         buf_ref, sem_ref):
    i = pl.program_id(0)
    n = pl.num_programs(0)
    tm = x_ref.shape[0]

    def start_tile(tile, slot):
        for k in range(TOP_K):
            _issue_row_copies(pos_ref, tile * tm * TOP_K + k, ys_ref, buf_ref.at[slot, k],
                              sem_ref.at[slot], tm, stride=TOP_K)

    @pl.when(i == 0)
    def _():
        start_tile(0, 0)

    @pl.when(i + 1 < n)
    def _():
        start_tile(i + 1, (i + 1) % 2)

    slot = i % 2
    for k in range(TOP_K):
        pltpu.make_async_copy(ys_ref.at[pl.ds(0, tm)], buf_ref.at[slot, k], sem_ref.at[slot]).wait()
    gt = gt_ref[...]
    moe = buf_ref[slot, 0] * gt[:, 0:1] + buf_ref[slot, 1] * gt[:, 1:2]
    xn = _layer_norm(ALPHA * x_ref[...] + moe, lg_ref[...], lb_ref[...])
    if final:
        @pl.when(i < n - 1)
        def _():
            o0_ref[...] = xn

        @pl.when(i == n - 1)
        def _():
            o1_ref[...] = xn
    else:
        o0_ref[...] = xn
        o1_ref[...] = xn.astype(BF16)


def _combine_norm(x, ys, pos, gates, ln_g, ln_b, final):
    tm = N_SAMPLE if final else ROW_TILE
    const = lambda i, pos: (0, 0)
    tile = lambda i, pos: (i, 0)
    if final:
        n_prompt_tiles = N_PROMPT // tm
        out_specs = [pl.BlockSpec((tm, D_MODEL), lambda i, pos: (jnp.minimum(i, n_prompt_tiles - 1), 0)),
                     pl.BlockSpec((tm, D_MODEL), const)]
        out_shape = [jax.ShapeDtypeStruct((N_PROMPT, D_MODEL), F32),
                     jax.ShapeDtypeStruct((N_SAMPLE, D_MODEL), F32)]
    else:
        out_specs = [pl.BlockSpec((tm, D_MODEL), tile), pl.BlockSpec((tm, D_MODEL), tile)]
        out_shape = [jax.ShapeDtypeStruct((N_TOK, D_MODEL), F32),
                     jax.ShapeDtypeStruct((N_TOK, D_MODEL), BF16)]
    return pl.pallas_call(
        functools.partial(_combine_norm_kernel, final),
        grid_spec=pltpu.PrefetchScalarGridSpec(
            num_scalar_prefetch=1,
            grid=(N_TOK // tm,),
            in_specs=[pl.BlockSpec((tm, D_MODEL), tile),
                      pl.BlockSpec(memory_space=pl.ANY),
                      pl.BlockSpec((tm, TOP_K), tile),
                      pl.BlockSpec((1, D_MODEL), const),
                      pl.BlockSpec((1, D_MODEL), const)],
            out_specs=out_specs,
            scratch_shapes=[pltpu.VMEM((2, TOP_K, tm, D_MODEL), F32),
                            pltpu.SemaphoreType.DMA((2,))]),
        out_shape=out_shape,
        compiler_params=_params("arbitrary"),
        name="combine_norm",
    )(pos, x, ys, gates, ln_g.reshape(1, D_MODEL), ln_b.reshape(1, D_MODEL))


def _routing_tables(ids):
    e = ids.reshape(-1)
    onehot = (e[:, None] == jnp.arange(N_EXPERTS, dtype=jnp.int32)[None, :]).astype(jnp.int32)
    csum = jnp.cumsum(onehot, axis=0)
    rank = jnp.sum(onehot * csum, axis=1) - 1
    sizes = csum[-1]
    tiles_per = (sizes + MOE_TILE - 1) // MOE_TILE
    tile_end = jnp.cumsum(tiles_per)
    tile_start = tile_end - tiles_per
    n_tiles = tile_end[-1]
    pos = (jnp.sum(onehot * (tile_start * MOE_TILE)[None, :], axis=1) + rank).astype(jnp.int32)
    tile = jnp.arange(MOE_MAX_TILES, dtype=jnp.int32)
    owner = jnp.sum((tile_end[None, :] <= jnp.minimum(tile, n_tiles - 1)[:, None]).astype(jnp.int32), axis=1)
    order = jnp.argsort(e, stable=True).astype(jnp.int32)
    owner_hot = (owner[:, None] == jnp.arange(N_EXPERTS, dtype=jnp.int32)[None, :]).astype(jnp.int32)
    size_start = jnp.cumsum(sizes) - sizes
    tile_first = jnp.sum(owner_hot * (size_start - tile_start * MOE_TILE)[None, :], axis=1)
    tile_limit = jnp.sum(owner_hot * (size_start + sizes)[None, :], axis=1)
    row = jnp.arange(N_ASSIGN_PAD, dtype=jnp.int32).reshape(MOE_MAX_TILES, MOE_TILE)
    src = tile_first[:, None] + row
    valid = src < tile_limit[:, None]
    row_token = jnp.where(valid, order[jnp.clip(src, 0, N_ASSIGN - 1)] // TOP_K, row % N_TOK).reshape(-1)
    experts = jnp.arange(N_EXPERTS, dtype=jnp.int32)
    later = lax.cummin(jnp.where(tiles_per > 0, experts, N_EXPERTS), reverse=True)
    next_expert = jnp.concatenate([later[1:], jnp.full((1,), N_EXPERTS, jnp.int32)])
    next_expert = jnp.where(next_expert >= N_EXPERTS, -1, next_expert)
    tile_next = jnp.sum(owner_hot * next_expert[None, :], axis=1)
    n_tiles = n_tiles.reshape(1).astype(jnp.int32)
    tables = (owner.astype(jnp.int32), tile_next.astype(jnp.int32), n_tiles)
    return pos, row_token.astype(jnp.int32), tables


def _moe_block(x, ids, gates, w1, w3, w2, layer, ln_g, ln_b, final):
    pos, row_token, tables = _routing_tables(ids)
    xs = _gather_rows(x, row_token, tables[2], BF16, "moe_gather")
    ys = _moe_ffn(xs, w1, w3, w2, layer, tables)
    return _combine_norm(x, ys, pos, gates, ln_g, ln_b, final)


def _conv_in_kernel(x_ref, wb_ref, wc_ref, wh_ref, cw_ref, s0_ref, s1_ref, gbz_ref, u_ref,
                    wbb_ref, wcb_ref, whb_ref, ubuf_ref):
    i = pl.program_id(1)
    tm = x_ref.shape[0]
    pad = SUBLANES

    @pl.when(i == 0)
    def _():
        _cast_weight(wb_ref, wbb_ref)
        _cast_weight(wc_ref, wcb_ref)
        _cast_weight(wh_ref, whb_ref)
        ubuf_ref[0:pad, :] = jnp.zeros((pad, ubuf_ref.shape[1]), F32)

    w0 = cw_ref[0:1, :]
    w1 = cw_ref[1:2, :]
    w2 = cw_ref[2:3, :]
    sub = ROW_TILE
    for s in range(tm // sub):
        lo, hi = s * sub, (s + 1) * sub
        x = x_ref[lo:hi, :]
        gb = _dot_nn(x, wbb_ref[...])
        u = _dot_nn(x, wcb_ref[...]) * _dot_nn(x, whb_ref[...])
        u_ref[lo:hi, :] = u
        ubuf_ref[pad + lo:pad + hi, :] = u
        t = (i * tm + lo + lax.broadcasted_iota(jnp.int32, (sub, 1), 0)) & (SEQ - 1)
        u_m1 = jnp.where(t >= 1, ubuf_ref[pad - 1 + lo:pad - 1 + hi, :], 0.0)
        u_m2 = jnp.where(t >= 2, ubuf_ref[pad - 2 + lo:pad - 2 + hi, :], 0.0)
        z = u_m2 * w0 + u_m1 * w1 + u * w2
        gbz_ref[lo:hi, :] = (gb * z).astype(BF16)
    ubuf_ref[0:pad, :] = ubuf_ref[tm:tm + pad, :]

    @pl.when(i == pl.num_programs(1) - 1)
    def _():
        first = sub - N_SAMPLE
        z_s = s0_ref[...] * w0 + s1_ref[...] * w1 + u[first:sub, :] * w2
        gbz_ref[tm - N_SAMPLE:tm, :] = (gb[first:sub, :] * z_s).astype(BF16)


def _conv_in(xb, w_in, conv_w, state):
    tn = 512
    nb = D_MODEL // tn
    s0 = state[:, 0, :]
    s1 = state[:, 1, :]
    return pl.pallas_call(
        _conv_in_kernel,
        grid=(nb, N_TOK // TOKEN_TILE),
        in_specs=[pl.BlockSpec((TOKEN_TILE, D_MODEL), lambda j, i: (i, 0)),
                  pl.BlockSpec((D_MODEL, tn), lambda j, i: (0, j)),
                  pl.BlockSpec((D_MODEL, tn), lambda j, i: (0, nb + j)),
                  pl.BlockSpec((D_MODEL, tn), lambda j, i: (0, 2 * nb + j)),
                  pl.BlockSpec((CONV_W, tn), lambda j, i: (0, j)),
                  pl.BlockSpec((N_SAMPLE, tn), lambda j, i: (0, j)),
                  pl.BlockSpec((N_SAMPLE, tn), lambda j, i: (0, j))],
        out_specs=[pl.BlockSpec((TOKEN_TILE, tn), lambda j, i: (i, j)),
                   pl.BlockSpec((TOKEN_TILE, tn), lambda j, i: (i, j))],
        out_shape=[jax.ShapeDtypeStruct((N_TOK, D_MODEL), BF16),
                   jax.ShapeDtypeStruct((N_TOK, D_MODEL), F32)],
        scratch_shapes=[pltpu.VMEM((D_MODEL, tn), BF16), pltpu.VMEM((D_MODEL, tn), BF16),
                        pltpu.VMEM((D_MODEL, tn), BF16),
                        pltpu.VMEM((TOKEN_TILE + 2 * SUBLANES, tn), F32)],
        compiler_params=_params("arbitrary", "arbitrary"),
        name="conv_in",
    )(xb, w_in, w_in, w_in, conv_w, s0, s1)


def kernel(x_prompt, x_sample, state_gla, state_conv, router_w, router_b, gla_w_in, gla_w_gate, gla_b_gate,
           gla_norm_g, gla_w_out, conv_w_in, conv_w, conv_w_out, ln_mix_g, ln_mix_b, ln_ffn_g, ln_ffn_b,
           moe_w1, moe_w3, moe_w2):
    xp = x_prompt.reshape(N_PROMPT, D_MODEL)
    xs = x_sample.reshape(N_SAMPLE, D_MODEL)
    router = _router_operands(router_w, router_b)

    w_in_t = gla_w_in[0].T
    qkvr, g = _gla_in_proj(xp, xs, w_in_t, gla_w_gate[0], gla_b_gate[0])
    norm_g = gla_norm_g[0].reshape(1, DV_TOTAL)
    og, s_prompt = _gla_prompt(qkvr, g, norm_g)
    og, s_sample = _gla_sample(qkvr, g, norm_g, state_gla[0], og)
    x1, ids, gates = _proj_norm_route(og, (xp, xs), gla_w_out[0], ln_mix_g[0], ln_mix_b[0], router)
    x2, x2b = _moe_block(x1, ids, gates, moe_w1, moe_w3, moe_w2, 0, ln_ffn_g[0], ln_ffn_b[0], False)

    gbz, u = _conv_in(x2b, conv_w_in[0], conv_w[0], state_conv[0])
    x3, ids, gates = _proj_norm_route(gbz, x2, conv_w_out[0], ln_mix_g[1], ln_mix_b[1], router)
    y_p, y_s = _moe_block(x3, ids, gates, moe_w1, moe_w3, moe_w2, 1, ln_ffn_g[1], ln_ffn_b[1], True)

    y_prompt = y_p.reshape(N_PROMPT_SEQ, SEQ, D_MODEL)
    y_sample = y_s.reshape(N_SAMPLE, 1, D_MODEL)
    conv_prompt = jnp.stack([u[(b + 1) * SEQ - (CONV_W - 1):(b + 1) * SEQ] for b in range(N_PROMPT_SEQ)])
    conv_sample = jnp.concatenate([state_conv[0][:, 1:, :], u[N_PROMPT:][:, None, :]], axis=1)
    return (y_prompt, y_sample, s_prompt[None], conv_prompt[None], s_sample[None], conv_sample[None])
```
